```python
import jax, jax.numpy as jnp
from jax import lax
import numpy as np

D_MODEL = 2048
BATCH = 4
SEQ = 2048
DEPTH = 2

N_CONV_LAYERS = DEPTH // 2
N_ATTN_LAYERS = DEPTH - N_CONV_LAYERS
CONV_WIDTH = 3
N_HEADS = 16
HEAD_DIM = D_MODEL // N_HEADS
MOBA_BLOCK = 256
MOBA_TOPK = 3
Q_CHUNK = 16
N_GROUPS = 4
EXPERTS_PER_GROUP = 8
N_EXPERTS = N_GROUPS * EXPERTS_PER_GROUP
EXPERT_TOPK = 2
D_EXPERT = D_MODEL // 4
MOE_ROWS = 128
NORM_EPS = 1e-6
ALIBI_MAX_BIAS = 8.0

kernel_name = 'yoco_shortconv_moba_hier_moe'


def rms_norm(x, g):
    xf = x.astype(jnp.float32)
    y = xf * lax.rsqrt(jnp.mean(xf * xf, axis=-1, keepdims=True) + NORM_EPS)
    return (y * g.astype(jnp.float32)).astype(x.dtype)


def alibi_slopes(n_heads):
    return jnp.exp2(-ALIBI_MAX_BIAS * jnp.arange(1, n_heads + 1, dtype=jnp.float32) / n_heads)


def short_conv_mixer(h, w_in, w_conv, w_out):
    D = h.shape[-1]
    b_gate, c_gate, xh = jnp.split(h @ w_in, 3, axis=-1)
    u = c_gate * xh
    u = lax.conv_general_dilated(u, w_conv[:, None, :].astype(u.dtype), (1,), [(CONV_WIDTH - 1, 0)],
                                 dimension_numbers=('NWC', 'WIO', 'NWC'), feature_group_count=D)
    return (b_gate * u) @ w_out


def shared_kv(h, kv_norm, w_kv, k_norm):
    Bsz, S, D = h.shape
    k, v = jnp.split(rms_norm(h, kv_norm) @ w_kv, 2, axis=-1)
    k = rms_norm(k.reshape(Bsz, S, N_HEADS, HEAD_DIM), k_norm).transpose(0, 2, 1, 3)
    v = v.reshape(Bsz, S, N_HEADS, HEAD_DIM).transpose(0, 2, 1, 3)
    n_blocks = -(-S // MOBA_BLOCK)
    pad = n_blocks * MOBA_BLOCK - S
    k = jnp.pad(k, ((0, 0), (0, 0), (0, pad), (0, 0)))
    v = jnp.pad(v, ((0, 0), (0, 0), (0, pad), (0, 0)))
    k_mean = k.astype(jnp.float32).reshape(Bsz, N_HEADS, n_blocks, MOBA_BLOCK, HEAD_DIM).mean(axis=3)
    return k, v, k_mean


def moba_attention(q, k, v, k_mean):
    Bsz, H, S, hd = q.shape
    nb = k_mean.shape[2]
    n_sel = min(MOBA_TOPK, nb)
    kb = k.reshape(Bsz * H, nb, MOBA_BLOCK, hd)
    vb = v.reshape(Bsz * H, nb, MOBA_BLOCK, hd)
    slopes = alibi_slopes(H)
    scale = hd ** -0.5
    blk_pos = jnp.arange(MOBA_BLOCK)
    gather = jax.vmap(lambda a, ix: a[ix])

    def chunk(c):
        t0 = c * Q_CHUNK
        i = t0 // MOBA_BLOCK
        t = t0 + jnp.arange(Q_CHUNK)
        qf = lax.dynamic_slice_in_dim(q, t0, Q_CHUNK, axis=2).astype(jnp.float32)
        gate = jnp.einsum('bhqd,bhnd->bhqn', qf, k_mean)
        gate = jnp.where(jnp.arange(nb) < i, gate, -jnp.inf)
        _, sel = lax.top_k(gate, n_sel)
        sel_ok = sel < i
        sel_flat = sel.reshape(Bsz * H, Q_CHUNK * n_sel)
        kg = gather(kb, sel_flat).reshape(Bsz, H, Q_CHUNK, n_sel, MOBA_BLOCK, hd).astype(jnp.float32)
        vg = gather(vb, sel_flat).reshape(Bsz, H, Q_CHUNK, n_sel, MOBA_BLOCK, hd).astype(jnp.float32)
        s_sel = jnp.einsum('bhqd,bhqnkd->bhqnk', qf, kg) * scale
        dist_sel = (t[:, None, None] - (sel[..., None] * MOBA_BLOCK + blk_pos)).astype(jnp.float32)
        s_sel = jnp.where(sel_ok[..., None], s_sel - slopes[:, None, None, None] * dist_sel, -jnp.inf)
        k_own = lax.dynamic_slice_in_dim(k, i * MOBA_BLOCK, MOBA_BLOCK, axis=2).astype(jnp.float32)
        v_own = lax.dynamic_slice_in_dim(v, i * MOBA_BLOCK, MOBA_BLOCK, axis=2).astype(jnp.float32)
        s_own = jnp.einsum('bhqd,bhkd->bhqk', qf, k_own) * scale
        dist_own = t[:, None] - (i * MOBA_BLOCK + blk_pos)
        s_own = jnp.where(dist_own >= 0, s_own - slopes[:, None, None] * dist_own.astype(jnp.float32), -jnp.inf)
        p = jax.nn.softmax(jnp.concatenate(
            [s_sel.reshape(Bsz, H, Q_CHUNK, n_sel * MOBA_BLOCK), s_own], axis=-1), axis=-1)
        p_sel = p[..., :n_sel * MOBA_BLOCK].reshape(Bsz, H, Q_CHUNK, n_sel, MOBA_BLOCK)
        p_own = p[..., n_sel * MOBA_BLOCK:]
        o = (jnp.einsum('bhqnk,bhqnkd->bhqd', p_sel, vg)
             + jnp.einsum('bhqk,bhkd->bhqd', p_own, v_own))
        return o.astype(q.dtype)

    out = lax.map(chunk, jnp.arange(S // Q_CHUNK))
    return out.transpose(1, 0, 3, 2, 4).reshape(Bsz, S, H * hd)


def hier_moe(h, w_rg, w_re, w_gate, w_up, w_down):
    N, D = h.shape
    hf = h.astype(jnp.float32)
    g_logits = hf @ w_rg.astype(jnp.float32)
    g_prob = jax.nn.softmax(g_logits, axis=-1)
    g_sel = jnp.argmax(g_logits, axis=-1)
    g_w = jnp.take_along_axis(g_prob, g_sel[:, None], axis=-1)
    e_logits = (hf @ w_re.astype(jnp.float32)).reshape(N, N_GROUPS, EXPERTS_PER_GROUP)
    e_logits = jnp.take_along_axis(e_logits, g_sel[:, None, None], axis=1)[:, 0]
    e_top, e_idx = lax.top_k(e_logits, EXPERT_TOPK)
    e_w = jax.nn.softmax(e_top, axis=-1) * g_w
    expert = g_sel[:, None] * EXPERTS_PER_GROUP + e_idx

    M = N * EXPERT_TOPK
    flat_e = expert.reshape(M)
    flat_tok = jnp.repeat(jnp.arange(N), EXPERT_TOPK)
    flat_w = e_w.reshape(M)
    order = jnp.argsort(flat_e)
    se = flat_e[order]
    counts = jnp.bincount(flat_e, length=N_EXPERTS)
    starts = jnp.cumsum(counts) - counts
    padded = (counts + MOE_ROWS - 1) // MOE_ROWS * MOE_ROWS
    pends = jnp.cumsum(padded)
    dest = (pends - padded)[se] + (jnp.arange(M) - starts[se])
    n_blocks = -(-(M + N_EXPERTS * (MOE_ROWS - 1)) // MOE_ROWS)
    cap = n_blocks * MOE_ROWS
    row_tok = jnp.full((cap,), N, dtype=jnp.int32).at[dest].set(flat_tok[order].astype(jnp.int32))
    row_w = jnp.zeros((cap,), jnp.float32).at[dest].set(flat_w[order])
    blk_e = jnp.minimum(jnp.searchsorted(pends, jnp.arange(n_blocks) * MOE_ROWS, side='right'), N_EXPERTS - 1)
    h_pad = jnp.concatenate([h, jnp.zeros((1, D), h.dtype)], axis=0)

    def expert_block(args):
        tok, e = args
        xb = h_pad[tok]
        return (jax.nn.silu(xb @ w_gate[e]) * (xb @ w_up[e])) @ w_down[e]

    y = lax.map(expert_block, (row_tok.reshape(n_blocks, MOE_ROWS), blk_e)).reshape(cap, D)
    y = y.astype(jnp.float32) * row_w[:, None]
    return jax.ops.segment_sum(y, row_tok, num_segments=N + 1)[:N].astype(h.dtype)


def setup_inputs(seed: int = 0) -> dict:
    key = jax.random.key(seed)
    ks = jax.random.split(key, 17)
    D = D_MODEL

    def nrm(k, shape, fan_in):
        return jax.random.normal(k, shape, jnp.float32) * fan_in ** -0.5

    def gain(k, shape):
        return 1.0 + 0.02 * jax.random.normal(k, shape, jnp.float32)

    return {
        'x': jax.random.normal(ks[0], (BATCH, SEQ, D), jnp.float32),
        'mix_norm': gain(ks[1], (DEPTH, D)),
        'ffn_norm': gain(ks[2], (DEPTH, D)),
        'conv_w_in': nrm(ks[3], (N_CONV_LAYERS, D, 3 * D), D),
        'conv_w': nrm(ks[4], (N_CONV_LAYERS, CONV_WIDTH, D), CONV_WIDTH),
        'conv_w_out': nrm(ks[5], (N_CONV_LAYERS, D, D), D),
        'kv_norm': gain(ks[6], (D,)),
        'w_kv': nrm(ks[7], (D, 2 * D), D),
        'k_norm': gain(ks[8], (HEAD_DIM,)),
        'w_q': nrm(ks[9], (N_ATTN_LAYERS, D, D), D),
        'q_norm': gain(ks[10], (N_ATTN_LAYERS, HEAD_DIM)),
        'w_o': nrm(ks[11], (N_ATTN_LAYERS, D, D), D),
        'router_group': nrm(ks[12], (DEPTH, D, N_GROUPS), D),
        'router_expert': nrm(ks[13], (DEPTH, D, N_EXPERTS), D),
        'w_gate': nrm(ks[14], (DEPTH, N_EXPERTS, D, D_EXPERT), D),
        'w_up': nrm(ks[15], (DEPTH, N_EXPERTS, D, D_EXPERT), D),
        'w_down': nrm(ks[16], (DEPTH, N_EXPERTS, D_EXPERT, D), D_EXPERT),
    }


def reference(x, mix_norm, ffn_norm, conv_w_in, conv_w, conv_w_out, kv_norm, w_kv, k_norm,
              w_q, q_norm, w_o, router_group, router_expert, w_gate, w_up, w_down):
    Bsz, S, D = x.shape
    h = x
    k_sh = v_sh = k_mean = None
    for l in range(DEPTH):
        if l < N_CONV_LAYERS:
            h = h + short_conv_mixer(rms_norm(h, mix_norm[l]), conv_w_in[l], conv_w[l], conv_w_out[l])
        else:
            j = l - N_CONV_LAYERS
            if j == 0:
                k_sh, v_sh, k_mean = shared_kv(h, kv_norm, w_kv, k_norm)
            q = rms_norm((rms_norm(h, mix_norm[l]) @ w_q[j]).reshape(Bsz, S, N_HEADS, HEAD_DIM), q_norm[j])
            attn = moba_attention(q.transpose(0, 2, 1, 3), k_sh, v_sh, k_mean)
            h = h + attn @ w_o[j]
        ffn = hier_moe(rms_norm(h, ffn_norm[l]).reshape(Bsz * S, D), router_group[l], router_expert[l],
                       w_gate[l], w_up[l], w_down[l])
        h = h + ffn.reshape(Bsz, S, D)
    return h
```

```python
import functools

import jax
import jax.numpy as jnp
from jax import lax
from jax.experimental import pallas as pl
from jax.experimental.pallas import tpu as pltpu

N_HEADS = 16
HEAD_DIM = 128
CONV_WIDTH = 3
MOBA_BLOCK = 256
MOBA_TOPK = 3
N_GROUPS = 4
EXPERTS_PER_GROUP = 8
N_EXPERTS = N_GROUPS * EXPERTS_PER_GROUP
EXPERT_TOPK = 2
NORM_EPS = 1e-6
ALIBI_MAX_BIAS = 8.0

LANES = 128
SUBLANES = 8
VMEM_LIMIT = 56 * 1024 * 1024

ROW_TILE = 1024
COL_TILE = 512
CONV_COL_TILE = 256
ROUTER_ROWS = 512
EXPERT_ROWS = 128
COMBINE_ROWS = 256

_BF16 = jnp.bfloat16
_F32 = jnp.float32


def _params(*semantics):
    return pltpu.CompilerParams(dimension_semantics=semantics, vmem_limit_bytes=VMEM_LIMIT)


def _rms_normalise(x, gain):
    ms = jnp.mean(x * x, axis=-1, keepdims=True)
    return x * lax.rsqrt(ms + NORM_EPS) * gain


def _bf16_dot(a, b):
    return jnp.dot(a, b, preferred_element_type=_F32)


def _norm_matmul_body(x_ref, g_ref, w_ref, *rest, head_norm):
    if head_norm:
        hg_ref, o_ref, xn_ref = rest
    else:
        o_ref, xn_ref = rest

    @pl.when(pl.program_id(1) == 0)
    def _():
        xn_ref[...] = _rms_normalise(x_ref[...], g_ref[...]).astype(_BF16)

    y = _bf16_dot(xn_ref[...], w_ref[...].astype(_BF16))
    if head_norm:
        for hh in range(y.shape[1] // HEAD_DIM):
            cols = slice(hh * HEAD_DIM, (hh + 1) * HEAD_DIM)
            o_ref[:, cols] = _rms_normalise(y[:, cols], hg_ref[...]).astype(o_ref.dtype)
    else:
        o_ref[...] = y.astype(o_ref.dtype)


def _norm_matmul(x, gain, w, col_off, n_out, head_gain=None):
    n, d = x.shape
    tm, tn = ROW_TILE, COL_TILE
    off = col_off // tn
    in_specs = [
        pl.BlockSpec((tm, d), lambda i, j: (i, 0)),
        pl.BlockSpec((1, d), lambda i, j: (0, 0)),
        pl.BlockSpec((d, tn), lambda i, j: (0, j + off)),
    ]
    args = [x, gain.reshape(1, d), w]
    if head_gain is not None:
        in_specs.append(pl.BlockSpec((1, HEAD_DIM), lambda i, j: (0, 0)))
        args.append(head_gain.reshape(1, HEAD_DIM))
    return pl.pallas_call(
        functools.partial(_norm_matmul_body, head_norm=head_gain is not None),
        grid=(n // tm, n_out // tn),
        in_specs=in_specs,
        out_specs=pl.BlockSpec((tm, tn), lambda i, j: (i, j)),
        out_shape=jax.ShapeDtypeStruct((n, n_out), _BF16),
        scratch_shapes=[pltpu.VMEM((tm, d), _BF16)],
        compiler_params=_params("arbitrary", "arbitrary"),
        name="norm_matmul",
    )(*args)


def _conv_in_body(x_ref, g_ref, wb_ref, wc_ref, wx_ref, cw_ref, z_ref, xn_ref, carry_ref,
                  *, tiles_per_seq):
    i = pl.program_id(0)
    j = pl.program_id(1)
    tm = x_ref.shape[0]

    @pl.when(j == 0)
    def _():
        xn_ref[...] = _rms_normalise(x_ref[...], g_ref[...]).astype(_BF16)

    @pl.when(i % tiles_per_seq == 0)
    def _():
        carry_ref[j] = jnp.zeros(carry_ref.shape[1:], _F32)

    xn = xn_ref[...]
    b_gate = _bf16_dot(xn, wb_ref[...].astype(_BF16))
    c_gate = _bf16_dot(xn, wc_ref[...].astype(_BF16))
    xh = _bf16_dot(xn, wx_ref[...].astype(_BF16))
    u = c_gate * xh
    prev = carry_ref[j]
    row = lax.broadcasted_iota(jnp.int32, u.shape, 0)
    last = prev[SUBLANES - 1:SUBLANES]
    u1 = jnp.where(row == 0, last, pltpu.roll(u, 1, axis=0))
    u2 = jnp.where(row == 0, prev[SUBLANES - 2:SUBLANES - 1],
                   jnp.where(row == 1, last, pltpu.roll(u, 2, axis=0)))
    cw = cw_ref[...]
    conv = cw[0:1] * u2 + cw[1:2] * u1 + cw[2:3] * u
    z_ref[...] = (b_gate * conv).astype(z_ref.dtype)
    carry_ref[j] = u[tm - SUBLANES:tm]


def _conv_in(x, gain, w_in, conv_w, seq_len):
    n, d = x.shape
    tm, tn = ROW_TILE, CONV_COL_TILE
    nj = d // tn
    assert CONV_WIDTH - 1 <= SUBLANES and seq_len % tm == 0
    return pl.pallas_call(
        functools.partial(_conv_in_body, tiles_per_seq=seq_len // tm),
        grid=(n // tm, nj),
        in_specs=[
            pl.BlockSpec((tm, d), lambda i, j: (i, 0)),
            pl.BlockSpec((1, d), lambda i, j: (0, 0)),
            pl.BlockSpec((d, tn), lambda i, j: (0, j)),
            pl.BlockSpec((d, tn), lambda i, j: (0, j + nj)),
            pl.BlockSpec((d, tn), lambda i, j: (0, j + 2 * nj)),
            pl.BlockSpec((CONV_WIDTH, tn), lambda i, j: (0, j)),
        ],
        out_specs=pl.BlockSpec((tm, tn), lambda i, j: (i, j)),
        out_shape=jax.ShapeDtypeStruct((n, d), _BF16),
        scratch_shapes=[pltpu.VMEM((tm, d), _BF16), pltpu.VMEM((nj, SUBLANES, tn), _F32)],
        compiler_params=_params("arbitrary", "arbitrary"),
        name="conv_in",
    )(x, gain.reshape(1, d), w_in, w_in, w_in, conv_w)


def _matmul_residual_body(x_ref, w_ref, r_ref, o_ref):
    o_ref[...] = r_ref[...] + _bf16_dot(x_ref[...], w_ref[...].astype(_BF16))


def _matmul_residual(x, w, res):
    n, k = x.shape
    d = w.shape[1]
    tm, tn = ROW_TILE, COL_TILE
    return pl.pallas_call(
        _matmul_residual_body,
        grid=(n // tm, d // tn),
        in_specs=[
            pl.BlockSpec((tm, k), lambda i, j: (i, 0)),
            pl.BlockSpec((k, tn), lambda i, j: (0, j)),
            pl.BlockSpec((tm, tn), lambda i, j: (i, j)),
        ],
        out_specs=pl.BlockSpec((tm, tn), lambda i, j: (i, j)),
        out_shape=jax.ShapeDtypeStruct((n, d), _F32),
        compiler_params=_params("arbitrary", "arbitrary"),
        name="matmul_residual",
    )(x, w, res)


def _moba_body(q_ref, k_ref, v_ref, o_ref, km_ref):
    head = pl.program_id(1)
    i = pl.program_id(2)
    kb = MOBA_BLOCK
    n_blocks = k_ref.shape[0] // kb
    nt = (((1,), (1,)), ((), ()))

    @pl.when(i == 0)
    def _():
        km_ref[...] = jnp.zeros(km_ref.shape, _F32)
        for jb in range(n_blocks):
            km_ref[jb:jb + 1, :] = jnp.mean(k_ref[jb * kb:(jb + 1) * kb, :].astype(_F32),
                                            axis=0, keepdims=True)

    q = q_ref[...]
    gate = lax.dot_general(q.astype(_F32), km_ref[...], nt,
                           precision=lax.Precision.HIGHEST, preferred_element_type=_F32)
    lane = lax.broadcasted_iota(jnp.int32, gate.shape, 1)
    gate = jnp.where(lane < i, gate, -jnp.inf)
    sel = jnp.zeros(gate.shape, _F32)
    for jb in range(n_blocks - 1):
        g_jb = jnp.max(jnp.where(lane == jb, gate, -jnp.inf), axis=1, keepdims=True)
        beats = (gate > g_jb) | ((gate == g_jb) & (lane < jb))
        n_beat = jnp.sum(beats.astype(_F32), axis=1, keepdims=True)
        sel = jnp.where((lane == jb) & (n_beat < MOBA_TOPK), 1.0, sel)

    scale = HEAD_DIM ** -0.5
    slope = jnp.exp2(jnp.full((1, kb), -ALIBI_MAX_BIAS / N_HEADS, _F32) * (head + 1).astype(_F32))
    rel = (lax.broadcasted_iota(jnp.int32, (kb, kb), 0)
           - lax.broadcasted_iota(jnp.int32, (kb, kb), 1))

    own = pl.multiple_of(i * kb, kb)
    s = lax.dot_general(q, k_ref[pl.ds(own, kb), :], nt, preferred_element_type=_F32)
    s = jnp.where(rel >= 0, s * scale - slope * rel.astype(_F32), -jnp.inf)
    m0 = jnp.max(s, axis=1, keepdims=True)
    p = jnp.exp(s - m0)
    l0 = jnp.sum(p, axis=1, keepdims=True)
    acc0 = _bf16_dot(p.astype(_BF16), v_ref[pl.ds(own, kb), :])

    def past_block(jb, carry):
        m, l, acc = carry
        start = pl.multiple_of(jb * kb, kb)
        picked = jnp.max(jnp.where(lane == jb, sel, 0.0), axis=1, keepdims=True) > 0.5
        s = lax.dot_general(q, k_ref[pl.ds(start, kb), :], nt, preferred_element_type=_F32)
        dist = (rel + (i - jb) * kb).astype(_F32)
        s = jnp.where(picked, s * scale - slope * dist, -jnp.inf)
        m_new = jnp.maximum(m, jnp.max(s, axis=1, keepdims=True))
        p = jnp.exp(s - m_new)
        alpha = jnp.exp(m - m_new)
        l = alpha * l + jnp.sum(p, axis=1, keepdims=True)
        acc = alpha * acc + _bf16_dot(p.astype(_BF16), v_ref[pl.ds(start, kb), :])
        return m_new, l, acc

    _, l, acc = lax.fori_loop(0, i, past_block, (m0, l0, acc0))
    o_ref[...] = (acc / l).astype(o_ref.dtype)


def _moba_attention(q, k, v, batch, seq_len):
    n, d = q.shape
    kb = MOBA_BLOCK
    nqb = seq_len // kb
    assert seq_len % kb == 0 and nqb <= LANES
    return pl.pallas_call(
        _moba_body,
        grid=(batch, N_HEADS, nqb),
        in_specs=[
            pl.BlockSpec((kb, HEAD_DIM), lambda b, h, i: (b * nqb + i, h)),
            pl.BlockSpec((seq_len, HEAD_DIM), lambda b, h, i: (b, h)),
            pl.BlockSpec((seq_len, HEAD_DIM), lambda b, h, i: (b, h)),
        ],
        out_specs=pl.BlockSpec((kb, HEAD_DIM), lambda b, h, i: (b * nqb + i, h)),
        out_shape=jax.ShapeDtypeStruct((n, d), _BF16),
        scratch_shapes=[pltpu.VMEM((LANES, HEAD_DIM), _F32)],
        compiler_params=_params("arbitrary", "arbitrary", "arbitrary"),
        name="moba_attention",
    )(q, k, v)


def _router_body(x_ref, g_ref, wr_ref, hn_ref, ids_ref, wts_ref):
    hn = _rms_normalise(x_ref[...], g_ref[...])
    hn_ref[...] = hn
    logits = jnp.dot(hn, wr_ref[...], precision=lax.Precision.HIGHEST, preferred_element_type=_F32)
    lane = lax.broadcasted_iota(jnp.int32, logits.shape, 1)
    far = jnp.int32(LANES)

    def first_lane(mask):
        return jnp.min(jnp.where(mask, lane, far), axis=1, keepdims=True)

    is_group = (lane >= N_EXPERTS) & (lane < N_EXPERTS + N_GROUPS)
    g_max = jnp.max(jnp.where(is_group, logits, -jnp.inf), axis=1, keepdims=True)
    g_sel = first_lane(is_group & (logits == g_max)) - N_EXPERTS
    g_w = 1.0 / jnp.sum(jnp.where(is_group, jnp.exp(logits - g_max), 0.0), axis=1, keepdims=True)

    in_group = (lane >= g_sel * EXPERTS_PER_GROUP) & (lane < (g_sel + 1) * EXPERTS_PER_GROUP)
    e_logits = jnp.where(in_group, logits, -jnp.inf)
    top1 = jnp.max(e_logits, axis=1, keepdims=True)
    idx1 = first_lane(in_group & (logits == top1))
    rest = in_group & (lane != idx1)
    top2 = jnp.max(jnp.where(rest, logits, -jnp.inf), axis=1, keepdims=True)
    idx2 = first_lane(rest & (logits == top2))
    t = jnp.exp(top2 - top1)
    w1 = 1.0 / (1.0 + t) * g_w
    w2 = t / (1.0 + t) * g_w
    ids_ref[...] = jnp.where(lane == 0, idx1, jnp.where(lane == 1, idx2, 0))
    wts_ref[...] = jnp.where(lane == 0, w1, jnp.where(lane == 1, w2, 0.0))


def _router(x, gain, w_rg, w_re):
    n, d = x.shape
    tm = ROUTER_ROWS
    assert EXPERT_TOPK == 2 and N_EXPERTS + N_GROUPS <= LANES
    wr = jnp.concatenate(
        [w_re, w_rg, jnp.zeros((d, LANES - N_EXPERTS - N_GROUPS), _F32)], axis=1)
    row = lambda i: (i, 0)
    return pl.pallas_call(
        _router_body,
        grid=(n // tm,),
        in_specs=[
            pl.BlockSpec((tm, d), row),
            pl.BlockSpec((1, d), lambda i: (0, 0)),
            pl.BlockSpec((d, LANES), lambda i: (0, 0)),
        ],
        out_specs=[pl.BlockSpec((tm, d), row), pl.BlockSpec((tm, LANES), row),
                   pl.BlockSpec((tm, LANES), row)],
        out_shape=[jax.ShapeDtypeStruct((n, d), _F32),
                   jax.ShapeDtypeStruct((n, LANES), jnp.int32),
                   jax.ShapeDtypeStruct((n, LANES), _F32)],
        compiler_params=_params("arbitrary"),
        name="router",
    )(x, gain.reshape(1, d), wr)


def _expert_body(blk_e_ref, row_tok_ref, hn_hbm, wg_ref, wu_ref, wd_ref, y_ref,
                 xs_ref, wg_bf, wu_bf, wd_bf, sem):
    g = pl.program_id(0)
    n_steps = pl.num_programs(0)
    rows = xs_ref.shape[1]
    slot = g % 2

    def row_copy(tok, s, r):
        return pltpu.make_async_copy(hn_hbm.at[pl.ds(tok, 1)], xs_ref.at[s, pl.ds(r, 1)], sem.at[s])

    def start_gather(block, s):
        def body(r, c):
            row_copy(row_tok_ref[block * rows + r], s, r).start()
            return c
        lax.fori_loop(0, rows, body, 0)

    def wait_gather(s):
        def body(r, c):
            row_copy(0, s, r).wait()
            return c
        lax.fori_loop(0, rows, body, 0)

    @pl.when(g == 0)
    def _():
        start_gather(0, 0)

    @pl.when(g + 1 < n_steps)
    def _():
        start_gather(g + 1, 1 - slot)

    wait_gather(slot)

    e = blk_e_ref[g]
    @pl.when((g == 0) | (e != blk_e_ref[jnp.maximum(g - 1, 0)]))
    def _():
        wg_bf[...] = wg_ref[0, 0].astype(_BF16)
        wu_bf[...] = wu_ref[0, 0].astype(_BF16)
        wd_bf[...] = wd_ref[0, 0].astype(_BF16)

    xb = xs_ref[slot].astype(_BF16)
    act = jax.nn.silu(_bf16_dot(xb, wg_bf[...])) * _bf16_dot(xb, wu_bf[...])
    y_ref[...] = _bf16_dot(act.astype(_BF16), wd_bf[...])


def _expert_mlp(blk_e, row_tok, hn, layer, w_gate, w_up, w_down):
    d = hn.shape[1]
    de = w_gate.shape[3]
    rows = EXPERT_ROWS
    n_blocks = blk_e.shape[0]
    grid_spec = pltpu.PrefetchScalarGridSpec(
        num_scalar_prefetch=2,
        grid=(n_blocks,),
        in_specs=[
            pl.BlockSpec(memory_space=pl.ANY),
            pl.BlockSpec((1, 1, d, de), lambda g, be, rt: (layer, be[g], 0, 0)),
            pl.BlockSpec((1, 1, d, de), lambda g, be, rt: (layer, be[g], 0, 0)),
            pl.BlockSpec((1, 1, de, d), lambda g, be, rt: (layer, be[g], 0, 0)),
        ],
        out_specs=pl.BlockSpec((rows, d), lambda g, be, rt: (g, 0)),
        scratch_shapes=[
            pltpu.VMEM((2, rows, d), _F32),
            pltpu.VMEM((d, de), _BF16),
            pltpu.VMEM((d, de), _BF16),
            pltpu.VMEM((de, d), _BF16),
            pltpu.SemaphoreType.DMA((2,)),
        ],
    )
    return pl.pallas_call(
        _expert_body,
        grid_spec=grid_spec,
        out_shape=jax.ShapeDtypeStruct((n_blocks * rows, d), _F32),
        compiler_params=_params("arbitrary"),
        name="expert_mlp",
    )(blk_e, row_tok, hn, w_gate, w_up, w_down)


def _combine_body(pos_ref, y_hbm, wts_ref, h_ref, o_ref, ys_ref, sem):
    i = pl.program_id(0)
    n_steps = pl.num_programs(0)
    tm = h_ref.shape[0]
    slot = i % 2

    def row_copy(p, s, k, r):
        return pltpu.make_async_copy(y_hbm.at[pl.ds(p, 1)], ys_ref.at[s, k, pl.ds(r, 1)], sem.at[s])

    def start_gather(tile, s):
        def body(r, c):
            base = (tile * tm + r) * EXPERT_TOPK
            for k in range(EXPERT_TOPK):
                row_copy(pos_ref[base + k], s, k, r).start()
            return c
        lax.fori_loop(0, tm, body, 0)

    def wait_gather(s):
        def body(r, c):
            for k in range(EXPERT_TOPK):
                row_copy(0, s, k, r).wait()
            return c
        lax.fori_loop(0, tm, body, 0)

    @pl.when(i == 0)
    def _():
        start_gather(0, 0)

    @pl.when(i + 1 < n_steps)
    def _():
        start_gather(i + 1, 1 - slot)

    wait_gather(slot)
    w = wts_ref[...]
    o_ref[...] = h_ref[...] + (w[:, 0:1] * ys_ref[slot, 0] + w[:, 1:2] * ys_ref[slot, 1])


def _combine(pos, y, wts, h):
    n, d = h.shape
    tm = COMBINE_ROWS
    grid_spec = pltpu.PrefetchScalarGridSpec(
        num_scalar_prefetch=1,
        grid=(n // tm,),
        in_specs=[
            pl.BlockSpec(memory_space=pl.ANY),
            pl.BlockSpec((tm, LANES), lambda i, p: (i, 0)),
            pl.BlockSpec((tm, d), lambda i, p: (i, 0)),
        ],
        out_specs=pl.BlockSpec((tm, d), lambda i, p: (i, 0)),
        scratch_shapes=[
            pltpu.VMEM((2, EXPERT_TOPK, tm, d), _F32),
            pltpu.SemaphoreType.DMA((2,)),
        ],
    )
    return pl.pallas_call(
        _combine_body,
        grid_spec=grid_spec,
        out_shape=jax.ShapeDtypeStruct((n, d), _F32),
        compiler_params=_params("arbitrary"),
        name="combine",
    )(pos, y, wts, h)


def _dispatch_plan(ids):
    n = ids.shape[0]
    rows = EXPERT_ROWS
    m = n * EXPERT_TOPK
    flat_e = ids[:, :EXPERT_TOPK].reshape(m)
    order = jnp.argsort(flat_e)
    se = flat_e[order]
    counts = jnp.bincount(flat_e, length=N_EXPERTS)
    starts = jnp.cumsum(counts) - counts
    padded = (counts + rows - 1) // rows * rows
    pends = jnp.cumsum(padded)
    dest = ((pends - padded)[se] + (jnp.arange(m) - starts[se])).astype(jnp.int32)
    n_blocks = -(-(m + N_EXPERTS * (rows - 1)) // rows)
    row_tok = jnp.zeros((n_blocks * rows,), jnp.int32).at[dest].set((order // EXPERT_TOPK).astype(jnp.int32))
    pos = jnp.zeros((m,), jnp.int32).at[order].set(dest)
    blk_e = jnp.minimum(jnp.searchsorted(pends, jnp.arange(n_blocks) * rows, side='right'),
                        N_EXPERTS - 1).astype(jnp.int32)
    return blk_e, row_tok, pos


def _hier_moe_residual(h, gain, w_rg, w_re, layer, w_gate, w_up, w_down):
    hn, ids, wts = _router(h, gain, w_rg, w_re)
    blk_e, row_tok, pos = _dispatch_plan(ids)
    y = _expert_mlp(blk_e, row_tok, hn, layer, w_gate, w_up, w_down)
    return _combine(pos, y, wts, h)


def kernel(x, mix_norm, ffn_norm, conv_w_in, conv_w, conv_w_out, kv_norm, w_kv, k_norm, w_q, q_norm,
           w_o, router_group, router_expert, w_gate, w_up, w_down):
    bsz, seq_len, d = x.shape
    depth = mix_norm.shape[0]
    n_conv = conv_w_in.shape[0]
    assert d == N_HEADS * HEAD_DIM
    h = x.reshape(bsz * seq_len, d)
    k_sh = v_sh = None
    for l in range(depth):
        if l < n_conv:
            z = _conv_in(h, mix_norm[l], conv_w_in[l], conv_w[l], seq_len)
            h = _matmul_residual(z, conv_w_out[l], h)
        else:
            j = l - n_conv
            if j == 0:
                k_sh = _norm_matmul(h, kv_norm, w_kv, 0, d, head_gain=k_norm)
                v_sh = _norm_matmul(h, kv_norm, w_kv, d, d)
            q = _norm_matmul(h, mix_norm[l], w_q[j], 0, d, head_gain=q_norm[j])
            attn = _moba_attention(q, k_sh, v_sh, bsz, seq_len)
            h = _matmul_residual(attn, w_o[j], h)
        h = _hier_moe_residual(h, ffn_norm[l], router_group[l], router_expert[l],
                               l, w_gate, w_up, w_down)
    return h.reshape(bsz, seq_len, d)
```

```python
import functools

import jax
import jax.numpy as jnp
from jax import lax
from jax.experimental import pallas as pl
from jax.experimental.pallas import tpu as pltpu

N_HEADS = 16
HEAD_DIM = 128
CONV_WIDTH = 3
MOBA_BLOCK = 256
MOBA_TOPK = 3
N_GROUPS = 4
EXPERTS_PER_GROUP = 8
N_EXPERTS = N_GROUPS * EXPERTS_PER_GROUP
EXPERT_TOPK = 2
NORM_EPS = 1e-6
ALIBI_MAX_BIAS = 8.0

LANES = 128
SUBLANES = 8
VMEM_LIMIT = 56 * 1024 * 1024

ROW_TILE = 1024
COL_TILE = 512
CONV_COL_TILE = 256
ROUTER_ROWS = 512
EXPERT_ROWS = 128
COMBINE_ROWS = 256
GATHER_UNROLL = 8

_BF16 = jnp.bfloat16
_F32 = jnp.float32


def _params(*semantics):
    return pltpu.CompilerParams(dimension_semantics=semantics, vmem_limit_bytes=VMEM_LIMIT)


def _rms_normalise(x, gain):
    ms = jnp.mean(x * x, axis=-1, keepdims=True)
    return x * lax.rsqrt(ms + NORM_EPS) * gain


def _bf16_dot(a, b):
    return jnp.dot(a, b, preferred_element_type=_F32)


def _norm_matmul_body(x_ref, g_ref, w_ref, *rest, head_norm):
    if head_norm:
        hg_ref, o_ref, xn_ref = rest
    else:
        o_ref, xn_ref = rest

    @pl.when(pl.program_id(1) == 0)
    def _():
        xn_ref[...] = _rms_normalise(x_ref[...], g_ref[...]).astype(_BF16)

    y = _bf16_dot(xn_ref[...], w_ref[...].astype(_BF16))
    if head_norm:
        for hh in range(y.shape[1] // HEAD_DIM):
            cols = slice(hh * HEAD_DIM, (hh + 1) * HEAD_DIM)
            o_ref[:, cols] = _rms_normalise(y[:, cols], hg_ref[...]).astype(o_ref.dtype)
    else:
        o_ref[...] = y.astype(o_ref.dtype)


def _norm_matmul(x, gain, w, col_off, n_out, head_gain=None):
    n, d = x.shape
    tm, tn = ROW_TILE, COL_TILE
    off = col_off // tn
    in_specs = [
        pl.BlockSpec((tm, d), lambda i, j: (i, 0)),
        pl.BlockSpec((1, d), lambda i, j: (0, 0)),
        pl.BlockSpec((d, tn), lambda i, j: (0, j + off)),
    ]
    args = [x, gain.reshape(1, d), w]
    if head_gain is not None:
        in_specs.append(pl.BlockSpec((1, HEAD_DIM), lambda i, j: (0, 0)))
        args.append(head_gain.reshape(1, HEAD_DIM))
    return pl.pallas_call(
        functools.partial(_norm_matmul_body, head_norm=head_gain is not None),
        grid=(n // tm, n_out // tn),
        in_specs=in_specs,
        out_specs=pl.BlockSpec((tm, tn), lambda i, j: (i, j)),
        out_shape=jax.ShapeDtypeStruct((n, n_out), _BF16),
        scratch_shapes=[pltpu.VMEM((tm, d), _BF16)],
        compiler_params=_params("arbitrary", "arbitrary"),
        name="norm_matmul",
    )(*args)


def _conv_in_body(x_ref, g_ref, wb_ref, wc_ref, wx_ref, cw_ref, z_ref, xn_ref, carry_ref,
                  *, tiles_per_seq):
    i = pl.program_id(0)
    j = pl.program_id(1)
    tm = x_ref.shape[0]

    @pl.when(j == 0)
    def _():
        xn_ref[...] = _rms_normalise(x_ref[...], g_ref[...]).astype(_BF16)

    @pl.when(i % tiles_per_seq == 0)
    def _():
        carry_ref[j] = jnp.zeros(carry_ref.shape[1:], _F32)

    xn = xn_ref[...]
    b_gate = _bf16_dot(xn, wb_ref[...].astype(_BF16))
    c_gate = _bf16_dot(xn, wc_ref[...].astype(_BF16))
    xh = _bf16_dot(xn, wx_ref[...].astype(_BF16))
    u = c_gate * xh
    prev = carry_ref[j]
    row = lax.broadcasted_iota(jnp.int32, u.shape, 0)
    last = prev[SUBLANES - 1:SUBLANES]
    u1 = jnp.where(row == 0, last, pltpu.roll(u, 1, axis=0))
    u2 = jnp.where(row == 0, prev[SUBLANES - 2:SUBLANES - 1],
                   jnp.where(row == 1, last, pltpu.roll(u, 2, axis=0)))
    cw = cw_ref[...]
    conv = cw[0:1] * u2 + cw[1:2] * u1 + cw[2:3] * u
    z_ref[...] = (b_gate * conv).astype(z_ref.dtype)
    carry_ref[j] = u[tm - SUBLANES:tm]


def _conv_in(x, gain, w_in, conv_w, seq_len):
    n, d = x.shape
    tm, tn = ROW_TILE, CONV_COL_TILE
    nj = d // tn
    assert CONV_WIDTH - 1 <= SUBLANES and seq_len % tm == 0
    return pl.pallas_call(
        functools.partial(_conv_in_body, tiles_per_seq=seq_len // tm),
        grid=(n // tm, nj),
        in_specs=[
            pl.BlockSpec((tm, d), lambda i, j: (i, 0)),
            pl.BlockSpec((1, d), lambda i, j: (0, 0)),
            pl.BlockSpec((d, tn), lambda i, j: (0, j)),
            pl.BlockSpec((d, tn), lambda i, j: (0, j + nj)),
            pl.BlockSpec((d, tn), lambda i, j: (0, j + 2 * nj)),
            pl.BlockSpec((CONV_WIDTH, tn), lambda i, j: (0, j)),
        ],
        out_specs=pl.BlockSpec((tm, tn), lambda i, j: (i, j)),
        out_shape=jax.ShapeDtypeStruct((n, d), _BF16),
        scratch_shapes=[pltpu.VMEM((tm, d), _BF16), pltpu.VMEM((nj, SUBLANES, tn), _F32)],
        compiler_params=_params("arbitrary", "arbitrary"),
        name="conv_in",
    )(x, gain.reshape(1, d), w_in, w_in, w_in, conv_w)


def _matmul_residual_body(x_ref, w_ref, r_ref, o_ref):
    o_ref[...] = r_ref[...] + _bf16_dot(x_ref[...], w_ref[...].astype(_BF16))


def _matmul_residual(x, w, res):
    n, k = x.shape
    d = w.shape[1]
    tm, tn = ROW_TILE, COL_TILE
    return pl.pallas_call(
        _matmul_residual_body,
        grid=(n // tm, d // tn),
        in_specs=[
            pl.BlockSpec((tm, k), lambda i, j: (i, 0)),
            pl.BlockSpec((k, tn), lambda i, j: (0, j)),
            pl.BlockSpec((tm, tn), lambda i, j: (i, j)),
        ],
        out_specs=pl.BlockSpec((tm, tn), lambda i, j: (i, j)),
        out_shape=jax.ShapeDtypeStruct((n, d), _F32),
        compiler_params=_params("arbitrary", "arbitrary"),
        name="matmul_residual",
    )(x, w, res)


def _moba_body(q_ref, k_ref, v_ref, o_ref, km_ref, vt_ref):
    head = pl.program_id(1)
    i = pl.program_id(2)
    kb = MOBA_BLOCK
    n_blocks = k_ref.shape[0] // kb
    nt = (((1,), (1,)), ((), ()))

    @pl.when(i == 0)
    def _():
        for jb in range(n_blocks):
            rows = slice(jb * kb, (jb + 1) * kb)
            km_ref[jb:jb + 1, :] = jnp.mean(k_ref[rows, :].astype(_F32), axis=0, keepdims=True)
            vt_ref[jb] = v_ref[rows, :].astype(_F32).T.astype(_BF16)

    q = q_ref[...]
    gate = lax.dot_general(km_ref[...], q.astype(_F32), nt,
                           precision=lax.Precision.HIGHEST, preferred_element_type=_F32)
    blk = lax.broadcasted_iota(jnp.int32, gate.shape, 0)
    gate = jnp.where(blk < i, gate, -jnp.inf)
    sel = jnp.zeros(gate.shape, _F32)
    for jb in range(n_blocks - 1):
        g_jb = gate[jb:jb + 1, :]
        beats = (gate > g_jb) | ((gate == g_jb) & (blk < jb))
        n_beat = jnp.sum(beats.astype(_F32), axis=0, keepdims=True)
        sel = jnp.where((blk == jb) & (blk < i) & (n_beat < MOBA_TOPK), 1.0, sel)

    scale = HEAD_DIM ** -0.5
    slope = jnp.exp2(jnp.full((1, kb), -ALIBI_MAX_BIAS / N_HEADS, _F32) * (head + 1).astype(_F32))
    rel = (lax.broadcasted_iota(jnp.int32, (kb, kb), 1)
           - lax.broadcasted_iota(jnp.int32, (kb, kb), 0))

    def attend(own):
        scores = []
        for jb in range(own + 1):
            s = lax.dot_general(k_ref[jb * kb:(jb + 1) * kb, :], q, nt, preferred_element_type=_F32)
            s = s * scale - slope * (rel + (own - jb) * kb).astype(_F32)
            visible = (rel >= 0) if jb == own else (sel[jb:jb + 1, :] > 0.5)
            scores.append(jnp.where(visible, s, -jnp.inf))
        m = functools.reduce(jnp.maximum, [jnp.max(s, axis=0, keepdims=True) for s in scores])
        probs = [jnp.exp(s - m) for s in scores]
        l = functools.reduce(jnp.add, [jnp.sum(p, axis=0, keepdims=True) for p in probs])
        acc = functools.reduce(jnp.add, [_bf16_dot(vt_ref[jb], p.astype(_BF16))
                                         for jb, p in enumerate(probs)])
        o_ref[...] = (acc / l).T.astype(o_ref.dtype)

    for own in range(n_blocks):
        pl.when(i == own)(functools.partial(attend, own))


def _moba_attention(q, k, v, batch, seq_len):
    n, d = q.shape
    kb = MOBA_BLOCK
    nqb = seq_len // kb
    assert seq_len % kb == 0 and nqb == SUBLANES
    return pl.pallas_call(
        _moba_body,
        grid=(batch, N_HEADS, nqb),
        in_specs=[
            pl.BlockSpec((kb, HEAD_DIM), lambda b, h, i: (b * nqb + i, h)),
            pl.BlockSpec((seq_len, HEAD_DIM), lambda b, h, i: (b, h)),
            pl.BlockSpec((seq_len, HEAD_DIM), lambda b, h, i: (b, h)),
        ],
        out_specs=pl.BlockSpec((kb, HEAD_DIM), lambda b, h, i: (b * nqb + i, h)),
        out_shape=jax.ShapeDtypeStruct((n, d), _BF16),
        scratch_shapes=[pltpu.VMEM((nqb, HEAD_DIM), _F32),
                        pltpu.VMEM((nqb, HEAD_DIM, kb), _BF16)],
        compiler_params=_params("arbitrary", "arbitrary", "arbitrary"),
        name="moba_attention",
    )(q, k, v)


def _router_body(x_ref, g_ref, wr_ref, hn_ref, ids_ref, wts_ref, cnt_ref, run_ref):
    step = pl.program_id(0)
    hn = _rms_normalise(x_ref[...], g_ref[...])
    hn_ref[...] = hn
    logits = jnp.dot(hn, wr_ref[...], precision=lax.Precision.HIGHEST, preferred_element_type=_F32)
    lane = lax.broadcasted_iota(jnp.int32, logits.shape, 1)
    far = jnp.int32(LANES)

    def first_lane(mask):
        return jnp.min(jnp.where(mask, lane, far), axis=1, keepdims=True)

    is_group = (lane >= N_EXPERTS) & (lane < N_EXPERTS + N_GROUPS)
    g_max = jnp.max(jnp.where(is_group, logits, -jnp.inf), axis=1, keepdims=True)
    g_sel = first_lane(is_group & (logits == g_max)) - N_EXPERTS
    g_w = 1.0 / jnp.sum(jnp.where(is_group, jnp.exp(logits - g_max), 0.0), axis=1, keepdims=True)

    in_group = (lane >= g_sel * EXPERTS_PER_GROUP) & (lane < (g_sel + 1) * EXPERTS_PER_GROUP)
    top1 = jnp.max(jnp.where(in_group, logits, -jnp.inf), axis=1, keepdims=True)
    idx1 = first_lane(in_group & (logits == top1))
    rest = in_group & (lane != idx1)
    top2 = jnp.max(jnp.where(rest, logits, -jnp.inf), axis=1, keepdims=True)
    idx2 = first_lane(rest & (logits == top2))
    t = jnp.exp(top2 - top1)
    w1 = 1.0 / (1.0 + t) * g_w
    w2 = t / (1.0 + t) * g_w

    @pl.when(step == 0)
    def _():
        run_ref[...] = jnp.zeros(run_ref.shape, _F32)

    tm = logits.shape[0]
    chosen = (lane == idx1) | (lane == idx2)
    earlier = (lax.broadcasted_iota(jnp.int32, (tm, tm), 1)
               < lax.broadcasted_iota(jnp.int32, (tm, tm), 0))
    before = run_ref[...] + _bf16_dot(jnp.where(earlier, 1.0, 0.0).astype(_BF16),
                                      jnp.where(chosen, 1.0, 0.0).astype(_BF16))
    rank1 = jnp.sum(jnp.where(lane == idx1, before, 0.0), axis=1, keepdims=True).astype(jnp.int32)
    rank2 = jnp.sum(jnp.where(lane == idx2, before, 0.0), axis=1, keepdims=True).astype(jnp.int32)
    run_ref[...] += jnp.sum(jnp.where(chosen, 1.0, 0.0), axis=0, keepdims=True)

    ids_ref[...] = jnp.where(lane == 0, idx1, jnp.where(lane == 1, idx2,
                             jnp.where(lane == 2, rank1, jnp.where(lane == 3, rank2, 0))))
    wts_ref[...] = jnp.where(lane == 0, w1, jnp.where(lane == 1, w2, 0.0))
    cnt_ref[...] = jnp.broadcast_to(run_ref[...], cnt_ref.shape).astype(jnp.int32)


def _router(x, gain, w_rg, w_re):
    n, d = x.shape
    tm = ROUTER_ROWS
    assert EXPERT_TOPK == 2 and N_EXPERTS + N_GROUPS <= LANES
    wr = jnp.concatenate(
        [w_re, w_rg, jnp.zeros((d, LANES - N_EXPERTS - N_GROUPS), _F32)], axis=1)
    row = lambda i: (i, 0)
    fixed = lambda i: (0, 0)
    return pl.pallas_call(
        _router_body,
        grid=(n // tm,),
        in_specs=[
            pl.BlockSpec((tm, d), row),
            pl.BlockSpec((1, d), fixed),
            pl.BlockSpec((d, LANES), fixed),
        ],
        out_specs=[pl.BlockSpec((tm, d), row), pl.BlockSpec((tm, LANES), row),
                   pl.BlockSpec((tm, LANES), row), pl.BlockSpec((SUBLANES, LANES), fixed)],
        out_shape=[jax.ShapeDtypeStruct((n, d), _F32),
                   jax.ShapeDtypeStruct((n, LANES), jnp.int32),
                   jax.ShapeDtypeStruct((n, LANES), _F32),
                   jax.ShapeDtypeStruct((SUBLANES, LANES), jnp.int32)],
        scratch_shapes=[pltpu.VMEM((1, LANES), _F32)],
        compiler_params=_params("arbitrary"),
        name="router",
    )(x, gain.reshape(1, d), wr)


def _expert_body(blk_e_ref, row_tok_ref, hn_hbm, wg_ref, wu_ref, wd_ref, y_ref,
                 xs_ref, wg_bf, wu_bf, wd_bf, sem):
    g = pl.program_id(0)
    n_steps = pl.num_programs(0)
    rows = xs_ref.shape[1]
    slot = g % 2

    def row_copy(tok, s, r):
        return pltpu.make_async_copy(hn_hbm.at[pl.ds(tok, 1)], xs_ref.at[s, pl.ds(r, 1)], sem.at[s])

    def start_gather(block, s):
        def body(r, c):
            row_copy(row_tok_ref[block * rows + r], s, r).start()
            return c
        lax.fori_loop(0, rows, body, 0, unroll=GATHER_UNROLL)

    def wait_gather(s):
        pltpu.make_async_copy(hn_hbm.at[pl.ds(0, rows)], xs_ref.at[s], sem.at[s]).wait()

    @pl.when(g == 0)
    def _():
        start_gather(0, 0)

    @pl.when(g + 1 < n_steps)
    def _():
        start_gather(g + 1, 1 - slot)

    wait_gather(slot)

    e = blk_e_ref[g]
    @pl.when((g == 0) | (e != blk_e_ref[jnp.maximum(g - 1, 0)]))
    def _():
        wg_bf[...] = wg_ref[0, 0].astype(_BF16)
        wu_bf[...] = wu_ref[0, 0].astype(_BF16)
        wd_bf[...] = wd_ref[0, 0].astype(_BF16)

    xb = xs_ref[slot].astype(_BF16)
    act = jax.nn.silu(_bf16_dot(xb, wg_bf[...])) * _bf16_dot(xb, wu_bf[...])
    y_ref[...] = _bf16_dot(act.astype(_BF16), wd_bf[...])


def _expert_mlp(blk_e, row_tok, hn, layer, w_gate, w_up, w_down):
    d = hn.shape[1]
    de = w_gate.shape[3]
    rows = EXPERT_ROWS
    n_blocks = blk_e.shape[0]
    grid_spec = pltpu.PrefetchScalarGridSpec(
        num_scalar_prefetch=2,
        grid=(n_blocks,),
        in_specs=[
            pl.BlockSpec(memory_space=pl.ANY),
            pl.BlockSpec((1, 1, d, de), lambda g, be, rt: (layer, be[g], 0, 0)),
            pl.BlockSpec((1, 1, d, de), lambda g, be, rt: (layer, be[g], 0, 0)),
            pl.BlockSpec((1, 1, de, d), lambda g, be, rt: (layer, be[g], 0, 0)),
        ],
        out_specs=pl.BlockSpec((rows, d), lambda g, be, rt: (g, 0)),
        scratch_shapes=[
            pltpu.VMEM((2, rows, d), _F32),
            pltpu.VMEM((d, de), _BF16),
            pltpu.VMEM((d, de), _BF16),
            pltpu.VMEM((de, d), _BF16),
            pltpu.SemaphoreType.DMA((2,)),
        ],
    )
    return pl.pallas_call(
        _expert_body,
        grid_spec=grid_spec,
        out_shape=jax.ShapeDtypeStruct((n_blocks * rows, d), _F32),
        compiler_params=_params("arbitrary"),
        name="expert_mlp",
    )(blk_e, row_tok, hn, w_gate, w_up, w_down)


def _combine_body(pos_ref, y_hbm, wts_ref, h_ref, o_ref, ys_ref, sem):
    i = pl.program_id(0)
    n_steps = pl.num_programs(0)
    tm = h_ref.shape[0]
    slot = i % 2

    def row_copy(p, s, k, r):
        return pltpu.make_async_copy(y_hbm.at[pl.ds(p, 1)], ys_ref.at[s, k, pl.ds(r, 1)], sem.at[s])

    def start_gather(tile, s):
        def body(r, c):
            base = (tile * tm + r) * EXPERT_TOPK
            for k in range(EXPERT_TOPK):
                row_copy(pos_ref[base + k], s, k, r).start()
            return c
        lax.fori_loop(0, tm, body, 0, unroll=GATHER_UNROLL)

    def wait_gather(s):
        for k in range(EXPERT_TOPK):
            pltpu.make_async_copy(y_hbm.at[pl.ds(0, tm)], ys_ref.at[s, k], sem.at[s]).wait()

    @pl.when(i == 0)
    def _():
        start_gather(0, 0)

    @pl.when(i + 1 < n_steps)
    def _():
        start_gather(i + 1, 1 - slot)

    wait_gather(slot)
    w = wts_ref[...]
    o_ref[...] = h_ref[...] + (w[:, 0:1] * ys_ref[slot, 0] + w[:, 1:2] * ys_ref[slot, 1])


def _combine(pos, y, wts, h):
    n, d = h.shape
    tm = COMBINE_ROWS
    grid_spec = pltpu.PrefetchScalarGridSpec(
        num_scalar_prefetch=1,
        grid=(n // tm,),
        in_specs=[
            pl.BlockSpec(memory_space=pl.ANY),
            pl.BlockSpec((tm, LANES), lambda i, p: (i, 0)),
            pl.BlockSpec((tm, d), lambda i, p: (i, 0)),
        ],
        out_specs=pl.BlockSpec((tm, d), lambda i, p: (i, 0)),
        scratch_shapes=[
            pltpu.VMEM((2, EXPERT_TOPK, tm, d), _F32),
            pltpu.SemaphoreType.DMA((2,)),
        ],
    )
    return pl.pallas_call(
        _combine_body,
        grid_spec=grid_spec,
        out_shape=jax.ShapeDtypeStruct((n, d), _F32),
        compiler_params=_params("arbitrary"),
        name="combine",
    )(pos, y, wts, h)


def _dispatch_plan(ids, counts):
    n = ids.shape[0]
    rows = EXPERT_ROWS
    m = n * EXPERT_TOPK
    n_blocks = -(-(m + N_EXPERTS * (rows - 1)) // rows)
    flat_e = ids[:, :EXPERT_TOPK].reshape(m)
    rank = ids[:, EXPERT_TOPK:2 * EXPERT_TOPK].reshape(m)
    counts = counts[0, :N_EXPERTS]
    starts = jnp.cumsum(counts) - counts
    padded = (counts + rows - 1) // rows * rows
    pends = jnp.cumsum(padded)
    pstarts = pends - padded
    expert_iota = jnp.arange(N_EXPERTS, dtype=jnp.int32)
    pos = rank + jnp.sum(jnp.where(flat_e[:, None] == expert_iota[None, :], pstarts[None, :], 0), axis=1)
    blk_start = jnp.arange(n_blocks, dtype=jnp.int32) * rows
    blk_e = jnp.minimum(jnp.sum(pends[None, :] <= blk_start[:, None], axis=1), N_EXPERTS - 1)
    blk_e = blk_e.astype(jnp.int32)
    order = jnp.argsort(flat_e)
    sorted_idx = (blk_start - pstarts[blk_e] + starts[blk_e])[:, None] + jnp.arange(rows)[None, :]
    valid = sorted_idx < (starts + counts)[blk_e][:, None]
    row_tok = jnp.where(valid, order[jnp.clip(sorted_idx, 0, m - 1)] // EXPERT_TOPK, 0)
    return blk_e, row_tok.reshape(n_blocks * rows).astype(jnp.int32), pos.astype(jnp.int32)


def _hier_moe_residual(h, gain, w_rg, w_re, layer, w_gate, w_up, w_down):
    hn, ids, wts, counts = _router(h, gain, w_rg, w_re)
    blk_e, row_tok, pos = _dispatch_plan(ids, counts)
    y = _expert_mlp(blk_e, row_tok, hn, layer, w_gate, w_up, w_down)
    return _combine(pos, y, wts, h)


def kernel(x, mix_norm, ffn_norm, conv_w_in, conv_w, conv_w_out, kv_norm, w_kv, k_norm, w_q, q_norm,
           w_o, router_group, router_expert, w_gate, w_up, w_down):
    bsz, seq_len, d = x.shape
    depth = mix_norm.shape[0]
    n_conv = conv_w_in.shape[0]
    assert d == N_HEADS * HEAD_DIM
    h = x.reshape(bsz * seq_len, d)
    k_sh = v_sh = None
    for l in range(depth):
        if l < n_conv:
            z = _conv_in(h, mix_norm[l], conv_w_in[l], conv_w[l], seq_len)
            h = _matmul_residual(z, conv_w_out[l], h)
        else:
            j = l - n_conv
            if j == 0:
                k_sh = _norm_matmul(h, kv_norm, w_kv, 0, d, head_gain=k_norm)
                v_sh = _norm_matmul(h, kv_norm, w_kv, d, d)
            q = _norm_matmul(h, mix_norm[l], w_q[j], 0, d, head_gain=q_norm[j])
            attn = _moba_attention(q, k_sh, v_sh, bsz, seq_len)
            h = _matmul_residual(attn, w_o[j], h)
        h = _hier_moe_residual(h, ffn_norm[l], router_group[l], router_expert[l],
                               l, w_gate, w_up, w_down)
    return h.reshape(bsz, seq_len, d)
```

```python
import functools

import jax
import jax.numpy as jnp
from jax import lax
from jax.experimental import pallas as pl
from jax.experimental.pallas import tpu as pltpu

N_HEADS = 16
HEAD_DIM = 128
CONV_WIDTH = 3
MOBA_BLOCK = 256
MOBA_TOPK = 3
N_GROUPS = 4
EXPERTS_PER_GROUP = 8
N_EXPERTS = N_GROUPS * EXPERTS_PER_GROUP
EXPERT_TOPK = 2
NORM_EPS = 1e-6
ALIBI_MAX_BIAS = 8.0

LANES = 128
SUBLANES = 8
VMEM_LIMIT = 56 * 1024 * 1024

ROW_TILE = 1024
COL_TILE = 512
CONV_COL_TILE = 256
ROUTER_ROWS = 512
EXPERT_ROWS = 128
COMBINE_ROWS = 256

_BF16 = jnp.bfloat16
_F32 = jnp.float32


def _params(*semantics):
    return pltpu.CompilerParams(dimension_semantics=semantics, vmem_limit_bytes=VMEM_LIMIT)


def _rms_normalise(x, gain):
    ms = jnp.mean(x * x, axis=-1, keepdims=True)
    return x * lax.rsqrt(ms + NORM_EPS) * gain


def _bf16_dot(a, b):
    return jnp.dot(a, b, preferred_element_type=_F32)


def _norm_matmul_body(x_ref, g_ref, w_ref, *rest, head_norm):
    if head_norm:
        hg_ref, o_ref, xn_ref = rest
    else:
        o_ref, xn_ref = rest

    @pl.when(pl.program_id(1) == 0)
    def _():
        xn_ref[...] = _rms_normalise(x_ref[...], g_ref[...]).astype(_BF16)

    y = _bf16_dot(xn_ref[...], w_ref[...].astype(_BF16))
    if head_norm:
        for hh in range(y.shape[1] // HEAD_DIM):
            cols = slice(hh * HEAD_DIM, (hh + 1) * HEAD_DIM)
            o_ref[:, cols] = _rms_normalise(y[:, cols], hg_ref[...]).astype(o_ref.dtype)
    else:
        o_ref[...] = y.astype(o_ref.dtype)


def _norm_matmul(x, gain, w, col_off, n_out, head_gain=None):
    n, d = x.shape
    tm, tn = ROW_TILE, COL_TILE
    off = col_off // tn
    in_specs = [
        pl.BlockSpec((tm, d), lambda i, j: (i, 0)),
        pl.BlockSpec((1, d), lambda i, j: (0, 0)),
        pl.BlockSpec((d, tn), lambda i, j: (0, j + off)),
    ]
    args = [x, gain.reshape(1, d), w]
    if head_gain is not None:
        in_specs.append(pl.BlockSpec((1, HEAD_DIM), lambda i, j: (0, 0)))
        args.append(head_gain.reshape(1, HEAD_DIM))
    return pl.pallas_call(
        functools.partial(_norm_matmul_body, head_norm=head_gain is not None),
        grid=(n // tm, n_out // tn),
        in_specs=in_specs,
        out_specs=pl.BlockSpec((tm, tn), lambda i, j: (i, j)),
        out_shape=jax.ShapeDtypeStruct((n, n_out), _BF16),
        scratch_shapes=[pltpu.VMEM((tm, d), _BF16)],
        compiler_params=_params("arbitrary", "arbitrary"),
        name="norm_matmul",
    )(*args)


def _conv_in_body(x_ref, g_ref, wb_ref, wc_ref, wx_ref, cw_ref, z_ref, xn_ref, carry_ref,
                  *, tiles_per_seq):
    i = pl.program_id(0)
    j = pl.program_id(1)
    tm = x_ref.shape[0]

    @pl.when(j == 0)
    def _():
        xn_ref[...] = _rms_normalise(x_ref[...], g_ref[...]).astype(_BF16)

    @pl.when(i % tiles_per_seq == 0)
    def _():
        carry_ref[j] = jnp.zeros(carry_ref.shape[1:], _F32)

    xn = xn_ref[...]
    b_gate = _bf16_dot(xn, wb_ref[...].astype(_BF16))
    c_gate = _bf16_dot(xn, wc_ref[...].astype(_BF16))
    xh = _bf16_dot(xn, wx_ref[...].astype(_BF16))
    u = c_gate * xh
    prev = carry_ref[j]
    row = lax.broadcasted_iota(jnp.int32, u.shape, 0)
    last = prev[SUBLANES - 1:SUBLANES]
    u1 = jnp.where(row == 0, last, pltpu.roll(u, 1, axis=0))
    u2 = jnp.where(row == 0, prev[SUBLANES - 2:SUBLANES - 1],
                   jnp.where(row == 1, last, pltpu.roll(u, 2, axis=0)))
    cw = cw_ref[...]
    conv = cw[0:1] * u2 + cw[1:2] * u1 + cw[2:3] * u
    z_ref[...] = (b_gate * conv).astype(z_ref.dtype)
    carry_ref[j] = u[tm - SUBLANES:tm]


def _conv_in(x, gain, w_in, conv_w, seq_len):
    n, d = x.shape
    tm, tn = ROW_TILE, CONV_COL_TILE
    nj = d // tn
    assert CONV_WIDTH - 1 <= SUBLANES and seq_len % tm == 0
    return pl.pallas_call(
        functools.partial(_conv_in_body, tiles_per_seq=seq_len // tm),
        grid=(n // tm, nj),
        in_specs=[
            pl.BlockSpec((tm, d), lambda i, j: (i, 0)),
            pl.BlockSpec((1, d), lambda i, j: (0, 0)),
            pl.BlockSpec((d, tn), lambda i, j: (0, j)),
            pl.BlockSpec((d, tn), lambda i, j: (0, j + nj)),
            pl.BlockSpec((d, tn), lambda i, j: (0, j + 2 * nj)),
            pl.BlockSpec((CONV_WIDTH, tn), lambda i, j: (0, j)),
        ],
        out_specs=pl.BlockSpec((tm, tn), lambda i, j: (i, j)),
        out_shape=jax.ShapeDtypeStruct((n, d), _BF16),
        scratch_shapes=[pltpu.VMEM((tm, d), _BF16), pltpu.VMEM((nj, SUBLANES, tn), _F32)],
        compiler_params=_params("arbitrary", "arbitrary"),
        name="conv_in",
    )(x, gain.reshape(1, d), w_in, w_in, w_in, conv_w)


def _matmul_residual_body(x_ref, w_ref, r_ref, o_ref):
    o_ref[...] = r_ref[...] + _bf16_dot(x_ref[...], w_ref[...].astype(_BF16))


def _matmul_residual(x, w, res):
    n, k = x.shape
    d = w.shape[1]
    tm, tn = ROW_TILE, COL_TILE
    return pl.pallas_call(
        _matmul_residual_body,
        grid=(n // tm, d // tn),
        in_specs=[
            pl.BlockSpec((tm, k), lambda i, j: (i, 0)),
            pl.BlockSpec((k, tn), lambda i, j: (0, j)),
            pl.BlockSpec((tm, tn), lambda i, j: (i, j)),
        ],
        out_specs=pl.BlockSpec((tm, tn), lambda i, j: (i, j)),
        out_shape=jax.ShapeDtypeStruct((n, d), _F32),
        compiler_params=_params("arbitrary", "arbitrary"),
        name="matmul_residual",
    )(x, w, res)


def _moba_body(q_ref, k_ref, v_ref, o_ref, km_ref, vt_ref):
    head = pl.program_id(1)
    i = pl.program_id(2)
    kb = MOBA_BLOCK
    n_blocks = k_ref.shape[0] // kb
    nt = (((1,), (1,)), ((), ()))

    @pl.when(i == 0)
    def _():
        for jb in range(n_blocks):
            rows = slice(jb * kb, (jb + 1) * kb)
            km_ref[jb:jb + 1, :] = jnp.mean(k_ref[rows, :].astype(_F32), axis=0, keepdims=True)
            vt_ref[jb] = v_ref[rows, :].astype(_F32).T.astype(_BF16)

    q = q_ref[...]
    gate = lax.dot_general(km_ref[...], q.astype(_F32), nt,
                           precision=lax.Precision.HIGHEST, preferred_element_type=_F32)
    blk = lax.broadcasted_iota(jnp.int32, gate.shape, 0)
    gate = jnp.where(blk < i, gate, -jnp.inf)
    sel = jnp.zeros(gate.shape, _F32)
    for jb in range(n_blocks - 1):
        g_jb = gate[jb:jb + 1, :]
        beats = (gate > g_jb) | ((gate == g_jb) & (blk < jb))
        n_beat = jnp.sum(beats.astype(_F32), axis=0, keepdims=True)
        sel = jnp.where((blk == jb) & (blk < i) & (n_beat < MOBA_TOPK), 1.0, sel)

    scale = HEAD_DIM ** -0.5
    slope = jnp.exp2(jnp.full((1, kb), -ALIBI_MAX_BIAS / N_HEADS, _F32) * (head + 1).astype(_F32))
    rel = (lax.broadcasted_iota(jnp.int32, (kb, kb), 1)
           - lax.broadcasted_iota(jnp.int32, (kb, kb), 0))

    def attend(own):
        scores = []
        for jb in range(own + 1):
            s = lax.dot_general(k_ref[jb * kb:(jb + 1) * kb, :], q, nt, preferred_element_type=_F32)
            s = s * scale - slope * (rel + (own - jb) * kb).astype(_F32)
            visible = (rel >= 0) if jb == own else (sel[jb:jb + 1, :] > 0.5)
            scores.append(jnp.where(visible, s, -jnp.inf))
        m = functools.reduce(jnp.maximum, [jnp.max(s, axis=0, keepdims=True) for s in scores])
        probs = [jnp.exp(s - m) for s in scores]
        l = functools.reduce(jnp.add, [jnp.sum(p, axis=0, keepdims=True) for p in probs])
        acc = functools.reduce(jnp.add, [_bf16_dot(vt_ref[jb], p.astype(_BF16))
                                         for jb, p in enumerate(probs)])
        o_ref[...] = (acc / l).T.astype(o_ref.dtype)

    for own in range(n_blocks):
        pl.when(i == own)(functools.partial(attend, own))


def _moba_attention(q, k, v, batch, seq_len):
    n, d = q.shape
    kb = MOBA_BLOCK
    nqb = seq_len // kb
    assert seq_len % kb == 0 and nqb == SUBLANES
    return pl.pallas_call(
        _moba_body,
        grid=(batch, N_HEADS, nqb),
        in_specs=[
            pl.BlockSpec((kb, HEAD_DIM), lambda b, h, i: (b * nqb + i, h)),
            pl.BlockSpec((seq_len, HEAD_DIM), lambda b, h, i: (b, h)),
            pl.BlockSpec((seq_len, HEAD_DIM), lambda b, h, i: (b, h)),
        ],
        out_specs=pl.BlockSpec((kb, HEAD_DIM), lambda b, h, i: (b * nqb + i, h)),
        out_shape=jax.ShapeDtypeStruct((n, d), _BF16),
        scratch_shapes=[pltpu.VMEM((nqb, HEAD_DIM), _F32),
                        pltpu.VMEM((nqb, HEAD_DIM, kb), _BF16)],
        compiler_params=_params("arbitrary", "arbitrary", "arbitrary"),
        name="moba_attention",
    )(q, k, v)


def _router_body(x_ref, g_ref, wr_ref, hn_ref, ids_ref, wts_ref, cnt_ref, run_ref):
    step = pl.program_id(0)
    hn = _rms_normalise(x_ref[...], g_ref[...])
    hn_ref[...] = hn
    logits = jnp.dot(hn, wr_ref[...], precision=lax.Precision.HIGHEST, preferred_element_type=_F32)
    lane = lax.broadcasted_iota(jnp.int32, logits.shape, 1)
    far = jnp.int32(LANES)

    def first_lane(mask):
        return jnp.min(jnp.where(mask, lane, far), axis=1, keepdims=True)

    is_group = (lane >= N_EXPERTS) & (lane < N_EXPERTS + N_GROUPS)
    g_max = jnp.max(jnp.where(is_group, logits, -jnp.inf), axis=1, keepdims=True)
    g_sel = first_lane(is_group & (logits == g_max)) - N_EXPERTS
    g_w = 1.0 / jnp.sum(jnp.where(is_group, jnp.exp(logits - g_max), 0.0), axis=1, keepdims=True)

    in_group = (lane >= g_sel * EXPERTS_PER_GROUP) & (lane < (g_sel + 1) * EXPERTS_PER_GROUP)
    top1 = jnp.max(jnp.where(in_group, logits, -jnp.inf), axis=1, keepdims=True)
    idx1 = first_lane(in_group & (logits == top1))
    rest = in_group & (lane != idx1)
    top2 = jnp.max(jnp.where(rest, logits, -jnp.inf), axis=1, keepdims=True)
    idx2 = first_lane(rest & (logits == top2))
    t = jnp.exp(top2 - top1)
    w1 = 1.0 / (1.0 + t) * g_w
    w2 = t / (1.0 + t) * g_w

    @pl.when(step == 0)
    def _():
        run_ref[...] = jnp.zeros(run_ref.shape, _F32)

    tm = logits.shape[0]
    chosen = (lane == idx1) | (lane == idx2)
    earlier = (lax.broadcasted_iota(jnp.int32, (tm, tm), 1)
               < lax.broadcasted_iota(jnp.int32, (tm, tm), 0))
    before = run_ref[...] + _bf16_dot(jnp.where(earlier, 1.0, 0.0).astype(_BF16),
                                      jnp.where(chosen, 1.0, 0.0).astype(_BF16))
    rank1 = jnp.sum(jnp.where(lane == idx1, before, 0.0), axis=1, keepdims=True).astype(jnp.int32)
    rank2 = jnp.sum(jnp.where(lane == idx2, before, 0.0), axis=1, keepdims=True).astype(jnp.int32)
    run_ref[...] += jnp.sum(jnp.where(chosen, 1.0, 0.0), axis=0, keepdims=True)

    ids_ref[...] = jnp.where(lane == 0, idx1, jnp.where(lane == 1, idx2,
                             jnp.where(lane == 2, rank1, jnp.where(lane == 3, rank2, 0))))
    wts_ref[...] = jnp.where(lane == 0, w1, jnp.where(lane == 1, w2, 0.0))
    cnt_ref[...] = jnp.broadcast_to(run_ref[...], cnt_ref.shape).astype(jnp.int32)


def _router(x, gain, w_rg, w_re):
    n, d = x.shape
    tm = ROUTER_ROWS
    assert EXPERT_TOPK == 2 and N_EXPERTS + N_GROUPS <= LANES
    wr = jnp.concatenate(
        [w_re, w_rg, jnp.zeros((d, LANES - N_EXPERTS - N_GROUPS), _F32)], axis=1)
    row = lambda i: (i, 0)
    fixed = lambda i: (0, 0)
    return pl.pallas_call(
        _router_body,
        grid=(n // tm,),
        in_specs=[
            pl.BlockSpec((tm, d), row),
            pl.BlockSpec((1, d), fixed),
            pl.BlockSpec((d, LANES), fixed),
        ],
        out_specs=[pl.BlockSpec((tm, d), row), pl.BlockSpec((tm, LANES), row),
                   pl.BlockSpec((tm, LANES), row), pl.BlockSpec((SUBLANES, LANES), fixed)],
        out_shape=[jax.ShapeDtypeStruct((n, d), _F32),
                   jax.ShapeDtypeStruct((n, LANES), jnp.int32),
                   jax.ShapeDtypeStruct((n, LANES), _F32),
                   jax.ShapeDtypeStruct((SUBLANES, LANES), jnp.int32)],
        scratch_shapes=[pltpu.VMEM((1, LANES), _F32)],
        compiler_params=_params("arbitrary"),
        name="router",
    )(x, gain.reshape(1, d), wr)


def _expert_body(bounds_ref, row_tok_ref, hn_hbm, wg_ref, wu_ref, wd_ref, y_hbm,
                 xs_ref, ys_ref, zero_ref, wg_bf, wu_bf, wd_bf, gather_sem, out_sem, tail_sem):
    e = pl.program_id(0)
    n_experts = pl.num_programs(0)
    rows = xs_ref.shape[1]
    first, stop = bounds_ref[e], bounds_ref[e + 1]
    n_total = bounds_ref[n_experts]

    def row_copy(tok, s, r):
        return pltpu.make_async_copy(hn_hbm.at[pl.ds(tok, 1)], xs_ref.at[s, pl.ds(r, 1)],
                                     gather_sem.at[s])

    def start_gather(block, s):
        for r in range(rows):
            row_copy(row_tok_ref[block * rows + r], s, r).start(priority=r % 2)

    def wait_gather(s):
        pltpu.make_async_copy(hn_hbm.at[pl.ds(0, rows)], xs_ref.at[s], gather_sem.at[s]).wait()

    def out_copy(block, s):
        dst = y_hbm.at[pl.ds(pl.multiple_of(block * rows, rows), rows)]
        return pltpu.make_async_copy(ys_ref.at[s], dst, out_sem.at[s])

    def tail_copy(block):
        dst = y_hbm.at[pl.ds(pl.multiple_of(block * rows, rows), rows)]
        return pltpu.make_async_copy(zero_ref, dst, tail_sem.at[0])

    def for_tail_blocks(fn):
        def body(block, carry):
            fn(tail_copy(block))
            return carry
        lax.fori_loop(n_total, y_hbm.shape[0] // rows, body, 0)

    @pl.when(e == 0)
    def _():
        start_gather(0, 0)
        zero_ref[...] = jnp.zeros(zero_ref.shape, _F32)
        for_tail_blocks(lambda copy: copy.start())

    @pl.when(stop > first)
    def _():
        wg_bf[...] = wg_ref[0, 0].astype(_BF16)
        wu_bf[...] = wu_ref[0, 0].astype(_BF16)
        wd_bf[...] = wd_ref[0, 0].astype(_BF16)

    def block_step(block, carry):
        s = block % 2
        start_gather(jnp.minimum(block + 1, n_total - 1), 1 - s)
        wait_gather(s)
        xb = xs_ref[s].astype(_BF16)
        act = jax.nn.silu(_bf16_dot(xb, wg_bf[...])) * _bf16_dot(xb, wu_bf[...])
        y = _bf16_dot(act.astype(_BF16), wd_bf[...])

        @pl.when(block >= 2)
        def _():
            out_copy(block - 2, s).wait()

        ys_ref[s] = y
        out_copy(block, s).start()
        return carry

    lax.fori_loop(first, stop, block_step, 0)

    @pl.when(e == n_experts - 1)
    def _():
        wait_gather(n_total % 2)
        for_tail_blocks(lambda copy: copy.wait())
        for back in (1, 2):
            @pl.when(n_total >= back)
            def _():
                out_copy(n_total - back, (n_total - back) % 2).wait()


def _expert_mlp(bounds, row_tok, hn, layer, w_gate, w_up, w_down):
    d = hn.shape[1]
    n_experts, de = w_gate.shape[1], w_gate.shape[3]
    rows = EXPERT_ROWS
    grid_spec = pltpu.PrefetchScalarGridSpec(
        num_scalar_prefetch=2,
        grid=(n_experts,),
        in_specs=[
            pl.BlockSpec(memory_space=pl.ANY),
            pl.BlockSpec((1, 1, d, de), lambda e, bd, rt: (layer, e, 0, 0)),
            pl.BlockSpec((1, 1, d, de), lambda e, bd, rt: (layer, e, 0, 0)),
            pl.BlockSpec((1, 1, de, d), lambda e, bd, rt: (layer, e, 0, 0)),
        ],
        out_specs=pl.BlockSpec(memory_space=pl.ANY),
        scratch_shapes=[
            pltpu.VMEM((2, rows, d), _F32),
            pltpu.VMEM((2, rows, d), _F32),
            pltpu.VMEM((rows, d), _F32),
            pltpu.VMEM((d, de), _BF16),
            pltpu.VMEM((d, de), _BF16),
            pltpu.VMEM((de, d), _BF16),
            pltpu.SemaphoreType.DMA((2,)),
            pltpu.SemaphoreType.DMA((2,)),
            pltpu.SemaphoreType.DMA((1,)),
        ],
    )
    return pl.pallas_call(
        _expert_body,
        grid_spec=grid_spec,
        out_shape=jax.ShapeDtypeStruct((row_tok.shape[0], d), _F32),
        compiler_params=_params("arbitrary"),
        name="expert_mlp",
    )(bounds, row_tok, hn, w_gate, w_up, w_down)


def _combine_body(pos_ref, y_hbm, wts_ref, h_ref, o_ref, ys_ref, sem):
    i = pl.program_id(0)
    n_steps = pl.num_programs(0)
    tm = h_ref.shape[0]
    slot = i % 2

    def row_copy(p, s, k, r):
        return pltpu.make_async_copy(y_hbm.at[pl.ds(p, 1)], ys_ref.at[s, k, pl.ds(r, 1)], sem.at[s])

    def start_gather(tile, s):
        for r in range(tm):
            for k in range(EXPERT_TOPK):
                row_copy(pos_ref[(tile * tm + r) * EXPERT_TOPK + k], s, k, r).start(priority=k)

    def wait_gather(s):
        for k in range(EXPERT_TOPK):
            pltpu.make_async_copy(y_hbm.at[pl.ds(0, tm)], ys_ref.at[s, k], sem.at[s]).wait()

    @pl.when(i == 0)
    def _():
        start_gather(0, 0)

    @pl.when(i + 1 < n_steps)
    def _():
        start_gather(i + 1, 1 - slot)

    wait_gather(slot)
    w = wts_ref[...]
    o_ref[...] = h_ref[...] + (w[:, 0:1] * ys_ref[slot, 0] + w[:, 1:2] * ys_ref[slot, 1])


def _combine(pos, y, wts, h):
    n, d = h.shape
    tm = COMBINE_ROWS
    grid_spec = pltpu.PrefetchScalarGridSpec(
        num_scalar_prefetch=1,
        grid=(n // tm,),
        in_specs=[
            pl.BlockSpec(memory_space=pl.ANY),
            pl.BlockSpec((tm, LANES), lambda i, p: (i, 0)),
            pl.BlockSpec((tm, d), lambda i, p: (i, 0)),
        ],
        out_specs=pl.BlockSpec((tm, d), lambda i, p: (i, 0)),
        scratch_shapes=[
            pltpu.VMEM((2, EXPERT_TOPK, tm, d), _F32),
            pltpu.SemaphoreType.DMA((2,)),
        ],
    )
    return pl.pallas_call(
        _combine_body,
        grid_spec=grid_spec,
        out_shape=jax.ShapeDtypeStruct((n, d), _F32),
        compiler_params=_params("arbitrary"),
        name="combine",
    )(pos, y, wts, h)


def _dispatch_plan(ids, counts):
    n = ids.shape[0]
    rows = EXPERT_ROWS
    m = n * EXPERT_TOPK
    n_blocks = -(-(m + N_EXPERTS * (rows - 1)) // rows)
    flat_e = ids[:, :EXPERT_TOPK].reshape(m)
    rank = ids[:, EXPERT_TOPK:2 * EXPERT_TOPK].reshape(m)
    counts = counts[0, :N_EXPERTS]
    starts = jnp.cumsum(counts) - counts
    padded = (counts + rows - 1) // rows * rows
    pends = jnp.cumsum(padded)
    pstarts = pends - padded
    expert_iota = jnp.arange(N_EXPERTS, dtype=jnp.int32)
    pos = rank + jnp.sum(jnp.where(flat_e[:, None] == expert_iota[None, :], pstarts[None, :], 0), axis=1)
    blk_start = jnp.arange(n_blocks, dtype=jnp.int32) * rows
    blk_e = jnp.minimum(jnp.sum(pends[None, :] <= blk_start[:, None], axis=1), N_EXPERTS - 1)
    blk_e = blk_e.astype(jnp.int32)
    order = jnp.argsort(flat_e)
    sorted_idx = (blk_start - pstarts[blk_e] + starts[blk_e])[:, None] + jnp.arange(rows)[None, :]
    valid = sorted_idx < (starts + counts)[blk_e][:, None]
    row_tok = jnp.where(valid, order[jnp.clip(sorted_idx, 0, m - 1)] // EXPERT_TOPK, 0)
    bounds = jnp.concatenate([pstarts, pends[-1:]]) // rows
    return (bounds.astype(jnp.int32), row_tok.reshape(n_blocks * rows).astype(jnp.int32),
            pos.astype(jnp.int32))


def _hier_moe_residual(h, gain, w_rg, w_re, layer, w_gate, w_up, w_down):
    hn, ids, wts, counts = _router(h, gain, w_rg, w_re)
    bounds, row_tok, pos = _dispatch_plan(ids, counts)
    y = _expert_mlp(bounds, row_tok, hn, layer, w_gate, w_up, w_down)
    return _combine(pos, y, wts, h)


def kernel(x, mix_norm, ffn_norm, conv_w_in, conv_w, conv_w_out, kv_norm, w_kv, k_norm, w_q, q_norm,
           w_o, router_group, router_expert, w_gate, w_up, w_down):
    bsz, seq_len, d = x.shape
    depth = mix_norm.shape[0]
    n_conv = conv_w_in.shape[0]
    assert d == N_HEADS * HEAD_DIM
    h = x.reshape(bsz * seq_len, d)
    k_sh = v_sh = None
    for l in range(depth):
        if l < n_conv:
            z = _conv_in(h, mix_norm[l], conv_w_in[l], conv_w[l], seq_len)
            h = _matmul_residual(z, conv_w_out[l], h)
        else:
            j = l - n_conv
            if j == 0:
                k_sh = _norm_matmul(h, kv_norm, w_kv, 0, d, head_gain=k_norm)
                v_sh = _norm_matmul(h, kv_norm, w_kv, d, d)
            q = _norm_matmul(h, mix_norm[l], w_q[j], 0, d, head_gain=q_norm[j])
            attn = _moba_attention(q, k_sh, v_sh, bsz, seq_len)
            h = _matmul_residual(attn, w_o[j], h)
        h = _hier_moe_residual(h, ffn_norm[l], router_group[l], router_expert[l],
                               l, w_gate, w_up, w_down)
    return h.reshape(bsz, seq_len, d)
```

```python
import functools

import jax
import jax.numpy as jnp
from jax import lax
from jax.experimental import pallas as pl
from jax.experimental.pallas import tpu as pltpu

N_HEADS = 16
HEAD_DIM = 128
CONV_WIDTH = 3
MOBA_BLOCK = 256
MOBA_TOPK = 3
N_GROUPS = 4
EXPERTS_PER_GROUP = 8
N_EXPERTS = N_GROUPS * EXPERTS_PER_GROUP
EXPERT_TOPK = 2
NORM_EPS = 1e-6
ALIBI_MAX_BIAS = 8.0

LANES = 128
SUBLANES = 8
VMEM_LIMIT = 56 * 1024 * 1024

ROW_TILE = 1024
COL_TILE = 512
CONV_COL_TILE = 256
ROUTER_ROWS = 512
EXPERT_ROWS = 128
COMBINE_ROWS = 256
ATTN_GROUP = 8

_BF16 = jnp.bfloat16
_F32 = jnp.float32


def _params(*semantics):
    return pltpu.CompilerParams(dimension_semantics=semantics, vmem_limit_bytes=VMEM_LIMIT)


def _rms_normalise(x, gain):
    ms = jnp.mean(x * x, axis=-1, keepdims=True)
    return x * lax.rsqrt(ms + NORM_EPS) * gain


def _bf16_dot(a, b):
    return jnp.dot(a, b, preferred_element_type=_F32)


def _norm_matmul_body(x_ref, g_ref, w_ref, *rest, head_norm):
    if head_norm:
        hg_ref, o_ref, xn_ref = rest
    else:
        o_ref, xn_ref = rest

    @pl.when(pl.program_id(1) == 0)
    def _():
        xn_ref[...] = _rms_normalise(x_ref[...], g_ref[...]).astype(_BF16)

    y = _bf16_dot(xn_ref[...], w_ref[...].astype(_BF16))
    if head_norm:
        for hh in range(y.shape[1] // HEAD_DIM):
            cols = slice(hh * HEAD_DIM, (hh + 1) * HEAD_DIM)
            o_ref[:, cols] = _rms_normalise(y[:, cols], hg_ref[...]).astype(o_ref.dtype)
    else:
        o_ref[...] = y.astype(o_ref.dtype)


def _norm_matmul(x, gain, w, col_off, n_out, head_gain=None):
    n, d = x.shape
    tm, tn = ROW_TILE, COL_TILE
    off = col_off // tn
    in_specs = [
        pl.BlockSpec((tm, d), lambda i, j: (i, 0)),
        pl.BlockSpec((1, d), lambda i, j: (0, 0)),
        pl.BlockSpec((d, tn), lambda i, j: (0, j + off)),
    ]
    args = [x, gain.reshape(1, d), w]
    if head_gain is not None:
        in_specs.append(pl.BlockSpec((1, HEAD_DIM), lambda i, j: (0, 0)))
        args.append(head_gain.reshape(1, HEAD_DIM))
    return pl.pallas_call(
        functools.partial(_norm_matmul_body, head_norm=head_gain is not None),
        grid=(n // tm, n_out // tn),
        in_specs=in_specs,
        out_specs=pl.BlockSpec((tm, tn), lambda i, j: (i, j)),
        out_shape=jax.ShapeDtypeStruct((n, n_out), _BF16),
        scratch_shapes=[pltpu.VMEM((tm, d), _BF16)],
        compiler_params=_params("arbitrary", "arbitrary"),
        name="norm_matmul",
    )(*args)


def _conv_in_body(x_ref, g_ref, wb_ref, wc_ref, wx_ref, cw_ref, z_ref, xn_ref, carry_ref,
                  *, tiles_per_seq):
    i = pl.program_id(0)
    j = pl.program_id(1)
    tm = x_ref.shape[0]

    @pl.when(j == 0)
    def _():
        xn_ref[...] = _rms_normalise(x_ref[...], g_ref[...]).astype(_BF16)

    @pl.when(i % tiles_per_seq == 0)
    def _():
        carry_ref[j] = jnp.zeros(carry_ref.shape[1:], _F32)

    xn = xn_ref[...]
    b_gate = _bf16_dot(xn, wb_ref[...].astype(_BF16))
    c_gate = _bf16_dot(xn, wc_ref[...].astype(_BF16))
    xh = _bf16_dot(xn, wx_ref[...].astype(_BF16))
    u = c_gate * xh
    prev = carry_ref[j]
    row = lax.broadcasted_iota(jnp.int32, u.shape, 0)
    last = prev[SUBLANES - 1:SUBLANES]
    u1 = jnp.where(row == 0, last, pltpu.roll(u, 1, axis=0))
    u2 = jnp.where(row == 0, prev[SUBLANES - 2:SUBLANES - 1],
                   jnp.where(row == 1, last, pltpu.roll(u, 2, axis=0)))
    cw = cw_ref[...]
    conv = cw[0:1] * u2 + cw[1:2] * u1 + cw[2:3] * u
    z_ref[...] = (b_gate * conv).astype(z_ref.dtype)
    carry_ref[j] = u[tm - SUBLANES:tm]


def _conv_in(x, gain, w_in, conv_w, seq_len):
    n, d = x.shape
    tm, tn = ROW_TILE, CONV_COL_TILE
    nj = d // tn
    assert CONV_WIDTH - 1 <= SUBLANES and seq_len % tm == 0
    return pl.pallas_call(
        functools.partial(_conv_in_body, tiles_per_seq=seq_len // tm),
        grid=(n // tm, nj),
        in_specs=[
            pl.BlockSpec((tm, d), lambda i, j: (i, 0)),
            pl.BlockSpec((1, d), lambda i, j: (0, 0)),
            pl.BlockSpec((d, tn), lambda i, j: (0, j)),
            pl.BlockSpec((d, tn), lambda i, j: (0, j + nj)),
            pl.BlockSpec((d, tn), lambda i, j: (0, j + 2 * nj)),
            pl.BlockSpec((CONV_WIDTH, tn), lambda i, j: (0, j)),
        ],
        out_specs=pl.BlockSpec((tm, tn), lambda i, j: (i, j)),
        out_shape=jax.ShapeDtypeStruct((n, d), _BF16),
        scratch_shapes=[pltpu.VMEM((tm, d), _BF16), pltpu.VMEM((nj, SUBLANES, tn), _F32)],
        compiler_params=_params("arbitrary", "arbitrary"),
        name="conv_in",
    )(x, gain.reshape(1, d), w_in, w_in, w_in, conv_w)


def _matmul_residual_body(x_ref, w_ref, r_ref, o_ref):
    o_ref[...] = r_ref[...] + _bf16_dot(x_ref[...], w_ref[...].astype(_BF16))


def _matmul_residual(x, w, res):
    n, k = x.shape
    d = w.shape[1]
    tm, tn = ROW_TILE, COL_TILE
    return pl.pallas_call(
        _matmul_residual_body,
        grid=(n // tm, d // tn),
        in_specs=[
            pl.BlockSpec((tm, k), lambda i, j: (i, 0)),
            pl.BlockSpec((k, tn), lambda i, j: (0, j)),
            pl.BlockSpec((tm, tn), lambda i, j: (i, j)),
        ],
        out_specs=pl.BlockSpec((tm, tn), lambda i, j: (i, j)),
        out_shape=jax.ShapeDtypeStruct((n, d), _F32),
        compiler_params=_params("arbitrary", "arbitrary"),
        name="matmul_residual",
    )(x, w, res)


def _moba_body(q_ref, k_ref, v_ref, o_ref, km_ref, vt_ref, bias_ref):
    head = pl.program_id(1)
    i = pl.program_id(2)
    kb = MOBA_BLOCK
    n_blocks = k_ref.shape[0] // kb
    nt = (((1,), (1,)), ((), ()))
    log2e = 1.4426950408889634
    scale = HEAD_DIM ** -0.5 * log2e
    slope = log2e * jnp.exp2(
        jnp.full((1, kb), -ALIBI_MAX_BIAS / N_HEADS, _F32) * (head + 1).astype(_F32))

    @pl.when(i == 0)
    def _():
        for jb in range(n_blocks):
            rows = slice(jb * kb, (jb + 1) * kb)
            km_ref[jb:jb + 1, :] = jnp.mean(k_ref[rows, :].astype(_F32), axis=0, keepdims=True)
            vt_ref[jb] = v_ref[rows, :].astype(_F32).T.astype(_BF16)
        rel = (lax.broadcasted_iota(jnp.int32, (kb, kb), 1)
               - lax.broadcasted_iota(jnp.int32, (kb, kb), 0))
        bias = slope * rel.astype(_F32)
        bias_ref[0] = bias
        bias_ref[1] = jnp.where(rel >= 0, bias, jnp.inf)

    q = q_ref[...]
    gate = lax.dot_general(km_ref[...], q.astype(_F32), nt,
                           precision=lax.Precision.HIGHEST, preferred_element_type=_F32)
    blk = lax.broadcasted_iota(jnp.int32, gate.shape, 0)
    gate = jnp.where(blk < i, gate, -jnp.inf)
    sel = jnp.zeros(gate.shape, _F32)
    for jb in range(n_blocks - 1):
        g_jb = gate[jb:jb + 1, :]
        beats = (gate > g_jb) | ((gate == g_jb) & (blk < jb))
        n_beat = jnp.sum(beats.astype(_F32), axis=0, keepdims=True)
        sel = jnp.where((blk == jb) & (blk < i) & (n_beat < MOBA_TOPK), 1.0, sel)

    def attend_group(own, blocks):
        tiles, shifts, picked, col_max = [], [], [], []
        for jb in blocks:
            raw = lax.dot_general(k_ref[jb * kb:(jb + 1) * kb, :], q, nt, preferred_element_type=_F32)
            tile = raw * scale - bias_ref[1 if jb == own else 0]
            mx = jnp.max(tile, axis=0, keepdims=True)
            if jb == own:
                shift, pick = None, None
            else:
                shift = slope * float((own - jb) * kb)
                pick = sel[jb:jb + 1, :] > 0.5
                mx = jnp.where(pick, mx - shift, -jnp.inf)
            tiles.append(tile), shifts.append(shift), picked.append(pick), col_max.append(mx)
        m = functools.reduce(jnp.maximum, col_max)
        l = acc = None
        for n, jb in enumerate(blocks):
            offset = m if jb == own else jnp.where(picked[n], m + shifts[n], jnp.inf)
            p = jnp.exp2(tiles[n] - offset)
            p_sum = jnp.sum(p, axis=0, keepdims=True)
            pv = _bf16_dot(vt_ref[jb], p.astype(_BF16))
            l = p_sum if l is None else l + p_sum
            acc = pv if acc is None else acc + pv
        return m, l, acc

    def attend(own):
        order = list(range(own, -1, -1))
        groups = [order[g:g + ATTN_GROUP] for g in range(0, own + 1, ATTN_GROUP)]
        parts = [attend_group(own, blocks) for blocks in groups]
        m = functools.reduce(jnp.maximum, [part[0] for part in parts])
        l = acc = None
        for m_g, l_g, acc_g in parts:
            w = jnp.exp2(m_g - m)
            l = w * l_g if l is None else l + w * l_g
            acc = w * acc_g if acc is None else acc + w * acc_g
        o_ref[...] = (acc / l).T.astype(o_ref.dtype)

    for own in range(n_blocks):
        pl.when(i == own)(functools.partial(attend, own))


def _moba_attention(q, k, v, batch, seq_len):
    n, d = q.shape
    kb = MOBA_BLOCK
    nqb = seq_len // kb
    assert seq_len % kb == 0 and nqb == SUBLANES
    return pl.pallas_call(
        _moba_body,
        grid=(batch, N_HEADS, nqb),
        in_specs=[
            pl.BlockSpec((kb, HEAD_DIM), lambda b, h, i: (b * nqb + i, h)),
            pl.BlockSpec((seq_len, HEAD_DIM), lambda b, h, i: (b, h)),
            pl.BlockSpec((seq_len, HEAD_DIM), lambda b, h, i: (b, h)),
        ],
        out_specs=pl.BlockSpec((kb, HEAD_DIM), lambda b, h, i: (b * nqb + i, h)),
        out_shape=jax.ShapeDtypeStruct((n, d), _BF16),
        scratch_shapes=[pltpu.VMEM((nqb, HEAD_DIM), _F32),
                        pltpu.VMEM((nqb, HEAD_DIM, kb), _BF16),
                        pltpu.VMEM((2, kb, kb), _F32)],
        compiler_params=_params("arbitrary", "arbitrary", "arbitrary"),
        name="moba_attention",
    )(q, k, v)


def _router_body(x_ref, g_ref, wr_ref, hn_ref, ids_ref, wts_ref, cnt_ref, run_ref):
    step = pl.program_id(0)
    hn = _rms_normalise(x_ref[...], g_ref[...])
    hn_ref[...] = hn
    logits = jnp.dot(hn, wr_ref[...], precision=lax.Precision.HIGHEST, preferred_element_type=_F32)
    lane = lax.broadcasted_iota(jnp.int32, logits.shape, 1)
    far = jnp.int32(LANES)

    def first_lane(mask):
        return jnp.min(jnp.where(mask, lane, far), axis=1, keepdims=True)

    is_group = (lane >= N_EXPERTS) & (lane < N_EXPERTS + N_GROUPS)
    g_max = jnp.max(jnp.where(is_group, logits, -jnp.inf), axis=1, keepdims=True)
    g_sel = first_lane(is_group & (logits == g_max)) - N_EXPERTS
    g_w = 1.0 / jnp.sum(jnp.where(is_group, jnp.exp(logits - g_max), 0.0), axis=1, keepdims=True)

    in_group = (lane >= g_sel * EXPERTS_PER_GROUP) & (lane < (g_sel + 1) * EXPERTS_PER_GROUP)
    top1 = jnp.max(jnp.where(in_group, logits, -jnp.inf), axis=1, keepdims=True)
    idx1 = first_lane(in_group & (logits == top1))
    rest = in_group & (lane != idx1)
    top2 = jnp.max(jnp.where(rest, logits, -jnp.inf), axis=1, keepdims=True)
    idx2 = first_lane(rest & (logits == top2))
    t = jnp.exp(top2 - top1)
    w1 = 1.0 / (1.0 + t) * g_w
    w2 = t / (1.0 + t) * g_w

    @pl.when(step == 0)
    def _():
        run_ref[...] = jnp.zeros(run_ref.shape, _F32)

    tm = logits.shape[0]
    chosen = (lane == idx1) | (lane == idx2)
    earlier = (lax.broadcasted_iota(jnp.int32, (tm, tm), 1)
               < lax.broadcasted_iota(jnp.int32, (tm, tm), 0))
    before = run_ref[...] + _bf16_dot(jnp.where(earlier, 1.0, 0.0).astype(_BF16),
                                      jnp.where(chosen, 1.0, 0.0).astype(_BF16))
    rank1 = jnp.sum(jnp.where(lane == idx1, before, 0.0), axis=1, keepdims=True).astype(jnp.int32)
    rank2 = jnp.sum(jnp.where(lane == idx2, before, 0.0), axis=1, keepdims=True).astype(jnp.int32)
    run_ref[...] += jnp.sum(jnp.where(chosen, 1.0, 0.0), axis=0, keepdims=True)

    ids_ref[...] = jnp.where(lane == 0, idx1, jnp.where(lane == 1, idx2,
                             jnp.where(lane == 2, rank1, jnp.where(lane == 3, rank2, 0))))
    wts_ref[...] = jnp.where(lane == 0, w1, jnp.where(lane == 1, w2, 0.0))
    cnt_ref[...] = jnp.broadcast_to(run_ref[...], cnt_ref.shape).astype(jnp.int32)


def _router(x, gain, w_rg, w_re):
    n, d = x.shape
    tm = ROUTER_ROWS
    assert EXPERT_TOPK == 2 and N_EXPERTS + N_GROUPS <= LANES
    wr = jnp.concatenate(
        [w_re, w_rg, jnp.zeros((d, LANES - N_EXPERTS - N_GROUPS), _F32)], axis=1)
    row = lambda i: (i, 0)
    fixed = lambda i: (0, 0)
    return pl.pallas_call(
        _router_body,
        grid=(n // tm,),
        in_specs=[
            pl.BlockSpec((tm, d), row),
            pl.BlockSpec((1, d), fixed),
            pl.BlockSpec((d, LANES), fixed),
        ],
        out_specs=[pl.BlockSpec((tm, d), row), pl.BlockSpec((tm, LANES), row),
                   pl.BlockSpec((tm, LANES), row), pl.BlockSpec((SUBLANES, LANES), fixed)],
        out_shape=[jax.ShapeDtypeStruct((n, d), _F32),
                   jax.ShapeDtypeStruct((n, LANES), jnp.int32),
                   jax.ShapeDtypeStruct((n, LANES), _F32),
                   jax.ShapeDtypeStruct((SUBLANES, LANES), jnp.int32)],
        scratch_shapes=[pltpu.VMEM((1, LANES), _F32)],
        compiler_params=_params("arbitrary"),
        name="router",
    )(x, gain.reshape(1, d), wr)


def _expert_body(bounds_ref, row_tok_ref, hn_hbm, wg_ref, wu_ref, wd_ref, y_hbm,
                 xs_ref, ys_ref, zero_ref, wg_bf, wu_bf, wd_bf, gather_sem, out_sem, tail_sem):
    e = pl.program_id(0)
    n_experts = pl.num_programs(0)
    rows = xs_ref.shape[1]
    first, stop = bounds_ref[e], bounds_ref[e + 1]
    n_total = bounds_ref[n_experts]

    def row_copy(tok, s, r):
        return pltpu.make_async_copy(hn_hbm.at[pl.ds(tok, 1)], xs_ref.at[s, pl.ds(r, 1)],
                                     gather_sem.at[s])

    def start_gather(block, s):
        for r in range(rows):
            row_copy(row_tok_ref[block * rows + r], s, r).start(priority=1)

    def wait_gather(s):
        pltpu.make_async_copy(hn_hbm.at[pl.ds(0, rows)], xs_ref.at[s], gather_sem.at[s]).wait()

    def out_copy(block, s):
        dst = y_hbm.at[pl.ds(pl.multiple_of(block * rows, rows), rows)]
        return pltpu.make_async_copy(ys_ref.at[s], dst, out_sem.at[s])

    def tail_copy(block):
        dst = y_hbm.at[pl.ds(pl.multiple_of(block * rows, rows), rows)]
        return pltpu.make_async_copy(zero_ref, dst, tail_sem.at[0])

    def for_tail_blocks(fn):
        def body(block, carry):
            fn(tail_copy(block))
            return carry
        lax.fori_loop(n_total, y_hbm.shape[0] // rows, body, 0)

    @pl.when(e == 0)
    def _():
        start_gather(0, 0)
        zero_ref[...] = jnp.zeros(zero_ref.shape, _F32)
        for_tail_blocks(lambda copy: copy.start())

    @pl.when(stop > first)
    def _():
        wg_bf[...] = wg_ref[0, 0].astype(_BF16)
        wu_bf[...] = wu_ref[0, 0].astype(_BF16)
        wd_bf[...] = wd_ref[0, 0].astype(_BF16)

    def block_step(block, carry):
        s = block % 2
        start_gather(jnp.minimum(block + 1, n_total - 1), 1 - s)
        wait_gather(s)
        xb = xs_ref[s].astype(_BF16)
        act = jax.nn.silu(_bf16_dot(xb, wg_bf[...])) * _bf16_dot(xb, wu_bf[...])
        y = _bf16_dot(act.astype(_BF16), wd_bf[...])

        @pl.when(block >= 2)
        def _():
            out_copy(block - 2, s).wait()

        ys_ref[s] = y
        out_copy(block, s).start()
        return carry

    lax.fori_loop(first, stop, block_step, 0)

    @pl.when(e == n_experts - 1)
    def _():
        wait_gather(n_total % 2)
        for_tail_blocks(lambda copy: copy.wait())
        for back in (1, 2):
            @pl.when(n_total >= back)
            def _():
                out_copy(n_total - back, (n_total - back) % 2).wait()


def _expert_mlp(bounds, row_tok, hn, layer, w_gate, w_up, w_down):
    d = hn.shape[1]
    n_experts, de = w_gate.shape[1], w_gate.shape[3]
    rows = EXPERT_ROWS
    grid_spec = pltpu.PrefetchScalarGridSpec(
        num_scalar_prefetch=2,
        grid=(n_experts,),
        in_specs=[
            pl.BlockSpec(memory_space=pl.ANY),
            pl.BlockSpec((1, 1, d, de), lambda e, bd, rt: (layer, e, 0, 0)),
            pl.BlockSpec((1, 1, d, de), lambda e, bd, rt: (layer, e, 0, 0)),
            pl.BlockSpec((1, 1, de, d), lambda e, bd, rt: (layer, e, 0, 0)),
        ],
        out_specs=pl.BlockSpec(memory_space=pl.ANY),
        scratch_shapes=[
            pltpu.VMEM((2, rows, d), _F32),
            pltpu.VMEM((2, rows, d), _F32),
            pltpu.VMEM((rows, d), _F32),
            pltpu.VMEM((d, de), _BF16),
            pltpu.VMEM((d, de), _BF16),
            pltpu.VMEM((de, d), _BF16),
            pltpu.SemaphoreType.DMA((2,)),
            pltpu.SemaphoreType.DMA((2,)),
            pltpu.SemaphoreType.DMA((1,)),
        ],
    )
    return pl.pallas_call(
        _expert_body,
        grid_spec=grid_spec,
        out_shape=jax.ShapeDtypeStruct((row_tok.shape[0], d), _F32),
        compiler_params=_params("arbitrary"),
        name="expert_mlp",
    )(bounds, row_tok, hn, w_gate, w_up, w_down)


def _combine_body(pos_ref, y_hbm, wts_ref, h_ref, o_ref, ys_ref, sem):
    i = pl.program_id(0)
    n_steps = pl.num_programs(0)
    tm = h_ref.shape[0]
    slot = i % 2

    def row_copy(p, s, k, r):
        return pltpu.make_async_copy(y_hbm.at[pl.ds(p, 1)], ys_ref.at[s, k, pl.ds(r, 1)], sem.at[s])

    def start_gather(tile, s):
        for r in range(tm):
            for k in range(EXPERT_TOPK):
                row_copy(pos_ref[(tile * tm + r) * EXPERT_TOPK + k], s, k, r).start(priority=k)

    def wait_gather(s):
        for k in range(EXPERT_TOPK):
            pltpu.make_async_copy(y_hbm.at[pl.ds(0, tm)], ys_ref.at[s, k], sem.at[s]).wait()

    @pl.when(i == 0)
    def _():
        start_gather(0, 0)

    @pl.when(i + 1 < n_steps)
    def _():
        start_gather(i + 1, 1 - slot)

    wait_gather(slot)
    w = wts_ref[...]
    o_ref[...] = h_ref[...] + (w[:, 0:1] * ys_ref[slot, 0] + w[:, 1:2] * ys_ref[slot, 1])


def _combine(pos, y, wts, h):
    n, d = h.shape
    tm = COMBINE_ROWS
    grid_spec = pltpu.PrefetchScalarGridSpec(
        num_scalar_prefetch=1,
        grid=(n // tm,),
        in_specs=[
            pl.BlockSpec(memory_space=pl.ANY),
            pl.BlockSpec((tm, LANES), lambda i, p: (i, 0)),
            pl.BlockSpec((tm, d), lambda i, p: (i, 0)),
        ],
        out_specs=pl.BlockSpec((tm, d), lambda i, p: (i, 0)),
        scratch_shapes=[
            pltpu.VMEM((2, EXPERT_TOPK, tm, d), _F32),
            pltpu.SemaphoreType.DMA((2,)),
        ],
    )
    return pl.pallas_call(
        _combine_body,
        grid_spec=grid_spec,
        out_shape=jax.ShapeDtypeStruct((n, d), _F32),
        compiler_params=_params("arbitrary"),
        name="combine",
    )(pos, y, wts, h)


def _dispatch_plan(ids, counts):
    n = ids.shape[0]
    rows = EXPERT_ROWS
    m = n * EXPERT_TOPK
    n_blocks = -(-(m + N_EXPERTS * (rows - 1)) // rows)
    flat_e = ids[:, :EXPERT_TOPK].reshape(m)
    rank = ids[:, EXPERT_TOPK:2 * EXPERT_TOPK].reshape(m)
    counts = counts[0, :N_EXPERTS]
    starts = jnp.cumsum(counts) - counts
    padded = (counts + rows - 1) // rows * rows
    pends = jnp.cumsum(padded)
    pstarts = pends - padded
    expert_iota = jnp.arange(N_EXPERTS, dtype=jnp.int32)
    pos = rank + jnp.sum(jnp.where(flat_e[:, None] == expert_iota[None, :], pstarts[None, :], 0), axis=1)
    blk_start = jnp.arange(n_blocks, dtype=jnp.int32) * rows
    blk_e = jnp.minimum(jnp.sum(pends[None, :] <= blk_start[:, None], axis=1), N_EXPERTS - 1)
    blk_e = blk_e.astype(jnp.int32)
    order = jnp.argsort(flat_e)
    sorted_idx = (blk_start - pstarts[blk_e] + starts[blk_e])[:, None] + jnp.arange(rows)[None, :]
    valid = sorted_idx < (starts + counts)[blk_e][:, None]
    row_tok = jnp.where(valid, order[jnp.clip(sorted_idx, 0, m - 1)] // EXPERT_TOPK, 0)
    bounds = jnp.concatenate([pstarts, pends[-1:]]) // rows
    return (bounds.astype(jnp.int32), row_tok.reshape(n_blocks * rows).astype(jnp.int32),
            pos.astype(jnp.int32))


def _hier_moe_residual(h, gain, w_rg, w_re, layer, w_gate, w_up, w_down):
    hn, ids, wts, counts = _router(h, gain, w_rg, w_re)
    bounds, row_tok, pos = _dispatch_plan(ids, counts)
    y = _expert_mlp(bounds, row_tok, hn, layer, w_gate, w_up, w_down)
    return _combine(pos, y, wts, h)


def kernel(x, mix_norm, ffn_norm, conv_w_in, conv_w, conv_w_out, kv_norm, w_kv, k_norm, w_q, q_norm,
           w_o, router_group, router_expert, w_gate, w_up, w_down):
    bsz, seq_len, d = x.shape
    depth = mix_norm.shape[0]
    n_conv = conv_w_in.shape[0]
    assert d == N_HEADS * HEAD_DIM
    h = x.reshape(bsz * seq_len, d)
    k_sh = v_sh = None
    for l in range(depth):
        if l < n_conv:
            z = _conv_in(h, mix_norm[l], conv_w_in[l], conv_w[l], seq_len)
            h = _matmul_residual(z, conv_w_out[l], h)
        else:
            j = l - n_conv
            if j == 0:
                k_sh = _norm_matmul(h, kv_norm, w_kv, 0, d, head_gain=k_norm)
                v_sh = _norm_matmul(h, kv_norm, w_kv, d, d)
            q = _norm_matmul(h, mix_norm[l], w_q[j], 0, d, head_gain=q_norm[j])
            attn = _moba_attention(q, k_sh, v_sh, bsz, seq_len)
            h = _matmul_residual(attn, w_o[j], h)
        h = _hier_moe_residual(h, ffn_norm[l], router_group[l], router_expert[l],
                               l, w_gate, w_up, w_down)
    return h.reshape(bsz, seq_len, d)
```

```python
import functools

import jax
import jax.numpy as jnp
from jax import lax
from jax.experimental import pallas as pl
from jax.experimental.pallas import tpu as pltpu

N_HEADS = 16
HEAD_DIM = 128
CONV_WIDTH = 3
MOBA_BLOCK = 256
MOBA_TOPK = 3
N_GROUPS = 4
EXPERTS_PER_GROUP = 8
N_EXPERTS = N_GROUPS * EXPERTS_PER_GROUP
EXPERT_TOPK = 2
NORM_EPS = 1e-6
ALIBI_MAX_BIAS = 8.0

LANES = 128
SUBLANES = 8
VMEM_LIMIT = 56 * 1024 * 1024

ROW_TILE = 1024
COL_TILE = 512
CONV_COL_TILE = 256
ROUTER_ROWS = 512
EXPERT_ROWS = 128
EXPERT_SLOTS = 4
COMBINE_ROWS = 256

_BF16 = jnp.bfloat16
_F32 = jnp.float32


def _params(*semantics):
    return pltpu.CompilerParams(dimension_semantics=semantics, vmem_limit_bytes=VMEM_LIMIT)


def _rms_normalise(x, gain):
    ms = jnp.mean(x * x, axis=-1, keepdims=True)
    return x * lax.rsqrt(ms + NORM_EPS) * gain


def _bf16_dot(a, b):
    return jnp.dot(a, b, preferred_element_type=_F32)


def _norm_matmul_body(x_ref, g_ref, w_ref, *rest, head_norm):
    if head_norm:
        hg_ref, o_ref, xn_ref = rest
    else:
        o_ref, xn_ref = rest

    @pl.when(pl.program_id(1) == 0)
    def _():
        xn_ref[...] = _rms_normalise(x_ref[...], g_ref[...]).astype(_BF16)

    y = _bf16_dot(xn_ref[...], w_ref[...].astype(_BF16))
    if head_norm:
        for hh in range(y.shape[1] // HEAD_DIM):
            cols = slice(hh * HEAD_DIM, (hh + 1) * HEAD_DIM)
            o_ref[:, cols] = _rms_normalise(y[:, cols], hg_ref[...]).astype(o_ref.dtype)
    else:
        o_ref[...] = y.astype(o_ref.dtype)


def _norm_matmul(x, gain, w, col_off, n_out, head_gain=None):
    n, d = x.shape
    tm, tn = ROW_TILE, COL_TILE
    off = col_off // tn
    in_specs = [
        pl.BlockSpec((tm, d), lambda i, j: (i, 0)),
        pl.BlockSpec((1, d), lambda i, j: (0, 0)),
        pl.BlockSpec((d, tn), lambda i, j: (0, j + off)),
    ]
    args = [x, gain.reshape(1, d), w]
    if head_gain is not None:
        in_specs.append(pl.BlockSpec((1, HEAD_DIM), lambda i, j: (0, 0)))
        args.append(head_gain.reshape(1, HEAD_DIM))
    return pl.pallas_call(
        functools.partial(_norm_matmul_body, head_norm=head_gain is not None),
        grid=(n // tm, n_out // tn),
        in_specs=in_specs,
        out_specs=pl.BlockSpec((tm, tn), lambda i, j: (i, j)),
        out_shape=jax.ShapeDtypeStruct((n, n_out), _BF16),
        scratch_shapes=[pltpu.VMEM((tm, d), _BF16)],
        compiler_params=_params("arbitrary", "arbitrary"),
        name="norm_matmul",
    )(*args)


def _conv_in_body(x_ref, g_ref, wb_ref, wc_ref, wx_ref, cw_ref, z_ref, xn_ref, carry_ref,
                  *, tiles_per_seq):
    i = pl.program_id(0)
    j = pl.program_id(1)
    tm = x_ref.shape[0]

    @pl.when(j == 0)
    def _():
        xn_ref[...] = _rms_normalise(x_ref[...], g_ref[...]).astype(_BF16)

    @pl.when(i % tiles_per_seq == 0)
    def _():
        carry_ref[j] = jnp.zeros(carry_ref.shape[1:], _F32)

    xn = xn_ref[...]
    b_gate = _bf16_dot(xn, wb_ref[...].astype(_BF16))
    c_gate = _bf16_dot(xn, wc_ref[...].astype(_BF16))
    xh = _bf16_dot(xn, wx_ref[...].astype(_BF16))
    u = c_gate * xh
    prev = carry_ref[j]
    row = lax.broadcasted_iota(jnp.int32, u.shape, 0)
    last = prev[SUBLANES - 1:SUBLANES]
    u1 = jnp.where(row == 0, last, pltpu.roll(u, 1, axis=0))
    u2 = jnp.where(row == 0, prev[SUBLANES - 2:SUBLANES - 1],
                   jnp.where(row == 1, last, pltpu.roll(u, 2, axis=0)))
    cw = cw_ref[...]
    conv = cw[0:1] * u2 + cw[1:2] * u1 + cw[2:3] * u
    z_ref[...] = (b_gate * conv).astype(z_ref.dtype)
    carry_ref[j] = u[tm - SUBLANES:tm]


def _conv_in(x, gain, w_in, conv_w, seq_len):
    n, d = x.shape
    tm, tn = ROW_TILE, CONV_COL_TILE
    nj = d // tn
    assert CONV_WIDTH - 1 <= SUBLANES and seq_len % tm == 0
    return pl.pallas_call(
        functools.partial(_conv_in_body, tiles_per_seq=seq_len // tm),
        grid=(n // tm, nj),
        in_specs=[
            pl.BlockSpec((tm, d), lambda i, j: (i, 0)),
            pl.BlockSpec((1, d), lambda i, j: (0, 0)),
            pl.BlockSpec((d, tn), lambda i, j: (0, j)),
            pl.BlockSpec((d, tn), lambda i, j: (0, j + nj)),
            pl.BlockSpec((d, tn), lambda i, j: (0, j + 2 * nj)),
            pl.BlockSpec((CONV_WIDTH, tn), lambda i, j: (0, j)),
        ],
        out_specs=pl.BlockSpec((tm, tn), lambda i, j: (i, j)),
        out_shape=jax.ShapeDtypeStruct((n, d), _BF16),
        scratch_shapes=[pltpu.VMEM((tm, d), _BF16), pltpu.VMEM((nj, SUBLANES, tn), _F32)],
        compiler_params=_params("arbitrary", "arbitrary"),
        name="conv_in",
    )(x, gain.reshape(1, d), w_in, w_in, w_in, conv_w)


def _matmul_residual_body(x_ref, w_ref, r_ref, o_ref):
    o_ref[...] = r_ref[...] + _bf16_dot(x_ref[...], w_ref[...].astype(_BF16))


def _matmul_residual(x, w, res):
    n, k = x.shape
    d = w.shape[1]
    tm, tn = ROW_TILE, COL_TILE
    return pl.pallas_call(
        _matmul_residual_body,
        grid=(n // tm, d // tn),
        in_specs=[
            pl.BlockSpec((tm, k), lambda i, j: (i, 0)),
            pl.BlockSpec((k, tn), lambda i, j: (0, j)),
            pl.BlockSpec((tm, tn), lambda i, j: (i, j)),
        ],
        out_specs=pl.BlockSpec((tm, tn), lambda i, j: (i, j)),
        out_shape=jax.ShapeDtypeStruct((n, d), _F32),
        compiler_params=_params("arbitrary", "arbitrary"),
        name="matmul_residual",
    )(x, w, res)


def _moba_body(q_ref, k_ref, v_ref, o_ref, km_ref, vt_ref, bias_ref):
    head = pl.program_id(1)
    i = pl.program_id(2)
    kb = MOBA_BLOCK
    n_blocks = k_ref.shape[0] // kb
    nt = (((1,), (1,)), ((), ()))
    log2e = 1.4426950408889634
    scale = HEAD_DIM ** -0.5 * log2e
    slope = log2e * jnp.exp2(
        jnp.full((1, kb), -ALIBI_MAX_BIAS / N_HEADS, _F32) * (head + 1).astype(_F32))

    @pl.when(i == 0)
    def _():
        for jb in range(n_blocks):
            rows = slice(jb * kb, (jb + 1) * kb)
            km_ref[jb:jb + 1, :] = jnp.mean(k_ref[rows, :].astype(_F32), axis=0, keepdims=True)
            vt_ref[:, rows] = v_ref[rows, :].astype(_F32).T.astype(_BF16)
        rel = (lax.broadcasted_iota(jnp.int32, (kb, kb), 1)
               - lax.broadcasted_iota(jnp.int32, (kb, kb), 0))
        bias = slope * rel.astype(_F32)
        bias_ref[0] = bias
        bias_ref[1] = jnp.where(rel >= 0, bias, jnp.inf)

    q = q_ref[...]
    gate = lax.dot_general(km_ref[...], q.astype(_F32), nt,
                           precision=lax.Precision.HIGHEST, preferred_element_type=_F32)
    blk = lax.broadcasted_iota(jnp.int32, gate.shape, 0)
    gate = jnp.where(blk < i, gate, -jnp.inf)
    sel = jnp.zeros(gate.shape, _F32)
    for jb in range(n_blocks - 1):
        g_jb = gate[jb:jb + 1, :]
        beats = (gate > g_jb) | ((gate == g_jb) & (blk < jb))
        n_beat = jnp.sum(beats.astype(_F32), axis=0, keepdims=True)
        sel = jnp.where((blk == jb) & (blk < i) & (n_beat < MOBA_TOPK), 1.0, sel)

    def attend(own):
        nb = own + 1
        raw_all = lax.dot_general(k_ref[0:nb * kb, :], q, nt, preferred_element_type=_F32)
        tiles, shifts, picked, col_max = [], [], [], []
        for jb in range(nb):
            tile = raw_all[jb * kb:(jb + 1) * kb, :] * scale - bias_ref[1 if jb == own else 0]
            mx = jnp.max(tile, axis=0, keepdims=True)
            if jb == own:
                shift, pick = None, None
            else:
                shift = slope * float((own - jb) * kb)
                pick = sel[jb:jb + 1, :] > 0.5
                mx = jnp.where(pick, mx - shift, -jnp.inf)
            tiles.append(tile), shifts.append(shift), picked.append(pick), col_max.append(mx)
        m = functools.reduce(jnp.maximum, col_max)
        probs, l = [], None
        for jb in range(nb):
            offset = m if jb == own else jnp.where(picked[jb], m + shifts[jb], jnp.inf)
            p = jnp.exp2(tiles[jb] - offset)
            p_sum = jnp.sum(p, axis=0, keepdims=True)
            l = p_sum if l is None else l + p_sum
            probs.append(p.astype(_BF16))
        p_all = probs[0] if nb == 1 else jnp.concatenate(probs, axis=0)
        acc = _bf16_dot(vt_ref[:, 0:nb * kb], p_all)
        o_ref[...] = (acc / l).T.astype(o_ref.dtype)

    for own in range(n_blocks):
        pl.when(i == own)(functools.partial(attend, own))


def _moba_attention(q, k, v, batch, seq_len):
    n, d = q.shape
    kb = MOBA_BLOCK
    nqb = seq_len // kb
    assert seq_len % kb == 0 and nqb == SUBLANES
    return pl.pallas_call(
        _moba_body,
        grid=(batch, N_HEADS, nqb),
        in_specs=[
            pl.BlockSpec((kb, HEAD_DIM), lambda b, h, i: (b * nqb + i, h)),
            pl.BlockSpec((seq_len, HEAD_DIM), lambda b, h, i: (b, h)),
            pl.BlockSpec((seq_len, HEAD_DIM), lambda b, h, i: (b, h)),
        ],
        out_specs=pl.BlockSpec((kb, HEAD_DIM), lambda b, h, i: (b * nqb + i, h)),
        out_shape=jax.ShapeDtypeStruct((n, d), _BF16),
        scratch_shapes=[pltpu.VMEM((nqb, HEAD_DIM), _F32),
                        pltpu.VMEM((HEAD_DIM, seq_len), _BF16),
                        pltpu.VMEM((2, kb, kb), _F32)],
        compiler_params=_params("arbitrary", "arbitrary", "arbitrary"),
        name="moba_attention",
    )(q, k, v)


def _router_body(x_ref, g_ref, wr_ref, hn_ref, ids_ref, wts_ref, cnt_ref, run_ref):
    step = pl.program_id(0)
    hn = _rms_normalise(x_ref[...], g_ref[...])
    hn_ref[...] = hn
    logits = jnp.dot(hn, wr_ref[...], precision=lax.Precision.HIGHEST, preferred_element_type=_F32)
    lane = lax.broadcasted_iota(jnp.int32, logits.shape, 1)
    far = jnp.int32(LANES)

    def first_lane(mask):
        return jnp.min(jnp.where(mask, lane, far), axis=1, keepdims=True)

    is_group = (lane >= N_EXPERTS) & (lane < N_EXPERTS + N_GROUPS)
    g_max = jnp.max(jnp.where(is_group, logits, -jnp.inf), axis=1, keepdims=True)
    g_sel = first_lane(is_group & (logits == g_max)) - N_EXPERTS
    g_w = 1.0 / jnp.sum(jnp.where(is_group, jnp.exp(logits - g_max), 0.0), axis=1, keepdims=True)

    in_group = (lane >= g_sel * EXPERTS_PER_GROUP) & (lane < (g_sel + 1) * EXPERTS_PER_GROUP)
    top1 = jnp.max(jnp.where(in_group, logits, -jnp.inf), axis=1, keepdims=True)
    idx1 = first_lane(in_group & (logits == top1))
    rest = in_group & (lane != idx1)
    top2 = jnp.max(jnp.where(rest, logits, -jnp.inf), axis=1, keepdims=True)
    idx2 = first_lane(rest & (logits == top2))
    t = jnp.exp(top2 - top1)
    w1 = 1.0 / (1.0 + t) * g_w
    w2 = t / (1.0 + t) * g_w

    @pl.when(step == 0)
    def _():
        run_ref[...] = jnp.zeros(run_ref.shape, _F32)

    tm = logits.shape[0]
    chosen = (lane == idx1) | (lane == idx2)
    earlier = (lax.broadcasted_iota(jnp.int32, (tm, tm), 1)
               < lax.broadcasted_iota(jnp.int32, (tm, tm), 0))
    before = run_ref[...] + _bf16_dot(jnp.where(earlier, 1.0, 0.0).astype(_BF16),
                                      jnp.where(chosen, 1.0, 0.0).astype(_BF16))
    rank1 = jnp.sum(jnp.where(lane == idx1, before, 0.0), axis=1, keepdims=True).astype(jnp.int32)
    rank2 = jnp.sum(jnp.where(lane == idx2, before, 0.0), axis=1, keepdims=True).astype(jnp.int32)
    run_ref[...] += jnp.sum(jnp.where(chosen, 1.0, 0.0), axis=0, keepdims=True)

    ids_ref[...] = jnp.where(lane == 0, idx1, jnp.where(lane == 1, idx2,
                             jnp.where(lane == 2, rank1, jnp.where(lane == 3, rank2, 0))))
    wts_ref[...] = jnp.where(lane == 0, w1, jnp.where(lane == 1, w2, 0.0))
    cnt_ref[...] = jnp.broadcast_to(run_ref[...], cnt_ref.shape).astype(jnp.int32)


def _router(x, gain, w_rg, w_re):
    n, d = x.shape
    tm = ROUTER_ROWS
    assert EXPERT_TOPK == 2 and N_EXPERTS + N_GROUPS <= LANES
    wr = jnp.concatenate(
        [w_re, w_rg, jnp.zeros((d, LANES - N_EXPERTS - N_GROUPS), _F32)], axis=1)
    row = lambda i: (i, 0)
    fixed = lambda i: (0, 0)
    return pl.pallas_call(
        _router_body,
        grid=(n // tm,),
        in_specs=[
            pl.BlockSpec((tm, d), row),
            pl.BlockSpec((1, d), fixed),
            pl.BlockSpec((d, LANES), fixed),
        ],
        out_specs=[pl.BlockSpec((tm, d), row), pl.BlockSpec((tm, LANES), row),
                   pl.BlockSpec((tm, LANES), row), pl.BlockSpec((SUBLANES, LANES), fixed)],
        out_shape=[jax.ShapeDtypeStruct((n, d), _F32),
                   jax.ShapeDtypeStruct((n, LANES), jnp.int32),
                   jax.ShapeDtypeStruct((n, LANES), _F32),
                   jax.ShapeDtypeStruct((SUBLANES, LANES), jnp.int32)],
        scratch_shapes=[pltpu.VMEM((1, LANES), _F32)],
        compiler_params=_params("arbitrary"),
        name="router",
    )(x, gain.reshape(1, d), wr)


def _expert_body(bounds_ref, row_tok_ref, hn_hbm, wg_ref, wu_ref, wd_ref, y_hbm,
                 xs_ref, ys_ref, zero_ref, wg_bf, wu_bf, wd_bf, gather_sem, out_sem, tail_sem):
    e = pl.program_id(0)
    n_experts = pl.num_programs(0)
    n_slots, rows = xs_ref.shape[0], xs_ref.shape[1]
    ahead = n_slots - 1
    first, stop = bounds_ref[e], bounds_ref[e + 1]
    n_total = bounds_ref[n_experts]

    def row_copy(tok, s, r):
        return pltpu.make_async_copy(hn_hbm.at[pl.ds(tok, 1)], xs_ref.at[s, pl.ds(r, 1)],
                                     gather_sem.at[s])

    def start_gather(block, unrolled):
        s = block % n_slots
        base = jnp.minimum(block, n_total - 1) * rows
        if unrolled:
            for r in range(rows):
                row_copy(row_tok_ref[base + r], s, r).start(priority=1)
        else:
            def body(r, carry):
                row_copy(row_tok_ref[base + r], s, r).start(priority=1)
                return carry
            lax.fori_loop(0, rows, body, 0)

    def wait_gather(block):
        s = block % n_slots
        pltpu.make_async_copy(hn_hbm.at[pl.ds(0, rows)], xs_ref.at[s], gather_sem.at[s]).wait()

    def out_copy(block):
        s = block % n_slots
        dst = y_hbm.at[pl.ds(pl.multiple_of(block * rows, rows), rows)]
        return pltpu.make_async_copy(ys_ref.at[s], dst, out_sem.at[s])

    def tail_copy(block):
        dst = y_hbm.at[pl.ds(pl.multiple_of(block * rows, rows), rows)]
        return pltpu.make_async_copy(zero_ref, dst, tail_sem.at[0])

    def for_tail_blocks(fn):
        def body(block, carry):
            fn(tail_copy(block))
            return carry
        lax.fori_loop(n_total, y_hbm.shape[0] // rows, body, 0)

    @pl.when(e == 0)
    def _():
        for block in range(ahead):
            start_gather(block, unrolled=False)
        zero_ref[...] = jnp.zeros(zero_ref.shape, _F32)
        for_tail_blocks(lambda copy: copy.start())

    @pl.when(stop > first)
    def _():
        wg_bf[...] = wg_ref[0, 0].astype(_BF16)
        wu_bf[...] = wu_ref[0, 0].astype(_BF16)
        wd_bf[...] = wd_ref[0, 0].astype(_BF16)

    def block_step(block, carry):
        start_gather(block + ahead, unrolled=True)
        wait_gather(block)
        xb = xs_ref[block % n_slots].astype(_BF16)
        act = jax.nn.silu(_bf16_dot(xb, wg_bf[...])) * _bf16_dot(xb, wu_bf[...])
        y = _bf16_dot(act.astype(_BF16), wd_bf[...])

        @pl.when(block >= n_slots)
        def _():
            out_copy(block - n_slots).wait()

        ys_ref[block % n_slots] = y
        out_copy(block).start()
        return carry

    lax.fori_loop(first, stop, block_step, 0)

    @pl.when(e == n_experts - 1)
    def _():
        for extra in range(ahead):
            wait_gather(n_total + extra)
        for_tail_blocks(lambda copy: copy.wait())
        for back in range(1, n_slots + 1):
            @pl.when(n_total >= back)
            def _():
                out_copy(n_total - back).wait()


def _expert_mlp(bounds, row_tok, hn, layer, w_gate, w_up, w_down):
    d = hn.shape[1]
    n_experts, de = w_gate.shape[1], w_gate.shape[3]
    rows = EXPERT_ROWS
    grid_spec = pltpu.PrefetchScalarGridSpec(
        num_scalar_prefetch=2,
        grid=(n_experts,),
        in_specs=[
            pl.BlockSpec(memory_space=pl.ANY),
            pl.BlockSpec((1, 1, d, de), lambda e, bd, rt: (layer, e, 0, 0)),
            pl.BlockSpec((1, 1, d, de), lambda e, bd, rt: (layer, e, 0, 0)),
            pl.BlockSpec((1, 1, de, d), lambda e, bd, rt: (layer, e, 0, 0)),
        ],
        out_specs=pl.BlockSpec(memory_space=pl.ANY),
        scratch_shapes=[
            pltpu.VMEM((EXPERT_SLOTS, rows, d), _F32),
            pltpu.VMEM((EXPERT_SLOTS, rows, d), _F32),
            pltpu.VMEM((rows, d), _F32),
            pltpu.VMEM((d, de), _BF16),
            pltpu.VMEM((d, de), _BF16),
            pltpu.VMEM((de, d), _BF16),
            pltpu.SemaphoreType.DMA((EXPERT_SLOTS,)),
            pltpu.SemaphoreType.DMA((EXPERT_SLOTS,)),
            pltpu.SemaphoreType.DMA((1,)),
        ],
    )
    return pl.pallas_call(
        _expert_body,
        grid_spec=grid_spec,
        out_shape=jax.ShapeDtypeStruct((row_tok.shape[0], d), _F32),
        compiler_params=_params("arbitrary"),
        name="expert_mlp",
    )(bounds, row_tok, hn, w_gate, w_up, w_down)


def _combine_body(pos_ref, y_hbm, wts_ref, h_ref, o_ref, ys_ref, sem):
    i = pl.program_id(0)
    n_steps = pl.num_programs(0)
    tm = h_ref.shape[0]
    slot = i % 2

    def row_copy(p, s, k, r):
        return pltpu.make_async_copy(y_hbm.at[pl.ds(p, 1)], ys_ref.at[s, k, pl.ds(r, 1)], sem.at[s])

    def start_gather(tile, s):
        for r in range(tm):
            for k in range(EXPERT_TOPK):
                row_copy(pos_ref[(tile * tm + r) * EXPERT_TOPK + k], s, k, r).start(priority=k)

    def wait_gather(s):
        for k in range(EXPERT_TOPK):
            pltpu.make_async_copy(y_hbm.at[pl.ds(0, tm)], ys_ref.at[s, k], sem.at[s]).wait()

    @pl.when(i == 0)
    def _():
        start_gather(0, 0)

    @pl.when(i + 1 < n_steps)
    def _():
        start_gather(i + 1, 1 - slot)

    wait_gather(slot)
    w = wts_ref[...]
    o_ref[...] = h_ref[...] + (w[:, 0:1] * ys_ref[slot, 0] + w[:, 1:2] * ys_ref[slot, 1])


def _combine(pos, y, wts, h):
    n, d = h.shape
    tm = COMBINE_ROWS
    grid_spec = pltpu.PrefetchScalarGridSpec(
        num_scalar_prefetch=1,
        grid=(n // tm,),
        in_specs=[
            pl.BlockSpec(memory_space=pl.ANY),
            pl.BlockSpec((tm, LANES), lambda i, p: (i, 0)),
            pl.BlockSpec((tm, d), lambda i, p: (i, 0)),
        ],
        out_specs=pl.BlockSpec((tm, d), lambda i, p: (i, 0)),
        scratch_shapes=[
            pltpu.VMEM((2, EXPERT_TOPK, tm, d), _F32),
            pltpu.SemaphoreType.DMA((2,)),
        ],
    )
    return pl.pallas_call(
        _combine_body,
        grid_spec=grid_spec,
        out_shape=jax.ShapeDtypeStruct((n, d), _F32),
        compiler_params=_params("arbitrary"),
        name="combine",
    )(pos, y, wts, h)


def _dispatch_plan(ids, counts):
    n = ids.shape[0]
    rows = EXPERT_ROWS
    m = n * EXPERT_TOPK
    n_blocks = -(-(m + N_EXPERTS * (rows - 1)) // rows)
    flat_e = ids[:, :EXPERT_TOPK].reshape(m)
    rank = ids[:, EXPERT_TOPK:2 * EXPERT_TOPK].reshape(m)
    counts = counts[0, :N_EXPERTS]
    starts = jnp.cumsum(counts) - counts
    padded = (counts + rows - 1) // rows * rows
    pends = jnp.cumsum(padded)
    pstarts = pends - padded
    expert_iota = jnp.arange(N_EXPERTS, dtype=jnp.int32)
    pos = rank + jnp.sum(jnp.where(flat_e[:, None] == expert_iota[None, :], pstarts[None, :], 0), axis=1)
    blk_start = jnp.arange(n_blocks, dtype=jnp.int32) * rows
    blk_e = jnp.minimum(jnp.sum(pends[None, :] <= blk_start[:, None], axis=1), N_EXPERTS - 1)
    blk_e = blk_e.astype(jnp.int32)
    order = jnp.argsort(flat_e)
    sorted_idx = (blk_start - pstarts[blk_e] + starts[blk_e])[:, None] + jnp.arange(rows)[None, :]
    valid = sorted_idx < (starts + counts)[blk_e][:, None]
    row_tok = jnp.where(valid, order[jnp.clip(sorted_idx, 0, m - 1)] // EXPERT_TOPK, 0)
    bounds = jnp.concatenate([pstarts, pends[-1:]]) // rows
    return (bounds.astype(jnp.int32), row_tok.reshape(n_blocks * rows).astype(jnp.int32),
            pos.astype(jnp.int32))


def _hier_moe_residual(h, gain, w_rg, w_re, layer, w_gate, w_up, w_down):
    hn, ids, wts, counts = _router(h, gain, w_rg, w_re)
    bounds, row_tok, pos = _dispatch_plan(ids, counts)
    y = _expert_mlp(bounds, row_tok, hn, layer, w_gate, w_up, w_down)
    return _combine(pos, y, wts, h)


def kernel(x, mix_norm, ffn_norm, conv_w_in, conv_w, conv_w_out, kv_norm, w_kv, k_norm, w_q, q_norm,
           w_o, router_group, router_expert, w_gate, w_up, w_down):
    bsz, seq_len, d = x.shape
    depth = mix_norm.shape[0]
    n_conv = conv_w_in.shape[0]
    assert d == N_HEADS * HEAD_DIM
    h = x.reshape(bsz * seq_len, d)
    k_sh = v_sh = None
    for l in range(depth):
        if l < n_conv:
            z = _conv_in(h, mix_norm[l], conv_w_in[l], conv_w[l], seq_len)
            h = _matmul_residual(z, conv_w_out[l], h)
        else:
            j = l - n_conv
            if j == 0:
                k_sh = _norm_matmul(h, kv_norm, w_kv, 0, d, head_gain=k_norm)
                v_sh = _norm_matmul(h, kv_norm, w_kv, d, d)
            q = _norm_matmul(h, mix_norm[l], w_q[j], 0, d, head_gain=q_norm[j])
            attn = _moba_attention(q, k_sh, v_sh, bsz, seq_len)
            h = _matmul_residual(attn, w_o[j], h)
        h = _hier_moe_residual(h, ffn_norm[l], router_group[l], router_expert[l],
                               l, w_gate, w_up, w_down)
    return h.reshape(bsz, seq_len, d)
```

```python
import functools

import jax
import jax.numpy as jnp
from jax import lax
from jax.experimental import pallas as pl
from jax.experimental.pallas import tpu as pltpu

N_HEADS = 16
HEAD_DIM = 128
CONV_WIDTH = 3
MOBA_BLOCK = 256
MOBA_TOPK = 3
N_GROUPS = 4
EXPERTS_PER_GROUP = 8
N_EXPERTS = N_GROUPS * EXPERTS_PER_GROUP
EXPERT_TOPK = 2
NORM_EPS = 1e-6
ALIBI_MAX_BIAS = 8.0

LANES = 128
SUBLANES = 8
VMEM_LIMIT = 56 * 1024 * 1024

ROW_TILE = 1024
COL_TILE = 512
CONV_COL_TILE = 256
ROUTER_ROWS = 512
EXPERT_ROWS = 128
EXPERT_SLOTS = 4
COMBINE_ROWS = 256

_BF16 = jnp.bfloat16
_F32 = jnp.float32


def _params(*semantics):
    return pltpu.CompilerParams(dimension_semantics=semantics, vmem_limit_bytes=VMEM_LIMIT)


def _rms_normalise(x, gain):
    ms = jnp.mean(x * x, axis=-1, keepdims=True)
    return x * lax.rsqrt(ms + NORM_EPS) * gain


def _bf16_dot(a, b):
    return jnp.dot(a, b, preferred_element_type=_F32)


def _pack_bf16_halves(x):
    c = x.shape[1] // 2
    as_bits = lambda v: lax.bitcast_convert_type(v.astype(_BF16).astype(_F32), jnp.uint32)
    return (as_bits(x[:, :c]) >> 16) | (as_bits(x[:, c:]) & jnp.uint32(0xFFFF0000))


def _unpack_bf16_halves(p, dtype):
    lo = lax.bitcast_convert_type(p << 16, _F32).astype(dtype)
    hi = lax.bitcast_convert_type(p & jnp.uint32(0xFFFF0000), _F32).astype(dtype)
    return lo, hi


def _norm_matmul_body(x_ref, g_ref, w_ref, *rest, head_norm):
    if head_norm:
        hg_ref, o_ref, xn_ref = rest
    else:
        o_ref, xn_ref = rest

    @pl.when(pl.program_id(1) == 0)
    def _():
        xn_ref[...] = _rms_normalise(x_ref[...], g_ref[...]).astype(_BF16)

    y = _bf16_dot(xn_ref[...], w_ref[...].astype(_BF16))
    if head_norm:
        for hh in range(y.shape[1] // HEAD_DIM):
            cols = slice(hh * HEAD_DIM, (hh + 1) * HEAD_DIM)
            o_ref[:, cols] = _rms_normalise(y[:, cols], hg_ref[...]).astype(o_ref.dtype)
    else:
        o_ref[...] = y.astype(o_ref.dtype)


def _norm_matmul(x, gain, w, col_off, n_out, head_gain=None):
    n, d = x.shape
    tm, tn = ROW_TILE, COL_TILE
    off = col_off // tn
    in_specs = [
        pl.BlockSpec((tm, d), lambda i, j: (i, 0)),
        pl.BlockSpec((1, d), lambda i, j: (0, 0)),
        pl.BlockSpec((d, tn), lambda i, j: (0, j + off)),
    ]
    args = [x, gain.reshape(1, d), w]
    if head_gain is not None:
        in_specs.append(pl.BlockSpec((1, HEAD_DIM), lambda i, j: (0, 0)))
        args.append(head_gain.reshape(1, HEAD_DIM))
    return pl.pallas_call(
        functools.partial(_norm_matmul_body, head_norm=head_gain is not None),
        grid=(n // tm, n_out // tn),
        in_specs=in_specs,
        out_specs=pl.BlockSpec((tm, tn), lambda i, j: (i, j)),
        out_shape=jax.ShapeDtypeStruct((n, n_out), _BF16),
        scratch_shapes=[pltpu.VMEM((tm, d), _BF16)],
        compiler_params=_params("arbitrary", "arbitrary"),
        name="norm_matmul",
    )(*args)


def _conv_in_body(x_ref, g_ref, wb_ref, wc_ref, wx_ref, cw_ref, z_ref, xn_ref, carry_ref,
                  *, tiles_per_seq):
    i = pl.program_id(0)
    j = pl.program_id(1)
    tm = x_ref.shape[0]

    @pl.when(j == 0)
    def _():
        xn_ref[...] = _rms_normalise(x_ref[...], g_ref[...]).astype(_BF16)

    @pl.when(i % tiles_per_seq == 0)
    def _():
        carry_ref[j] = jnp.zeros(carry_ref.shape[1:], _F32)

    xn = xn_ref[...]
    b_gate = _bf16_dot(xn, wb_ref[...].astype(_BF16))
    c_gate = _bf16_dot(xn, wc_ref[...].astype(_BF16))
    xh = _bf16_dot(xn, wx_ref[...].astype(_BF16))
    u = c_gate * xh
    prev = carry_ref[j]
    row = lax.broadcasted_iota(jnp.int32, u.shape, 0)
    last = prev[SUBLANES - 1:SUBLANES]
    u1 = jnp.where(row == 0, last, pltpu.roll(u, 1, axis=0))
    u2 = jnp.where(row == 0, prev[SUBLANES - 2:SUBLANES - 1],
                   jnp.where(row == 1, last, pltpu.roll(u, 2, axis=0)))
    cw = cw_ref[...]
    conv = cw[0:1] * u2 + cw[1:2] * u1 + cw[2:3] * u
    z_ref[...] = (b_gate * conv).astype(z_ref.dtype)
    carry_ref[j] = u[tm - SUBLANES:tm]


def _conv_in(x, gain, w_in, conv_w, seq_len):
    n, d = x.shape
    tm, tn = ROW_TILE, CONV_COL_TILE
    nj = d // tn
    assert CONV_WIDTH - 1 <= SUBLANES and seq_len % tm == 0
    return pl.pallas_call(
        functools.partial(_conv_in_body, tiles_per_seq=seq_len // tm),
        grid=(n // tm, nj),
        in_specs=[
            pl.BlockSpec((tm, d), lambda i, j: (i, 0)),
            pl.BlockSpec((1, d), lambda i, j: (0, 0)),
            pl.BlockSpec((d, tn), lambda i, j: (0, j)),
            pl.BlockSpec((d, tn), lambda i, j: (0, j + nj)),
            pl.BlockSpec((d, tn), lambda i, j: (0, j + 2 * nj)),
            pl.BlockSpec((CONV_WIDTH, tn), lambda i, j: (0, j)),
        ],
        out_specs=pl.BlockSpec((tm, tn), lambda i, j: (i, j)),
        out_shape=jax.ShapeDtypeStruct((n, d), _BF16),
        scratch_shapes=[pltpu.VMEM((tm, d), _BF16), pltpu.VMEM((nj, SUBLANES, tn), _F32)],
        compiler_params=_params("arbitrary", "arbitrary"),
        name="conv_in",
    )(x, gain.reshape(1, d), w_in, w_in, w_in, conv_w)


def _matmul_residual_body(x_ref, w_ref, r_ref, o_ref):
    o_ref[...] = r_ref[...] + _bf16_dot(x_ref[...], w_ref[...].astype(_BF16))


def _matmul_residual(x, w, res):
    n, k = x.shape
    d = w.shape[1]
    tm, tn = ROW_TILE, COL_TILE
    return pl.pallas_call(
        _matmul_residual_body,
        grid=(n // tm, d // tn),
        in_specs=[
            pl.BlockSpec((tm, k), lambda i, j: (i, 0)),
            pl.BlockSpec((k, tn), lambda i, j: (0, j)),
            pl.BlockSpec((tm, tn), lambda i, j: (i, j)),
        ],
        out_specs=pl.BlockSpec((tm, tn), lambda i, j: (i, j)),
        out_shape=jax.ShapeDtypeStruct((n, d), _F32),
        compiler_params=_params("arbitrary", "arbitrary"),
        name="matmul_residual",
    )(x, w, res)


def _moba_body(q_ref, k_ref, v_ref, o_ref, km_ref, vt_ref, bias_ref):
    head = pl.program_id(1)
    i = pl.program_id(2)
    kb = MOBA_BLOCK
    n_blocks = k_ref.shape[0] // kb
    nt = (((1,), (1,)), ((), ()))
    log2e = 1.4426950408889634
    scale = HEAD_DIM ** -0.5 * log2e
    slope = log2e * jnp.exp2(
        jnp.full((1, kb), -ALIBI_MAX_BIAS / N_HEADS, _F32) * (head + 1).astype(_F32))

    @pl.when(i == 0)
    def _():
        for jb in range(n_blocks):
            rows = slice(jb * kb, (jb + 1) * kb)
            km_ref[jb:jb + 1, :] = jnp.mean(k_ref[rows, :].astype(_F32), axis=0, keepdims=True)
            vt_ref[:, rows] = v_ref[rows, :].astype(_F32).T.astype(_BF16)
        rel = (lax.broadcasted_iota(jnp.int32, (kb, kb), 1)
               - lax.broadcasted_iota(jnp.int32, (kb, kb), 0))
        bias = slope * rel.astype(_F32)
        bias_ref[0] = bias
        bias_ref[1] = jnp.where(rel >= 0, bias, jnp.inf)

    q = q_ref[...]
    gate = lax.dot_general(km_ref[...], q.astype(_F32), nt,
                           precision=lax.Precision.HIGHEST, preferred_element_type=_F32)
    blk = lax.broadcasted_iota(jnp.int32, gate.shape, 0)
    gate = jnp.where(blk < i, gate, -jnp.inf)
    sel = jnp.zeros(gate.shape, _F32)
    for jb in range(n_blocks - 1):
        g_jb = gate[jb:jb + 1, :]
        beats = (gate > g_jb) | ((gate == g_jb) & (blk < jb))
        n_beat = jnp.sum(beats.astype(_F32), axis=0, keepdims=True)
        sel = jnp.where((blk == jb) & (blk < i) & (n_beat < MOBA_TOPK), 1.0, sel)

    def attend(own):
        nb = own + 1
        raw_all = lax.dot_general(k_ref[0:nb * kb, :], q, nt, preferred_element_type=_F32)
        tiles, shifts, picked, col_max = [], [], [], []
        for jb in range(nb):
            tile = raw_all[jb * kb:(jb + 1) * kb, :] * scale - bias_ref[1 if jb == own else 0]
            mx = jnp.max(tile, axis=0, keepdims=True)
            if jb == own:
                shift, pick = None, None
            else:
                shift = slope * float((own - jb) * kb)
                pick = sel[jb:jb + 1, :] > 0.5
                mx = jnp.where(pick, mx - shift, -jnp.inf)
            tiles.append(tile), shifts.append(shift), picked.append(pick), col_max.append(mx)
        m = functools.reduce(jnp.maximum, col_max)
        probs, l = [], None
        for jb in range(nb):
            offset = m if jb == own else jnp.where(picked[jb], m + shifts[jb], jnp.inf)
            p = jnp.exp2(tiles[jb] - offset)
            p_sum = jnp.sum(p, axis=0, keepdims=True)
            l = p_sum if l is None else l + p_sum
            probs.append(p.astype(_BF16))
        p_all = probs[0] if nb == 1 else jnp.concatenate(probs, axis=0)
        acc = _bf16_dot(vt_ref[:, 0:nb * kb], p_all)
        o_ref[...] = (acc / l).T.astype(o_ref.dtype)

    for own in range(n_blocks):
        pl.when(i == own)(functools.partial(attend, own))


def _moba_attention(q, k, v, batch, seq_len):
    n, d = q.shape
    kb = MOBA_BLOCK
    nqb = seq_len // kb
    assert seq_len % kb == 0 and nqb == SUBLANES
    return pl.pallas_call(
        _moba_body,
        grid=(batch, N_HEADS, nqb),
        in_specs=[
            pl.BlockSpec((kb, HEAD_DIM), lambda b, h, i: (b * nqb + i, h)),
            pl.BlockSpec((seq_len, HEAD_DIM), lambda b, h, i: (b, h)),
            pl.BlockSpec((seq_len, HEAD_DIM), lambda b, h, i: (b, h)),
        ],
        out_specs=pl.BlockSpec((kb, HEAD_DIM), lambda b, h, i: (b * nqb + i, h)),
        out_shape=jax.ShapeDtypeStruct((n, d), _BF16),
        scratch_shapes=[pltpu.VMEM((nqb, HEAD_DIM), _F32),
                        pltpu.VMEM((HEAD_DIM, seq_len), _BF16),
                        pltpu.VMEM((2, kb, kb), _F32)],
        compiler_params=_params("arbitrary", "arbitrary", "arbitrary"),
        name="moba_attention",
    )(q, k, v)


def _router_body(x_ref, g_ref, wr_ref, hn_ref, ids_ref, wts_ref, cnt_ref, run_ref):
    step = pl.program_id(0)
    hn = _rms_normalise(x_ref[...], g_ref[...])
    hn_ref[...] = _pack_bf16_halves(hn)
    logits = jnp.dot(hn, wr_ref[...], precision=lax.Precision.HIGHEST, preferred_element_type=_F32)
    lane = lax.broadcasted_iota(jnp.int32, logits.shape, 1)
    far = jnp.int32(LANES)

    def first_lane(mask):
        return jnp.min(jnp.where(mask, lane, far), axis=1, keepdims=True)

    is_group = (lane >= N_EXPERTS) & (lane < N_EXPERTS + N_GROUPS)
    g_max = jnp.max(jnp.where(is_group, logits, -jnp.inf), axis=1, keepdims=True)
    g_sel = first_lane(is_group & (logits == g_max)) - N_EXPERTS
    g_w = 1.0 / jnp.sum(jnp.where(is_group, jnp.exp(logits - g_max), 0.0), axis=1, keepdims=True)

    in_group = (lane >= g_sel * EXPERTS_PER_GROUP) & (lane < (g_sel + 1) * EXPERTS_PER_GROUP)
    top1 = jnp.max(jnp.where(in_group, logits, -jnp.inf), axis=1, keepdims=True)
    idx1 = first_lane(in_group & (logits == top1))
    rest = in_group & (lane != idx1)
    top2 = jnp.max(jnp.where(rest, logits, -jnp.inf), axis=1, keepdims=True)
    idx2 = first_lane(rest & (logits == top2))
    t = jnp.exp(top2 - top1)
    w1 = 1.0 / (1.0 + t) * g_w
    w2 = t / (1.0 + t) * g_w

    @pl.when(step == 0)
    def _():
        run_ref[...] = jnp.zeros(run_ref.shape, _F32)

    tm = logits.shape[0]
    chosen = (lane == idx1) | (lane == idx2)
    earlier = (lax.broadcasted_iota(jnp.int32, (tm, tm), 1)
               < lax.broadcasted_iota(jnp.int32, (tm, tm), 0))
    before = run_ref[...] + _bf16_dot(jnp.where(earlier, 1.0, 0.0).astype(_BF16),
                                      jnp.where(chosen, 1.0, 0.0).astype(_BF16))
    rank1 = jnp.sum(jnp.where(lane == idx1, before, 0.0), axis=1, keepdims=True).astype(jnp.int32)
    rank2 = jnp.sum(jnp.where(lane == idx2, before, 0.0), axis=1, keepdims=True).astype(jnp.int32)
    run_ref[...] += jnp.sum(jnp.where(chosen, 1.0, 0.0), axis=0, keepdims=True)

    ids_ref[...] = jnp.where(lane == 0, idx1, jnp.where(lane == 1, idx2,
                             jnp.where(lane == 2, rank1, jnp.where(lane == 3, rank2, 0))))
    wts_ref[...] = jnp.where(lane == 0, w1, jnp.where(lane == 1, w2, 0.0))
    cnt_ref[...] = jnp.broadcast_to(run_ref[...], cnt_ref.shape).astype(jnp.int32)


def _router(x, gain, w_rg, w_re):
    n, d = x.shape
    tm = ROUTER_ROWS
    assert EXPERT_TOPK == 2 and N_EXPERTS + N_GROUPS <= LANES
    wr = jnp.concatenate(
        [w_re, w_rg, jnp.zeros((d, LANES - N_EXPERTS - N_GROUPS), _F32)], axis=1)
    row = lambda i: (i, 0)
    fixed = lambda i: (0, 0)
    return pl.pallas_call(
        _router_body,
        grid=(n // tm,),
        in_specs=[
            pl.BlockSpec((tm, d), row),
            pl.BlockSpec((1, d), fixed),
            pl.BlockSpec((d, LANES), fixed),
        ],
        out_specs=[pl.BlockSpec((tm, d // 2), row), pl.BlockSpec((tm, LANES), row),
                   pl.BlockSpec((tm, LANES), row), pl.BlockSpec((SUBLANES, LANES), fixed)],
        out_shape=[jax.ShapeDtypeStruct((n, d // 2), jnp.uint32),
                   jax.ShapeDtypeStruct((n, LANES), jnp.int32),
                   jax.ShapeDtypeStruct((n, LANES), _F32),
                   jax.ShapeDtypeStruct((SUBLANES, LANES), jnp.int32)],
        scratch_shapes=[pltpu.VMEM((1, LANES), _F32)],
        compiler_params=_params("arbitrary"),
        name="router",
    )(x, gain.reshape(1, d), wr)


def _expert_body(bounds_ref, row_tok_ref, hn_hbm, wg_ref, wu_ref, wd_ref, y_hbm,
                 xs_ref, ys_ref, zero_ref, wg_bf, wu_bf, wd_bf, gather_sem, out_sem, tail_sem):
    e = pl.program_id(0)
    n_experts = pl.num_programs(0)
    n_slots, rows = xs_ref.shape[0], xs_ref.shape[1]
    ahead = n_slots - 1
    first, stop = bounds_ref[e], bounds_ref[e + 1]
    n_total = bounds_ref[n_experts]

    def row_copy(tok, s, r):
        return pltpu.make_async_copy(hn_hbm.at[pl.ds(tok, 1)], xs_ref.at[s, pl.ds(r, 1)],
                                     gather_sem.at[s])

    def start_gather(block, unrolled):
        s = block % n_slots
        base = jnp.minimum(block, n_total - 1) * rows
        if unrolled:
            for r in range(rows):
                row_copy(row_tok_ref[base + r], s, r).start(priority=1)
        else:
            def body(r, carry):
                row_copy(row_tok_ref[base + r], s, r).start(priority=1)
                return carry
            lax.fori_loop(0, rows, body, 0)

    def wait_gather(block):
        s = block % n_slots
        pltpu.make_async_copy(hn_hbm.at[pl.ds(0, rows)], xs_ref.at[s], gather_sem.at[s]).wait()

    def out_copy(block):
        s = block % n_slots
        dst = y_hbm.at[pl.ds(pl.multiple_of(block * rows, rows), rows)]
        return pltpu.make_async_copy(ys_ref.at[s], dst, out_sem.at[s])

    def tail_copy(block):
        dst = y_hbm.at[pl.ds(pl.multiple_of(block * rows, rows), rows)]
        return pltpu.make_async_copy(zero_ref, dst, tail_sem.at[0])

    def for_tail_blocks(fn):
        def body(block, carry):
            fn(tail_copy(block))
            return carry
        lax.fori_loop(n_total, y_hbm.shape[0] // rows, body, 0)

    @pl.when(e == 0)
    def _():
        for block in range(ahead):
            start_gather(block, unrolled=False)
        zero_ref[...] = jnp.zeros(zero_ref.shape, zero_ref.dtype)
        for_tail_blocks(lambda copy: copy.start())

    @pl.when(stop > first)
    def _():
        wg_bf[...] = wg_ref[0, 0].astype(_BF16)
        wu_bf[...] = wu_ref[0, 0].astype(_BF16)
        wd_bf[...] = wd_ref[0, 0].astype(_BF16)

    def block_step(block, carry):
        start_gather(block + ahead, unrolled=True)
        wait_gather(block)
        x_lo, x_hi = _unpack_bf16_halves(xs_ref[block % n_slots], _BF16)
        half = x_lo.shape[1]

        def x_dot(w_bf):
            return _bf16_dot(x_lo, w_bf[:half, :]) + _bf16_dot(x_hi, w_bf[half:, :])

        act = jax.nn.silu(x_dot(wg_bf)) * x_dot(wu_bf)
        y = _bf16_dot(act.astype(_BF16), wd_bf[...])

        @pl.when(block >= n_slots)
        def _():
            out_copy(block - n_slots).wait()

        ys_ref[block % n_slots] = _pack_bf16_halves(y)
        out_copy(block).start()
        return carry

    lax.fori_loop(first, stop, block_step, 0)

    @pl.when(e == n_experts - 1)
    def _():
        for extra in range(ahead):
            wait_gather(n_total + extra)
        for_tail_blocks(lambda copy: copy.wait())
        for back in range(1, n_slots + 1):
            @pl.when(n_total >= back)
            def _():
                out_copy(n_total - back).wait()


def _expert_mlp(bounds, row_tok, hn, layer, w_gate, w_up, w_down):
    n_experts, d, de = w_gate.shape[1:]
    rows = EXPERT_ROWS
    assert hn.shape[1] * 2 == d and hn.dtype == jnp.uint32
    grid_spec = pltpu.PrefetchScalarGridSpec(
        num_scalar_prefetch=2,
        grid=(n_experts,),
        in_specs=[
            pl.BlockSpec(memory_space=pl.ANY),
            pl.BlockSpec((1, 1, d, de), lambda e, bd, rt: (layer, e, 0, 0)),
            pl.BlockSpec((1, 1, d, de), lambda e, bd, rt: (layer, e, 0, 0)),
            pl.BlockSpec((1, 1, de, d), lambda e, bd, rt: (layer, e, 0, 0)),
        ],
        out_specs=pl.BlockSpec(memory_space=pl.ANY),
        scratch_shapes=[
            pltpu.VMEM((EXPERT_SLOTS, rows, d // 2), jnp.uint32),
            pltpu.VMEM((EXPERT_SLOTS, rows, d // 2), jnp.uint32),
            pltpu.VMEM((rows, d // 2), jnp.uint32),
            pltpu.VMEM((d, de), _BF16),
            pltpu.VMEM((d, de), _BF16),
            pltpu.VMEM((de, d), _BF16),
            pltpu.SemaphoreType.DMA((EXPERT_SLOTS,)),
            pltpu.SemaphoreType.DMA((EXPERT_SLOTS,)),
            pltpu.SemaphoreType.DMA((1,)),
        ],
    )
    return pl.pallas_call(
        _expert_body,
        grid_spec=grid_spec,
        out_shape=jax.ShapeDtypeStruct((row_tok.shape[0], d // 2), jnp.uint32),
        compiler_params=_params("arbitrary"),
        name="expert_mlp",
    )(bounds, row_tok, hn, w_gate, w_up, w_down)


def _combine_body(pos_ref, y_hbm, wts_ref, h_ref, o_ref, ys_ref, sem):
    i = pl.program_id(0)
    n_steps = pl.num_programs(0)
    tm = h_ref.shape[0]
    slot = i % 2

    def row_copy(p, s, k, r):
        return pltpu.make_async_copy(y_hbm.at[pl.ds(p, 1)], ys_ref.at[s, k, pl.ds(r, 1)], sem.at[s])

    def start_gather(tile, s):
        for r in range(tm):
            for k in range(EXPERT_TOPK):
                row_copy(pos_ref[(tile * tm + r) * EXPERT_TOPK + k], s, k, r).start(priority=k)

    def wait_gather(s):
        for k in range(EXPERT_TOPK):
            pltpu.make_async_copy(y_hbm.at[pl.ds(0, tm)], ys_ref.at[s, k], sem.at[s]).wait()

    @pl.when(i == 0)
    def _():
        start_gather(0, 0)

    @pl.when(i + 1 < n_steps)
    def _():
        start_gather(i + 1, 1 - slot)

    wait_gather(slot)
    w = wts_ref[...]
    half = ys_ref.shape[3]
    a_lo, a_hi = _unpack_bf16_halves(ys_ref[slot, 0], _F32)
    b_lo, b_hi = _unpack_bf16_halves(ys_ref[slot, 1], _F32)
    o_ref[:, :half] = h_ref[:, :half] + (w[:, 0:1] * a_lo + w[:, 1:2] * b_lo)
    o_ref[:, half:] = h_ref[:, half:] + (w[:, 0:1] * a_hi + w[:, 1:2] * b_hi)


def _combine(pos, y, wts, h):
    n, d = h.shape
    tm = COMBINE_ROWS
    grid_spec = pltpu.PrefetchScalarGridSpec(
        num_scalar_prefetch=1,
        grid=(n // tm,),
        in_specs=[
            pl.BlockSpec(memory_space=pl.ANY),
            pl.BlockSpec((tm, LANES), lambda i, p: (i, 0)),
            pl.BlockSpec((tm, d), lambda i, p: (i, 0)),
        ],
        out_specs=pl.BlockSpec((tm, d), lambda i, p: (i, 0)),
        scratch_shapes=[
            pltpu.VMEM((2, EXPERT_TOPK, tm, d // 2), jnp.uint32),
            pltpu.SemaphoreType.DMA((2,)),
        ],
    )
    return pl.pallas_call(
        _combine_body,
        grid_spec=grid_spec,
        out_shape=jax.ShapeDtypeStruct((n, d), _F32),
        compiler_params=_params("arbitrary"),
        name="combine",
    )(pos, y, wts, h)


def _dispatch_plan(ids, counts):
    n = ids.shape[0]
    rows = EXPERT_ROWS
    m = n * EXPERT_TOPK
    n_blocks = -(-(m + N_EXPERTS * (rows - 1)) // rows)
    flat_e = ids[:, :EXPERT_TOPK].reshape(m)
    rank = ids[:, EXPERT_TOPK:2 * EXPERT_TOPK].reshape(m)
    counts = counts[0, :N_EXPERTS]
    starts = jnp.cumsum(counts) - counts
    padded = (counts + rows - 1) // rows * rows
    pends = jnp.cumsum(padded)
    pstarts = pends - padded
    expert_iota = jnp.arange(N_EXPERTS, dtype=jnp.int32)
    pos = rank + jnp.sum(jnp.where(flat_e[:, None] == expert_iota[None, :], pstarts[None, :], 0), axis=1)
    blk_start = jnp.arange(n_blocks, dtype=jnp.int32) * rows
    blk_e = jnp.minimum(jnp.sum(pends[None, :] <= blk_start[:, None], axis=1), N_EXPERTS - 1)
    blk_e = blk_e.astype(jnp.int32)
    order = jnp.argsort(flat_e)
    sorted_idx = (blk_start - pstarts[blk_e] + starts[blk_e])[:, None] + jnp.arange(rows)[None, :]
    valid = sorted_idx < (starts + counts)[blk_e][:, None]
    row_tok = jnp.where(valid, order[jnp.clip(sorted_idx, 0, m - 1)] // EXPERT_TOPK, 0)
    bounds = jnp.concatenate([pstarts, pends[-1:]]) // rows
    return (bounds.astype(jnp.int32), row_tok.reshape(n_blocks * rows).astype(jnp.int32),
            pos.astype(jnp.int32))


def _hier_moe_residual(h, gain, w_rg, w_re, layer, w_gate, w_up, w_down):
    hn, ids, wts, counts = _router(h, gain, w_rg, w_re)
    bounds, row_tok, pos = _dispatch_plan(ids, counts)
    y = _expert_mlp(bounds, row_tok, hn, layer, w_gate, w_up, w_down)
    return _combine(pos, y, wts, h)


def kernel(x, mix_norm, ffn_norm, conv_w_in, conv_w, conv_w_out, kv_norm, w_kv, k_norm, w_q, q_norm,
           w_o, router_group, router_expert, w_gate, w_up, w_down):
    bsz, seq_len, d = x.shape
    depth = mix_norm.shape[0]
    n_conv = conv_w_in.shape[0]
    assert d == N_HEADS * HEAD_DIM
    h = x.reshape(bsz * seq_len, d)
    k_sh = v_sh = None
    for l in range(depth):
        if l < n_conv:
            z = _conv_in(h, mix_norm[l], conv_w_in[l], conv_w[l], seq_len)
            h = _matmul_residual(z, conv_w_out[l], h)
        else:
            j = l - n_conv
            if j == 0:
                k_sh = _norm_matmul(h, kv_norm, w_kv, 0, d, head_gain=k_norm)
                v_sh = _norm_matmul(h, kv_norm, w_kv, d, d)
            q = _norm_matmul(h, mix_norm[l], w_q[j], 0, d, head_gain=q_norm[j])
            attn = _moba_attention(q, k_sh, v_sh, bsz, seq_len)
            h = _matmul_residual(attn, w_o[j], h)
        h = _hier_moe_residual(h, ffn_norm[l], router_group[l], router_expert[l],
                               l, w_gate, w_up, w_down)
    return h.reshape(bsz, seq_len, d)
```

```python
import functools

import jax
import jax.numpy as jnp
from jax import lax
from jax.experimental import pallas as pl
from jax.experimental.pallas import tpu as pltpu

N_HEADS = 16
HEAD_DIM = 128
CONV_WIDTH = 3
MOBA_BLOCK = 256
MOBA_TOPK = 3
N_GROUPS = 4
EXPERTS_PER_GROUP = 8
N_EXPERTS = N_GROUPS * EXPERTS_PER_GROUP
EXPERT_TOPK = 2
NORM_EPS = 1e-6
ALIBI_MAX_BIAS = 8.0

LANES = 128
SUBLANES = 8
VMEM_LIMIT = 56 * 1024 * 1024

ROW_TILE = 1024
COL_TILE = 512
CONV_COL_TILE = 256
ROUTER_ROWS = 512
EXPERT_ROWS = 128
EXPERT_SLOTS = 4
COMBINE_ROWS = 256
ATTN_HEADS = 4

_BF16 = jnp.bfloat16
_F32 = jnp.float32


def _params(*semantics):
    return pltpu.CompilerParams(dimension_semantics=semantics, vmem_limit_bytes=VMEM_LIMIT)


def _rms_normalise(x, gain):
    ms = jnp.mean(x * x, axis=-1, keepdims=True)
    return x * lax.rsqrt(ms + NORM_EPS) * gain


def _bf16_dot(a, b):
    return jnp.dot(a, b, preferred_element_type=_F32)


def _pack_bf16_halves(x):
    c = x.shape[1] // 2
    as_bits = lambda v: lax.bitcast_convert_type(v.astype(_BF16).astype(_F32), jnp.uint32)
    return (as_bits(x[:, :c]) >> 16) | (as_bits(x[:, c:]) & jnp.uint32(0xFFFF0000))


def _unpack_bf16_halves(p, dtype):
    lo = lax.bitcast_convert_type(p << 16, _F32).astype(dtype)
    hi = lax.bitcast_convert_type(p & jnp.uint32(0xFFFF0000), _F32).astype(dtype)
    return lo, hi


def _norm_matmul_body(x_ref, g_ref, w_ref, *rest, head_norm):
    if head_norm:
        hg_ref, o_ref, xn_ref = rest
    else:
        o_ref, xn_ref = rest

    @pl.when(pl.program_id(1) == 0)
    def _():
        xn_ref[...] = _rms_normalise(x_ref[...], g_ref[...]).astype(_BF16)

    y = _bf16_dot(xn_ref[...], w_ref[...].astype(_BF16))
    if head_norm:
        for hh in range(y.shape[1] // HEAD_DIM):
            cols = slice(hh * HEAD_DIM, (hh + 1) * HEAD_DIM)
            o_ref[:, cols] = _rms_normalise(y[:, cols], hg_ref[...]).astype(o_ref.dtype)
    else:
        o_ref[...] = y.astype(o_ref.dtype)


def _norm_matmul(x, gain, w, col_off, n_out, head_gain=None):
    n, d = x.shape
    tm, tn = ROW_TILE, COL_TILE
    off = col_off // tn
    in_specs = [
        pl.BlockSpec((tm, d), lambda i, j: (i, 0)),
        pl.BlockSpec((1, d), lambda i, j: (0, 0)),
        pl.BlockSpec((d, tn), lambda i, j: (0, j + off)),
    ]
    args = [x, gain.reshape(1, d), w]
    if head_gain is not None:
        in_specs.append(pl.BlockSpec((1, HEAD_DIM), lambda i, j: (0, 0)))
        args.append(head_gain.reshape(1, HEAD_DIM))
    return pl.pallas_call(
        functools.partial(_norm_matmul_body, head_norm=head_gain is not None),
        grid=(n // tm, n_out // tn),
        in_specs=in_specs,
        out_specs=pl.BlockSpec((tm, tn), lambda i, j: (i, j)),
        out_shape=jax.ShapeDtypeStruct((n, n_out), _BF16),
        scratch_shapes=[pltpu.VMEM((tm, d), _BF16)],
        compiler_params=_params("arbitrary", "arbitrary"),
        name="norm_matmul",
    )(*args)


def _conv_in_body(x_ref, g_ref, wb_ref, wc_ref, wx_ref, cw_ref, z_ref, xn_ref, carry_ref,
                  *, tiles_per_seq):
    i = pl.program_id(0)
    j = pl.program_id(1)
    tm = x_ref.shape[0]

    @pl.when(j == 0)
    def _():
        xn_ref[...] = _rms_normalise(x_ref[...], g_ref[...]).astype(_BF16)

    @pl.when(i % tiles_per_seq == 0)
    def _():
        carry_ref[j] = jnp.zeros(carry_ref.shape[1:], _F32)

    xn = xn_ref[...]
    b_gate = _bf16_dot(xn, wb_ref[...].astype(_BF16))
    c_gate = _bf16_dot(xn, wc_ref[...].astype(_BF16))
    xh = _bf16_dot(xn, wx_ref[...].astype(_BF16))
    u = c_gate * xh
    prev = carry_ref[j]
    row = lax.broadcasted_iota(jnp.int32, u.shape, 0)
    last = prev[SUBLANES - 1:SUBLANES]
    u1 = jnp.where(row == 0, last, pltpu.roll(u, 1, axis=0))
    u2 = jnp.where(row == 0, prev[SUBLANES - 2:SUBLANES - 1],
                   jnp.where(row == 1, last, pltpu.roll(u, 2, axis=0)))
    cw = cw_ref[...]
    conv = cw[0:1] * u2 + cw[1:2] * u1 + cw[2:3] * u
    z_ref[...] = (b_gate * conv).astype(z_ref.dtype)
    carry_ref[j] = u[tm - SUBLANES:tm]


def _conv_in(x, gain, w_in, conv_w, seq_len):
    n, d = x.shape
    tm, tn = ROW_TILE, CONV_COL_TILE
    nj = d // tn
    assert CONV_WIDTH - 1 <= SUBLANES and seq_len % tm == 0
    return pl.pallas_call(
        functools.partial(_conv_in_body, tiles_per_seq=seq_len // tm),
        grid=(n // tm, nj),
        in_specs=[
            pl.BlockSpec((tm, d), lambda i, j: (i, 0)),
            pl.BlockSpec((1, d), lambda i, j: (0, 0)),
            pl.BlockSpec((d, tn), lambda i, j: (0, j)),
            pl.BlockSpec((d, tn), lambda i, j: (0, j + nj)),
            pl.BlockSpec((d, tn), lambda i, j: (0, j + 2 * nj)),
            pl.BlockSpec((CONV_WIDTH, tn), lambda i, j: (0, j)),
        ],
        out_specs=pl.BlockSpec((tm, tn), lambda i, j: (i, j)),
        out_shape=jax.ShapeDtypeStruct((n, d), _BF16),
        scratch_shapes=[pltpu.VMEM((tm, d), _BF16), pltpu.VMEM((nj, SUBLANES, tn), _F32)],
        compiler_params=_params("arbitrary", "arbitrary"),
        name="conv_in",
    )(x, gain.reshape(1, d), w_in, w_in, w_in, conv_w)


def _matmul_residual_body(x_ref, w_ref, r_ref, o_ref):
    o_ref[...] = r_ref[...] + _bf16_dot(x_ref[...], w_ref[...].astype(_BF16))


def _matmul_residual(x, w, res):
    n, k = x.shape
    d = w.shape[1]
    tm, tn = ROW_TILE, COL_TILE
    return pl.pallas_call(
        _matmul_residual_body,
        grid=(n // tm, d // tn),
        in_specs=[
            pl.BlockSpec((tm, k), lambda i, j: (i, 0)),
            pl.BlockSpec((k, tn), lambda i, j: (0, j)),
            pl.BlockSpec((tm, tn), lambda i, j: (i, j)),
        ],
        out_specs=pl.BlockSpec((tm, tn), lambda i, j: (i, j)),
        out_shape=jax.ShapeDtypeStruct((n, d), _F32),
        compiler_params=_params("arbitrary", "arbitrary"),
        name="matmul_residual",
    )(x, w, res)


def _moba_body(q_ref, k_ref, v_ref, o_ref, km_ref, vt_ref, bias_ref):
    heads = range(q_ref.shape[1] // HEAD_DIM)
    head0 = pl.program_id(1) * len(heads)
    i = pl.program_id(2)
    kb = MOBA_BLOCK
    n_blocks = k_ref.shape[0] // kb
    nt = (((1,), (1,)), ((), ()))
    log2e = 1.4426950408889634
    scale = HEAD_DIM ** -0.5 * log2e
    slopes = [log2e * jnp.exp2(jnp.full((1, kb), -ALIBI_MAX_BIAS / N_HEADS, _F32)
                               * (head0 + hh + 1).astype(_F32)) for hh in heads]

    def lanes(hh):
        return slice(hh * HEAD_DIM, (hh + 1) * HEAD_DIM)

    @pl.when(i == 0)
    def _():
        rel = (lax.broadcasted_iota(jnp.int32, (kb, kb), 1)
               - lax.broadcasted_iota(jnp.int32, (kb, kb), 0))
        for hh in heads:
            for jb in range(n_blocks):
                rows = slice(jb * kb, (jb + 1) * kb)
                km_ref[hh, jb:jb + 1, :] = jnp.mean(k_ref[rows, lanes(hh)].astype(_F32),
                                                    axis=0, keepdims=True)
                vt_ref[hh, :, rows] = v_ref[rows, lanes(hh)].astype(_F32).T.astype(_BF16)
            bias = slopes[hh] * rel.astype(_F32)
            bias_ref[hh, 0] = bias
            bias_ref[hh, 1] = jnp.where(rel >= 0, bias, jnp.inf)

    q = [q_ref[:, lanes(hh)] for hh in heads]
    sel = []
    for hh in heads:
        gate = lax.dot_general(km_ref[hh], q[hh].astype(_F32), nt,
                               precision=lax.Precision.HIGHEST, preferred_element_type=_F32)
        blk = lax.broadcasted_iota(jnp.int32, gate.shape, 0)
        gate = jnp.where(blk < i, gate, -jnp.inf)
        picked = jnp.zeros(gate.shape, _F32)
        for jb in range(n_blocks - 1):
            g_jb = gate[jb:jb + 1, :]
            beats = (gate > g_jb) | ((gate == g_jb) & (blk < jb))
            n_beat = jnp.sum(beats.astype(_F32), axis=0, keepdims=True)
            picked = jnp.where((blk == jb) & (blk < i) & (n_beat < MOBA_TOPK), 1.0, picked)
        sel.append(picked)

    def attend(own):
        nb = own + 1
        raw = [lax.dot_general(k_ref[0:nb * kb, lanes(hh)], q[hh], nt, preferred_element_type=_F32)
               for hh in heads]
        tiles, col_max = [[] for _ in heads], [[] for _ in heads]
        for jb in range(nb):
            for hh in heads:
                tile = raw[hh][jb * kb:(jb + 1) * kb, :] * scale - bias_ref[hh, 1 if jb == own else 0]
                mx = jnp.max(tile, axis=0, keepdims=True)
                if jb != own:
                    mx = jnp.where(sel[hh][jb:jb + 1, :] > 0.5,
                                   mx - slopes[hh] * float((own - jb) * kb), -jnp.inf)
                tiles[hh].append(tile), col_max[hh].append(mx)
        m = [functools.reduce(jnp.maximum, col_max[hh]) for hh in heads]
        probs, l = [[] for _ in heads], [None for _ in heads]
        for jb in range(nb):
            for hh in heads:
                offset = m[hh] if jb == own else jnp.where(
                    sel[hh][jb:jb + 1, :] > 0.5, m[hh] + slopes[hh] * float((own - jb) * kb), jnp.inf)
                p = jnp.exp2(tiles[hh][jb] - offset)
                p_sum = jnp.sum(p, axis=0, keepdims=True)
                l[hh] = p_sum if l[hh] is None else l[hh] + p_sum
                probs[hh].append(p.astype(_BF16))
        for hh in heads:
            p_all = probs[hh][0] if nb == 1 else jnp.concatenate(probs[hh], axis=0)
            acc = _bf16_dot(vt_ref[hh, :, 0:nb * kb], p_all)
            o_ref[:, lanes(hh)] = (acc / l[hh]).T.astype(o_ref.dtype)

    for own in range(n_blocks):
        pl.when(i == own)(functools.partial(attend, own))


def _moba_attention(q, k, v, batch, seq_len):
    n, d = q.shape
    kb = MOBA_BLOCK
    nqb = seq_len // kb
    hp = ATTN_HEADS
    assert seq_len % kb == 0 and nqb == SUBLANES and N_HEADS % hp == 0
    return pl.pallas_call(
        _moba_body,
        grid=(batch, N_HEADS // hp, nqb),
        in_specs=[
            pl.BlockSpec((kb, hp * HEAD_DIM), lambda b, h, i: (b * nqb + i, h)),
            pl.BlockSpec((seq_len, hp * HEAD_DIM), lambda b, h, i: (b, h)),
            pl.BlockSpec((seq_len, hp * HEAD_DIM), lambda b, h, i: (b, h)),
        ],
        out_specs=pl.BlockSpec((kb, hp * HEAD_DIM), lambda b, h, i: (b * nqb + i, h)),
        out_shape=jax.ShapeDtypeStruct((n, d), _BF16),
        scratch_shapes=[pltpu.VMEM((hp, nqb, HEAD_DIM), _F32),
                        pltpu.VMEM((hp, HEAD_DIM, seq_len), _BF16),
                        pltpu.VMEM((hp, 2, kb, kb), _F32)],
        compiler_params=_params("arbitrary", "arbitrary", "arbitrary"),
        name="moba_attention",
    )(q, k, v)


def _router_body(x_ref, g_ref, wr_ref, hn_ref, ids_ref, wts_ref, cnt_ref, run_ref):
    step = pl.program_id(0)
    hn = _rms_normalise(x_ref[...], g_ref[...])
    hn_ref[...] = _pack_bf16_halves(hn)
    logits = jnp.dot(hn, wr_ref[...], precision=lax.Precision.HIGHEST, preferred_element_type=_F32)
    lane = lax.broadcasted_iota(jnp.int32, logits.shape, 1)
    far = jnp.int32(LANES)

    def first_lane(mask):
        return jnp.min(jnp.where(mask, lane, far), axis=1, keepdims=True)

    is_group = (lane >= N_EXPERTS) & (lane < N_EXPERTS + N_GROUPS)
    g_max = jnp.max(jnp.where(is_group, logits, -jnp.inf), axis=1, keepdims=True)
    g_sel = first_lane(is_group & (logits == g_max)) - N_EXPERTS
    g_w = 1.0 / jnp.sum(jnp.where(is_group, jnp.exp(logits - g_max), 0.0), axis=1, keepdims=True)

    in_group = (lane >= g_sel * EXPERTS_PER_GROUP) & (lane < (g_sel + 1) * EXPERTS_PER_GROUP)
    top1 = jnp.max(jnp.where(in_group, logits, -jnp.inf), axis=1, keepdims=True)
    idx1 = first_lane(in_group & (logits == top1))
    rest = in_group & (lane != idx1)
    top2 = jnp.max(jnp.where(rest, logits, -jnp.inf), axis=1, keepdims=True)
    idx2 = first_lane(rest & (logits == top2))
    t = jnp.exp(top2 - top1)
    w1 = 1.0 / (1.0 + t) * g_w
    w2 = t / (1.0 + t) * g_w

    @pl.when(step == 0)
    def _():
        run_ref[...] = jnp.zeros(run_ref.shape, _F32)

    tm = logits.shape[0]
    chosen = (lane == idx1) | (lane == idx2)
    earlier = (lax.broadcasted_iota(jnp.int32, (tm, tm), 1)
               < lax.broadcasted_iota(jnp.int32, (tm, tm), 0))
    before = run_ref[...] + _bf16_dot(jnp.where(earlier, 1.0, 0.0).astype(_BF16),
                                      jnp.where(chosen, 1.0, 0.0).astype(_BF16))
    rank1 = jnp.sum(jnp.where(lane == idx1, before, 0.0), axis=1, keepdims=True).astype(jnp.int32)
    rank2 = jnp.sum(jnp.where(lane == idx2, before, 0.0), axis=1, keepdims=True).astype(jnp.int32)
    run_ref[...] += jnp.sum(jnp.where(chosen, 1.0, 0.0), axis=0, keepdims=True)

    ids_ref[...] = jnp.where(lane == 0, idx1, jnp.where(lane == 1, idx2,
                             jnp.where(lane == 2, rank1, jnp.where(lane == 3, rank2, 0))))
    wts_ref[...] = jnp.where(lane == 0, w1, jnp.where(lane == 1, w2, 0.0))
    cnt_ref[...] = jnp.broadcast_to(run_ref[...], cnt_ref.shape).astype(jnp.int32)


def _router(x, gain, w_rg, w_re):
    n, d = x.shape
    tm = ROUTER_ROWS
    assert EXPERT_TOPK == 2 and N_EXPERTS + N_GROUPS <= LANES
    wr = jnp.concatenate(
        [w_re, w_rg, jnp.zeros((d, LANES - N_EXPERTS - N_GROUPS), _F32)], axis=1)
    row = lambda i: (i, 0)
    fixed = lambda i: (0, 0)
    return pl.pallas_call(
        _router_body,
        grid=(n // tm,),
        in_specs=[
            pl.BlockSpec((tm, d), row),
            pl.BlockSpec((1, d), fixed),
            pl.BlockSpec((d, LANES), fixed),
        ],
        out_specs=[pl.BlockSpec((tm, d // 2), row), pl.BlockSpec((tm, LANES), row),
                   pl.BlockSpec((tm, LANES), row), pl.BlockSpec((SUBLANES, LANES), fixed)],
        out_shape=[jax.ShapeDtypeStruct((n, d // 2), jnp.uint32),
                   jax.ShapeDtypeStruct((n, LANES), jnp.int32),
                   jax.ShapeDtypeStruct((n, LANES), _F32),
                   jax.ShapeDtypeStruct((SUBLANES, LANES), jnp.int32)],
        scratch_shapes=[pltpu.VMEM((1, LANES), _F32)],
        compiler_params=_params("arbitrary"),
        name="router",
    )(x, gain.reshape(1, d), wr)


def _expert_body(bounds_ref, row_tok_ref, hn_hbm, wg_ref, wu_ref, wd_ref, y_hbm,
                 xs_ref, ys_ref, zero_ref, wg_bf, wu_bf, wd_bf, gather_sem, out_sem, tail_sem):
    e = pl.program_id(0)
    n_experts = pl.num_programs(0)
    n_slots, rows = xs_ref.shape[0], xs_ref.shape[1]
    ahead = n_slots - 1
    first, stop = bounds_ref[e], bounds_ref[e + 1]
    n_total = bounds_ref[n_experts]

    def row_copy(tok, s, r):
        return pltpu.make_async_copy(hn_hbm.at[pl.ds(tok, 1)], xs_ref.at[s, pl.ds(r, 1)],
                                     gather_sem.at[s])

    def start_gather(block, unrolled):
        s = block % n_slots
        base = jnp.minimum(block, n_total - 1) * rows
        if unrolled:
            for r in range(rows):
                row_copy(row_tok_ref[base + r], s, r).start(priority=1)
        else:
            def body(r, carry):
                row_copy(row_tok_ref[base + r], s, r).start(priority=1)
                return carry
            lax.fori_loop(0, rows, body, 0)

    def wait_gather(block):
        s = block % n_slots
        pltpu.make_async_copy(hn_hbm.at[pl.ds(0, rows)], xs_ref.at[s], gather_sem.at[s]).wait()

    def out_copy(block):
        s = block % n_slots
        dst = y_hbm.at[pl.ds(pl.multiple_of(block * rows, rows), rows)]
        return pltpu.make_async_copy(ys_ref.at[s], dst, out_sem.at[s])

    def tail_copy(block):
        dst = y_hbm.at[pl.ds(pl.multiple_of(block * rows, rows), rows)]
        return pltpu.make_async_copy(zero_ref, dst, tail_sem.at[0])

    def for_tail_blocks(fn):
        def body(block, carry):
            fn(tail_copy(block))
            return carry
        lax.fori_loop(n_total, y_hbm.shape[0] // rows, body, 0)

    @pl.when(e == 0)
    def _():
        for block in range(ahead):
            start_gather(block, unrolled=False)
        zero_ref[...] = jnp.zeros(zero_ref.shape, zero_ref.dtype)
        for_tail_blocks(lambda copy: copy.start())

    @pl.when(stop > first)
    def _():
        wg_bf[...] = wg_ref[0, 0].astype(_BF16)
        wu_bf[...] = wu_ref[0, 0].astype(_BF16)
        wd_bf[...] = wd_ref[0, 0].astype(_BF16)

    def block_step(block, carry):
        start_gather(block + ahead, unrolled=True)
        wait_gather(block)
        x_lo, x_hi = _unpack_bf16_halves(xs_ref[block % n_slots], _BF16)
        half = x_lo.shape[1]

        def x_dot(w_bf):
            return _bf16_dot(x_lo, w_bf[:half, :]) + _bf16_dot(x_hi, w_bf[half:, :])

        act = jax.nn.silu(x_dot(wg_bf)) * x_dot(wu_bf)
        y = _bf16_dot(act.astype(_BF16), wd_bf[...])

        @pl.when(block >= n_slots)
        def _():
            out_copy(block - n_slots).wait()

        ys_ref[block % n_slots] = _pack_bf16_halves(y)
        out_copy(block).start()
        return carry

    lax.fori_loop(first, stop, block_step, 0)

    @pl.when(e == n_experts - 1)
    def _():
        for extra in range(ahead):
            wait_gather(n_total + extra)
        for_tail_blocks(lambda copy: copy.wait())
        for back in range(1, n_slots + 1):
            @pl.when(n_total >= back)
            def _():
                out_copy(n_total - back).wait()


def _expert_mlp(bounds, row_tok, hn, layer, w_gate, w_up, w_down):
    n_experts, d, de = w_gate.shape[1:]
    rows = EXPERT_ROWS
    assert hn.shape[1] * 2 == d and hn.dtype == jnp.uint32
    grid_spec = pltpu.PrefetchScalarGridSpec(
        num_scalar_prefetch=2,
        grid=(n_experts,),
        in_specs=[
            pl.BlockSpec(memory_space=pl.ANY),
            pl.BlockSpec((1, 1, d, de), lambda e, bd, rt: (layer, e, 0, 0)),
            pl.BlockSpec((1, 1, d, de), lambda e, bd, rt: (layer, e, 0, 0)),
            pl.BlockSpec((1, 1, de, d), lambda e, bd, rt: (layer, e, 0, 0)),
        ],
        out_specs=pl.BlockSpec(memory_space=pl.ANY),
        scratch_shapes=[
            pltpu.VMEM((EXPERT_SLOTS, rows, d // 2), jnp.uint32),
            pltpu.VMEM((EXPERT_SLOTS, rows, d // 2), jnp.uint32),
            pltpu.VMEM((rows, d // 2), jnp.uint32),
            pltpu.VMEM((d, de), _BF16),
            pltpu.VMEM((d, de), _BF16),
            pltpu.VMEM((de, d), _BF16),
            pltpu.SemaphoreType.DMA((EXPERT_SLOTS,)),
            pltpu.SemaphoreType.DMA((EXPERT_SLOTS,)),
            pltpu.SemaphoreType.DMA((1,)),
        ],
    )
    return pl.pallas_call(
        _expert_body,
        grid_spec=grid_spec,
        out_shape=jax.ShapeDtypeStruct((row_tok.shape[0], d // 2), jnp.uint32),
        compiler_params=_params("arbitrary"),
        name="expert_mlp",
    )(bounds, row_tok, hn, w_gate, w_up, w_down)


def _combine_body(pos_ref, y_hbm, wts_ref, h_ref, o_ref, ys_ref, sem):
    i = pl.program_id(0)
    n_steps = pl.num_programs(0)
    tm = h_ref.shape[0]
    slot = i % 2

    def row_copy(p, s, k, r):
        return pltpu.make_async_copy(y_hbm.at[pl.ds(p, 1)], ys_ref.at[s, k, pl.ds(r, 1)], sem.at[s])

    def start_gather(tile, s):
        for r in range(tm):
            for k in range(EXPERT_TOPK):
                row_copy(pos_ref[(tile * tm + r) * EXPERT_TOPK + k], s, k, r).start(priority=k)

    def wait_gather(s):
        for k in range(EXPERT_TOPK):
            pltpu.make_async_copy(y_hbm.at[pl.ds(0, tm)], ys_ref.at[s, k], sem.at[s]).wait()

    @pl.when(i == 0)
    def _():
        start_gather(0, 0)

    @pl.when(i + 1 < n_steps)
    def _():
        start_gather(i + 1, 1 - slot)

    wait_gather(slot)
    w = wts_ref[...]
    half = ys_ref.shape[3]
    a_lo, a_hi = _unpack_bf16_halves(ys_ref[slot, 0], _F32)
    b_lo, b_hi = _unpack_bf16_halves(ys_ref[slot, 1], _F32)
    o_ref[:, :half] = h_ref[:, :half] + (w[:, 0:1] * a_lo + w[:, 1:2] * b_lo)
    o_ref[:, half:] = h_ref[:, half:] + (w[:, 0:1] * a_hi + w[:, 1:2] * b_hi)


def _combine(pos, y, wts, h):
    n, d = h.shape
    tm = COMBINE_ROWS
    grid_spec = pltpu.PrefetchScalarGridSpec(
        num_scalar_prefetch=1,
        grid=(n // tm,),
        in_specs=[
            pl.BlockSpec(memory_space=pl.ANY),
            pl.BlockSpec((tm, LANES), lambda i, p: (i, 0)),
            pl.BlockSpec((tm, d), lambda i, p: (i, 0)),
        ],
        out_specs=pl.BlockSpec((tm, d), lambda i, p: (i, 0)),
        scratch_shapes=[
            pltpu.VMEM((2, EXPERT_TOPK, tm, d // 2), jnp.uint32),
            pltpu.SemaphoreType.DMA((2,)),
        ],
    )
    return pl.pallas_call(
        _combine_body,
        grid_spec=grid_spec,
        out_shape=jax.ShapeDtypeStruct((n, d), _F32),
        compiler_params=_params("arbitrary"),
        name="combine",
    )(pos, y, wts, h)


def _dispatch_plan(ids, counts):
    n = ids.shape[0]
    rows = EXPERT_ROWS
    m = n * EXPERT_TOPK
    n_blocks = -(-(m + N_EXPERTS * (rows - 1)) // rows)
    flat_e = ids[:, :EXPERT_TOPK].reshape(m)
    rank = ids[:, EXPERT_TOPK:2 * EXPERT_TOPK].reshape(m)
    counts = counts[0, :N_EXPERTS]
    starts = jnp.cumsum(counts) - counts
    padded = (counts + rows - 1) // rows * rows
    pends = jnp.cumsum(padded)
    pstarts = pends - padded
    expert_iota = jnp.arange(N_EXPERTS, dtype=jnp.int32)
    pos = rank + jnp.sum(jnp.where(flat_e[:, None] == expert_iota[None, :], pstarts[None, :], 0), axis=1)
    blk_start = jnp.arange(n_blocks, dtype=jnp.int32) * rows
    blk_e = jnp.minimum(jnp.sum(pends[None, :] <= blk_start[:, None], axis=1), N_EXPERTS - 1)
    blk_e = blk_e.astype(jnp.int32)
    order = jnp.argsort(flat_e)
    sorted_idx = (blk_start - pstarts[blk_e] + starts[blk_e])[:, None] + jnp.arange(rows)[None, :]
    valid = sorted_idx < (starts + counts)[blk_e][:, None]
    row_tok = jnp.where(valid, order[jnp.clip(sorted_idx, 0, m - 1)] // EXPERT_TOPK, 0)
    bounds = jnp.concatenate([pstarts, pends[-1:]]) // rows
    return (bounds.astype(jnp.int32), row_tok.reshape(n_blocks * rows).astype(jnp.int32),
            pos.astype(jnp.int32))


def _hier_moe_residual(h, gain, w_rg, w_re, layer, w_gate, w_up, w_down):
    hn, ids, wts, counts = _router(h, gain, w_rg, w_re)
    bounds, row_tok, pos = _dispatch_plan(ids, counts)
    y = _expert_mlp(bounds, row_tok, hn, layer, w_gate, w_up, w_down)
    return _combine(pos, y, wts, h)


def kernel(x, mix_norm, ffn_norm, conv_w_in, conv_w, conv_w_out, kv_norm, w_kv, k_norm, w_q, q_norm,
           w_o, router_group, router_expert, w_gate, w_up, w_down):
    bsz, seq_len, d = x.shape
    depth = mix_norm.shape[0]
    n_conv = conv_w_in.shape[0]
    assert d == N_HEADS * HEAD_DIM
    h = x.reshape(bsz * seq_len, d)
    k_sh = v_sh = None
    for l in range(depth):
        if l < n_conv:
            z = _conv_in(h, mix_norm[l], conv_w_in[l], conv_w[l], seq_len)
            h = _matmul_residual(z, conv_w_out[l], h)
        else:
            j = l - n_conv
            if j == 0:
                k_sh = _norm_matmul(h, kv_norm, w_kv, 0, d, head_gain=k_norm)
                v_sh = _norm_matmul(h, kv_norm, w_kv, d, d)
            q = _norm_matmul(h, mix_norm[l], w_q[j], 0, d, head_gain=q_norm[j])
            attn = _moba_attention(q, k_sh, v_sh, bsz, seq_len)
            h = _matmul_residual(attn, w_o[j], h)
        h = _hier_moe_residual(h, ffn_norm[l], router_group[l], router_expert[l],
                               l, w_gate, w_up, w_down)
    return h.reshape(bsz, seq_len, d)
```

```python
import functools

import jax
import jax.numpy as jnp
from jax import lax
from jax.experimental import pallas as pl
from jax.experimental.pallas import tpu as pltpu

N_HEADS = 16
HEAD_DIM = 128
CONV_WIDTH = 3
MOBA_BLOCK = 256
MOBA_TOPK = 3
N_GROUPS = 4
EXPERTS_PER_GROUP = 8
N_EXPERTS = N_GROUPS * EXPERTS_PER_GROUP
EXPERT_TOPK = 2
NORM_EPS = 1e-6
ALIBI_MAX_BIAS = 8.0

LANES = 128
SUBLANES = 8
VMEM_LIMIT = 56 * 1024 * 1024

ROW_TILE = 1024
COL_TILE = 512
CONV_COL_TILE = 256
ROUTER_ROWS = 512
EXPERT_ROWS = 128
EXPERT_SLOTS = 4
COMBINE_ROWS = 256
ATTN_HEADS = 4

_BF16 = jnp.bfloat16
_F32 = jnp.float32


def _params(*semantics):
    return pltpu.CompilerParams(dimension_semantics=semantics, vmem_limit_bytes=VMEM_LIMIT)


def _rms_normalise(x, gain):
    ms = jnp.mean(x * x, axis=-1, keepdims=True)
    return x * lax.rsqrt(ms + NORM_EPS) * gain


def _bf16_dot(a, b):
    return jnp.dot(a, b, preferred_element_type=_F32)


def _pack_bf16_halves(x):
    c = x.shape[1] // 2
    as_bits = lambda v: lax.bitcast_convert_type(v.astype(_BF16).astype(_F32), jnp.uint32)
    return (as_bits(x[:, :c]) >> 16) | (as_bits(x[:, c:]) & jnp.uint32(0xFFFF0000))


def _unpack_bf16_halves(p, dtype):
    lo = lax.bitcast_convert_type(p << 16, _F32).astype(dtype)
    hi = lax.bitcast_convert_type(p & jnp.uint32(0xFFFF0000), _F32).astype(dtype)
    return lo, hi


def _qkv_body(x_ref, g_ref, w_ref, hg_ref, o_ref, xn_ref, *, q_tiles, qk_tiles):
    j = pl.program_id(1)

    @pl.when(j == 0)
    def _():
        x = x_ref[...]
        normed = x * lax.rsqrt(jnp.mean(x * x, axis=-1, keepdims=True) + NORM_EPS)
        xn_ref[0] = (normed * g_ref[0:1, :]).astype(_BF16)
        xn_ref[1] = (normed * g_ref[1:2, :]).astype(_BF16)

    def project(stream, head_gain_row):
        y = _bf16_dot(xn_ref[stream], w_ref[...])
        if head_gain_row is None:
            o_ref[...] = y.astype(o_ref.dtype)
            return
        head_gain = hg_ref[head_gain_row:head_gain_row + 1, :]
        for hh in range(y.shape[1] // HEAD_DIM):
            cols = slice(hh * HEAD_DIM, (hh + 1) * HEAD_DIM)
            o_ref[:, cols] = _rms_normalise(y[:, cols], head_gain).astype(o_ref.dtype)

    pl.when(j < q_tiles)(functools.partial(project, 0, 0))
    pl.when((j >= q_tiles) & (j < qk_tiles))(functools.partial(project, 1, 1))
    pl.when(j >= qk_tiles)(functools.partial(project, 1, None))


def _qkv_projection(x, gains, w, head_gains, n_q, n_k):
    n, d = x.shape
    tm, tn = ROW_TILE, COL_TILE
    n_out = w.shape[1]
    assert n_q % tn == 0 and n_k % tn == 0 and n_out % tn == 0
    return pl.pallas_call(
        functools.partial(_qkv_body, q_tiles=n_q // tn, qk_tiles=(n_q + n_k) // tn),
        grid=(n // tm, n_out // tn),
        in_specs=[
            pl.BlockSpec((tm, d), lambda i, j: (i, 0)),
            pl.BlockSpec((2, d), lambda i, j: (0, 0)),
            pl.BlockSpec((d, tn), lambda i, j: (0, j)),
            pl.BlockSpec((2, HEAD_DIM), lambda i, j: (0, 0)),
        ],
        out_specs=pl.BlockSpec((tm, tn), lambda i, j: (i, j)),
        out_shape=jax.ShapeDtypeStruct((n, n_out), _BF16),
        scratch_shapes=[pltpu.VMEM((2, tm, d), _BF16)],
        compiler_params=_params("arbitrary", "arbitrary"),
        name="qkv_projection",
    )(x, gains, w, head_gains)


def _conv_in_body(x_ref, g_ref, wb_ref, wc_ref, wx_ref, cw_ref, z_ref, xn_ref, carry_ref,
                  *, tiles_per_seq):
    i = pl.program_id(0)
    j = pl.program_id(1)
    tm = x_ref.shape[0]

    @pl.when(j == 0)
    def _():
        xn_ref[...] = _rms_normalise(x_ref[...], g_ref[...]).astype(_BF16)

    @pl.when(i % tiles_per_seq == 0)
    def _():
        carry_ref[j] = jnp.zeros(carry_ref.shape[1:], _F32)

    xn = xn_ref[...]
    b_gate = _bf16_dot(xn, wb_ref[...])
    c_gate = _bf16_dot(xn, wc_ref[...])
    xh = _bf16_dot(xn, wx_ref[...])
    u = c_gate * xh
    prev = carry_ref[j]
    row = lax.broadcasted_iota(jnp.int32, u.shape, 0)
    last = prev[SUBLANES - 1:SUBLANES]
    u1 = jnp.where(row == 0, last, pltpu.roll(u, 1, axis=0))
    u2 = jnp.where(row == 0, prev[SUBLANES - 2:SUBLANES - 1],
                   jnp.where(row == 1, last, pltpu.roll(u, 2, axis=0)))
    cw = cw_ref[...]
    conv = cw[0:1] * u2 + cw[1:2] * u1 + cw[2:3] * u
    z_ref[...] = (b_gate * conv).astype(z_ref.dtype)
    carry_ref[j] = u[tm - SUBLANES:tm]


def _conv_in(x, gain, w_in, conv_w, seq_len):
    n, d = x.shape
    tm, tn = ROW_TILE, CONV_COL_TILE
    nj = d // tn
    assert CONV_WIDTH - 1 <= SUBLANES and seq_len % tm == 0
    return pl.pallas_call(
        functools.partial(_conv_in_body, tiles_per_seq=seq_len // tm),
        grid=(n // tm, nj),
        in_specs=[
            pl.BlockSpec((tm, d), lambda i, j: (i, 0)),
            pl.BlockSpec((1, d), lambda i, j: (0, 0)),
            pl.BlockSpec((d, tn), lambda i, j: (0, j)),
            pl.BlockSpec((d, tn), lambda i, j: (0, j + nj)),
            pl.BlockSpec((d, tn), lambda i, j: (0, j + 2 * nj)),
            pl.BlockSpec((CONV_WIDTH, tn), lambda i, j: (0, j)),
        ],
        out_specs=pl.BlockSpec((tm, tn), lambda i, j: (i, j)),
        out_shape=jax.ShapeDtypeStruct((n, d), _BF16),
        scratch_shapes=[pltpu.VMEM((tm, d), _BF16), pltpu.VMEM((nj, SUBLANES, tn), _F32)],
        compiler_params=_params("arbitrary", "arbitrary"),
        name="conv_in",
    )(x, gain.reshape(1, d), w_in, w_in, w_in, conv_w)


def _matmul_residual_body(x_ref, w_ref, r_ref, o_ref):
    o_ref[...] = r_ref[...] + _bf16_dot(x_ref[...], w_ref[...])


def _matmul_residual(x, w, res):
    n, k = x.shape
    d = w.shape[1]
    tm, tn = ROW_TILE, COL_TILE
    return pl.pallas_call(
        _matmul_residual_body,
        grid=(n // tm, d // tn),
        in_specs=[
            pl.BlockSpec((tm, k), lambda i, j: (i, 0)),
            pl.BlockSpec((k, tn), lambda i, j: (0, j)),
            pl.BlockSpec((tm, tn), lambda i, j: (i, j)),
        ],
        out_specs=pl.BlockSpec((tm, tn), lambda i, j: (i, j)),
        out_shape=jax.ShapeDtypeStruct((n, d), _F32),
        compiler_params=_params("arbitrary", "arbitrary"),
        name="matmul_residual",
    )(x, w, res)


def _moba_body(q_ref, k_ref, v_ref, o_ref, km_ref, vt_ref, bias_ref):
    heads = range(q_ref.shape[1] // HEAD_DIM)
    head0 = pl.program_id(1) * len(heads)
    i = pl.program_id(2)
    kb = MOBA_BLOCK
    n_blocks = k_ref.shape[0] // kb
    nt = (((1,), (1,)), ((), ()))
    log2e = 1.4426950408889634
    scale = HEAD_DIM ** -0.5 * log2e
    slopes = [log2e * jnp.exp2(jnp.full((1, kb), -ALIBI_MAX_BIAS / N_HEADS, _F32)
                               * (head0 + hh + 1).astype(_F32)) for hh in heads]

    def lanes(hh):
        return slice(hh * HEAD_DIM, (hh + 1) * HEAD_DIM)

    @pl.when(i == 0)
    def _():
        rel = (lax.broadcasted_iota(jnp.int32, (kb, kb), 1)
               - lax.broadcasted_iota(jnp.int32, (kb, kb), 0))
        for hh in heads:
            for jb in range(n_blocks):
                rows = slice(jb * kb, (jb + 1) * kb)
                km_ref[hh, jb:jb + 1, :] = jnp.mean(k_ref[rows, lanes(hh)].astype(_F32),
                                                    axis=0, keepdims=True)
                vt_ref[hh, :, rows] = v_ref[rows, lanes(hh)].astype(_F32).T.astype(_BF16)
            bias = slopes[hh] * rel.astype(_F32)
            bias_ref[hh, 0] = bias
            bias_ref[hh, 1] = jnp.where(rel >= 0, bias, jnp.inf)

    q = [q_ref[:, lanes(hh)] for hh in heads]
    sel = []
    for hh in heads:
        gate = lax.dot_general(km_ref[hh], q[hh].astype(_F32), nt,
                               precision=lax.Precision.HIGHEST, preferred_element_type=_F32)
        blk = lax.broadcasted_iota(jnp.int32, gate.shape, 0)
        gate = jnp.where(blk < i, gate, -jnp.inf)
        picked = jnp.zeros(gate.shape, _F32)
        for jb in range(n_blocks - 1):
            g_jb = gate[jb:jb + 1, :]
            beats = (gate > g_jb) | ((gate == g_jb) & (blk < jb))
            n_beat = jnp.sum(beats.astype(_F32), axis=0, keepdims=True)
            picked = jnp.where((blk == jb) & (blk < i) & (n_beat < MOBA_TOPK), 1.0, picked)
        sel.append(picked)

    def attend(own):
        nb = own + 1
        raw = [lax.dot_general(k_ref[0:nb * kb, lanes(hh)], q[hh], nt, preferred_element_type=_F32)
               for hh in heads]
        tiles, col_max = [[] for _ in heads], [[] for _ in heads]
        for jb in range(nb):
            for hh in heads:
                tile = raw[hh][jb * kb:(jb + 1) * kb, :] * scale - bias_ref[hh, 1 if jb == own else 0]
                mx = jnp.max(tile, axis=0, keepdims=True)
                if jb != own:
                    mx = jnp.where(sel[hh][jb:jb + 1, :] > 0.5,
                                   mx - slopes[hh] * float((own - jb) * kb), -jnp.inf)
                tiles[hh].append(tile), col_max[hh].append(mx)
        m = [functools.reduce(jnp.maximum, col_max[hh]) for hh in heads]
        probs, l = [[] for _ in heads], [None for _ in heads]
        for jb in range(nb):
            for hh in heads:
                offset = m[hh] if jb == own else jnp.where(
                    sel[hh][jb:jb + 1, :] > 0.5, m[hh] + slopes[hh] * float((own - jb) * kb), jnp.inf)
                p = jnp.exp2(tiles[hh][jb] - offset)
                p_sum = jnp.sum(p, axis=0, keepdims=True)
                l[hh] = p_sum if l[hh] is None else l[hh] + p_sum
                probs[hh].append(p.astype(_BF16))
        for hh in heads:
            p_all = probs[hh][0] if nb == 1 else jnp.concatenate(probs[hh], axis=0)
            acc = _bf16_dot(vt_ref[hh, :, 0:nb * kb], p_all)
            o_ref[:, lanes(hh)] = (acc / l[hh]).T.astype(o_ref.dtype)

    for own in range(n_blocks):
        pl.when(i == own)(functools.partial(attend, own))


def _moba_attention(q, q_col, kv, k_col, v_col, batch, seq_len):
    n = q.shape[0]
    d = N_HEADS * HEAD_DIM
    kb = MOBA_BLOCK
    nqb = seq_len // kb
    hp = ATTN_HEADS
    width = hp * HEAD_DIM
    assert seq_len % kb == 0 and nqb == SUBLANES and N_HEADS % hp == 0
    assert q_col % width == 0 and k_col % width == 0 and v_col % width == 0
    q_blk, k_blk, v_blk = q_col // width, k_col // width, v_col // width
    return pl.pallas_call(
        _moba_body,
        grid=(batch, N_HEADS // hp, nqb),
        in_specs=[
            pl.BlockSpec((kb, width), lambda b, h, i: (b * nqb + i, q_blk + h)),
            pl.BlockSpec((seq_len, width), lambda b, h, i: (b, k_blk + h)),
            pl.BlockSpec((seq_len, width), lambda b, h, i: (b, v_blk + h)),
        ],
        out_specs=pl.BlockSpec((kb, width), lambda b, h, i: (b * nqb + i, h)),
        out_shape=jax.ShapeDtypeStruct((n, d), _BF16),
        scratch_shapes=[pltpu.VMEM((hp, nqb, HEAD_DIM), _F32),
                        pltpu.VMEM((hp, HEAD_DIM, seq_len), _BF16),
                        pltpu.VMEM((hp, 2, kb, kb), _F32)],
        compiler_params=_params("arbitrary", "arbitrary", "arbitrary"),
        name="moba_attention",
    )(q, kv, kv)


def _router_body(x_ref, g_ref, wr_ref, hn_ref, ids_ref, wts_ref, cnt_ref, run_ref):
    step = pl.program_id(0)
    hn = _rms_normalise(x_ref[...], g_ref[...])
    hn_ref[...] = _pack_bf16_halves(hn)
    logits = jnp.dot(hn, wr_ref[...], precision=lax.Precision.HIGHEST, preferred_element_type=_F32)
    lane = lax.broadcasted_iota(jnp.int32, logits.shape, 1)
    far = jnp.int32(LANES)

    def first_lane(mask):
        return jnp.min(jnp.where(mask, lane, far), axis=1, keepdims=True)

    is_group = (lane >= N_EXPERTS) & (lane < N_EXPERTS + N_GROUPS)
    g_max = jnp.max(jnp.where(is_group, logits, -jnp.inf), axis=1, keepdims=True)
    g_sel = first_lane(is_group & (logits == g_max)) - N_EXPERTS
    g_w = 1.0 / jnp.sum(jnp.where(is_group, jnp.exp(logits - g_max), 0.0), axis=1, keepdims=True)

    in_group = (lane >= g_sel * EXPERTS_PER_GROUP) & (lane < (g_sel + 1) * EXPERTS_PER_GROUP)
    top1 = jnp.max(jnp.where(in_group, logits, -jnp.inf), axis=1, keepdims=True)
    idx1 = first_lane(in_group & (logits == top1))
    rest = in_group & (lane != idx1)
    top2 = jnp.max(jnp.where(rest, logits, -jnp.inf), axis=1, keepdims=True)
    idx2 = first_lane(rest & (logits == top2))
    t = jnp.exp(top2 - top1)
    w1 = 1.0 / (1.0 + t) * g_w
    w2 = t / (1.0 + t) * g_w

    @pl.when(step == 0)
    def _():
        run_ref[...] = jnp.zeros(run_ref.shape, _F32)

    tm = logits.shape[0]
    chosen = (lane == idx1) | (lane == idx2)
    earlier = (lax.broadcasted_iota(jnp.int32, (tm, tm), 1)
               < lax.broadcasted_iota(jnp.int32, (tm, tm), 0))
    before = run_ref[...] + _bf16_dot(jnp.where(earlier, 1.0, 0.0).astype(_BF16),
                                      jnp.where(chosen, 1.0, 0.0).astype(_BF16))
    rank1 = jnp.sum(jnp.where(lane == idx1, before, 0.0), axis=1, keepdims=True).astype(jnp.int32)
    rank2 = jnp.sum(jnp.where(lane == idx2, before, 0.0), axis=1, keepdims=True).astype(jnp.int32)
    run_ref[...] += jnp.sum(jnp.where(chosen, 1.0, 0.0), axis=0, keepdims=True)

    ids_ref[...] = jnp.where(lane == 0, idx1, jnp.where(lane == 1, idx2,
                             jnp.where(lane == 2, rank1, jnp.where(lane == 3, rank2, 0))))
    wts_ref[...] = jnp.where(lane == 0, w1, jnp.where(lane == 1, w2, 0.0))
    cnt_ref[...] = jnp.broadcast_to(run_ref[...], cnt_ref.shape).astype(jnp.int32)


def _router(x, gain, w_rg, w_re):
    n, d = x.shape
    tm = ROUTER_ROWS
    assert EXPERT_TOPK == 2 and N_EXPERTS + N_GROUPS <= LANES
    wr = jnp.concatenate(
        [w_re, w_rg, jnp.zeros((d, LANES - N_EXPERTS - N_GROUPS), _F32)], axis=1)
    row = lambda i: (i, 0)
    fixed = lambda i: (0, 0)
    return pl.pallas_call(
        _router_body,
        grid=(n // tm,),
        in_specs=[
            pl.BlockSpec((tm, d), row),
            pl.BlockSpec((1, d), fixed),
            pl.BlockSpec((d, LANES), fixed),
        ],
        out_specs=[pl.BlockSpec((tm, d // 2), row), pl.BlockSpec((tm, LANES), row),
                   pl.BlockSpec((tm, LANES), row), pl.BlockSpec((SUBLANES, LANES), fixed)],
        out_shape=[jax.ShapeDtypeStruct((n, d // 2), jnp.uint32),
                   jax.ShapeDtypeStruct((n, LANES), jnp.int32),
                   jax.ShapeDtypeStruct((n, LANES), _F32),
                   jax.ShapeDtypeStruct((SUBLANES, LANES), jnp.int32)],
        scratch_shapes=[pltpu.VMEM((1, LANES), _F32)],
        compiler_params=_params("arbitrary"),
        name="router",
    )(x, gain.reshape(1, d), wr)


def _expert_body(bounds_ref, row_tok_ref, hn_hbm, wg_ref, wu_ref, wd_ref, y_hbm,
                 xs_ref, ys_ref, zero_ref, wg_bf, wu_bf, wd_bf, gather_sem, out_sem, tail_sem):
    e = pl.program_id(0)
    n_experts = pl.num_programs(0)
    n_slots, rows = xs_ref.shape[0], xs_ref.shape[1]
    ahead = n_slots - 1
    first, stop = bounds_ref[e], bounds_ref[e + 1]
    n_total = bounds_ref[n_experts]

    def row_copy(tok, s, r):
        return pltpu.make_async_copy(hn_hbm.at[pl.ds(tok, 1)], xs_ref.at[s, pl.ds(r, 1)],
                                     gather_sem.at[s])

    def start_gather(block, unrolled):
        s = block % n_slots
        base = jnp.minimum(block, n_total - 1) * rows
        if unrolled:
            for r in range(rows):
                row_copy(row_tok_ref[base + r], s, r).start(priority=1)
        else:
            def body(r, carry):
                row_copy(row_tok_ref[base + r], s, r).start(priority=1)
                return carry
            lax.fori_loop(0, rows, body, 0)

    def wait_gather(block):
        s = block % n_slots
        pltpu.make_async_copy(hn_hbm.at[pl.ds(0, rows)], xs_ref.at[s], gather_sem.at[s]).wait()

    def out_copy(block):
        s = block % n_slots
        dst = y_hbm.at[pl.ds(pl.multiple_of(block * rows, rows), rows)]
        return pltpu.make_async_copy(ys_ref.at[s], dst, out_sem.at[s])

    def tail_copy(block):
        dst = y_hbm.at[pl.ds(pl.multiple_of(block * rows, rows), rows)]
        return pltpu.make_async_copy(zero_ref, dst, tail_sem.at[0])

    def for_tail_blocks(fn):
        def body(block, carry):
            fn(tail_copy(block))
            return carry
        lax.fori_loop(n_total, y_hbm.shape[0] // rows, body, 0)

    @pl.when(e == 0)
    def _():
        for block in range(ahead):
            start_gather(block, unrolled=False)
        zero_ref[...] = jnp.zeros(zero_ref.shape, zero_ref.dtype)
        for_tail_blocks(lambda copy: copy.start())

    @pl.when(stop > first)
    def _():
        wg_bf[...] = wg_ref[0, 0].astype(_BF16)
        wu_bf[...] = wu_ref[0, 0].astype(_BF16)
        wd_bf[...] = wd_ref[0, 0].astype(_BF16)

    def block_step(block, carry):
        start_gather(block + ahead, unrolled=True)
        wait_gather(block)
        x_lo, x_hi = _unpack_bf16_halves(xs_ref[block % n_slots], _BF16)
        half = x_lo.shape[1]

        def x_dot(w_bf):
            return _bf16_dot(x_lo, w_bf[:half, :]) + _bf16_dot(x_hi, w_bf[half:, :])

        act = jax.nn.silu(x_dot(wg_bf)) * x_dot(wu_bf)
        y = _bf16_dot(act.astype(_BF16), wd_bf[...])

        @pl.when(block >= n_slots)
        def _():
            out_copy(block - n_slots).wait()

        ys_ref[block % n_slots] = _pack_bf16_halves(y)
        out_copy(block).start()
        return carry

    lax.fori_loop(first, stop, block_step, 0)

    @pl.when(e == n_experts - 1)
    def _():
        for extra in range(ahead):
            wait_gather(n_total + extra)
        for_tail_blocks(lambda copy: copy.wait())
        for back in range(1, n_slots + 1):
            @pl.when(n_total >= back)
            def _():
                out_copy(n_total - back).wait()


def _expert_mlp(bounds, row_tok, hn, layer, w_gate, w_up, w_down):
    n_experts, d, de = w_gate.shape[1:]
    rows = EXPERT_ROWS
    assert hn.shape[1] * 2 == d and hn.dtype == jnp.uint32
    grid_spec = pltpu.PrefetchScalarGridSpec(
        num_scalar_prefetch=2,
        grid=(n_experts,),
        in_specs=[
            pl.BlockSpec(memory_space=pl.ANY),
            pl.BlockSpec((1, 1, d, de), lambda e, bd, rt: (layer, e, 0, 0)),
            pl.BlockSpec((1, 1, d, de), lambda e, bd, rt: (layer, e, 0, 0)),
            pl.BlockSpec((1, 1, de, d), lambda e, bd, rt: (layer, e, 0, 0)),
        ],
        out_specs=pl.BlockSpec(memory_space=pl.ANY),
        scratch_shapes=[
            pltpu.VMEM((EXPERT_SLOTS, rows, d // 2), jnp.uint32),
            pltpu.VMEM((EXPERT_SLOTS, rows, d // 2), jnp.uint32),
            pltpu.VMEM((rows, d // 2), jnp.uint32),
            pltpu.VMEM((d, de), _BF16),
            pltpu.VMEM((d, de), _BF16),
            pltpu.VMEM((de, d), _BF16),
            pltpu.SemaphoreType.DMA((EXPERT_SLOTS,)),
            pltpu.SemaphoreType.DMA((EXPERT_SLOTS,)),
            pltpu.SemaphoreType.DMA((1,)),
        ],
    )
    return pl.pallas_call(
        _expert_body,
        grid_spec=grid_spec,
        out_shape=jax.ShapeDtypeStruct((row_tok.shape[0], d // 2), jnp.uint32),
        compiler_params=_params("arbitrary"),
        name="expert_mlp",
    )(bounds, row_tok, hn, w_gate, w_up, w_down)


def _combine_body(pos_ref, y_hbm, wts_ref, h_ref, o_ref, ys_ref, sem):
    i = pl.program_id(0)
    n_steps = pl.num_programs(0)
    tm = h_ref.shape[0]
    slot = i % 2

    def row_copy(p, s, k, r):
        return pltpu.make_async_copy(y_hbm.at[pl.ds(p, 1)], ys_ref.at[s, k, pl.ds(r, 1)], sem.at[s])

    def start_gather(tile, s):
        for r in range(tm):
            for k in range(EXPERT_TOPK):
                row_copy(pos_ref[(tile * tm + r) * EXPERT_TOPK + k], s, k, r).start(priority=k)

    def wait_gather(s):
        for k in range(EXPERT_TOPK):
            pltpu.make_async_copy(y_hbm.at[pl.ds(0, tm)], ys_ref.at[s, k], sem.at[s]).wait()

    @pl.when(i == 0)
    def _():
        start_gather(0, 0)

    @pl.when(i + 1 < n_steps)
    def _():
        start_gather(i + 1, 1 - slot)

    wait_gather(slot)
    w = wts_ref[...]
    half = ys_ref.shape[3]
    a_lo, a_hi = _unpack_bf16_halves(ys_ref[slot, 0], _F32)
    b_lo, b_hi = _unpack_bf16_halves(ys_ref[slot, 1], _F32)
    o_ref[:, :half] = h_ref[:, :half] + (w[:, 0:1] * a_lo + w[:, 1:2] * b_lo)
    o_ref[:, half:] = h_ref[:, half:] + (w[:, 0:1] * a_hi + w[:, 1:2] * b_hi)


def _combine(pos, y, wts, h):
    n, d = h.shape
    tm = COMBINE_ROWS
    grid_spec = pltpu.PrefetchScalarGridSpec(
        num_scalar_prefetch=1,
        grid=(n // tm,),
        in_specs=[
            pl.BlockSpec(memory_space=pl.ANY),
            pl.BlockSpec((tm, LANES), lambda i, p: (i, 0)),
            pl.BlockSpec((tm, d), lambda i, p: (i, 0)),
        ],
        out_specs=pl.BlockSpec((tm, d), lambda i, p: (i, 0)),
        scratch_shapes=[
            pltpu.VMEM((2, EXPERT_TOPK, tm, d // 2), jnp.uint32),
            pltpu.SemaphoreType.DMA((2,)),
        ],
    )
    return pl.pallas_call(
        _combine_body,
        grid_spec=grid_spec,
        out_shape=jax.ShapeDtypeStruct((n, d), _F32),
        compiler_params=_params("arbitrary"),
        name="combine",
    )(pos, y, wts, h)


def _dispatch_plan(ids, counts):
    n = ids.shape[0]
    rows = EXPERT_ROWS
    m = n * EXPERT_TOPK
    n_blocks = -(-(m + N_EXPERTS * (rows - 1)) // rows)
    flat_e = ids[:, :EXPERT_TOPK].reshape(m)
    rank = ids[:, EXPERT_TOPK:2 * EXPERT_TOPK].reshape(m)
    counts = counts[0, :N_EXPERTS]
    starts = jnp.cumsum(counts) - counts
    padded = (counts + rows - 1) // rows * rows
    pends = jnp.cumsum(padded)
    pstarts = pends - padded
    expert_iota = jnp.arange(N_EXPERTS, dtype=jnp.int32)
    pos = rank + jnp.sum(jnp.where(flat_e[:, None] == expert_iota[None, :], pstarts[None, :], 0), axis=1)
    blk_start = jnp.arange(n_blocks, dtype=jnp.int32) * rows
    blk_e = jnp.minimum(jnp.sum(pends[None, :] <= blk_start[:, None], axis=1), N_EXPERTS - 1)
    blk_e = blk_e.astype(jnp.int32)
    order = jnp.argsort(flat_e)
    sorted_idx = (blk_start - pstarts[blk_e] + starts[blk_e])[:, None] + jnp.arange(rows)[None, :]
    valid = sorted_idx < (starts + counts)[blk_e][:, None]
    row_tok = jnp.where(valid, order[jnp.clip(sorted_idx, 0, m - 1)] // EXPERT_TOPK, 0)
    bounds = jnp.concatenate([pstarts, pends[-1:]]) // rows
    return (bounds.astype(jnp.int32), row_tok.reshape(n_blocks * rows).astype(jnp.int32),
            pos.astype(jnp.int32))


def _hier_moe_residual(h, gain, w_rg, w_re, layer, w_gate, w_up, w_down):
    hn, ids, wts, counts = _router(h, gain, w_rg, w_re)
    bounds, row_tok, pos = _dispatch_plan(ids, counts)
    y = _expert_mlp(bounds, row_tok, hn, layer, w_gate, w_up, w_down)
    return _combine(pos, y, wts, h)


def kernel(x, mix_norm, ffn_norm, conv_w_in, conv_w, conv_w_out, kv_norm, w_kv, k_norm, w_q, q_norm,
           w_o, router_group, router_expert, w_gate, w_up, w_down):
    bsz, seq_len, d = x.shape
    depth = mix_norm.shape[0]
    n_conv = conv_w_in.shape[0]
    assert d == N_HEADS * HEAD_DIM
    h = x.reshape(bsz * seq_len, d)
    kv = None
    for l in range(depth):
        if l < n_conv:
            z = _conv_in(h, mix_norm[l], conv_w_in[l].astype(_BF16), conv_w[l], seq_len)
            h = _matmul_residual(z, conv_w_out[l].astype(_BF16), h)
        else:
            j = l - n_conv
            gains = jnp.stack([mix_norm[l], kv_norm])
            head_gains = jnp.stack([q_norm[j], k_norm])
            if j == 0:
                w = jnp.concatenate([w_q[j], w_kv], axis=1).astype(_BF16)
                q = kv = _qkv_projection(h, gains, w, head_gains, d, d)
            else:
                q = _qkv_projection(h, gains, w_q[j].astype(_BF16), head_gains, d, 0)
            attn = _moba_attention(q, 0, kv, d, 2 * d, bsz, seq_len)
            h = _matmul_residual(attn, w_o[j].astype(_BF16), h)
        h = _hier_moe_residual(h, ffn_norm[l], router_group[l], router_expert[l],
                               l, w_gate, w_up, w_down)
    return h.reshape(bsz, seq_len, d)
```

```python
import functools

import jax
import jax.numpy as jnp
from jax import lax
from jax.experimental import pallas as pl
from jax.experimental.pallas import tpu as pltpu

N_HEADS = 16
HEAD_DIM = 128
CONV_WIDTH = 3
MOBA_BLOCK = 256
MOBA_TOPK = 3
N_GROUPS = 4
EXPERTS_PER_GROUP = 8
N_EXPERTS = N_GROUPS * EXPERTS_PER_GROUP
EXPERT_TOPK = 2
NORM_EPS = 1e-6
ALIBI_MAX_BIAS = 8.0

LANES = 128
SUBLANES = 8
VMEM_LIMIT = 56 * 1024 * 1024
QKV_VMEM_LIMIT = 60 * 1024 * 1024

ROW_TILE = 1024
COL_TILE = 512
CONV_COL_TILE = 256
ROUTER_ROWS = 512
EXPERT_ROWS = 128
EXPERT_SLOTS = 4
COMBINE_ROWS = 256
ATTN_HEADS = 4

_BF16 = jnp.bfloat16
_F32 = jnp.float32


def _params(*semantics, vmem_limit=VMEM_LIMIT):
    return pltpu.CompilerParams(dimension_semantics=semantics, vmem_limit_bytes=vmem_limit)


def _rms_normalise(x, gain):
    ms = jnp.mean(x * x, axis=-1, keepdims=True)
    return x * lax.rsqrt(ms + NORM_EPS) * gain


def _bf16_dot(a, b):
    return jnp.dot(a, b, preferred_element_type=_F32)


def _pack_bf16_halves(x):
    c = x.shape[1] // 2
    as_bits = lambda v: lax.bitcast_convert_type(v.astype(_BF16).astype(_F32), jnp.uint32)
    return (as_bits(x[:, :c]) >> 16) | (as_bits(x[:, c:]) & jnp.uint32(0xFFFF0000))


def _unpack_bf16_halves(p, dtype):
    lo = lax.bitcast_convert_type(p << 16, _F32).astype(dtype)
    hi = lax.bitcast_convert_type(p & jnp.uint32(0xFFFF0000), _F32).astype(dtype)
    return lo, hi


def _qkv_body(x_ref, g_ref, wq_ref, wkv_ref, hg_ref, o_ref, xn_ref, *, q_tiles, qk_tiles):
    j = pl.program_id(1)

    @pl.when(j == 0)
    def _():
        x = x_ref[...]
        normed = x * lax.rsqrt(jnp.mean(x * x, axis=-1, keepdims=True) + NORM_EPS)
        xn_ref[0] = (normed * g_ref[0:1, :]).astype(_BF16)
        xn_ref[1] = (normed * g_ref[1:2, :]).astype(_BF16)

    def project(stream, head_gain_row):
        w_ref = wq_ref if stream == 0 else wkv_ref
        y = _bf16_dot(xn_ref[stream], w_ref[...].astype(_BF16))
        if head_gain_row is None:
            o_ref[...] = y.astype(o_ref.dtype)
            return
        head_gain = hg_ref[head_gain_row:head_gain_row + 1, :]
        for hh in range(y.shape[1] // HEAD_DIM):
            cols = slice(hh * HEAD_DIM, (hh + 1) * HEAD_DIM)
            o_ref[:, cols] = _rms_normalise(y[:, cols], head_gain).astype(o_ref.dtype)

    pl.when(j < q_tiles)(functools.partial(project, 0, 0))
    pl.when((j >= q_tiles) & (j < qk_tiles))(functools.partial(project, 1, 1))
    pl.when(j >= qk_tiles)(functools.partial(project, 1, None))


def _qkv_projection(x, gains, w_q, w_kv, head_gains, with_kv):
    n, d = x.shape
    tm, tn = ROW_TILE, COL_TILE
    n_q, n_k = w_q.shape[1], w_kv.shape[1] // 2
    n_out = n_q + 2 * n_k if with_kv else n_q
    assert n_q % tn == 0 and n_k % tn == 0
    q_tiles = n_q // tn
    return pl.pallas_call(
        functools.partial(_qkv_body, q_tiles=q_tiles, qk_tiles=(n_q + n_k) // tn),
        grid=(n // tm, n_out // tn),
        in_specs=[
            pl.BlockSpec((tm, d), lambda i, j: (i, 0)),
            pl.BlockSpec((2, d), lambda i, j: (0, 0)),
            pl.BlockSpec((d, tn), lambda i, j: (0, jnp.minimum(j, q_tiles - 1))),
            pl.BlockSpec((d, tn), lambda i, j: (0, jnp.maximum(j - q_tiles, 0))),
            pl.BlockSpec((2, HEAD_DIM), lambda i, j: (0, 0)),
        ],
        out_specs=pl.BlockSpec((tm, tn), lambda i, j: (i, j)),
        out_shape=jax.ShapeDtypeStruct((n, n_out), _BF16),
        scratch_shapes=[pltpu.VMEM((2, tm, d), _BF16)],
        compiler_params=_params("arbitrary", "arbitrary", vmem_limit=QKV_VMEM_LIMIT),
        name="qkv_projection",
    )(x, gains, w_q, w_kv, head_gains)


def _conv_in_body(x_ref, g_ref, wb_ref, wc_ref, wx_ref, cw_ref, z_ref, xn_ref, carry_ref,
                  *, tiles_per_seq):
    i = pl.program_id(0)
    j = pl.program_id(1)
    tm = x_ref.shape[0]

    @pl.when(j == 0)
    def _():
        xn_ref[...] = _rms_normalise(x_ref[...], g_ref[...]).astype(_BF16)

    @pl.when(i % tiles_per_seq == 0)
    def _():
        carry_ref[j] = jnp.zeros(carry_ref.shape[1:], _F32)

    xn = xn_ref[...]
    b_gate = _bf16_dot(xn, wb_ref[...].astype(_BF16))
    c_gate = _bf16_dot(xn, wc_ref[...].astype(_BF16))
    xh = _bf16_dot(xn, wx_ref[...].astype(_BF16))
    u = c_gate * xh
    prev = carry_ref[j]
    row = lax.broadcasted_iota(jnp.int32, u.shape, 0)
    last = prev[SUBLANES - 1:SUBLANES]
    u1 = jnp.where(row == 0, last, pltpu.roll(u, 1, axis=0))
    u2 = jnp.where(row == 0, prev[SUBLANES - 2:SUBLANES - 1],
                   jnp.where(row == 1, last, pltpu.roll(u, 2, axis=0)))
    cw = cw_ref[...]
    conv = cw[0:1] * u2 + cw[1:2] * u1 + cw[2:3] * u
    z_ref[...] = (b_gate * conv).astype(z_ref.dtype)
    carry_ref[j] = u[tm - SUBLANES:tm]


def _conv_in(x, gain, w_in, conv_w, seq_len):
    n, d = x.shape
    tm, tn = ROW_TILE, CONV_COL_TILE
    nj = d // tn
    assert CONV_WIDTH - 1 <= SUBLANES and seq_len % tm == 0
    return pl.pallas_call(
        functools.partial(_conv_in_body, tiles_per_seq=seq_len // tm),
        grid=(n // tm, nj),
        in_specs=[
            pl.BlockSpec((tm, d), lambda i, j: (i, 0)),
            pl.BlockSpec((1, d), lambda i, j: (0, 0)),
            pl.BlockSpec((d, tn), lambda i, j: (0, j)),
            pl.BlockSpec((d, tn), lambda i, j: (0, j + nj)),
            pl.BlockSpec((d, tn), lambda i, j: (0, j + 2 * nj)),
            pl.BlockSpec((CONV_WIDTH, tn), lambda i, j: (0, j)),
        ],
        out_specs=pl.BlockSpec((tm, tn), lambda i, j: (i, j)),
        out_shape=jax.ShapeDtypeStruct((n, d), _BF16),
        scratch_shapes=[pltpu.VMEM((tm, d), _BF16), pltpu.VMEM((nj, SUBLANES, tn), _F32)],
        compiler_params=_params("arbitrary", "arbitrary"),
        name="conv_in",
    )(x, gain.reshape(1, d), w_in, w_in, w_in, conv_w)


def _matmul_residual_body(x_ref, w_ref, r_ref, o_ref):
    o_ref[...] = r_ref[...] + _bf16_dot(x_ref[...], w_ref[...].astype(_BF16))


def _matmul_residual(x, w, res):
    n, k = x.shape
    d = w.shape[1]
    tm, tn = ROW_TILE, COL_TILE
    return pl.pallas_call(
        _matmul_residual_body,
        grid=(n // tm, d // tn),
        in_specs=[
            pl.BlockSpec((tm, k), lambda i, j: (i, 0)),
            pl.BlockSpec((k, tn), lambda i, j: (0, j)),
            pl.BlockSpec((tm, tn), lambda i, j: (i, j)),
        ],
        out_specs=pl.BlockSpec((tm, tn), lambda i, j: (i, j)),
        out_shape=jax.ShapeDtypeStruct((n, d), _F32),
        compiler_params=_params("arbitrary", "arbitrary"),
        name="matmul_residual",
    )(x, w, res)


def _moba_body(q_ref, k_ref, v_ref, o_ref, km_ref, vt_ref, bias_ref):
    heads = range(q_ref.shape[1] // HEAD_DIM)
    head0 = pl.program_id(1) * len(heads)
    i = pl.program_id(2)
    kb = MOBA_BLOCK
    n_blocks = k_ref.shape[0] // kb
    nt = (((1,), (1,)), ((), ()))
    log2e = 1.4426950408889634
    scale = HEAD_DIM ** -0.5 * log2e
    slopes = [log2e * jnp.exp2(jnp.full((1, kb), -ALIBI_MAX_BIAS / N_HEADS, _F32)
                               * (head0 + hh + 1).astype(_F32)) for hh in heads]

    def lanes(hh):
        return slice(hh * HEAD_DIM, (hh + 1) * HEAD_DIM)

    @pl.when(i == 0)
    def _():
        rel = (lax.broadcasted_iota(jnp.int32, (kb, kb), 1)
               - lax.broadcasted_iota(jnp.int32, (kb, kb), 0))
        for hh in heads:
            for jb in range(n_blocks):
                rows = slice(jb * kb, (jb + 1) * kb)
                km_ref[hh, jb:jb + 1, :] = jnp.mean(k_ref[rows, lanes(hh)].astype(_F32),
                                                    axis=0, keepdims=True)
                vt_ref[hh, :, rows] = v_ref[rows, lanes(hh)].astype(_F32).T.astype(_BF16)
            bias = slopes[hh] * rel.astype(_F32)
            bias_ref[hh, 0] = bias
            bias_ref[hh, 1] = jnp.where(rel >= 0, bias, jnp.inf)

    q = [q_ref[:, lanes(hh)] for hh in heads]
    sel = []
    for hh in heads:
        gate = lax.dot_general(km_ref[hh], q[hh].astype(_F32), nt,
                               precision=lax.Precision.HIGHEST, preferred_element_type=_F32)
        blk = lax.broadcasted_iota(jnp.int32, gate.shape, 0)
        gate = jnp.where(blk < i, gate, -jnp.inf)
        picked = jnp.zeros(gate.shape, _F32)
        for jb in range(n_blocks - 1):
            g_jb = gate[jb:jb + 1, :]
            beats = (gate > g_jb) | ((gate == g_jb) & (blk < jb))
            n_beat = jnp.sum(beats.astype(_F32), axis=0, keepdims=True)
            picked = jnp.where((blk == jb) & (blk < i) & (n_beat < MOBA_TOPK), 1.0, picked)
        sel.append(picked)

    def attend(own):
        nb = own + 1
        raw = [lax.dot_general(k_ref[0:nb * kb, lanes(hh)], q[hh], nt, preferred_element_type=_F32)
               for hh in heads]
        tiles, col_max = [[] for _ in heads], [[] for _ in heads]
        for jb in range(nb):
            for hh in heads:
                tile = raw[hh][jb * kb:(jb + 1) * kb, :] * scale - bias_ref[hh, 1 if jb == own else 0]
                mx = jnp.max(tile, axis=0, keepdims=True)
                if jb != own:
                    mx = jnp.where(sel[hh][jb:jb + 1, :] > 0.5,
                                   mx - slopes[hh] * float((own - jb) * kb), -jnp.inf)
                tiles[hh].append(tile), col_max[hh].append(mx)
        m = [functools.reduce(jnp.maximum, col_max[hh]) for hh in heads]
        probs, l = [[] for _ in heads], [None for _ in heads]
        for jb in range(nb):
            for hh in heads:
                offset = m[hh] if jb == own else jnp.where(
                    sel[hh][jb:jb + 1, :] > 0.5, m[hh] + slopes[hh] * float((own - jb) * kb), jnp.inf)
                p = jnp.exp2(tiles[hh][jb] - offset)
                p_sum = jnp.sum(p, axis=0, keepdims=True)
                l[hh] = p_sum if l[hh] is None else l[hh] + p_sum
                probs[hh].append(p.astype(_BF16))
        for hh in heads:
            p_all = probs[hh][0] if nb == 1 else jnp.concatenate(probs[hh], axis=0)
            acc = _bf16_dot(vt_ref[hh, :, 0:nb * kb], p_all)
            o_ref[:, lanes(hh)] = (acc / l[hh]).T.astype(o_ref.dtype)

    for own in range(n_blocks):
        pl.when(i == own)(functools.partial(attend, own))


def _moba_attention(q, q_col, kv, k_col, v_col, batch, seq_len):
    n = q.shape[0]
    d = N_HEADS * HEAD_DIM
    kb = MOBA_BLOCK
    nqb = seq_len // kb
    hp = ATTN_HEADS
    width = hp * HEAD_DIM
    assert seq_len % kb == 0 and nqb == SUBLANES and N_HEADS % hp == 0
    assert q_col % width == 0 and k_col % width == 0 and v_col % width == 0
    q_blk, k_blk, v_blk = q_col // width, k_col // width, v_col // width
    return pl.pallas_call(
        _moba_body,
        grid=(batch, N_HEADS // hp, nqb),
        in_specs=[
            pl.BlockSpec((kb, width), lambda b, h, i: (b * nqb + i, q_blk + h)),
            pl.BlockSpec((seq_len, width), lambda b, h, i: (b, k_blk + h)),
            pl.BlockSpec((seq_len, width), lambda b, h, i: (b, v_blk + h)),
        ],
        out_specs=pl.BlockSpec((kb, width), lambda b, h, i: (b * nqb + i, h)),
        out_shape=jax.ShapeDtypeStruct((n, d), _BF16),
        scratch_shapes=[pltpu.VMEM((hp, nqb, HEAD_DIM), _F32),
                        pltpu.VMEM((hp, HEAD_DIM, seq_len), _BF16),
                        pltpu.VMEM((hp, 2, kb, kb), _F32)],
        compiler_params=_params("arbitrary", "arbitrary", "arbitrary"),
        name="moba_attention",
    )(q, kv, kv)


def _router_body(x_ref, g_ref, wr_ref, hn_ref, ids_ref, wts_ref, cnt_ref, run_ref):
    step = pl.program_id(0)
    hn = _rms_normalise(x_ref[...], g_ref[...])
    hn_ref[...] = _pack_bf16_halves(hn)
    logits = lax.dot_general(wr_ref[...], hn, (((1,), (1,)), ((), ())),
                             precision=lax.Precision.HIGHEST, preferred_element_type=_F32)
    row = lax.broadcasted_iota(jnp.int32, logits.shape, 0)
    far = jnp.int32(logits.shape[0])

    def first_row(mask):
        return jnp.min(jnp.where(mask, row, far), axis=0, keepdims=True)

    is_group = (row >= N_EXPERTS) & (row < N_EXPERTS + N_GROUPS)
    g_max = jnp.max(jnp.where(is_group, logits, -jnp.inf), axis=0, keepdims=True)
    g_sel = first_row(is_group & (logits == g_max)) - N_EXPERTS
    g_w = 1.0 / jnp.sum(jnp.where(is_group, jnp.exp(logits - g_max), 0.0), axis=0, keepdims=True)

    in_group = (row >= g_sel * EXPERTS_PER_GROUP) & (row < (g_sel + 1) * EXPERTS_PER_GROUP)
    top1 = jnp.max(jnp.where(in_group, logits, -jnp.inf), axis=0, keepdims=True)
    idx1 = first_row(in_group & (logits == top1))
    rest = in_group & (row != idx1)
    top2 = jnp.max(jnp.where(rest, logits, -jnp.inf), axis=0, keepdims=True)
    idx2 = first_row(rest & (logits == top2))
    t = jnp.exp(top2 - top1)
    w1 = 1.0 / (1.0 + t) * g_w
    w2 = t / (1.0 + t) * g_w

    @pl.when(step == 0)
    def _():
        run_ref[...] = jnp.zeros(run_ref.shape, _F32)

    tm = logits.shape[1]
    chosen = jnp.where((row == idx1) | (row == idx2), 1.0, 0.0)
    earlier = (lax.broadcasted_iota(jnp.int32, (tm, tm), 0)
               < lax.broadcasted_iota(jnp.int32, (tm, tm), 1))
    before = run_ref[:, 0:1] + _bf16_dot(chosen.astype(_BF16),
                                         jnp.where(earlier, 1.0, 0.0).astype(_BF16))
    rank1 = jnp.sum(jnp.where(row == idx1, before, 0.0), axis=0, keepdims=True).astype(jnp.int32)
    rank2 = jnp.sum(jnp.where(row == idx2, before, 0.0), axis=0, keepdims=True).astype(jnp.int32)
    run_ref[...] += jnp.sum(chosen, axis=1, keepdims=True)

    out_row = lax.broadcasted_iota(jnp.int32, ids_ref.shape, 0)
    ids_ref[...] = jnp.where(out_row == 0, idx1, jnp.where(out_row == 1, idx2,
                             jnp.where(out_row == 2, rank1, jnp.where(out_row == 3, rank2, 0))))
    wts_ref[...] = jnp.where(row == 0, w1, jnp.where(row == 1, w2, 0.0)).T
    cnt_ref[...] = run_ref[...].astype(jnp.int32)


def _router(x, gain, w_rg, w_re):
    n, d = x.shape
    tm = ROUTER_ROWS
    assert EXPERT_TOPK == 2 and N_EXPERTS + N_GROUPS <= LANES
    wr = jnp.concatenate(
        [w_re.T, w_rg.T, jnp.zeros((LANES - N_EXPERTS - N_GROUPS, d), _F32)], axis=0)
    row = lambda i: (i, 0)
    col = lambda i: (0, i)
    fixed = lambda i: (0, 0)
    return pl.pallas_call(
        _router_body,
        grid=(n // tm,),
        in_specs=[
            pl.BlockSpec((tm, d), row),
            pl.BlockSpec((1, d), fixed),
            pl.BlockSpec((LANES, d), fixed),
        ],
        out_specs=[pl.BlockSpec((tm, d // 2), row), pl.BlockSpec((SUBLANES, tm), col),
                   pl.BlockSpec((tm, LANES), row), pl.BlockSpec((LANES, LANES), fixed)],
        out_shape=[jax.ShapeDtypeStruct((n, d // 2), jnp.uint32),
                   jax.ShapeDtypeStruct((SUBLANES, n), jnp.int32),
                   jax.ShapeDtypeStruct((n, LANES), _F32),
                   jax.ShapeDtypeStruct((LANES, LANES), jnp.int32)],
        scratch_shapes=[pltpu.VMEM((LANES, LANES), _F32)],
        compiler_params=_params("arbitrary"),
        name="router",
    )(x, gain.reshape(1, d), wr)


def _expert_body(bounds_ref, row_tok_ref, hn_hbm, wg_ref, wu_ref, wd_ref, y_hbm,
                 xs_ref, ys_ref, zero_ref, wg_bf, wu_bf, wd_bf, gather_sem, out_sem, tail_sem):
    e = pl.program_id(0)
    n_experts = pl.num_programs(0)
    n_slots, rows = xs_ref.shape[0], xs_ref.shape[1]
    ahead = n_slots - 1
    first, stop = bounds_ref[e], bounds_ref[e + 1]
    n_total = bounds_ref[n_experts]

    def row_copy(tok, s, r):
        return pltpu.make_async_copy(hn_hbm.at[pl.ds(tok, 1)], xs_ref.at[s, pl.ds(r, 1)],
                                     gather_sem.at[s])

    def start_gather(block, unrolled):
        s = block % n_slots
        base = jnp.minimum(block, n_total - 1) * rows
        if unrolled:
            for r in range(rows):
                row_copy(row_tok_ref[base + r], s, r).start(priority=1)
        else:
            def body(r, carry):
                row_copy(row_tok_ref[base + r], s, r).start(priority=1)
                return carry
            lax.fori_loop(0, rows, body, 0)

    def wait_gather(block):
        s = block % n_slots
        pltpu.make_async_copy(hn_hbm.at[pl.ds(0, rows)], xs_ref.at[s], gather_sem.at[s]).wait()

    def out_copy(block):
        s = block % n_slots
        dst = y_hbm.at[pl.ds(pl.multiple_of(block * rows, rows), rows)]
        return pltpu.make_async_copy(ys_ref.at[s], dst, out_sem.at[s])

    def tail_copy(block):
        dst = y_hbm.at[pl.ds(pl.multiple_of(block * rows, rows), rows)]
        return pltpu.make_async_copy(zero_ref, dst, tail_sem.at[0])

    def for_tail_blocks(fn):
        def body(block, carry):
            fn(tail_copy(block))
            return carry
        lax.fori_loop(n_total, y_hbm.shape[0] // rows, body, 0)

    @pl.when(e == 0)
    def _():
        for block in range(ahead):
            start_gather(block, unrolled=False)
        zero_ref[...] = jnp.zeros(zero_ref.shape, zero_ref.dtype)
        for_tail_blocks(lambda copy: copy.start())

    @pl.when(stop > first)
    def _():
        wg_bf[...] = wg_ref[0, 0].astype(_BF16)
        wu_bf[...] = wu_ref[0, 0].astype(_BF16)
        wd_bf[...] = wd_ref[0, 0].astype(_BF16)

    def block_step(block, carry):
        start_gather(block + ahead, unrolled=True)
        wait_gather(block)
        x_lo, x_hi = _unpack_bf16_halves(xs_ref[block % n_slots], _BF16)
        half = x_lo.shape[1]

        def x_dot(w_bf):
            return _bf16_dot(x_lo, w_bf[:half, :]) + _bf16_dot(x_hi, w_bf[half:, :])

        act = jax.nn.silu(x_dot(wg_bf)) * x_dot(wu_bf)
        y = _bf16_dot(act.astype(_BF16), wd_bf[...])

        @pl.when(block >= n_slots)
        def _():
            out_copy(block - n_slots).wait()

        ys_ref[block % n_slots] = _pack_bf16_halves(y)
        out_copy(block).start()
        return carry

    lax.fori_loop(first, stop, block_step, 0)

    @pl.when(e == n_experts - 1)
    def _():
        for extra in range(ahead):
            wait_gather(n_total + extra)
        for_tail_blocks(lambda copy: copy.wait())
        for back in range(1, n_slots + 1):
            @pl.when(n_total >= back)
            def _():
                out_copy(n_total - back).wait()


def _expert_mlp(bounds, row_tok, hn, layer, w_gate, w_up, w_down):
    n_experts, d, de = w_gate.shape[1:]
    rows = EXPERT_ROWS
    assert hn.shape[1] * 2 == d and hn.dtype == jnp.uint32
    grid_spec = pltpu.PrefetchScalarGridSpec(
        num_scalar_prefetch=2,
        grid=(n_experts,),
        in_specs=[
            pl.BlockSpec(memory_space=pl.ANY),
            pl.BlockSpec((1, 1, d, de), lambda e, bd, rt: (layer, e, 0, 0)),
            pl.BlockSpec((1, 1, d, de), lambda e, bd, rt: (layer, e, 0, 0)),
            pl.BlockSpec((1, 1, de, d), lambda e, bd, rt: (layer, e, 0, 0)),
        ],
        out_specs=pl.BlockSpec(memory_space=pl.ANY),
        scratch_shapes=[
            pltpu.VMEM((EXPERT_SLOTS, rows, d // 2), jnp.uint32),
            pltpu.VMEM((EXPERT_SLOTS, rows, d // 2), jnp.uint32),
            pltpu.VMEM((rows, d // 2), jnp.uint32),
            pltpu.VMEM((d, de), _BF16),
            pltpu.VMEM((d, de), _BF16),
            pltpu.VMEM((de, d), _BF16),
            pltpu.SemaphoreType.DMA((EXPERT_SLOTS,)),
            pltpu.SemaphoreType.DMA((EXPERT_SLOTS,)),
            pltpu.SemaphoreType.DMA((1,)),
        ],
    )
    return pl.pallas_call(
        _expert_body,
        grid_spec=grid_spec,
        out_shape=jax.ShapeDtypeStruct((row_tok.shape[0], d // 2), jnp.uint32),
        compiler_params=_params("arbitrary"),
        name="expert_mlp",
    )(bounds, row_tok, hn, w_gate, w_up, w_down)


def _combine_body(pos_ref, y_hbm, wts_ref, h_ref, o_ref, ys_ref, sem):
    i = pl.program_id(0)
    n_steps = pl.num_programs(0)
    tm = h_ref.shape[0]
    n_tokens = n_steps * tm
    slot = i % 2

    def row_copy(p, s, k, r):
        return pltpu.make_async_copy(y_hbm.at[pl.ds(p, 1)], ys_ref.at[s, k, pl.ds(r, 1)], sem.at[s])

    def start_gather(tile, s):
        for r in range(tm):
            for k in range(EXPERT_TOPK):
                row_copy(pos_ref[k * n_tokens + tile * tm + r], s, k, r).start(priority=k)

    def wait_gather(s):
        for k in range(EXPERT_TOPK):
            pltpu.make_async_copy(y_hbm.at[pl.ds(0, tm)], ys_ref.at[s, k], sem.at[s]).wait()

    @pl.when(i == 0)
    def _():
        start_gather(0, 0)

    @pl.when(i + 1 < n_steps)
    def _():
        start_gather(i + 1, 1 - slot)

    wait_gather(slot)
    w = wts_ref[...]
    half = ys_ref.shape[3]
    a_lo, a_hi = _unpack_bf16_halves(ys_ref[slot, 0], _F32)
    b_lo, b_hi = _unpack_bf16_halves(ys_ref[slot, 1], _F32)
    o_ref[:, :half] = h_ref[:, :half] + (w[:, 0:1] * a_lo + w[:, 1:2] * b_lo)
    o_ref[:, half:] = h_ref[:, half:] + (w[:, 0:1] * a_hi + w[:, 1:2] * b_hi)


def _combine(pos, y, wts, h):
    n, d = h.shape
    tm = COMBINE_ROWS
    grid_spec = pltpu.PrefetchScalarGridSpec(
        num_scalar_prefetch=1,
        grid=(n // tm,),
        in_specs=[
            pl.BlockSpec(memory_space=pl.ANY),
            pl.BlockSpec((tm, LANES), lambda i, p: (i, 0)),
            pl.BlockSpec((tm, d), lambda i, p: (i, 0)),
        ],
        out_specs=pl.BlockSpec((tm, d), lambda i, p: (i, 0)),
        scratch_shapes=[
            pltpu.VMEM((2, EXPERT_TOPK, tm, d // 2), jnp.uint32),
            pltpu.SemaphoreType.DMA((2,)),
        ],
    )
    return pl.pallas_call(
        _combine_body,
        grid_spec=grid_spec,
        out_shape=jax.ShapeDtypeStruct((n, d), _F32),
        compiler_params=_params("arbitrary"),
        name="combine",
    )(pos, y, wts, h)


def _dispatch_plan(ids, counts):
    n = ids.shape[1]
    rows = EXPERT_ROWS
    m = n * EXPERT_TOPK
    n_blocks = -(-(m + N_EXPERTS * (rows - 1)) // rows)
    flat_e = ids[:EXPERT_TOPK].reshape(m)
    rank = ids[EXPERT_TOPK:2 * EXPERT_TOPK].reshape(m)
    counts = counts[:N_EXPERTS, 0]
    starts = jnp.cumsum(counts) - counts
    padded = (counts + rows - 1) // rows * rows
    pends = jnp.cumsum(padded)
    pstarts = pends - padded
    pos = rank
    for e in range(N_EXPERTS):
        pos = pos + jnp.where(flat_e == e, pstarts[e], 0)
    blk_start = jnp.arange(n_blocks, dtype=jnp.int32) * rows
    blk_e = jnp.minimum(jnp.sum(pends[None, :] <= blk_start[:, None], axis=1), N_EXPERTS - 1)
    token = jnp.arange(m, dtype=jnp.int32) % n
    order = jnp.argsort(flat_e * n + token)
    sorted_idx = (blk_start - pstarts[blk_e] + starts[blk_e])[:, None] + jnp.arange(rows)[None, :]
    valid = sorted_idx < (starts + counts)[blk_e][:, None]
    row_tok = jnp.where(valid, order[jnp.clip(sorted_idx, 0, m - 1)] % n, 0)
    bounds = jnp.concatenate([pstarts, pends[-1:]]) // rows
    return (bounds.astype(jnp.int32), row_tok.reshape(n_blocks * rows).astype(jnp.int32),
            pos.astype(jnp.int32))


def _hier_moe_residual(h, gain, w_rg, w_re, layer, w_gate, w_up, w_down):
    hn, ids, wts, counts = _router(h, gain, w_rg, w_re)
    bounds, row_tok, pos = _dispatch_plan(ids, counts)
    y = _expert_mlp(bounds, row_tok, hn, layer, w_gate, w_up, w_down)
    return _combine(pos, y, wts, h)


def kernel(x, mix_norm, ffn_norm, conv_w_in, conv_w, conv_w_out, kv_norm, w_kv, k_norm, w_q, q_norm,
           w_o, router_group, router_expert, w_gate, w_up, w_down):
    bsz, seq_len, d = x.shape
    depth = mix_norm.shape[0]
    n_conv = conv_w_in.shape[0]
    assert d == N_HEADS * HEAD_DIM
    h = x.reshape(bsz * seq_len, d)
    kv = None
    for l in range(depth):
        if l < n_conv:
            z = _conv_in(h, mix_norm[l], conv_w_in[l], conv_w[l], seq_len)
            h = _matmul_residual(z, conv_w_out[l], h)
        else:
            j = l - n_conv
            gains = jnp.stack([mix_norm[l], kv_norm])
            head_gains = jnp.stack([q_norm[j], k_norm])
            q = _qkv_projection(h, gains, w_q[j], w_kv, head_gains, with_kv=(j == 0))
            if j == 0:
                kv = q
            attn = _moba_attention(q, 0, kv, d, 2 * d, bsz, seq_len)
            h = _matmul_residual(attn, w_o[j], h)
        h = _hier_moe_residual(h, ffn_norm[l], router_group[l], router_expert[l],
                               l, w_gate, w_up, w_down)
    return h.reshape(bsz, seq_len, d)
```

```python
import functools

import jax
import jax.numpy as jnp
from jax import lax
from jax.experimental import pallas as pl
from jax.experimental.pallas import tpu as pltpu

N_HEADS = 16
HEAD_DIM = 128
CONV_WIDTH = 3
MOBA_BLOCK = 256
MOBA_TOPK = 3
N_GROUPS = 4
EXPERTS_PER_GROUP = 8
N_EXPERTS = N_GROUPS * EXPERTS_PER_GROUP
EXPERT_TOPK = 2
NORM_EPS = 1e-6
ALIBI_MAX_BIAS = 8.0

LANES = 128
SUBLANES = 8
VMEM_LIMIT = 56 * 1024 * 1024
QKV_VMEM_LIMIT = 60 * 1024 * 1024

ROW_TILE = 1024
COL_TILE = 512
CONV_COL_TILE = 256
ROUTER_ROWS = 512
EXPERT_ROWS = 128
EXPERT_SLOTS = 4
COMBINE_ROWS = 256
ATTN_HEADS = 4

_BF16 = jnp.bfloat16
_F32 = jnp.float32


def _params(*semantics, vmem_limit=VMEM_LIMIT):
    return pltpu.CompilerParams(dimension_semantics=semantics, vmem_limit_bytes=vmem_limit)


def _rms_normalise(x, gain):
    ms = jnp.mean(x * x, axis=-1, keepdims=True)
    return x * lax.rsqrt(ms + NORM_EPS) * gain


def _bf16_dot(a, b):
    return jnp.dot(a, b, preferred_element_type=_F32)


def _pack_bf16_halves(x):
    c = x.shape[1] // 2
    as_bits = lambda v: lax.bitcast_convert_type(v.astype(_BF16).astype(_F32), jnp.uint32)
    return (as_bits(x[:, :c]) >> 16) | (as_bits(x[:, c:]) & jnp.uint32(0xFFFF0000))


def _unpack_bf16_halves(p, dtype):
    lo = lax.bitcast_convert_type(p << 16, _F32).astype(dtype)
    hi = lax.bitcast_convert_type(p & jnp.uint32(0xFFFF0000), _F32).astype(dtype)
    return lo, hi


def _qkv_body(x_ref, g_ref, wq_ref, wkv_ref, hg_ref, o_ref, xn_ref, *, q_tiles, qk_tiles):
    j = pl.program_id(1)

    @pl.when(j == 0)
    def _():
        x = x_ref[...]
        normed = x * lax.rsqrt(jnp.mean(x * x, axis=-1, keepdims=True) + NORM_EPS)
        xn_ref[0] = (normed * g_ref[0:1, :]).astype(_BF16)
        xn_ref[1] = (normed * g_ref[1:2, :]).astype(_BF16)

    def project(stream, head_gain_row):
        w_ref = wq_ref if stream == 0 else wkv_ref
        y = _bf16_dot(xn_ref[stream], w_ref[...].astype(_BF16))
        if head_gain_row is None:
            o_ref[...] = y.astype(o_ref.dtype)
            return
        head_gain = hg_ref[head_gain_row:head_gain_row + 1, :]
        for hh in range(y.shape[1] // HEAD_DIM):
            cols = slice(hh * HEAD_DIM, (hh + 1) * HEAD_DIM)
            o_ref[:, cols] = _rms_normalise(y[:, cols], head_gain).astype(o_ref.dtype)

    pl.when(j < q_tiles)(functools.partial(project, 0, 0))
    pl.when((j >= q_tiles) & (j < qk_tiles))(functools.partial(project, 1, 1))
    pl.when(j >= qk_tiles)(functools.partial(project, 1, None))


def _qkv_projection(x, gains, w_q, w_kv, head_gains, with_kv):
    n, d = x.shape
    tm, tn = ROW_TILE, COL_TILE
    n_q, n_k = w_q.shape[1], w_kv.shape[1] // 2
    n_out = n_q + 2 * n_k if with_kv else n_q
    assert n_q % tn == 0 and n_k % tn == 0
    q_tiles = n_q // tn
    return pl.pallas_call(
        functools.partial(_qkv_body, q_tiles=q_tiles, qk_tiles=(n_q + n_k) // tn),
        grid=(n // tm, n_out // tn),
        in_specs=[
            pl.BlockSpec((tm, d), lambda i, j: (i, 0)),
            pl.BlockSpec((2, d), lambda i, j: (0, 0)),
            pl.BlockSpec((d, tn), lambda i, j: (0, jnp.minimum(j, q_tiles - 1))),
            pl.BlockSpec((d, tn), lambda i, j: (0, jnp.maximum(j - q_tiles, 0))),
            pl.BlockSpec((2, HEAD_DIM), lambda i, j: (0, 0)),
        ],
        out_specs=pl.BlockSpec((tm, tn), lambda i, j: (i, j)),
        out_shape=jax.ShapeDtypeStruct((n, n_out), _BF16),
        scratch_shapes=[pltpu.VMEM((2, tm, d), _BF16)],
        compiler_params=_params("arbitrary", "arbitrary", vmem_limit=QKV_VMEM_LIMIT),
        name="qkv_projection",
    )(x, gains, w_q, w_kv, head_gains)


def _conv_in_body(x_ref, g_ref, wb_ref, wc_ref, wx_ref, cw_ref, z_ref, xn_ref, carry_ref,
                  *, tiles_per_seq):
    i = pl.program_id(0)
    j = pl.program_id(1)
    tm = x_ref.shape[0]

    @pl.when(j == 0)
    def _():
        xn_ref[...] = _rms_normalise(x_ref[...], g_ref[...]).astype(_BF16)

    @pl.when(i % tiles_per_seq == 0)
    def _():
        carry_ref[j] = jnp.zeros(carry_ref.shape[1:], _F32)

    xn = xn_ref[...]
    b_gate = _bf16_dot(xn, wb_ref[...].astype(_BF16))
    c_gate = _bf16_dot(xn, wc_ref[...].astype(_BF16))
    xh = _bf16_dot(xn, wx_ref[...].astype(_BF16))
    u = c_gate * xh
    prev = carry_ref[j]
    row = lax.broadcasted_iota(jnp.int32, u.shape, 0)
    last = prev[SUBLANES - 1:SUBLANES]
    u1 = jnp.where(row == 0, last, pltpu.roll(u, 1, axis=0))
    u2 = jnp.where(row == 0, prev[SUBLANES - 2:SUBLANES - 1],
                   jnp.where(row == 1, last, pltpu.roll(u, 2, axis=0)))
    cw = cw_ref[...]
    conv = cw[0:1] * u2 + cw[1:2] * u1 + cw[2:3] * u
    z_ref[...] = (b_gate * conv).astype(z_ref.dtype)
    carry_ref[j] = u[tm - SUBLANES:tm]


def _conv_in(x, gain, w_in, conv_w, seq_len):
    n, d = x.shape
    tm, tn = ROW_TILE, CONV_COL_TILE
    nj = d // tn
    assert CONV_WIDTH - 1 <= SUBLANES and seq_len % tm == 0
    return pl.pallas_call(
        functools.partial(_conv_in_body, tiles_per_seq=seq_len // tm),
        grid=(n // tm, nj),
        in_specs=[
            pl.BlockSpec((tm, d), lambda i, j: (i, 0)),
            pl.BlockSpec((1, d), lambda i, j: (0, 0)),
            pl.BlockSpec((d, tn), lambda i, j: (0, j)),
            pl.BlockSpec((d, tn), lambda i, j: (0, j + nj)),
            pl.BlockSpec((d, tn), lambda i, j: (0, j + 2 * nj)),
            pl.BlockSpec((CONV_WIDTH, tn), lambda i, j: (0, j)),
        ],
        out_specs=pl.BlockSpec((tm, tn), lambda i, j: (i, j)),
        out_shape=jax.ShapeDtypeStruct((n, d), _BF16),
        scratch_shapes=[pltpu.VMEM((tm, d), _BF16), pltpu.VMEM((nj, SUBLANES, tn), _F32)],
        compiler_params=_params("arbitrary", "arbitrary"),
        name="conv_in",
    )(x, gain.reshape(1, d), w_in, w_in, w_in, conv_w)


def _matmul_residual_body(x_ref, w_ref, r_ref, o_ref):
    o_ref[...] = r_ref[...] + _bf16_dot(x_ref[...], w_ref[...].astype(_BF16))


def _matmul_residual(x, w, res):
    n, k = x.shape
    d = w.shape[1]
    tm, tn = ROW_TILE, COL_TILE
    return pl.pallas_call(
        _matmul_residual_body,
        grid=(n // tm, d // tn),
        in_specs=[
            pl.BlockSpec((tm, k), lambda i, j: (i, 0)),
            pl.BlockSpec((k, tn), lambda i, j: (0, j)),
            pl.BlockSpec((tm, tn), lambda i, j: (i, j)),
        ],
        out_specs=pl.BlockSpec((tm, tn), lambda i, j: (i, j)),
        out_shape=jax.ShapeDtypeStruct((n, d), _F32),
        compiler_params=_params("arbitrary", "arbitrary"),
        name="matmul_residual",
    )(x, w, res)


def _moba_body(q_ref, k_ref, v_ref, o_ref, km_ref, vt_ref, bias_ref):
    heads = range(q_ref.shape[1] // HEAD_DIM)
    head0 = pl.program_id(1) * len(heads)
    i = pl.program_id(2)
    kb = MOBA_BLOCK
    n_blocks = k_ref.shape[0] // kb
    nt = (((1,), (1,)), ((), ()))
    log2e = 1.4426950408889634
    scale = HEAD_DIM ** -0.5 * log2e
    slopes = [log2e * jnp.exp2(jnp.full((1, kb), -ALIBI_MAX_BIAS / N_HEADS, _F32)
                               * (head0 + hh + 1).astype(_F32)) for hh in heads]

    def lanes(hh):
        return slice(hh * HEAD_DIM, (hh + 1) * HEAD_DIM)

    @pl.when(i == 0)
    def _():
        rel = (lax.broadcasted_iota(jnp.int32, (kb, kb), 1)
               - lax.broadcasted_iota(jnp.int32, (kb, kb), 0))
        for hh in heads:
            for jb in range(n_blocks):
                rows = slice(jb * kb, (jb + 1) * kb)
                km_ref[hh, jb:jb + 1, :] = jnp.mean(k_ref[rows, lanes(hh)].astype(_F32),
                                                    axis=0, keepdims=True)
                vt_ref[hh, :, rows] = v_ref[rows, lanes(hh)].astype(_F32).T.astype(_BF16)
            bias = slopes[hh] * rel.astype(_F32)
            bias_ref[hh, 0] = bias
            bias_ref[hh, 1] = jnp.where(rel >= 0, bias, jnp.inf)

    q = [q_ref[:, lanes(hh)] for hh in heads]
    sel = []
    for hh in heads:
        gate = lax.dot_general(km_ref[hh], q[hh].astype(_F32), nt,
                               precision=lax.Precision.HIGHEST, preferred_element_type=_F32)
        blk = lax.broadcasted_iota(jnp.int32, gate.shape, 0)
        gate = jnp.where(blk < i, gate, -jnp.inf)
        picked = jnp.zeros(gate.shape, _F32)
        for jb in range(n_blocks - 1):
            g_jb = gate[jb:jb + 1, :]
            beats = (gate > g_jb) | ((gate == g_jb) & (blk < jb))
            n_beat = jnp.sum(beats.astype(_F32), axis=0, keepdims=True)
            picked = jnp.where((blk == jb) & (blk < i) & (n_beat < MOBA_TOPK), 1.0, picked)
        sel.append(picked)

    def attend(own):
        nb = own + 1
        raw = [lax.dot_general(k_ref[0:nb * kb, lanes(hh)], q[hh], nt, preferred_element_type=_F32)
               for hh in heads]
        tiles, col_max = [[] for _ in heads], [[] for _ in heads]
        for jb in range(nb):
            for hh in heads:
                tile = raw[hh][jb * kb:(jb + 1) * kb, :] * scale - bias_ref[hh, 1 if jb == own else 0]
                mx = jnp.max(tile, axis=0, keepdims=True)
                if jb != own:
                    mx = jnp.where(sel[hh][jb:jb + 1, :] > 0.5,
                                   mx - slopes[hh] * float((own - jb) * kb), -jnp.inf)
                tiles[hh].append(tile), col_max[hh].append(mx)
        m = [functools.reduce(jnp.maximum, col_max[hh]) for hh in heads]
        probs, l = [[] for _ in heads], [None for _ in heads]
        for jb in range(nb):
            for hh in heads:
                offset = m[hh] if jb == own else jnp.where(
                    sel[hh][jb:jb + 1, :] > 0.5, m[hh] + slopes[hh] * float((own - jb) * kb), jnp.inf)
                p = jnp.exp2(tiles[hh][jb] - offset)
                p_sum = jnp.sum(p, axis=0, keepdims=True)
                l[hh] = p_sum if l[hh] is None else l[hh] + p_sum
                probs[hh].append(p.astype(_BF16))
        for hh in heads:
            p_all = probs[hh][0] if nb == 1 else jnp.concatenate(probs[hh], axis=0)
            acc = _bf16_dot(vt_ref[hh, :, 0:nb * kb], p_all)
            o_ref[:, lanes(hh)] = (acc / l[hh]).T.astype(o_ref.dtype)

    for own in range(n_blocks):
        pl.when(i == own)(functools.partial(attend, own))


def _moba_attention(q, q_col, kv, k_col, v_col, batch, seq_len):
    n = q.shape[0]
    d = N_HEADS * HEAD_DIM
    kb = MOBA_BLOCK
    nqb = seq_len // kb
    hp = ATTN_HEADS
    width = hp * HEAD_DIM
    assert seq_len % kb == 0 and nqb == SUBLANES and N_HEADS % hp == 0
    assert q_col % width == 0 and k_col % width == 0 and v_col % width == 0
    q_blk, k_blk, v_blk = q_col // width, k_col // width, v_col // width
    return pl.pallas_call(
        _moba_body,
        grid=(batch, N_HEADS // hp, nqb),
        in_specs=[
            pl.BlockSpec((kb, width), lambda b, h, i: (b * nqb + i, q_blk + h)),
            pl.BlockSpec((seq_len, width), lambda b, h, i: (b, k_blk + h)),
            pl.BlockSpec((seq_len, width), lambda b, h, i: (b, v_blk + h)),
        ],
        out_specs=pl.BlockSpec((kb, width), lambda b, h, i: (b * nqb + i, h)),
        out_shape=jax.ShapeDtypeStruct((n, d), _BF16),
        scratch_shapes=[pltpu.VMEM((hp, nqb, HEAD_DIM), _F32),
                        pltpu.VMEM((hp, HEAD_DIM, seq_len), _BF16),
                        pltpu.VMEM((hp, 2, kb, kb), _F32)],
        compiler_params=_params("arbitrary", "arbitrary", "arbitrary"),
        name="moba_attention",
    )(q, kv, kv)


def _router_body(x_ref, g_ref, wr_ref, hn_ref, ids_ref, wts_ref, cnt_ref, run_ref):
    step = pl.program_id(0)
    hn = _rms_normalise(x_ref[...], g_ref[...])
    hn_ref[...] = _pack_bf16_halves(hn)
    logits = lax.dot_general(wr_ref[...], hn, (((1,), (1,)), ((), ())),
                             precision=lax.Precision.HIGHEST, preferred_element_type=_F32)
    row = lax.broadcasted_iota(jnp.int32, logits.shape, 0)
    far = jnp.int32(logits.shape[0])

    def first_row(mask):
        return jnp.min(jnp.where(mask, row, far), axis=0, keepdims=True)

    is_group = (row >= N_EXPERTS) & (row < N_EXPERTS + N_GROUPS)
    g_max = jnp.max(jnp.where(is_group, logits, -jnp.inf), axis=0, keepdims=True)
    g_sel = first_row(is_group & (logits == g_max)) - N_EXPERTS
    g_w = 1.0 / jnp.sum(jnp.where(is_group, jnp.exp(logits - g_max), 0.0), axis=0, keepdims=True)

    in_group = (row >= g_sel * EXPERTS_PER_GROUP) & (row < (g_sel + 1) * EXPERTS_PER_GROUP)
    top1 = jnp.max(jnp.where(in_group, logits, -jnp.inf), axis=0, keepdims=True)
    idx1 = first_row(in_group & (logits == top1))
    rest = in_group & (row != idx1)
    top2 = jnp.max(jnp.where(rest, logits, -jnp.inf), axis=0, keepdims=True)
    idx2 = first_row(rest & (logits == top2))
    t = jnp.exp(top2 - top1)
    w1 = 1.0 / (1.0 + t) * g_w
    w2 = t / (1.0 + t) * g_w

    @pl.when(step == 0)
    def _():
        run_ref[...] = jnp.zeros(run_ref.shape, _F32)

    tm = logits.shape[1]
    chosen = jnp.where((row == idx1) | (row == idx2), 1.0, 0.0)
    earlier = (lax.broadcasted_iota(jnp.int32, (tm, tm), 0)
               < lax.broadcasted_iota(jnp.int32, (tm, tm), 1))
    before = run_ref[:, 0:1] + _bf16_dot(chosen.astype(_BF16),
                                         jnp.where(earlier, 1.0, 0.0).astype(_BF16))
    rank1 = jnp.sum(jnp.where(row == idx1, before, 0.0), axis=0, keepdims=True).astype(jnp.int32)
    rank2 = jnp.sum(jnp.where(row == idx2, before, 0.0), axis=0, keepdims=True).astype(jnp.int32)
    run_ref[...] += jnp.sum(chosen, axis=1, keepdims=True)

    out_row = lax.broadcasted_iota(jnp.int32, ids_ref.shape, 0)
    ids_ref[...] = jnp.where(out_row == 0, idx1, jnp.where(out_row == 1, idx2,
                             jnp.where(out_row == 2, rank1, jnp.where(out_row == 3, rank2, 0))))
    wts_ref[...] = jnp.where(row == 0, w1, jnp.where(row == 1, w2, 0.0)).T
    cnt_ref[...] = run_ref[...].astype(jnp.int32)


def _router(x, gain, w_rg, w_re):
    n, d = x.shape
    tm = ROUTER_ROWS
    assert EXPERT_TOPK == 2 and N_EXPERTS + N_GROUPS <= LANES
    wr = jnp.concatenate(
        [w_re.T, w_rg.T, jnp.zeros((LANES - N_EXPERTS - N_GROUPS, d), _F32)], axis=0)
    row = lambda i: (i, 0)
    col = lambda i: (0, i)
    fixed = lambda i: (0, 0)
    return pl.pallas_call(
        _router_body,
        grid=(n // tm,),
        in_specs=[
            pl.BlockSpec((tm, d), row),
            pl.BlockSpec((1, d), fixed),
            pl.BlockSpec((LANES, d), fixed),
        ],
        out_specs=[pl.BlockSpec((tm, d // 2), row), pl.BlockSpec((SUBLANES, tm), col),
                   pl.BlockSpec((tm, LANES), row), pl.BlockSpec((LANES, LANES), fixed)],
        out_shape=[jax.ShapeDtypeStruct((n, d // 2), jnp.uint32),
                   jax.ShapeDtypeStruct((SUBLANES, n), jnp.int32),
                   jax.ShapeDtypeStruct((n, LANES), _F32),
                   jax.ShapeDtypeStruct((LANES, LANES), jnp.int32)],
        scratch_shapes=[pltpu.VMEM((LANES, LANES), _F32)],
        compiler_params=_params("arbitrary"),
        name="router",
    )(x, gain.reshape(1, d), wr)


def _expert_body(bounds_ref, row_tok_ref, hn_hbm, wg_ref, wu_ref, wd_ref, y_hbm,
                 xs_ref, ys_ref, zero_ref, wg_bf, wu_bf, wd_bf, gather_sem, out_sem, tail_sem):
    e = pl.program_id(0)
    n_experts = pl.num_programs(0)
    n_slots, rows = xs_ref.shape[0], xs_ref.shape[1]
    ahead = n_slots - 1
    first, stop = bounds_ref[e], bounds_ref[e + 1]
    n_total = bounds_ref[n_experts]

    def row_copy(tok, s, r):
        return pltpu.make_async_copy(hn_hbm.at[pl.ds(tok, 1)], xs_ref.at[s, pl.ds(r, 1)],
                                     gather_sem.at[s])

    def start_gather(block, unrolled):
        s = block % n_slots
        base = jnp.minimum(block, n_total - 1) * rows
        if unrolled:
            for r in range(rows):
                row_copy(row_tok_ref[base + r], s, r).start(priority=1)
        else:
            def body(r, carry):
                row_copy(row_tok_ref[base + r], s, r).start(priority=1)
                return carry
            lax.fori_loop(0, rows, body, 0)

    def wait_gather(block):
        s = block % n_slots
        pltpu.make_async_copy(hn_hbm.at[pl.ds(0, rows)], xs_ref.at[s], gather_sem.at[s]).wait()

    def out_copy(block):
        s = block % n_slots
        dst = y_hbm.at[pl.ds(pl.multiple_of(block * rows, rows), rows)]
        return pltpu.make_async_copy(ys_ref.at[s], dst, out_sem.at[s])

    def tail_copy(block):
        dst = y_hbm.at[pl.ds(pl.multiple_of(block * rows, rows), rows)]
        return pltpu.make_async_copy(zero_ref, dst, tail_sem.at[0])

    def for_tail_blocks(fn):
        def body(block, carry):
            fn(tail_copy(block))
            return carry
        lax.fori_loop(n_total, y_hbm.shape[0] // rows, body, 0)

    @pl.when(e == 0)
    def _():
        for block in range(ahead):
            start_gather(block, unrolled=False)
        zero_ref[...] = jnp.zeros(zero_ref.shape, zero_ref.dtype)
        for_tail_blocks(lambda copy: copy.start())

    @pl.when(stop > first)
    def _():
        wg_bf[...] = wg_ref[0, 0].astype(_BF16)
        wu_bf[...] = wu_ref[0, 0].astype(_BF16)
        wd_bf[...] = wd_ref[0, 0].astype(_BF16)

    def block_step(block, carry):
        start_gather(block + ahead, unrolled=True)
        wait_gather(block)
        x_lo, x_hi = _unpack_bf16_halves(xs_ref[block % n_slots], _BF16)
        half = x_lo.shape[1]

        def x_dot(w_bf):
            return _bf16_dot(x_lo, w_bf[:half, :]) + _bf16_dot(x_hi, w_bf[half:, :])

        act = jax.nn.silu(x_dot(wg_bf)) * x_dot(wu_bf)
        y = _bf16_dot(act.astype(_BF16), wd_bf[...])

        @pl.when(block >= n_slots)
        def _():
            out_copy(block - n_slots).wait()

        ys_ref[block % n_slots] = _pack_bf16_halves(y)
        out_copy(block).start()
        return carry

    lax.fori_loop(first, stop, block_step, 0)

    @pl.when(e == n_experts - 1)
    def _():
        for extra in range(ahead):
            wait_gather(n_total + extra)
        for_tail_blocks(lambda copy: copy.wait())
        for back in range(1, n_slots + 1):
            @pl.when(n_total >= back)
            def _():
                out_copy(n_total - back).wait()


def _expert_mlp(bounds, row_tok, hn, layer, w_gate, w_up, w_down):
    n_experts, d, de = w_gate.shape[1:]
    rows = EXPERT_ROWS
    assert hn.shape[1] * 2 == d and hn.dtype == jnp.uint32
    grid_spec = pltpu.PrefetchScalarGridSpec(
        num_scalar_prefetch=2,
        grid=(n_experts,),
        in_specs=[
            pl.BlockSpec(memory_space=pl.ANY),
            pl.BlockSpec((1, 1, d, de), lambda e, bd, rt: (layer, e, 0, 0)),
            pl.BlockSpec((1, 1, d, de), lambda e, bd, rt: (layer, e, 0, 0)),
            pl.BlockSpec((1, 1, de, d), lambda e, bd, rt: (layer, e, 0, 0)),
        ],
        out_specs=pl.BlockSpec(memory_space=pl.ANY),
        scratch_shapes=[
            pltpu.VMEM((EXPERT_SLOTS, rows, d // 2), jnp.uint32),
            pltpu.VMEM((EXPERT_SLOTS, rows, d // 2), jnp.uint32),
            pltpu.VMEM((rows, d // 2), jnp.uint32),
            pltpu.VMEM((d, de), _BF16),
            pltpu.VMEM((d, de), _BF16),
            pltpu.VMEM((de, d), _BF16),
            pltpu.SemaphoreType.DMA((EXPERT_SLOTS,)),
            pltpu.SemaphoreType.DMA((EXPERT_SLOTS,)),
            pltpu.SemaphoreType.DMA((1,)),
        ],
    )
    return pl.pallas_call(
        _expert_body,
        grid_spec=grid_spec,
        out_shape=jax.ShapeDtypeStruct((row_tok.shape[0], d // 2), jnp.uint32),
        compiler_params=_params("arbitrary"),
        name="expert_mlp",
    )(bounds, row_tok, hn, w_gate, w_up, w_down)


def _combine_body(pos_ref, y_hbm, wts_ref, h_ref, o_ref, ys_ref, sem):
    i = pl.program_id(0)
    n_steps = pl.num_programs(0)
    tm = h_ref.shape[0]
    n_tokens = n_steps * tm
    slot = i % 2

    def row_copy(p, s, k, r):
        return pltpu.make_async_copy(y_hbm.at[pl.ds(p, 1)], ys_ref.at[s, k, pl.ds(r, 1)], sem.at[s])

    def start_gather(tile, s):
        for r in range(tm):
            for k in range(EXPERT_TOPK):
                row_copy(pos_ref[k * n_tokens + tile * tm + r], s, k, r).start(priority=k)

    def wait_gather(s):
        for k in range(EXPERT_TOPK):
            pltpu.make_async_copy(y_hbm.at[pl.ds(0, tm)], ys_ref.at[s, k], sem.at[s]).wait()

    @pl.when(i == 0)
    def _():
        start_gather(0, 0)

    @pl.when(i + 1 < n_steps)
    def _():
        start_gather(i + 1, 1 - slot)

    wait_gather(slot)
    w = wts_ref[...]
    half = ys_ref.shape[3]
    a_lo, a_hi = _unpack_bf16_halves(ys_ref[slot, 0], _F32)
    b_lo, b_hi = _unpack_bf16_halves(ys_ref[slot, 1], _F32)
    o_ref[:, :half] = h_ref[:, :half] + (w[:, 0:1] * a_lo + w[:, 1:2] * b_lo)
    o_ref[:, half:] = h_ref[:, half:] + (w[:, 0:1] * a_hi + w[:, 1:2] * b_hi)


def _combine(pos, y, wts, h):
    n, d = h.shape
    tm = COMBINE_ROWS
    grid_spec = pltpu.PrefetchScalarGridSpec(
        num_scalar_prefetch=1,
        grid=(n // tm,),
        in_specs=[
            pl.BlockSpec(memory_space=pl.ANY),
            pl.BlockSpec((tm, LANES), lambda i, p: (i, 0)),
            pl.BlockSpec((tm, d), lambda i, p: (i, 0)),
        ],
        out_specs=pl.BlockSpec((tm, d), lambda i, p: (i, 0)),
        scratch_shapes=[
            pltpu.VMEM((2, EXPERT_TOPK, tm, d // 2), jnp.uint32),
            pltpu.SemaphoreType.DMA((2,)),
        ],
    )
    return pl.pallas_call(
        _combine_body,
        grid_spec=grid_spec,
        out_shape=jax.ShapeDtypeStruct((n, d), _F32),
        compiler_params=_params("arbitrary"),
        name="combine",
    )(pos, y, wts, h)


def _dispatch_plan(ids, counts):
    n = ids.shape[1]
    rows = EXPERT_ROWS
    m = n * EXPERT_TOPK
    n_blocks = -(-(m + N_EXPERTS * (rows - 1)) // rows)
    flat_e = ids[:EXPERT_TOPK].reshape(m)
    rank = ids[EXPERT_TOPK:2 * EXPERT_TOPK].reshape(m)
    counts = counts[:N_EXPERTS, 0]
    starts = jnp.cumsum(counts) - counts
    padded = (counts + rows - 1) // rows * rows
    pends = jnp.cumsum(padded)
    pstarts = pends - padded
    pos = rank
    for e in range(N_EXPERTS):
        pos = pos + jnp.where(flat_e == e, pstarts[e], 0)
    blk_start = jnp.arange(n_blocks, dtype=jnp.int32)[:, None] * rows
    in_seg = (pstarts[None, :] <= blk_start) & (blk_start < pends[None, :])
    shift = jnp.sum(jnp.where(in_seg, (starts - pstarts)[None, :], 0), axis=1, keepdims=True)
    seg_end = jnp.sum(jnp.where(in_seg, (starts + counts)[None, :], 0), axis=1, keepdims=True)
    token = jnp.arange(m, dtype=jnp.int32) % n
    order = jnp.argsort(flat_e * n + token)
    sorted_idx = blk_start + shift + jnp.arange(rows, dtype=jnp.int32)[None, :]
    valid = sorted_idx < seg_end
    row_tok = jnp.where(valid, order[jnp.clip(sorted_idx, 0, m - 1)] % n, 0)
    bounds = jnp.concatenate([pstarts, pends[-1:]]) // rows
    return (bounds.astype(jnp.int32), row_tok.reshape(n_blocks * rows).astype(jnp.int32),
            pos.astype(jnp.int32))


def _hier_moe_residual(h, gain, w_rg, w_re, layer, w_gate, w_up, w_down):
    hn, ids, wts, counts = _router(h, gain, w_rg, w_re)
    bounds, row_tok, pos = _dispatch_plan(ids, counts)
    y = _expert_mlp(bounds, row_tok, hn, layer, w_gate, w_up, w_down)
    return _combine(pos, y, wts, h)


def kernel(x, mix_norm, ffn_norm, conv_w_in, conv_w, conv_w_out, kv_norm, w_kv, k_norm, w_q, q_norm,
           w_o, router_group, router_expert, w_gate, w_up, w_down):
    bsz, seq_len, d = x.shape
    depth = mix_norm.shape[0]
    n_conv = conv_w_in.shape[0]
    assert d == N_HEADS * HEAD_DIM
    h = x.reshape(bsz * seq_len, d)
    kv = None
    for l in range(depth):
        if l < n_conv:
            z = _conv_in(h, mix_norm[l], conv_w_in[l], conv_w[l], seq_len)
            h = _matmul_residual(z, conv_w_out[l], h)
        else:
            j = l - n_conv
            gains = jnp.stack([mix_norm[l], kv_norm])
            head_gains = jnp.stack([q_norm[j], k_norm])
            q = _qkv_projection(h, gains, w_q[j], w_kv, head_gains, with_kv=(j == 0))
            if j == 0:
                kv = q
            attn = _moba_attention(q, 0, kv, d, 2 * d, bsz, seq_len)
            h = _matmul_residual(attn, w_o[j], h)
        h = _hier_moe_residual(h, ffn_norm[l], router_group[l], router_expert[l],
                               l, w_gate, w_up, w_down)
    return h.reshape(bsz, seq_len, d)
```

```python
import functools

import jax
import jax.numpy as jnp
from jax import lax
from jax.experimental import pallas as pl
from jax.experimental.pallas import tpu as pltpu

N_HEADS = 16
HEAD_DIM = 128
CONV_WIDTH = 3
MOBA_BLOCK = 256
MOBA_TOPK = 3
N_GROUPS = 4
EXPERTS_PER_GROUP = 8
N_EXPERTS = N_GROUPS * EXPERTS_PER_GROUP
EXPERT_TOPK = 2
NORM_EPS = 1e-6
ALIBI_MAX_BIAS = 8.0

LANES = 128
SUBLANES = 8
VMEM_LIMIT = 56 * 1024 * 1024
QKV_VMEM_LIMIT = 60 * 1024 * 1024

ROW_TILE = 1024
COL_TILE = 512
CONV_COL_TILE = 256
ROUTER_ROWS = 512
EXPERT_ROWS = 128
EXPERT_SLOTS = 4
EXPERT_ISSUE_GROUPS = 8
COMBINE_ROWS = 256
ATTN_HEADS = 4

_BF16 = jnp.bfloat16
_F32 = jnp.float32


def _params(*semantics, vmem_limit=VMEM_LIMIT):
    return pltpu.CompilerParams(dimension_semantics=semantics, vmem_limit_bytes=vmem_limit)


def _rms_normalise(x, gain):
    ms = jnp.mean(x * x, axis=-1, keepdims=True)
    return x * lax.rsqrt(ms + NORM_EPS) * gain


def _bf16_dot(a, b):
    return jnp.dot(a, b, preferred_element_type=_F32)


def _pack_bf16_halves(x):
    c = x.shape[1] // 2
    return _pack_bf16_pair(x[:, :c], x[:, c:])


def _pack_bf16_pair(lo, hi):
    as_bits = lambda v: lax.bitcast_convert_type(v.astype(_BF16).astype(_F32), jnp.uint32)
    return (as_bits(lo) >> 16) | (as_bits(hi) & jnp.uint32(0xFFFF0000))


def _unpack_bf16_halves(p, dtype):
    lo = lax.bitcast_convert_type(p << 16, _F32).astype(dtype)
    hi = lax.bitcast_convert_type(p & jnp.uint32(0xFFFF0000), _F32).astype(dtype)
    return lo, hi


def _qkv_body(x_ref, g_ref, wq_ref, wkv_ref, hg_ref, o_ref, xn_ref, *, q_tiles, qk_tiles):
    j = pl.program_id(1)

    @pl.when(j == 0)
    def _():
        x = x_ref[...]
        normed = x * lax.rsqrt(jnp.mean(x * x, axis=-1, keepdims=True) + NORM_EPS)
        xn_ref[0] = (normed * g_ref[0:1, :]).astype(_BF16)
        xn_ref[1] = (normed * g_ref[1:2, :]).astype(_BF16)

    def project(stream, head_gain_row):
        w_ref = wq_ref if stream == 0 else wkv_ref
        y = _bf16_dot(xn_ref[stream], w_ref[...].astype(_BF16))
        if head_gain_row is None:
            o_ref[...] = y.astype(o_ref.dtype)
            return
        head_gain = hg_ref[head_gain_row:head_gain_row + 1, :]
        for hh in range(y.shape[1] // HEAD_DIM):
            cols = slice(hh * HEAD_DIM, (hh + 1) * HEAD_DIM)
            o_ref[:, cols] = _rms_normalise(y[:, cols], head_gain).astype(o_ref.dtype)

    pl.when(j < q_tiles)(functools.partial(project, 0, 0))
    pl.when((j >= q_tiles) & (j < qk_tiles))(functools.partial(project, 1, 1))
    pl.when(j >= qk_tiles)(functools.partial(project, 1, None))


def _qkv_projection(x, gains, w_q, w_kv, head_gains, with_kv):
    n, d = x.shape
    tm, tn = ROW_TILE, COL_TILE
    n_q, n_k = w_q.shape[1], w_kv.shape[1] // 2
    n_out = n_q + 2 * n_k if with_kv else n_q
    assert n_q % tn == 0 and n_k % tn == 0
    q_tiles = n_q // tn
    return pl.pallas_call(
        functools.partial(_qkv_body, q_tiles=q_tiles, qk_tiles=(n_q + n_k) // tn),
        grid=(n // tm, n_out // tn),
        in_specs=[
            pl.BlockSpec((tm, d), lambda i, j: (i, 0)),
            pl.BlockSpec((2, d), lambda i, j: (0, 0)),
            pl.BlockSpec((d, tn), lambda i, j: (0, jnp.minimum(j, q_tiles - 1))),
            pl.BlockSpec((d, tn), lambda i, j: (0, jnp.maximum(j - q_tiles, 0))),
            pl.BlockSpec((2, HEAD_DIM), lambda i, j: (0, 0)),
        ],
        out_specs=pl.BlockSpec((tm, tn), lambda i, j: (i, j)),
        out_shape=jax.ShapeDtypeStruct((n, n_out), _BF16),
        scratch_shapes=[pltpu.VMEM((2, tm, d), _BF16)],
        compiler_params=_params("arbitrary", "arbitrary", vmem_limit=QKV_VMEM_LIMIT),
        name="qkv_projection",
    )(x, gains, w_q, w_kv, head_gains)


def _conv_in_body(x_ref, g_ref, wb_ref, wc_ref, wx_ref, cw_ref, z_ref, xn_ref, carry_ref,
                  *, tiles_per_seq):
    i = pl.program_id(0)
    j = pl.program_id(1)
    tm = x_ref.shape[0]

    @pl.when(j == 0)
    def _():
        xn_ref[...] = _rms_normalise(x_ref[...], g_ref[...]).astype(_BF16)

    @pl.when(i % tiles_per_seq == 0)
    def _():
        carry_ref[j] = jnp.zeros(carry_ref.shape[1:], _F32)

    xn = xn_ref[...]
    b_gate = _bf16_dot(xn, wb_ref[...].astype(_BF16))
    c_gate = _bf16_dot(xn, wc_ref[...].astype(_BF16))
    xh = _bf16_dot(xn, wx_ref[...].astype(_BF16))
    u = c_gate * xh
    prev = carry_ref[j]
    row = lax.broadcasted_iota(jnp.int32, u.shape, 0)
    last = prev[SUBLANES - 1:SUBLANES]
    u1 = jnp.where(row == 0, last, pltpu.roll(u, 1, axis=0))
    u2 = jnp.where(row == 0, prev[SUBLANES - 2:SUBLANES - 1],
                   jnp.where(row == 1, last, pltpu.roll(u, 2, axis=0)))
    cw = cw_ref[...]
    conv = cw[0:1] * u2 + cw[1:2] * u1 + cw[2:3] * u
    z_ref[...] = (b_gate * conv).astype(z_ref.dtype)
    carry_ref[j] = u[tm - SUBLANES:tm]


def _conv_in(x, gain, w_in, conv_w, seq_len):
    n, d = x.shape
    tm, tn = ROW_TILE, CONV_COL_TILE
    nj = d // tn
    assert CONV_WIDTH - 1 <= SUBLANES and seq_len % tm == 0
    return pl.pallas_call(
        functools.partial(_conv_in_body, tiles_per_seq=seq_len // tm),
        grid=(n // tm, nj),
        in_specs=[
            pl.BlockSpec((tm, d), lambda i, j: (i, 0)),
            pl.BlockSpec((1, d), lambda i, j: (0, 0)),
            pl.BlockSpec((d, tn), lambda i, j: (0, j)),
            pl.BlockSpec((d, tn), lambda i, j: (0, j + nj)),
            pl.BlockSpec((d, tn), lambda i, j: (0, j + 2 * nj)),
            pl.BlockSpec((CONV_WIDTH, tn), lambda i, j: (0, j)),
        ],
        out_specs=pl.BlockSpec((tm, tn), lambda i, j: (i, j)),
        out_shape=jax.ShapeDtypeStruct((n, d), _BF16),
        scratch_shapes=[pltpu.VMEM((tm, d), _BF16), pltpu.VMEM((nj, SUBLANES, tn), _F32)],
        compiler_params=_params("arbitrary", "arbitrary"),
        name="conv_in",
    )(x, gain.reshape(1, d), w_in, w_in, w_in, conv_w)


def _matmul_residual_body(x_ref, w_ref, r_ref, o_ref):
    o_ref[...] = r_ref[...] + _bf16_dot(x_ref[...], w_ref[...].astype(_BF16))


def _matmul_residual(x, w, res):
    n, k = x.shape
    d = w.shape[1]
    tm, tn = ROW_TILE, COL_TILE
    return pl.pallas_call(
        _matmul_residual_body,
        grid=(n // tm, d // tn),
        in_specs=[
            pl.BlockSpec((tm, k), lambda i, j: (i, 0)),
            pl.BlockSpec((k, tn), lambda i, j: (0, j)),
            pl.BlockSpec((tm, tn), lambda i, j: (i, j)),
        ],
        out_specs=pl.BlockSpec((tm, tn), lambda i, j: (i, j)),
        out_shape=jax.ShapeDtypeStruct((n, d), _F32),
        compiler_params=_params("arbitrary", "arbitrary"),
        name="matmul_residual",
    )(x, w, res)


def _moba_body(q_ref, k_ref, v_ref, o_ref, km_ref, vt_ref, bias_ref):
    heads = range(q_ref.shape[1] // HEAD_DIM)
    head0 = pl.program_id(1) * len(heads)
    i = pl.program_id(2)
    kb = MOBA_BLOCK
    n_blocks = k_ref.shape[0] // kb
    nt = (((1,), (1,)), ((), ()))
    log2e = 1.4426950408889634
    scale = HEAD_DIM ** -0.5 * log2e
    slopes = [log2e * jnp.exp2(jnp.full((1, kb), -ALIBI_MAX_BIAS / N_HEADS, _F32)
                               * (head0 + hh + 1).astype(_F32)) for hh in heads]

    def lanes(hh):
        return slice(hh * HEAD_DIM, (hh + 1) * HEAD_DIM)

    @pl.when(i == 0)
    def _():
        rel = (lax.broadcasted_iota(jnp.int32, (kb, kb), 1)
               - lax.broadcasted_iota(jnp.int32, (kb, kb), 0))
        for hh in heads:
            for jb in range(n_blocks):
                rows = slice(jb * kb, (jb + 1) * kb)
                km_ref[hh, jb:jb + 1, :] = jnp.mean(k_ref[rows, lanes(hh)].astype(_F32),
                                                    axis=0, keepdims=True)
                vt_ref[hh, :, rows] = v_ref[rows, lanes(hh)].astype(_F32).T.astype(_BF16)
            bias = slopes[hh] * rel.astype(_F32)
            bias_ref[hh, 0] = bias
            bias_ref[hh, 1] = jnp.where(rel >= 0, bias, jnp.inf)

    q = [q_ref[:, lanes(hh)] for hh in heads]
    sel = []
    for hh in heads:
        gate = lax.dot_general(km_ref[hh], q[hh].astype(_F32), nt,
                               precision=lax.Precision.HIGHEST, preferred_element_type=_F32)
        blk = lax.broadcasted_iota(jnp.int32, gate.shape, 0)
        gate = jnp.where(blk < i, gate, -jnp.inf)
        picked = jnp.zeros(gate.shape, _F32)
        for jb in range(n_blocks - 1):
            g_jb = gate[jb:jb + 1, :]
            beats = (gate > g_jb) | ((gate == g_jb) & (blk < jb))
            n_beat = jnp.sum(beats.astype(_F32), axis=0, keepdims=True)
            picked = jnp.where((blk == jb) & (blk < i) & (n_beat < MOBA_TOPK), 1.0, picked)
        sel.append(picked)

    def attend(own):
        nb = own + 1
        raw = [lax.dot_general(k_ref[0:nb * kb, lanes(hh)], q[hh], nt, preferred_element_type=_F32)
               for hh in heads]
        tiles, col_max = [[] for _ in heads], [[] for _ in heads]
        for jb in range(nb):
            for hh in heads:
                tile = raw[hh][jb * kb:(jb + 1) * kb, :] * scale - bias_ref[hh, 1 if jb == own else 0]
                mx = jnp.max(tile, axis=0, keepdims=True)
                if jb != own:
                    mx = jnp.where(sel[hh][jb:jb + 1, :] > 0.5,
                                   mx - slopes[hh] * float((own - jb) * kb), -jnp.inf)
                tiles[hh].append(tile), col_max[hh].append(mx)
        m = [functools.reduce(jnp.maximum, col_max[hh]) for hh in heads]
        probs, l = [[] for _ in heads], [None for _ in heads]
        for jb in range(nb):
            for hh in heads:
                offset = m[hh] if jb == own else jnp.where(
                    sel[hh][jb:jb + 1, :] > 0.5, m[hh] + slopes[hh] * float((own - jb) * kb), jnp.inf)
                p = jnp.exp2(tiles[hh][jb] - offset)
                p_sum = jnp.sum(p, axis=0, keepdims=True)
                l[hh] = p_sum if l[hh] is None else l[hh] + p_sum
                probs[hh].append(p.astype(_BF16))
        for hh in heads:
            p_all = probs[hh][0] if nb == 1 else jnp.concatenate(probs[hh], axis=0)
            acc = _bf16_dot(vt_ref[hh, :, 0:nb * kb], p_all)
            o_ref[:, lanes(hh)] = (acc / l[hh]).T.astype(o_ref.dtype)

    for own in range(n_blocks):
        pl.when(i == own)(functools.partial(attend, own))


def _moba_attention(q, q_col, kv, k_col, v_col, batch, seq_len):
    n = q.shape[0]
    d = N_HEADS * HEAD_DIM
    kb = MOBA_BLOCK
    nqb = seq_len // kb
    hp = ATTN_HEADS
    width = hp * HEAD_DIM
    assert seq_len % kb == 0 and nqb == SUBLANES and N_HEADS % hp == 0
    assert q_col % width == 0 and k_col % width == 0 and v_col % width == 0
    q_blk, k_blk, v_blk = q_col // width, k_col // width, v_col // width
    return pl.pallas_call(
        _moba_body,
        grid=(batch, N_HEADS // hp, nqb),
        in_specs=[
            pl.BlockSpec((kb, width), lambda b, h, i: (b * nqb + i, q_blk + h)),
            pl.BlockSpec((seq_len, width), lambda b, h, i: (b, k_blk + h)),
            pl.BlockSpec((seq_len, width), lambda b, h, i: (b, v_blk + h)),
        ],
        out_specs=pl.BlockSpec((kb, width), lambda b, h, i: (b * nqb + i, h)),
        out_shape=jax.ShapeDtypeStruct((n, d), _BF16),
        scratch_shapes=[pltpu.VMEM((hp, nqb, HEAD_DIM), _F32),
                        pltpu.VMEM((hp, HEAD_DIM, seq_len), _BF16),
                        pltpu.VMEM((hp, 2, kb, kb), _F32)],
        compiler_params=_params("arbitrary", "arbitrary", "arbitrary"),
        name="moba_attention",
    )(q, kv, kv)


def _router_body(x_ref, g_ref, wr_ref, hn_ref, ids_ref, wts_ref, cnt_ref, run_ref):
    step = pl.program_id(0)
    hn = _rms_normalise(x_ref[...], g_ref[...])
    hn_ref[...] = _pack_bf16_halves(hn)
    w = wr_ref[...]
    hn_hi, w_hi = hn.astype(_BF16), w.astype(_BF16)
    hn_lo = (hn - hn_hi.astype(_F32)).astype(_BF16)
    w_lo = (w - w_hi.astype(_F32)).astype(_BF16)
    logits = (_bf16_dot(hn_hi, w_hi) + (_bf16_dot(hn_hi, w_lo) + _bf16_dot(hn_lo, w_hi))).T
    row = lax.broadcasted_iota(jnp.int32, logits.shape, 0)
    far = jnp.int32(logits.shape[0])

    def first_row(mask):
        return jnp.min(jnp.where(mask, row, far), axis=0, keepdims=True)

    is_group = (row >= N_EXPERTS) & (row < N_EXPERTS + N_GROUPS)
    g_max = jnp.max(jnp.where(is_group, logits, -jnp.inf), axis=0, keepdims=True)
    g_sel = first_row(is_group & (logits == g_max)) - N_EXPERTS
    g_w = 1.0 / jnp.sum(jnp.where(is_group, jnp.exp(logits - g_max), 0.0), axis=0, keepdims=True)

    in_group = (row >= g_sel * EXPERTS_PER_GROUP) & (row < (g_sel + 1) * EXPERTS_PER_GROUP)
    top1 = jnp.max(jnp.where(in_group, logits, -jnp.inf), axis=0, keepdims=True)
    idx1 = first_row(in_group & (logits == top1))
    rest = in_group & (row != idx1)
    top2 = jnp.max(jnp.where(rest, logits, -jnp.inf), axis=0, keepdims=True)
    idx2 = first_row(rest & (logits == top2))
    t = jnp.exp(top2 - top1)
    w1 = 1.0 / (1.0 + t) * g_w
    w2 = t / (1.0 + t) * g_w

    @pl.when(step == 0)
    def _():
        run_ref[...] = jnp.zeros(run_ref.shape, _F32)

    tm = logits.shape[1]
    chosen = jnp.where((row == idx1) | (row == idx2), 1.0, 0.0)
    earlier = (lax.broadcasted_iota(jnp.int32, (tm, tm), 0)
               < lax.broadcasted_iota(jnp.int32, (tm, tm), 1))
    before = run_ref[:, 0:1] + _bf16_dot(chosen.astype(_BF16),
                                         jnp.where(earlier, 1.0, 0.0).astype(_BF16))
    rank1 = jnp.sum(jnp.where(row == idx1, before, 0.0), axis=0, keepdims=True).astype(jnp.int32)
    rank2 = jnp.sum(jnp.where(row == idx2, before, 0.0), axis=0, keepdims=True).astype(jnp.int32)
    run_ref[...] += jnp.sum(chosen, axis=1, keepdims=True)

    out_row = lax.broadcasted_iota(jnp.int32, ids_ref.shape, 0)
    ids_ref[...] = jnp.where(out_row == 0, idx1, jnp.where(out_row == 1, idx2,
                             jnp.where(out_row == 2, rank1, jnp.where(out_row == 3, rank2, 0))))
    wts_ref[...] = jnp.where(row == 0, w1, jnp.where(row == 1, w2, 0.0)).T
    cnt_ref[...] = run_ref[...].astype(jnp.int32)


def _router(x, gain, w_rg, w_re):
    n, d = x.shape
    tm = ROUTER_ROWS
    assert EXPERT_TOPK == 2 and N_EXPERTS + N_GROUPS <= LANES
    wr = jnp.concatenate(
        [w_re, w_rg, jnp.zeros((d, LANES - N_EXPERTS - N_GROUPS), _F32)], axis=1)
    row = lambda i: (i, 0)
    col = lambda i: (0, i)
    fixed = lambda i: (0, 0)
    return pl.pallas_call(
        _router_body,
        grid=(n // tm,),
        in_specs=[
            pl.BlockSpec((tm, d), row),
            pl.BlockSpec((1, d), fixed),
            pl.BlockSpec((d, LANES), fixed),
        ],
        out_specs=[pl.BlockSpec((tm, d // 2), row), pl.BlockSpec((SUBLANES, tm), col),
                   pl.BlockSpec((tm, LANES), row), pl.BlockSpec((LANES, LANES), fixed)],
        out_shape=[jax.ShapeDtypeStruct((n, d // 2), jnp.uint32),
                   jax.ShapeDtypeStruct((SUBLANES, n), jnp.int32),
                   jax.ShapeDtypeStruct((n, LANES), _F32),
                   jax.ShapeDtypeStruct((LANES, LANES), jnp.int32)],
        scratch_shapes=[pltpu.VMEM((LANES, LANES), _F32)],
        compiler_params=_params("arbitrary"),
        name="router",
    )(x, gain.reshape(1, d), wr)


def _expert_body(bounds_ref, row_tok_ref, hn_hbm, wg_ref, wu_ref, wd_ref, y_hbm,
                 xs_ref, ys_ref, zero_ref, wg_bf, wu_bf, wd_bf, gather_sem, out_sem, tail_sem):
    e = pl.program_id(0)
    n_experts = pl.num_programs(0)
    n_slots, rows = xs_ref.shape[0], xs_ref.shape[1]
    ahead = n_slots - 1
    first, stop = bounds_ref[e], bounds_ref[e + 1]
    n_total = bounds_ref[n_experts]

    def row_copy(tok, s, r):
        return pltpu.make_async_copy(hn_hbm.at[pl.ds(tok, 1)], xs_ref.at[s, pl.ds(r, 1)],
                                     gather_sem.at[s])

    def start_gather(block):
        def body(r, carry):
            row_copy(row_tok_ref[block * rows + r], block % n_slots, r).start(priority=1)
            return carry
        lax.fori_loop(0, rows, body, 0)

    def wait_gather(block):
        s = block % n_slots
        pltpu.make_async_copy(hn_hbm.at[pl.ds(0, rows)], xs_ref.at[s], gather_sem.at[s]).wait()

    def out_copy(block):
        s = block % n_slots
        dst = y_hbm.at[pl.ds(pl.multiple_of(block * rows, rows), rows)]
        return pltpu.make_async_copy(ys_ref.at[s], dst, out_sem.at[s])

    def tail_copy(block):
        dst = y_hbm.at[pl.ds(pl.multiple_of(block * rows, rows), rows)]
        return pltpu.make_async_copy(zero_ref, dst, tail_sem.at[0])

    def for_tail_blocks(fn):
        def body(block, carry):
            fn(tail_copy(block))
            return carry
        lax.fori_loop(n_total, y_hbm.shape[0] // rows, body, 0)

    @pl.when(e == 0)
    def _():
        for block in range(ahead):
            start_gather(block)
        zero_ref[...] = jnp.zeros(zero_ref.shape, zero_ref.dtype)
        for_tail_blocks(lambda copy: copy.start())

    @pl.when(stop > first)
    def _():
        wg_bf[...] = wg_ref[0, 0].astype(_BF16)
        wu_bf[...] = wu_ref[0, 0].astype(_BF16)
        wd_bf[...] = wd_ref[0, 0].astype(_BF16)

    def block_step(block, carry):
        @pl.when(block >= n_slots)
        def _():
            out_copy(block - n_slots).wait()

        nxt = block + ahead
        nxt_slot = nxt % n_slots
        nxt_base = jnp.minimum(nxt, n_total - 1) * rows
        per_group = rows // EXPERT_ISSUE_GROUPS

        def issue(group):
            for r in range(group * per_group, (group + 1) * per_group):
                row_copy(row_tok_ref[nxt_base + r], nxt_slot, r).start(priority=1)

        wait_gather(block)
        s = block % n_slots
        x_lo, x_hi = _unpack_bf16_halves(xs_ref[s], _BF16)
        half = x_lo.shape[1]
        gate = _bf16_dot(x_lo, wg_bf[:half, :])
        issue(0)
        gate = gate + _bf16_dot(x_hi, wg_bf[half:, :])
        issue(1)
        up = _bf16_dot(x_lo, wu_bf[:half, :])
        issue(2)
        up = up + _bf16_dot(x_hi, wu_bf[half:, :])
        issue(3)
        act = (jax.nn.silu(gate) * up).astype(_BF16)
        quarter = wd_bf.shape[1] // 4
        y = []
        for c in range(4):
            y.append(_bf16_dot(act, wd_bf[:, c * quarter:(c + 1) * quarter]))
            issue(4 + c)
        ys_ref[s, :, :quarter] = _pack_bf16_pair(y[0], y[2])
        ys_ref[s, :, quarter:] = _pack_bf16_pair(y[1], y[3])
        out_copy(block).start()
        return carry

    lax.fori_loop(first, stop, block_step, 0)

    @pl.when(e == n_experts - 1)
    def _():
        for extra in range(ahead):
            wait_gather(n_total + extra)
        for_tail_blocks(lambda copy: copy.wait())
        for back in range(1, n_slots + 1):
            @pl.when(n_total >= back)
            def _():
                out_copy(n_total - back).wait()


def _expert_mlp(bounds, row_tok, hn, layer, w_gate, w_up, w_down):
    n_experts, d, de = w_gate.shape[1:]
    rows = EXPERT_ROWS
    assert hn.shape[1] * 2 == d and hn.dtype == jnp.uint32
    grid_spec = pltpu.PrefetchScalarGridSpec(
        num_scalar_prefetch=2,
        grid=(n_experts,),
        in_specs=[
            pl.BlockSpec(memory_space=pl.ANY),
            pl.BlockSpec((1, 1, d, de), lambda e, bd, rt: (layer, e, 0, 0)),
            pl.BlockSpec((1, 1, d, de), lambda e, bd, rt: (layer, e, 0, 0)),
            pl.BlockSpec((1, 1, de, d), lambda e, bd, rt: (layer, e, 0, 0)),
        ],
        out_specs=pl.BlockSpec(memory_space=pl.ANY),
        scratch_shapes=[
            pltpu.VMEM((EXPERT_SLOTS, rows, d // 2), jnp.uint32),
            pltpu.VMEM((EXPERT_SLOTS, rows, d // 2), jnp.uint32),
            pltpu.VMEM((rows, d // 2), jnp.uint32),
            pltpu.VMEM((d, de), _BF16),
            pltpu.VMEM((d, de), _BF16),
            pltpu.VMEM((de, d), _BF16),
            pltpu.SemaphoreType.DMA((EXPERT_SLOTS,)),
            pltpu.SemaphoreType.DMA((EXPERT_SLOTS,)),
            pltpu.SemaphoreType.DMA((1,)),
        ],
    )
    return pl.pallas_call(
        _expert_body,
        grid_spec=grid_spec,
        out_shape=jax.ShapeDtypeStruct((row_tok.shape[0], d // 2), jnp.uint32),
        compiler_params=_params("arbitrary"),
        name="expert_mlp",
    )(bounds, row_tok, hn, w_gate, w_up, w_down)


def _combine_body(pos_ref, y_hbm, wts_ref, h_ref, o_ref, ys_ref, sem):
    i = pl.program_id(0)
    n_steps = pl.num_programs(0)
    tm = h_ref.shape[0]
    n_tokens = n_steps * tm
    slot = i % 2

    def row_copy(p, s, k, r):
        return pltpu.make_async_copy(y_hbm.at[pl.ds(p, 1)], ys_ref.at[s, k, pl.ds(r, 1)], sem.at[s])

    def start_gather(tile, s):
        for r in range(tm):
            for k in range(EXPERT_TOPK):
                row_copy(pos_ref[k * n_tokens + tile * tm + r], s, k, r).start(priority=k)

    def wait_gather(s):
        for k in range(EXPERT_TOPK):
            pltpu.make_async_copy(y_hbm.at[pl.ds(0, tm)], ys_ref.at[s, k], sem.at[s]).wait()

    @pl.when(i == 0)
    def _():
        start_gather(0, 0)

    @pl.when(i + 1 < n_steps)
    def _():
        start_gather(i + 1, 1 - slot)

    wait_gather(slot)
    w = wts_ref[...]
    half = ys_ref.shape[3]
    a_lo, a_hi = _unpack_bf16_halves(ys_ref[slot, 0], _F32)
    b_lo, b_hi = _unpack_bf16_halves(ys_ref[slot, 1], _F32)
    o_ref[:, :half] = h_ref[:, :half] + (w[:, 0:1] * a_lo + w[:, 1:2] * b_lo)
    o_ref[:, half:] = h_ref[:, half:] + (w[:, 0:1] * a_hi + w[:, 1:2] * b_hi)


def _combine(pos, y, wts, h):
    n, d = h.shape
    tm = COMBINE_ROWS
    grid_spec = pltpu.PrefetchScalarGridSpec(
        num_scalar_prefetch=1,
        grid=(n // tm,),
        in_specs=[
            pl.BlockSpec(memory_space=pl.ANY),
            pl.BlockSpec((tm, LANES), lambda i, p: (i, 0)),
            pl.BlockSpec((tm, d), lambda i, p: (i, 0)),
        ],
        out_specs=pl.BlockSpec((tm, d), lambda i, p: (i, 0)),
        scratch_shapes=[
            pltpu.VMEM((2, EXPERT_TOPK, tm, d // 2), jnp.uint32),
            pltpu.SemaphoreType.DMA((2,)),
        ],
    )
    return pl.pallas_call(
        _combine_body,
        grid_spec=grid_spec,
        out_shape=jax.ShapeDtypeStruct((n, d), _F32),
        compiler_params=_params("arbitrary"),
        name="combine",
    )(pos, y, wts, h)


def _dispatch_plan(ids, counts):
    n = ids.shape[1]
    rows = EXPERT_ROWS
    m = n * EXPERT_TOPK
    n_blocks = -(-(m + N_EXPERTS * (rows - 1)) // rows)
    flat_e = ids[:EXPERT_TOPK].reshape(m)
    rank = ids[EXPERT_TOPK:2 * EXPERT_TOPK].reshape(m)
    counts = counts[:N_EXPERTS, 0]
    starts = jnp.cumsum(counts) - counts
    padded = (counts + rows - 1) // rows * rows
    pends = jnp.cumsum(padded)
    pstarts = pends - padded
    pos = rank
    for e in range(N_EXPERTS):
        pos = pos + jnp.where(flat_e == e, pstarts[e], 0)
    blk_start = jnp.arange(n_blocks, dtype=jnp.int32)[:, None] * rows
    in_seg = (pstarts[None, :] <= blk_start) & (blk_start < pends[None, :])
    shift = jnp.sum(jnp.where(in_seg, (starts - pstarts)[None, :], 0), axis=1, keepdims=True)
    seg_end = jnp.sum(jnp.where(in_seg, (starts + counts)[None, :], 0), axis=1, keepdims=True)
    token = jnp.arange(m, dtype=jnp.int32) % n
    order = jnp.argsort(flat_e * n + token)
    sorted_idx = blk_start + shift + jnp.arange(rows, dtype=jnp.int32)[None, :]
    valid = sorted_idx < seg_end
    row_tok = jnp.where(valid, order[jnp.clip(sorted_idx, 0, m - 1)] % n, 0)
    bounds = jnp.concatenate([pstarts, pends[-1:]]) // rows
    return (bounds.astype(jnp.int32), row_tok.reshape(n_blocks * rows).astype(jnp.int32),
            pos.astype(jnp.int32))


def _hier_moe_residual(h, gain, w_rg, w_re, layer, w_gate, w_up, w_down):
    hn, ids, wts, counts = _router(h, gain, w_rg, w_re)
    bounds, row_tok, pos = _dispatch_plan(ids, counts)
    y = _expert_mlp(bounds, row_tok, hn, layer, w_gate, w_up, w_down)
    return _combine(pos, y, wts, h)


def kernel(x, mix_norm, ffn_norm, conv_w_in, conv_w, conv_w_out, kv_norm, w_kv, k_norm, w_q, q_norm,
           w_o, router_group, router_expert, w_gate, w_up, w_down):
    bsz, seq_len, d = x.shape
    depth = mix_norm.shape[0]
    n_conv = conv_w_in.shape[0]
    assert d == N_HEADS * HEAD_DIM
    h = x.reshape(bsz * seq_len, d)
    kv = None
    for l in range(depth):
        if l < n_conv:
            z = _conv_in(h, mix_norm[l], conv_w_in[l], conv_w[l], seq_len)
            h = _matmul_residual(z, conv_w_out[l], h)
        else:
            j = l - n_conv
            gains = jnp.stack([mix_norm[l], kv_norm])
            head_gains = jnp.stack([q_norm[j], k_norm])
            q = _qkv_projection(h, gains, w_q[j], w_kv, head_gains, with_kv=(j == 0))
            if j == 0:
                kv = q
            attn = _moba_attention(q, 0, kv, d, 2 * d, bsz, seq_len)
            h = _matmul_residual(attn, w_o[j], h)
        h = _hier_moe_residual(h, ffn_norm[l], router_group[l], router_expert[l],
                               l, w_gate, w_up, w_down)
    return h.reshape(bsz, seq_len, d)
```

```python
import functools

import jax
import jax.numpy as jnp
from jax import lax
from jax.experimental import pallas as pl
from jax.experimental.pallas import tpu as pltpu

N_HEADS = 16
HEAD_DIM = 128
CONV_WIDTH = 3
MOBA_BLOCK = 256
MOBA_TOPK = 3
N_GROUPS = 4
EXPERTS_PER_GROUP = 8
N_EXPERTS = N_GROUPS * EXPERTS_PER_GROUP
EXPERT_TOPK = 2
NORM_EPS = 1e-6
ALIBI_MAX_BIAS = 8.0

LANES = 128
SUBLANES = 8
VMEM_LIMIT = 56 * 1024 * 1024
QKV_VMEM_LIMIT = 60 * 1024 * 1024

ROW_TILE = 1024
COL_TILE = 512
CONV_COL_TILE = 256
ROUTER_ROWS = 512
EXPERT_ROWS = 128
EXPERT_SLOTS = 4
EXPERT_ISSUE_GROUPS = 8
COMBINE_ROWS = 256
ATTN_HEADS = 4

_BF16 = jnp.bfloat16
_F32 = jnp.float32


def _params(*semantics, vmem_limit=VMEM_LIMIT):
    return pltpu.CompilerParams(dimension_semantics=semantics, vmem_limit_bytes=vmem_limit)


def _rms_normalise(x, gain):
    ms = jnp.mean(x * x, axis=-1, keepdims=True)
    return x * lax.rsqrt(ms + NORM_EPS) * gain


def _bf16_dot(a, b):
    return jnp.dot(a, b, preferred_element_type=_F32)


def _pack_bf16_halves(x):
    c = x.shape[1] // 2
    return _pack_bf16_pair(x[:, :c], x[:, c:])


def _pack_bf16_pair(lo, hi):
    as_bits = lambda v: lax.bitcast_convert_type(v.astype(_BF16).astype(_F32), jnp.uint32)
    return (as_bits(lo) >> 16) | (as_bits(hi) & jnp.uint32(0xFFFF0000))


def _store_row_tiles(ref, value):
    rows, chunks = value.shape[0], value.shape[1] // LANES
    for c in range(chunks):
        ref[pl.ds(c, rows, stride=chunks), :] = value[:, c * LANES:(c + 1) * LANES]


def _load_row_tiles(ref, chunks):
    rows = ref.shape[0] // chunks
    return jnp.concatenate([ref[pl.ds(c, rows, stride=chunks), :] for c in range(chunks)], axis=1)


def _unpack_bf16_halves(p, dtype):
    lo = lax.bitcast_convert_type(p << 16, _F32).astype(dtype)
    hi = lax.bitcast_convert_type(p & jnp.uint32(0xFFFF0000), _F32).astype(dtype)
    return lo, hi


def _qkv_body(x_ref, g_ref, wq_ref, wkv_ref, hg_ref, o_ref, xn_ref, *, q_tiles, qk_tiles):
    j = pl.program_id(1)

    @pl.when(j == 0)
    def _():
        x = x_ref[...]
        normed = x * lax.rsqrt(jnp.mean(x * x, axis=-1, keepdims=True) + NORM_EPS)
        xn_ref[0] = (normed * g_ref[0:1, :]).astype(_BF16)
        xn_ref[1] = (normed * g_ref[1:2, :]).astype(_BF16)

    def project(stream, head_gain_row):
        w_ref = wq_ref if stream == 0 else wkv_ref
        y = _bf16_dot(xn_ref[stream], w_ref[...].astype(_BF16))
        if head_gain_row is None:
            o_ref[...] = y.astype(o_ref.dtype)
            return
        head_gain = hg_ref[head_gain_row:head_gain_row + 1, :]
        for hh in range(y.shape[1] // HEAD_DIM):
            cols = slice(hh * HEAD_DIM, (hh + 1) * HEAD_DIM)
            o_ref[:, cols] = _rms_normalise(y[:, cols], head_gain).astype(o_ref.dtype)

    pl.when(j < q_tiles)(functools.partial(project, 0, 0))
    pl.when((j >= q_tiles) & (j < qk_tiles))(functools.partial(project, 1, 1))
    pl.when(j >= qk_tiles)(functools.partial(project, 1, None))


def _qkv_projection(x, gains, w_q, w_kv, head_gains, with_kv):
    n, d = x.shape
    tm, tn = ROW_TILE, COL_TILE
    n_q, n_k = w_q.shape[1], w_kv.shape[1] // 2
    n_out = n_q + 2 * n_k if with_kv else n_q
    assert n_q % tn == 0 and n_k % tn == 0
    q_tiles = n_q // tn
    return pl.pallas_call(
        functools.partial(_qkv_body, q_tiles=q_tiles, qk_tiles=(n_q + n_k) // tn),
        grid=(n // tm, n_out // tn),
        in_specs=[
            pl.BlockSpec((tm, d), lambda i, j: (i, 0)),
            pl.BlockSpec((2, d), lambda i, j: (0, 0)),
            pl.BlockSpec((d, tn), lambda i, j: (0, jnp.minimum(j, q_tiles - 1))),
            pl.BlockSpec((d, tn), lambda i, j: (0, jnp.maximum(j - q_tiles, 0))),
            pl.BlockSpec((2, HEAD_DIM), lambda i, j: (0, 0)),
        ],
        out_specs=pl.BlockSpec((tm, tn), lambda i, j: (i, j)),
        out_shape=jax.ShapeDtypeStruct((n, n_out), _BF16),
        scratch_shapes=[pltpu.VMEM((2, tm, d), _BF16)],
        compiler_params=_params("arbitrary", "arbitrary", vmem_limit=QKV_VMEM_LIMIT),
        name="qkv_projection",
    )(x, gains, w_q, w_kv, head_gains)


def _conv_in_body(x_ref, g_ref, wb_ref, wc_ref, wx_ref, cw_ref, z_ref, xn_ref, carry_ref,
                  *, tiles_per_seq):
    i = pl.program_id(0)
    j = pl.program_id(1)
    tm = x_ref.shape[0]

    @pl.when(j == 0)
    def _():
        xn_ref[...] = _rms_normalise(x_ref[...], g_ref[...]).astype(_BF16)

    @pl.when(i % tiles_per_seq == 0)
    def _():
        carry_ref[j] = jnp.zeros(carry_ref.shape[1:], _F32)

    xn = xn_ref[...]
    b_gate = _bf16_dot(xn, wb_ref[...].astype(_BF16))
    c_gate = _bf16_dot(xn, wc_ref[...].astype(_BF16))
    xh = _bf16_dot(xn, wx_ref[...].astype(_BF16))
    u = c_gate * xh
    prev = carry_ref[j]
    row = lax.broadcasted_iota(jnp.int32, u.shape, 0)
    last = prev[SUBLANES - 1:SUBLANES]
    u1 = jnp.where(row == 0, last, pltpu.roll(u, 1, axis=0))
    u2 = jnp.where(row == 0, prev[SUBLANES - 2:SUBLANES - 1],
                   jnp.where(row == 1, last, pltpu.roll(u, 2, axis=0)))
    cw = cw_ref[...]
    conv = cw[0:1] * u2 + cw[1:2] * u1 + cw[2:3] * u
    z_ref[...] = (b_gate * conv).astype(z_ref.dtype)
    carry_ref[j] = u[tm - SUBLANES:tm]


def _conv_in(x, gain, w_in, conv_w, seq_len):
    n, d = x.shape
    tm, tn = ROW_TILE, CONV_COL_TILE
    nj = d // tn
    assert CONV_WIDTH - 1 <= SUBLANES and seq_len % tm == 0
    return pl.pallas_call(
        functools.partial(_conv_in_body, tiles_per_seq=seq_len // tm),
        grid=(n // tm, nj),
        in_specs=[
            pl.BlockSpec((tm, d), lambda i, j: (i, 0)),
            pl.BlockSpec((1, d), lambda i, j: (0, 0)),
            pl.BlockSpec((d, tn), lambda i, j: (0, j)),
            pl.BlockSpec((d, tn), lambda i, j: (0, j + nj)),
            pl.BlockSpec((d, tn), lambda i, j: (0, j + 2 * nj)),
            pl.BlockSpec((CONV_WIDTH, tn), lambda i, j: (0, j)),
        ],
        out_specs=pl.BlockSpec((tm, tn), lambda i, j: (i, j)),
        out_shape=jax.ShapeDtypeStruct((n, d), _BF16),
        scratch_shapes=[pltpu.VMEM((tm, d), _BF16), pltpu.VMEM((nj, SUBLANES, tn), _F32)],
        compiler_params=_params("arbitrary", "arbitrary"),
        name="conv_in",
    )(x, gain.reshape(1, d), w_in, w_in, w_in, conv_w)


def _matmul_residual_body(x_ref, w_ref, r_ref, o_ref):
    o_ref[...] = r_ref[...] + _bf16_dot(x_ref[...], w_ref[...].astype(_BF16))


def _matmul_residual(x, w, res):
    n, k = x.shape
    d = w.shape[1]
    tm, tn = ROW_TILE, COL_TILE
    return pl.pallas_call(
        _matmul_residual_body,
        grid=(n // tm, d // tn),
        in_specs=[
            pl.BlockSpec((tm, k), lambda i, j: (i, 0)),
            pl.BlockSpec((k, tn), lambda i, j: (0, j)),
            pl.BlockSpec((tm, tn), lambda i, j: (i, j)),
        ],
        out_specs=pl.BlockSpec((tm, tn), lambda i, j: (i, j)),
        out_shape=jax.ShapeDtypeStruct((n, d), _F32),
        compiler_params=_params("arbitrary", "arbitrary"),
        name="matmul_residual",
    )(x, w, res)


def _moba_body(q_ref, k_ref, v_ref, o_ref, km_ref, vt_ref, bias_ref):
    heads = range(q_ref.shape[1] // HEAD_DIM)
    head0 = pl.program_id(1) * len(heads)
    i = pl.program_id(2)
    kb = MOBA_BLOCK
    n_blocks = k_ref.shape[0] // kb
    nt = (((1,), (1,)), ((), ()))
    log2e = 1.4426950408889634
    scale = HEAD_DIM ** -0.5 * log2e
    slopes = [log2e * jnp.exp2(jnp.full((1, kb), -ALIBI_MAX_BIAS / N_HEADS, _F32)
                               * (head0 + hh + 1).astype(_F32)) for hh in heads]

    def lanes(hh):
        return slice(hh * HEAD_DIM, (hh + 1) * HEAD_DIM)

    @pl.when(i == 0)
    def _():
        rel = (lax.broadcasted_iota(jnp.int32, (kb, kb), 1)
               - lax.broadcasted_iota(jnp.int32, (kb, kb), 0))
        for hh in heads:
            for jb in range(n_blocks):
                rows = slice(jb * kb, (jb + 1) * kb)
                km_ref[hh, jb:jb + 1, :] = jnp.mean(k_ref[rows, lanes(hh)].astype(_F32),
                                                    axis=0, keepdims=True)
                vt_ref[hh, :, rows] = v_ref[rows, lanes(hh)].astype(_F32).T.astype(_BF16)
            bias = slopes[hh] * rel.astype(_F32)
            bias_ref[hh, 0] = bias
            bias_ref[hh, 1] = jnp.where(rel >= 0, bias, jnp.inf)

    q = [q_ref[:, lanes(hh)] for hh in heads]
    sel = []
    for hh in heads:
        gate = lax.dot_general(km_ref[hh], q[hh].astype(_F32), nt,
                               precision=lax.Precision.HIGHEST, preferred_element_type=_F32)
        blk = lax.broadcasted_iota(jnp.int32, gate.shape, 0)
        gate = jnp.where(blk < i, gate, -jnp.inf)
        picked = jnp.zeros(gate.shape, _F32)
        for jb in range(n_blocks - 1):
            g_jb = gate[jb:jb + 1, :]
            beats = (gate > g_jb) | ((gate == g_jb) & (blk < jb))
            n_beat = jnp.sum(beats.astype(_F32), axis=0, keepdims=True)
            picked = jnp.where((blk == jb) & (blk < i) & (n_beat < MOBA_TOPK), 1.0, picked)
        sel.append(picked)

    def attend(own):
        nb = own + 1
        raw = [lax.dot_general(k_ref[0:nb * kb, lanes(hh)], q[hh], nt, preferred_element_type=_F32)
               for hh in heads]
        tiles, col_max = [[] for _ in heads], [[] for _ in heads]
        for jb in range(nb):
            for hh in heads:
                tile = raw[hh][jb * kb:(jb + 1) * kb, :] * scale - bias_ref[hh, 1 if jb == own else 0]
                mx = jnp.max(tile, axis=0, keepdims=True)
                if jb != own:
                    mx = jnp.where(sel[hh][jb:jb + 1, :] > 0.5,
                                   mx - slopes[hh] * float((own - jb) * kb), -jnp.inf)
                tiles[hh].append(tile), col_max[hh].append(mx)
        m = [functools.reduce(jnp.maximum, col_max[hh]) for hh in heads]
        probs, l = [[] for _ in heads], [None for _ in heads]
        for jb in range(nb):
            for hh in heads:
                offset = m[hh] if jb == own else jnp.where(
                    sel[hh][jb:jb + 1, :] > 0.5, m[hh] + slopes[hh] * float((own - jb) * kb), jnp.inf)
                p = jnp.exp2(tiles[hh][jb] - offset)
                p_sum = jnp.sum(p, axis=0, keepdims=True)
                l[hh] = p_sum if l[hh] is None else l[hh] + p_sum
                probs[hh].append(p.astype(_BF16))
        for hh in heads:
            p_all = probs[hh][0] if nb == 1 else jnp.concatenate(probs[hh], axis=0)
            acc = _bf16_dot(vt_ref[hh, :, 0:nb * kb], p_all)
            o_ref[:, lanes(hh)] = (acc / l[hh]).T.astype(o_ref.dtype)

    for own in range(n_blocks):
        pl.when(i == own)(functools.partial(attend, own))


def _moba_attention(q, q_col, kv, k_col, v_col, batch, seq_len):
    n = q.shape[0]
    d = N_HEADS * HEAD_DIM
    kb = MOBA_BLOCK
    nqb = seq_len // kb
    hp = ATTN_HEADS
    width = hp * HEAD_DIM
    assert seq_len % kb == 0 and nqb == SUBLANES and N_HEADS % hp == 0
    assert q_col % width == 0 and k_col % width == 0 and v_col % width == 0
    q_blk, k_blk, v_blk = q_col // width, k_col // width, v_col // width
    return pl.pallas_call(
        _moba_body,
        grid=(batch, N_HEADS // hp, nqb),
        in_specs=[
            pl.BlockSpec((kb, width), lambda b, h, i: (b * nqb + i, q_blk + h)),
            pl.BlockSpec((seq_len, width), lambda b, h, i: (b, k_blk + h)),
            pl.BlockSpec((seq_len, width), lambda b, h, i: (b, v_blk + h)),
        ],
        out_specs=pl.BlockSpec((kb, width), lambda b, h, i: (b * nqb + i, h)),
        out_shape=jax.ShapeDtypeStruct((n, d), _BF16),
        scratch_shapes=[pltpu.VMEM((hp, nqb, HEAD_DIM), _F32),
                        pltpu.VMEM((hp, HEAD_DIM, seq_len), _BF16),
                        pltpu.VMEM((hp, 2, kb, kb), _F32)],
        compiler_params=_params("arbitrary", "arbitrary", "arbitrary"),
        name="moba_attention",
    )(q, kv, kv)


def _router_body(x_ref, g_ref, wr_ref, hn_ref, ids_ref, wts_ref, cnt_ref, run_ref):
    step = pl.program_id(0)
    hn = _rms_normalise(x_ref[...], g_ref[...])
    _store_row_tiles(hn_ref, _pack_bf16_halves(hn))
    w = wr_ref[...]
    hn_hi, w_hi = hn.astype(_BF16), w.astype(_BF16)
    hn_lo = (hn - hn_hi.astype(_F32)).astype(_BF16)
    w_lo = (w - w_hi.astype(_F32)).astype(_BF16)
    logits = (_bf16_dot(hn_hi, w_hi) + (_bf16_dot(hn_hi, w_lo) + _bf16_dot(hn_lo, w_hi))).T
    row = lax.broadcasted_iota(jnp.int32, logits.shape, 0)
    far = jnp.int32(logits.shape[0])

    def first_row(mask):
        return jnp.min(jnp.where(mask, row, far), axis=0, keepdims=True)

    is_group = (row >= N_EXPERTS) & (row < N_EXPERTS + N_GROUPS)
    g_max = jnp.max(jnp.where(is_group, logits, -jnp.inf), axis=0, keepdims=True)
    g_sel = first_row(is_group & (logits == g_max)) - N_EXPERTS
    g_w = 1.0 / jnp.sum(jnp.where(is_group, jnp.exp(logits - g_max), 0.0), axis=0, keepdims=True)

    in_group = (row >= g_sel * EXPERTS_PER_GROUP) & (row < (g_sel + 1) * EXPERTS_PER_GROUP)
    top1 = jnp.max(jnp.where(in_group, logits, -jnp.inf), axis=0, keepdims=True)
    idx1 = first_row(in_group & (logits == top1))
    rest = in_group & (row != idx1)
    top2 = jnp.max(jnp.where(rest, logits, -jnp.inf), axis=0, keepdims=True)
    idx2 = first_row(rest & (logits == top2))
    t = jnp.exp(top2 - top1)
    w1 = 1.0 / (1.0 + t) * g_w
    w2 = t / (1.0 + t) * g_w

    @pl.when(step == 0)
    def _():
        run_ref[...] = jnp.zeros(run_ref.shape, _F32)

    tm = logits.shape[1]
    chosen = jnp.where((row == idx1) | (row == idx2), 1.0, 0.0)
    earlier = (lax.broadcasted_iota(jnp.int32, (tm, tm), 0)
               < lax.broadcasted_iota(jnp.int32, (tm, tm), 1))
    before = run_ref[:, 0:1] + _bf16_dot(chosen.astype(_BF16),
                                         jnp.where(earlier, 1.0, 0.0).astype(_BF16))
    rank1 = jnp.sum(jnp.where(row == idx1, before, 0.0), axis=0, keepdims=True).astype(jnp.int32)
    rank2 = jnp.sum(jnp.where(row == idx2, before, 0.0), axis=0, keepdims=True).astype(jnp.int32)
    run_ref[...] += jnp.sum(chosen, axis=1, keepdims=True)

    out_row = lax.broadcasted_iota(jnp.int32, ids_ref.shape, 0)
    ids_ref[...] = jnp.where(out_row == 0, idx1, jnp.where(out_row == 1, idx2,
                             jnp.where(out_row == 2, rank1, jnp.where(out_row == 3, rank2, 0))))
    wts_ref[...] = jnp.where(row == 0, w1, jnp.where(row == 1, w2, 0.0)).T
    cnt_ref[...] = run_ref[...].astype(jnp.int32)


def _router(x, gain, w_rg, w_re):
    n, d = x.shape
    tm = ROUTER_ROWS
    assert EXPERT_TOPK == 2 and N_EXPERTS + N_GROUPS <= LANES
    chunks = d // 2 // LANES
    wr = jnp.concatenate(
        [w_re, w_rg, jnp.zeros((d, LANES - N_EXPERTS - N_GROUPS), _F32)], axis=1)
    row = lambda i: (i, 0)
    col = lambda i: (0, i)
    fixed = lambda i: (0, 0)
    return pl.pallas_call(
        _router_body,
        grid=(n // tm,),
        in_specs=[
            pl.BlockSpec((tm, d), row),
            pl.BlockSpec((1, d), fixed),
            pl.BlockSpec((d, LANES), fixed),
        ],
        out_specs=[pl.BlockSpec((tm * chunks, LANES), row), pl.BlockSpec((SUBLANES, tm), col),
                   pl.BlockSpec((tm, LANES), row), pl.BlockSpec((LANES, LANES), fixed)],
        out_shape=[jax.ShapeDtypeStruct((n * chunks, LANES), jnp.uint32),
                   jax.ShapeDtypeStruct((SUBLANES, n), jnp.int32),
                   jax.ShapeDtypeStruct((n, LANES), _F32),
                   jax.ShapeDtypeStruct((LANES, LANES), jnp.int32)],
        scratch_shapes=[pltpu.VMEM((LANES, LANES), _F32)],
        compiler_params=_params("arbitrary"),
        name="router",
    )(x, gain.reshape(1, d), wr)


def _expert_body(bounds_ref, row_tok_ref, hn_hbm, wg_ref, wu_ref, wd_ref, y_hbm,
                 xs_ref, ys_ref, zero_ref, wg_bf, wu_bf, wd_bf, gather_sem, out_sem, tail_sem):
    e = pl.program_id(0)
    n_experts = pl.num_programs(0)
    n_slots = xs_ref.shape[0]
    chunks = wg_bf.shape[0] // 2 // LANES
    rows = xs_ref.shape[1] // chunks
    tile_rows = rows * chunks
    ahead = n_slots - 1
    first, stop = bounds_ref[e], bounds_ref[e + 1]
    n_total = bounds_ref[n_experts]

    def row_copy(tok, s, r):
        src = hn_hbm.at[pl.ds(pl.multiple_of(tok * chunks, chunks), chunks)]
        return pltpu.make_async_copy(src, xs_ref.at[s, pl.ds(r * chunks, chunks)], gather_sem.at[s])

    def start_gather(block):
        def body(r, carry):
            row_copy(row_tok_ref[block * rows + r], block % n_slots, r).start(priority=1)
            return carry
        lax.fori_loop(0, rows, body, 0)

    def wait_gather(block):
        s = block % n_slots
        pltpu.make_async_copy(hn_hbm.at[pl.ds(0, rows * chunks)], xs_ref.at[s],
                              gather_sem.at[s]).wait()

    def out_copy(block):
        s = block % n_slots
        dst = y_hbm.at[pl.ds(pl.multiple_of(block * tile_rows, tile_rows), tile_rows)]
        return pltpu.make_async_copy(ys_ref.at[s], dst, out_sem.at[s])

    def tail_copy(block):
        dst = y_hbm.at[pl.ds(pl.multiple_of(block * tile_rows, tile_rows), tile_rows)]
        return pltpu.make_async_copy(zero_ref, dst, tail_sem.at[0])

    def for_tail_blocks(fn):
        def body(block, carry):
            fn(tail_copy(block))
            return carry
        lax.fori_loop(n_total, y_hbm.shape[0] // tile_rows, body, 0)

    @pl.when(e == 0)
    def _():
        for block in range(ahead):
            start_gather(block)
        zero_ref[...] = jnp.zeros(zero_ref.shape, zero_ref.dtype)
        for_tail_blocks(lambda copy: copy.start())

    @pl.when(stop > first)
    def _():
        wg_bf[...] = wg_ref[0, 0].astype(_BF16)
        wu_bf[...] = wu_ref[0, 0].astype(_BF16)
        wd_bf[...] = wd_ref[0, 0].astype(_BF16)

    def block_step(block, carry):
        @pl.when(block >= n_slots)
        def _():
            out_copy(block - n_slots).wait()

        nxt = block + ahead
        nxt_slot = nxt % n_slots
        nxt_base = jnp.minimum(nxt, n_total - 1) * rows
        per_group = rows // EXPERT_ISSUE_GROUPS

        def issue(group):
            for r in range(group * per_group, (group + 1) * per_group):
                row_copy(row_tok_ref[nxt_base + r], nxt_slot, r).start(priority=1)

        wait_gather(block)
        s = block % n_slots
        x_lo, x_hi = _unpack_bf16_halves(_load_row_tiles(xs_ref.at[s], chunks), _BF16)
        half = x_lo.shape[1]
        gate = _bf16_dot(x_lo, wg_bf[:half, :])
        issue(0)
        gate = gate + _bf16_dot(x_hi, wg_bf[half:, :])
        issue(1)
        up = _bf16_dot(x_lo, wu_bf[:half, :])
        issue(2)
        up = up + _bf16_dot(x_hi, wu_bf[half:, :])
        issue(3)
        act = (jax.nn.silu(gate) * up).astype(_BF16)
        quarter = wd_bf.shape[1] // 4
        y = []
        for c in range(4):
            y.append(_bf16_dot(act, wd_bf[:, c * quarter:(c + 1) * quarter]))
            issue(4 + c)
        packed = jnp.concatenate([_pack_bf16_pair(y[0], y[2]), _pack_bf16_pair(y[1], y[3])], axis=1)
        _store_row_tiles(ys_ref.at[s], packed)
        out_copy(block).start()
        return carry

    lax.fori_loop(first, stop, block_step, 0)

    @pl.when(e == n_experts - 1)
    def _():
        for extra in range(ahead):
            wait_gather(n_total + extra)
        for_tail_blocks(lambda copy: copy.wait())
        for back in range(1, n_slots + 1):
            @pl.when(n_total >= back)
            def _():
                out_copy(n_total - back).wait()


def _expert_mlp(bounds, row_tok, hn, layer, w_gate, w_up, w_down):
    n_experts, d, de = w_gate.shape[1:]
    rows = EXPERT_ROWS
    chunks = d // 2 // LANES
    assert hn.shape[1] == LANES and hn.dtype == jnp.uint32
    grid_spec = pltpu.PrefetchScalarGridSpec(
        num_scalar_prefetch=2,
        grid=(n_experts,),
        in_specs=[
            pl.BlockSpec(memory_space=pl.ANY),
            pl.BlockSpec((1, 1, d, de), lambda e, bd, rt: (layer, e, 0, 0)),
            pl.BlockSpec((1, 1, d, de), lambda e, bd, rt: (layer, e, 0, 0)),
            pl.BlockSpec((1, 1, de, d), lambda e, bd, rt: (layer, e, 0, 0)),
        ],
        out_specs=pl.BlockSpec(memory_space=pl.ANY),
        scratch_shapes=[
            pltpu.VMEM((EXPERT_SLOTS, rows * chunks, LANES), jnp.uint32),
            pltpu.VMEM((EXPERT_SLOTS, rows * chunks, LANES), jnp.uint32),
            pltpu.VMEM((rows * chunks, LANES), jnp.uint32),
            pltpu.VMEM((d, de), _BF16),
            pltpu.VMEM((d, de), _BF16),
            pltpu.VMEM((de, d), _BF16),
            pltpu.SemaphoreType.DMA((EXPERT_SLOTS,)),
            pltpu.SemaphoreType.DMA((EXPERT_SLOTS,)),
            pltpu.SemaphoreType.DMA((1,)),
        ],
    )
    return pl.pallas_call(
        _expert_body,
        grid_spec=grid_spec,
        out_shape=jax.ShapeDtypeStruct((row_tok.shape[0] * chunks, LANES), jnp.uint32),
        compiler_params=_params("arbitrary"),
        name="expert_mlp",
    )(bounds, row_tok, hn, w_gate, w_up, w_down)


def _combine_body(pos_ref, y_hbm, wts_ref, h_ref, o_ref, ys_ref, sem):
    i = pl.program_id(0)
    n_steps = pl.num_programs(0)
    tm = h_ref.shape[0]
    chunks = h_ref.shape[1] // 2 // LANES
    n_tokens = n_steps * tm
    slot = i % 2

    def row_copy(p, s, k, r):
        src = y_hbm.at[pl.ds(pl.multiple_of(p * chunks, chunks), chunks)]
        return pltpu.make_async_copy(src, ys_ref.at[s, k, pl.ds(r * chunks, chunks)], sem.at[s])

    def start_gather(tile, s):
        for r in range(tm):
            for k in range(EXPERT_TOPK):
                row_copy(pos_ref[k * n_tokens + tile * tm + r], s, k, r).start(priority=k)

    def wait_gather(s):
        for k in range(EXPERT_TOPK):
            pltpu.make_async_copy(y_hbm.at[pl.ds(0, tm * chunks)], ys_ref.at[s, k], sem.at[s]).wait()

    @pl.when(i == 0)
    def _():
        start_gather(0, 0)

    @pl.when(i + 1 < n_steps)
    def _():
        start_gather(i + 1, 1 - slot)

    wait_gather(slot)
    w = wts_ref[...]
    half = chunks * LANES
    a_lo, a_hi = _unpack_bf16_halves(_load_row_tiles(ys_ref.at[slot, 0], chunks), _F32)
    b_lo, b_hi = _unpack_bf16_halves(_load_row_tiles(ys_ref.at[slot, 1], chunks), _F32)
    o_ref[:, :half] = h_ref[:, :half] + (w[:, 0:1] * a_lo + w[:, 1:2] * b_lo)
    o_ref[:, half:] = h_ref[:, half:] + (w[:, 0:1] * a_hi + w[:, 1:2] * b_hi)


def _combine(pos, y, wts, h):
    n, d = h.shape
    tm = COMBINE_ROWS
    grid_spec = pltpu.PrefetchScalarGridSpec(
        num_scalar_prefetch=1,
        grid=(n // tm,),
        in_specs=[
            pl.BlockSpec(memory_space=pl.ANY),
            pl.BlockSpec((tm, LANES), lambda i, p: (i, 0)),
            pl.BlockSpec((tm, d), lambda i, p: (i, 0)),
        ],
        out_specs=pl.BlockSpec((tm, d), lambda i, p: (i, 0)),
        scratch_shapes=[
            pltpu.VMEM((2, EXPERT_TOPK, tm * (d // 2 // LANES), LANES), jnp.uint32),
            pltpu.SemaphoreType.DMA((2,)),
        ],
    )
    return pl.pallas_call(
        _combine_body,
        grid_spec=grid_spec,
        out_shape=jax.ShapeDtypeStruct((n, d), _F32),
        compiler_params=_params("arbitrary"),
        name="combine",
    )(pos, y, wts, h)


def _dispatch_plan(ids, counts):
    n = ids.shape[1]
    rows = EXPERT_ROWS
    m = n * EXPERT_TOPK
    n_blocks = -(-(m + N_EXPERTS * (rows - 1)) // rows)
    flat_e = ids[:EXPERT_TOPK].reshape(m)
    rank = ids[EXPERT_TOPK:2 * EXPERT_TOPK].reshape(m)
    counts = counts[:N_EXPERTS, 0]
    starts = jnp.cumsum(counts) - counts
    padded = (counts + rows - 1) // rows * rows
    pends = jnp.cumsum(padded)
    pstarts = pends - padded
    pos = rank
    for e in range(N_EXPERTS):
        pos = pos + jnp.where(flat_e == e, pstarts[e], 0)
    blk_start = jnp.arange(n_blocks, dtype=jnp.int32)[:, None] * rows
    in_seg = (pstarts[None, :] <= blk_start) & (blk_start < pends[None, :])
    shift = jnp.sum(jnp.where(in_seg, (starts - pstarts)[None, :], 0), axis=1, keepdims=True)
    seg_end = jnp.sum(jnp.where(in_seg, (starts + counts)[None, :], 0), axis=1, keepdims=True)
    token = jnp.arange(m, dtype=jnp.int32) % n
    order = jnp.argsort(flat_e * n + token)
    sorted_idx = blk_start + shift + jnp.arange(rows, dtype=jnp.int32)[None, :]
    valid = sorted_idx < seg_end
    row_tok = jnp.where(valid, order[jnp.clip(sorted_idx, 0, m - 1)] % n, 0)
    bounds = jnp.concatenate([pstarts, pends[-1:]]) // rows
    return (bounds.astype(jnp.int32), row_tok.reshape(n_blocks * rows).astype(jnp.int32),
            pos.astype(jnp.int32))


def _hier_moe_residual(h, gain, w_rg, w_re, layer, w_gate, w_up, w_down):
    hn, ids, wts, counts = _router(h, gain, w_rg, w_re)
    bounds, row_tok, pos = _dispatch_plan(ids, counts)
    y = _expert_mlp(bounds, row_tok, hn, layer, w_gate, w_up, w_down)
    return _combine(pos, y, wts, h)


def kernel(x, mix_norm, ffn_norm, conv_w_in, conv_w, conv_w_out, kv_norm, w_kv, k_norm, w_q, q_norm,
           w_o, router_group, router_expert, w_gate, w_up, w_down):
    bsz, seq_len, d = x.shape
    depth = mix_norm.shape[0]
    n_conv = conv_w_in.shape[0]
    assert d == N_HEADS * HEAD_DIM
    h = x.reshape(bsz * seq_len, d)
    kv = None
    for l in range(depth):
        if l < n_conv:
            z = _conv_in(h, mix_norm[l], conv_w_in[l], conv_w[l], seq_len)
            h = _matmul_residual(z, conv_w_out[l], h)
        else:
            j = l - n_conv
            gains = jnp.stack([mix_norm[l], kv_norm])
            head_gains = jnp.stack([q_norm[j], k_norm])
            q = _qkv_projection(h, gains, w_q[j], w_kv, head_gains, with_kv=(j == 0))
            if j == 0:
                kv = q
            attn = _moba_attention(q, 0, kv, d, 2 * d, bsz, seq_len)
            h = _matmul_residual(attn, w_o[j], h)
        h = _hier_moe_residual(h, ffn_norm[l], router_group[l], router_expert[l],
                               l, w_gate, w_up, w_down)
    return h.reshape(bsz, seq_len, d)
```

```python
import functools

import jax
import jax.numpy as jnp
from jax import lax
from jax.experimental import pallas as pl
from jax.experimental.pallas import tpu as pltpu

N_HEADS = 16
HEAD_DIM = 128
CONV_WIDTH = 3
MOBA_BLOCK = 256
MOBA_TOPK = 3
N_GROUPS = 4
EXPERTS_PER_GROUP = 8
N_EXPERTS = N_GROUPS * EXPERTS_PER_GROUP
EXPERT_TOPK = 2
NORM_EPS = 1e-6
ALIBI_MAX_BIAS = 8.0

LANES = 128
SUBLANES = 8
VMEM_LIMIT = 56 * 1024 * 1024
QKV_VMEM_LIMIT = 60 * 1024 * 1024

ROW_TILE = 1024
COL_TILE = 512
RES_COL_TILE = 1024
CONV_COL_TILE = 256
ROUTER_ROWS = 512
EXPERT_ROWS = 128
EXPERT_SLOTS = 4
EXPERT_ISSUE_GROUPS = 8
COMBINE_ROWS = 256
ATTN_HEADS = 8

_BF16 = jnp.bfloat16
_F32 = jnp.float32


def _params(*semantics, vmem_limit=VMEM_LIMIT):
    return pltpu.CompilerParams(dimension_semantics=semantics, vmem_limit_bytes=vmem_limit)


def _rms_normalise(x, gain):
    ms = jnp.mean(x * x, axis=-1, keepdims=True)
    return x * lax.rsqrt(ms + NORM_EPS) * gain


def _bf16_dot(a, b):
    return jnp.dot(a, b, preferred_element_type=_F32)


def _pack_bf16_halves(x):
    c = x.shape[1] // 2
    return _pack_bf16_pair(x[:, :c], x[:, c:])


def _pack_bf16_pair(lo, hi):
    as_bits = lambda v: lax.bitcast_convert_type(v.astype(_BF16).astype(_F32), jnp.uint32)
    return (as_bits(lo) >> 16) | (as_bits(hi) & jnp.uint32(0xFFFF0000))


def _store_row_tiles(ref, value):
    rows, chunks = value.shape[0], value.shape[1] // LANES
    for c in range(chunks):
        ref[pl.ds(c, rows, stride=chunks), :] = value[:, c * LANES:(c + 1) * LANES]


def _load_row_tiles(ref, chunks):
    rows = ref.shape[0] // chunks
    return jnp.concatenate([ref[pl.ds(c, rows, stride=chunks), :] for c in range(chunks)], axis=1)


def _unpack_bf16_halves(p, dtype):
    lo = lax.bitcast_convert_type(p << 16, _F32).astype(dtype)
    hi = lax.bitcast_convert_type(p & jnp.uint32(0xFFFF0000), _F32).astype(dtype)
    return lo, hi


def _qkv_body(x_ref, g_ref, wq_ref, wkv_ref, hg_ref, o_ref, xn_ref, *, q_tiles, qk_tiles):
    j = pl.program_id(1)

    @pl.when(j == 0)
    def _():
        x = x_ref[...]
        normed = x * lax.rsqrt(jnp.mean(x * x, axis=-1, keepdims=True) + NORM_EPS)
        xn_ref[0] = (normed * g_ref[0:1, :]).astype(_BF16)
        xn_ref[1] = (normed * g_ref[1:2, :]).astype(_BF16)

    def project(stream, head_gain_row):
        w_ref = wq_ref if stream == 0 else wkv_ref
        y = _bf16_dot(xn_ref[stream], w_ref[...].astype(_BF16))
        if head_gain_row is None:
            o_ref[...] = y.astype(o_ref.dtype)
            return
        head_gain = hg_ref[head_gain_row:head_gain_row + 1, :]
        for hh in range(y.shape[1] // HEAD_DIM):
            cols = slice(hh * HEAD_DIM, (hh + 1) * HEAD_DIM)
            o_ref[:, cols] = _rms_normalise(y[:, cols], head_gain).astype(o_ref.dtype)

    pl.when(j < q_tiles)(functools.partial(project, 0, 0))
    pl.when((j >= q_tiles) & (j < qk_tiles))(functools.partial(project, 1, 1))
    pl.when(j >= qk_tiles)(functools.partial(project, 1, None))


def _qkv_projection(x, gains, w_q, w_kv, head_gains, with_kv):
    n, d = x.shape
    tm, tn = ROW_TILE, COL_TILE
    n_q, n_k = w_q.shape[1], w_kv.shape[1] // 2
    n_out = n_q + 2 * n_k if with_kv else n_q
    assert n_q % tn == 0 and n_k % tn == 0
    q_tiles = n_q // tn
    return pl.pallas_call(
        functools.partial(_qkv_body, q_tiles=q_tiles, qk_tiles=(n_q + n_k) // tn),
        grid=(n // tm, n_out // tn),
        in_specs=[
            pl.BlockSpec((tm, d), lambda i, j: (i, 0)),
            pl.BlockSpec((2, d), lambda i, j: (0, 0)),
            pl.BlockSpec((d, tn), lambda i, j: (0, jnp.minimum(j, q_tiles - 1))),
            pl.BlockSpec((d, tn), lambda i, j: (0, jnp.maximum(j - q_tiles, 0))),
            pl.BlockSpec((2, HEAD_DIM), lambda i, j: (0, 0)),
        ],
        out_specs=pl.BlockSpec((tm, tn), lambda i, j: (i, j)),
        out_shape=jax.ShapeDtypeStruct((n, n_out), _BF16),
        scratch_shapes=[pltpu.VMEM((2, tm, d), _BF16)],
        compiler_params=_params("arbitrary", "arbitrary", vmem_limit=QKV_VMEM_LIMIT),
        name="qkv_projection",
    )(x, gains, w_q, w_kv, head_gains)


def _conv_in_body(x_ref, g_ref, wb_ref, wc_ref, wx_ref, cw_ref, z_ref, xn_ref, carry_ref,
                  *, tiles_per_seq):
    i = pl.program_id(0)
    j = pl.program_id(1)
    tm = x_ref.shape[0]

    @pl.when(j == 0)
    def _():
        xn_ref[...] = _rms_normalise(x_ref[...], g_ref[...]).astype(_BF16)

    @pl.when(i % tiles_per_seq == 0)
    def _():
        carry_ref[j] = jnp.zeros(carry_ref.shape[1:], _F32)

    xn = xn_ref[...]
    b_gate = _bf16_dot(xn, wb_ref[...].astype(_BF16))
    c_gate = _bf16_dot(xn, wc_ref[...].astype(_BF16))
    xh = _bf16_dot(xn, wx_ref[...].astype(_BF16))
    u = c_gate * xh
    prev = carry_ref[j]
    row = lax.broadcasted_iota(jnp.int32, u.shape, 0)
    last = prev[SUBLANES - 1:SUBLANES]
    u1 = jnp.where(row == 0, last, pltpu.roll(u, 1, axis=0))
    u2 = jnp.where(row == 0, prev[SUBLANES - 2:SUBLANES - 1],
                   jnp.where(row == 1, last, pltpu.roll(u, 2, axis=0)))
    cw = cw_ref[...]
    conv = cw[0:1] * u2 + cw[1:2] * u1 + cw[2:3] * u
    z_ref[...] = (b_gate * conv).astype(z_ref.dtype)
    carry_ref[j] = u[tm - SUBLANES:tm]


def _conv_in(x, gain, w_in, conv_w, seq_len):
    n, d = x.shape
    tm, tn = ROW_TILE, CONV_COL_TILE
    nj = d // tn
    assert CONV_WIDTH - 1 <= SUBLANES and seq_len % tm == 0
    return pl.pallas_call(
        functools.partial(_conv_in_body, tiles_per_seq=seq_len // tm),
        grid=(n // tm, nj),
        in_specs=[
            pl.BlockSpec((tm, d), lambda i, j: (i, 0)),
            pl.BlockSpec((1, d), lambda i, j: (0, 0)),
            pl.BlockSpec((d, tn), lambda i, j: (0, j)),
            pl.BlockSpec((d, tn), lambda i, j: (0, j + nj)),
            pl.BlockSpec((d, tn), lambda i, j: (0, j + 2 * nj)),
            pl.BlockSpec((CONV_WIDTH, tn), lambda i, j: (0, j)),
        ],
        out_specs=pl.BlockSpec((tm, tn), lambda i, j: (i, j)),
        out_shape=jax.ShapeDtypeStruct((n, d), _BF16),
        scratch_shapes=[pltpu.VMEM((tm, d), _BF16), pltpu.VMEM((nj, SUBLANES, tn), _F32)],
        compiler_params=_params("arbitrary", "arbitrary"),
        name="conv_in",
    )(x, gain.reshape(1, d), w_in, w_in, w_in, conv_w)


def _matmul_residual_body(x_ref, w_ref, r_ref, o_ref):
    o_ref[...] = r_ref[...] + _bf16_dot(x_ref[...], w_ref[...].astype(_BF16))


def _matmul_residual(x, w, res):
    n, k = x.shape
    d = w.shape[1]
    tm, tn = ROW_TILE, RES_COL_TILE
    return pl.pallas_call(
        _matmul_residual_body,
        grid=(n // tm, d // tn),
        in_specs=[
            pl.BlockSpec((tm, k), lambda i, j: (i, 0)),
            pl.BlockSpec((k, tn), lambda i, j: (0, j)),
            pl.BlockSpec((tm, tn), lambda i, j: (i, j)),
        ],
        out_specs=pl.BlockSpec((tm, tn), lambda i, j: (i, j)),
        out_shape=jax.ShapeDtypeStruct((n, d), _F32),
        compiler_params=_params("arbitrary", "arbitrary"),
        name="matmul_residual",
    )(x, w, res)


def _moba_body(q_ref, k_ref, v_ref, o_ref, km_ref, vt_ref, bias_ref):
    heads = range(q_ref.shape[1] // HEAD_DIM)
    head0 = pl.program_id(1) * len(heads)
    i = pl.program_id(2)
    kb = MOBA_BLOCK
    n_blocks = k_ref.shape[0] // kb
    nt = (((1,), (1,)), ((), ()))
    log2e = 1.4426950408889634
    scale = HEAD_DIM ** -0.5 * log2e
    slopes = [log2e * jnp.exp2(jnp.full((1, kb), -ALIBI_MAX_BIAS / N_HEADS, _F32)
                               * (head0 + hh + 1).astype(_F32)) for hh in heads]

    def lanes(hh):
        return slice(hh * HEAD_DIM, (hh + 1) * HEAD_DIM)

    @pl.when(i == 0)
    def _():
        rel = (lax.broadcasted_iota(jnp.int32, (kb, kb), 1)
               - lax.broadcasted_iota(jnp.int32, (kb, kb), 0))
        for hh in heads:
            for jb in range(n_blocks):
                rows = slice(jb * kb, (jb + 1) * kb)
                km_ref[hh, jb:jb + 1, :] = jnp.mean(k_ref[rows, lanes(hh)].astype(_F32),
                                                    axis=0, keepdims=True)
                vt_ref[hh, :, rows] = v_ref[rows, lanes(hh)].astype(_F32).T.astype(_BF16)
            bias = slopes[hh] * rel.astype(_F32)
            bias_ref[hh, 0] = bias
            bias_ref[hh, 1] = jnp.where(rel >= 0, bias, jnp.inf)

    q = [q_ref[:, lanes(hh)] for hh in heads]
    sel = []
    for hh in heads:
        gate = lax.dot_general(km_ref[hh], q[hh].astype(_F32), nt,
                               precision=lax.Precision.HIGHEST, preferred_element_type=_F32)
        blk = lax.broadcasted_iota(jnp.int32, gate.shape, 0)
        gate = jnp.where(blk < i, gate, -jnp.inf)
        picked = jnp.zeros(gate.shape, _F32)
        for jb in range(n_blocks - 1):
            g_jb = gate[jb:jb + 1, :]
            beats = (gate > g_jb) | ((gate == g_jb) & (blk < jb))
            n_beat = jnp.sum(beats.astype(_F32), axis=0, keepdims=True)
            picked = jnp.where((blk == jb) & (blk < i) & (n_beat < MOBA_TOPK), 1.0, picked)
        sel.append(picked)

    def attend(own):
        nb = own + 1
        raw = [lax.dot_general(k_ref[0:nb * kb, lanes(hh)], q[hh], nt, preferred_element_type=_F32)
               for hh in heads]
        tiles, col_max = [[] for _ in heads], [[] for _ in heads]
        for jb in range(nb):
            for hh in heads:
                tile = raw[hh][jb * kb:(jb + 1) * kb, :] * scale - bias_ref[hh, 1 if jb == own else 0]
                mx = jnp.max(tile, axis=0, keepdims=True)
                if jb != own:
                    mx = jnp.where(sel[hh][jb:jb + 1, :] > 0.5,
                                   mx - slopes[hh] * float((own - jb) * kb), -jnp.inf)
                tiles[hh].append(tile), col_max[hh].append(mx)
        m = [functools.reduce(jnp.maximum, col_max[hh]) for hh in heads]
        probs, l = [[] for _ in heads], [None for _ in heads]
        for jb in range(nb):
            for hh in heads:
                offset = m[hh] if jb == own else jnp.where(
                    sel[hh][jb:jb + 1, :] > 0.5, m[hh] + slopes[hh] * float((own - jb) * kb), jnp.inf)
                p = jnp.exp2(tiles[hh][jb] - offset)
                p_sum = jnp.sum(p, axis=0, keepdims=True)
                l[hh] = p_sum if l[hh] is None else l[hh] + p_sum
                probs[hh].append(p.astype(_BF16))
        for hh in heads:
            p_all = probs[hh][0] if nb == 1 else jnp.concatenate(probs[hh], axis=0)
            acc = _bf16_dot(vt_ref[hh, :, 0:nb * kb], p_all)
            o_ref[:, lanes(hh)] = (acc / l[hh]).T.astype(o_ref.dtype)

    for own in range(n_blocks):
        pl.when(i == own)(functools.partial(attend, own))


def _moba_attention(q, q_col, kv, k_col, v_col, batch, seq_len):
    n = q.shape[0]
    d = N_HEADS * HEAD_DIM
    kb = MOBA_BLOCK
    nqb = seq_len // kb
    hp = ATTN_HEADS
    width = hp * HEAD_DIM
    assert seq_len % kb == 0 and nqb == SUBLANES and N_HEADS % hp == 0
    assert q_col % width == 0 and k_col % width == 0 and v_col % width == 0
    q_blk, k_blk, v_blk = q_col // width, k_col // width, v_col // width
    return pl.pallas_call(
        _moba_body,
        grid=(batch, N_HEADS // hp, nqb),
        in_specs=[
            pl.BlockSpec((kb, width), lambda b, h, i: (b * nqb + i, q_blk + h)),
            pl.BlockSpec((seq_len, width), lambda b, h, i: (b, k_blk + h)),
            pl.BlockSpec((seq_len, width), lambda b, h, i: (b, v_blk + h)),
        ],
        out_specs=pl.BlockSpec((kb, width), lambda b, h, i: (b * nqb + i, h)),
        out_shape=jax.ShapeDtypeStruct((n, d), _BF16),
        scratch_shapes=[pltpu.VMEM((hp, nqb, HEAD_DIM), _F32),
                        pltpu.VMEM((hp, HEAD_DIM, seq_len), _BF16),
                        pltpu.VMEM((hp, 2, kb, kb), _F32)],
        compiler_params=_params("arbitrary", "arbitrary", "arbitrary"),
        name="moba_attention",
    )(q, kv, kv)


def _router_body(x_ref, g_ref, wr_ref, hn_ref, ids_ref, wts_ref, cnt_ref, run_ref):
    step = pl.program_id(0)
    hn = _rms_normalise(x_ref[...], g_ref[...])
    _store_row_tiles(hn_ref, _pack_bf16_halves(hn))
    w = wr_ref[...]
    hn_hi, w_hi = hn.astype(_BF16), w.astype(_BF16)
    hn_lo = (hn - hn_hi.astype(_F32)).astype(_BF16)
    w_lo = (w - w_hi.astype(_F32)).astype(_BF16)
    logits = (_bf16_dot(hn_hi, w_hi) + (_bf16_dot(hn_hi, w_lo) + _bf16_dot(hn_lo, w_hi))).T
    row = lax.broadcasted_iota(jnp.int32, logits.shape, 0)
    far = jnp.int32(logits.shape[0])

    def first_row(mask):
        return jnp.min(jnp.where(mask, row, far), axis=0, keepdims=True)

    is_group = (row >= N_EXPERTS) & (row < N_EXPERTS + N_GROUPS)
    g_max = jnp.max(jnp.where(is_group, logits, -jnp.inf), axis=0, keepdims=True)
    g_sel = first_row(is_group & (logits == g_max)) - N_EXPERTS
    g_w = 1.0 / jnp.sum(jnp.where(is_group, jnp.exp(logits - g_max), 0.0), axis=0, keepdims=True)

    in_group = (row >= g_sel * EXPERTS_PER_GROUP) & (row < (g_sel + 1) * EXPERTS_PER_GROUP)
    top1 = jnp.max(jnp.where(in_group, logits, -jnp.inf), axis=0, keepdims=True)
    idx1 = first_row(in_group & (logits == top1))
    rest = in_group & (row != idx1)
    top2 = jnp.max(jnp.where(rest, logits, -jnp.inf), axis=0, keepdims=True)
    idx2 = first_row(rest & (logits == top2))
    t = jnp.exp(top2 - top1)
    w1 = 1.0 / (1.0 + t) * g_w
    w2 = t / (1.0 + t) * g_w

    @pl.when(step == 0)
    def _():
        run_ref[...] = jnp.zeros(run_ref.shape, _F32)

    tm = logits.shape[1]
    chosen = jnp.where((row == idx1) | (row == idx2), 1.0, 0.0)
    earlier = (lax.broadcasted_iota(jnp.int32, (tm, tm), 0)
               < lax.broadcasted_iota(jnp.int32, (tm, tm), 1))
    before = run_ref[:, 0:1] + _bf16_dot(chosen.astype(_BF16),
                                         jnp.where(earlier, 1.0, 0.0).astype(_BF16))
    rank1 = jnp.sum(jnp.where(row == idx1, before, 0.0), axis=0, keepdims=True).astype(jnp.int32)
    rank2 = jnp.sum(jnp.where(row == idx2, before, 0.0), axis=0, keepdims=True).astype(jnp.int32)
    run_ref[...] += jnp.sum(chosen, axis=1, keepdims=True)

    out_row = lax.broadcasted_iota(jnp.int32, ids_ref.shape, 0)
    ids_ref[...] = jnp.where(out_row == 0, idx1, jnp.where(out_row == 1, idx2,
                             jnp.where(out_row == 2, rank1, jnp.where(out_row == 3, rank2, 0))))
    wts_ref[...] = jnp.where(row == 0, w1, jnp.where(row == 1, w2, 0.0)).T
    cnt_ref[...] = run_ref[...].astype(jnp.int32)


def _router(x, gain, w_rg, w_re):
    n, d = x.shape
    tm = ROUTER_ROWS
    assert EXPERT_TOPK == 2 and N_EXPERTS + N_GROUPS <= LANES
    chunks = d // 2 // LANES
    wr = jnp.concatenate(
        [w_re, w_rg, jnp.zeros((d, LANES - N_EXPERTS - N_GROUPS), _F32)], axis=1)
    row = lambda i: (i, 0)
    col = lambda i: (0, i)
    fixed = lambda i: (0, 0)
    return pl.pallas_call(
        _router_body,
        grid=(n // tm,),
        in_specs=[
            pl.BlockSpec((tm, d), row),
            pl.BlockSpec((1, d), fixed),
            pl.BlockSpec((d, LANES), fixed),
        ],
        out_specs=[pl.BlockSpec((tm * chunks, LANES), row), pl.BlockSpec((SUBLANES, tm), col),
                   pl.BlockSpec((tm, LANES), row), pl.BlockSpec((LANES, LANES), fixed)],
        out_shape=[jax.ShapeDtypeStruct((n * chunks, LANES), jnp.uint32),
                   jax.ShapeDtypeStruct((SUBLANES, n), jnp.int32),
                   jax.ShapeDtypeStruct((n, LANES), _F32),
                   jax.ShapeDtypeStruct((LANES, LANES), jnp.int32)],
        scratch_shapes=[pltpu.VMEM((LANES, LANES), _F32)],
        compiler_params=_params("arbitrary"),
        name="router",
    )(x, gain.reshape(1, d), wr)


def _expert_body(bounds_ref, row_tok_ref, hn_hbm, wg_ref, wu_ref, wd_ref, y_hbm,
                 xs_ref, ys_ref, zero_ref, wg_bf, wu_bf, wd_bf, gather_sem, out_sem, tail_sem):
    e = pl.program_id(0)
    n_experts = pl.num_programs(0)
    n_slots = xs_ref.shape[0]
    chunks = wg_bf.shape[0] // 2 // LANES
    rows = xs_ref.shape[1] // chunks
    tile_rows = rows * chunks
    ahead = n_slots - 1
    first, stop = bounds_ref[e], bounds_ref[e + 1]
    n_total = bounds_ref[n_experts]

    def row_copy(tok, s, r):
        src = hn_hbm.at[pl.ds(pl.multiple_of(tok * chunks, chunks), chunks)]
        return pltpu.make_async_copy(src, xs_ref.at[s, pl.ds(r * chunks, chunks)], gather_sem.at[s])

    def start_gather(block):
        def body(r, carry):
            row_copy(row_tok_ref[block * rows + r], block % n_slots, r).start(priority=1)
            return carry
        lax.fori_loop(0, rows, body, 0)

    def wait_gather(block):
        s = block % n_slots
        pltpu.make_async_copy(hn_hbm.at[pl.ds(0, rows * chunks)], xs_ref.at[s],
                              gather_sem.at[s]).wait()

    def out_copy(block):
        s = block % n_slots
        dst = y_hbm.at[pl.ds(pl.multiple_of(block * tile_rows, tile_rows), tile_rows)]
        return pltpu.make_async_copy(ys_ref.at[s], dst, out_sem.at[s])

    def tail_copy(block):
        dst = y_hbm.at[pl.ds(pl.multiple_of(block * tile_rows, tile_rows), tile_rows)]
        return pltpu.make_async_copy(zero_ref, dst, tail_sem.at[0])

    def for_tail_blocks(fn):
        def body(block, carry):
            fn(tail_copy(block))
            return carry
        lax.fori_loop(n_total, y_hbm.shape[0] // tile_rows, body, 0)

    @pl.when(e == 0)
    def _():
        for block in range(ahead):
            start_gather(block)
        zero_ref[...] = jnp.zeros(zero_ref.shape, zero_ref.dtype)
        for_tail_blocks(lambda copy: copy.start())

    @pl.when(stop > first)
    def _():
        wg_bf[...] = wg_ref[0, 0].astype(_BF16)
        wu_bf[...] = wu_ref[0, 0].astype(_BF16)
        wd_bf[...] = wd_ref[0, 0].astype(_BF16)

    def block_step(block, carry):
        @pl.when(block >= n_slots)
        def _():
            out_copy(block - n_slots).wait()

        nxt = block + ahead
        nxt_slot = nxt % n_slots
        nxt_base = jnp.minimum(nxt, n_total - 1) * rows
        per_group = rows // EXPERT_ISSUE_GROUPS

        def issue(group):
            for r in range(group * per_group, (group + 1) * per_group):
                row_copy(row_tok_ref[nxt_base + r], nxt_slot, r).start(priority=1)

        wait_gather(block)
        s = block % n_slots
        x_lo, x_hi = _unpack_bf16_halves(_load_row_tiles(xs_ref.at[s], chunks), _BF16)
        half = x_lo.shape[1]
        gate = _bf16_dot(x_lo, wg_bf[:half, :])
        issue(0)
        gate = gate + _bf16_dot(x_hi, wg_bf[half:, :])
        issue(1)
        up = _bf16_dot(x_lo, wu_bf[:half, :])
        issue(2)
        up = up + _bf16_dot(x_hi, wu_bf[half:, :])
        issue(3)
        act = (jax.nn.silu(gate) * up).astype(_BF16)
        quarter = wd_bf.shape[1] // 4
        y = []
        for c in range(4):
            y.append(_bf16_dot(act, wd_bf[:, c * quarter:(c + 1) * quarter]))
            issue(4 + c)
        packed = jnp.concatenate([_pack_bf16_pair(y[0], y[2]), _pack_bf16_pair(y[1], y[3])], axis=1)
        _store_row_tiles(ys_ref.at[s], packed)
        out_copy(block).start()
        return carry

    lax.fori_loop(first, stop, block_step, 0)

    @pl.when(e == n_experts - 1)
    def _():
        for extra in range(ahead):
            wait_gather(n_total + extra)
        for_tail_blocks(lambda copy: copy.wait())
        for back in range(1, n_slots + 1):
            @pl.when(n_total >= back)
            def _():
                out_copy(n_total - back).wait()


def _expert_mlp(bounds, row_tok, hn, layer, w_gate, w_up, w_down):
    n_experts, d, de = w_gate.shape[1:]
    rows = EXPERT_ROWS
    chunks = d // 2 // LANES
    assert hn.shape[1] == LANES and hn.dtype == jnp.uint32
    grid_spec = pltpu.PrefetchScalarGridSpec(
        num_scalar_prefetch=2,
        grid=(n_experts,),
        in_specs=[
            pl.BlockSpec(memory_space=pl.ANY),
            pl.BlockSpec((1, 1, d, de), lambda e, bd, rt: (layer, e, 0, 0)),
            pl.BlockSpec((1, 1, d, de), lambda e, bd, rt: (layer, e, 0, 0)),
            pl.BlockSpec((1, 1, de, d), lambda e, bd, rt: (layer, e, 0, 0)),
        ],
        out_specs=pl.BlockSpec(memory_space=pl.ANY),
        scratch_shapes=[
            pltpu.VMEM((EXPERT_SLOTS, rows * chunks, LANES), jnp.uint32),
            pltpu.VMEM((EXPERT_SLOTS, rows * chunks, LANES), jnp.uint32),
            pltpu.VMEM((rows * chunks, LANES), jnp.uint32),
            pltpu.VMEM((d, de), _BF16),
            pltpu.VMEM((d, de), _BF16),
            pltpu.VMEM((de, d), _BF16),
            pltpu.SemaphoreType.DMA((EXPERT_SLOTS,)),
            pltpu.SemaphoreType.DMA((EXPERT_SLOTS,)),
            pltpu.SemaphoreType.DMA((1,)),
        ],
    )
    return pl.pallas_call(
        _expert_body,
        grid_spec=grid_spec,
        out_shape=jax.ShapeDtypeStruct((row_tok.shape[0] * chunks, LANES), jnp.uint32),
        compiler_params=_params("arbitrary"),
        name="expert_mlp",
    )(bounds, row_tok, hn, w_gate, w_up, w_down)


def _combine_body(pos_ref, y_hbm, wts_ref, h_ref, o_ref, ys_ref, sem):
    i = pl.program_id(0)
    n_steps = pl.num_programs(0)
    tm = h_ref.shape[0]
    chunks = h_ref.shape[1] // 2 // LANES
    n_tokens = n_steps * tm
    slot = i % 2

    def row_copy(p, s, k, r):
        src = y_hbm.at[pl.ds(pl.multiple_of(p * chunks, chunks), chunks)]
        return pltpu.make_async_copy(src, ys_ref.at[s, k, pl.ds(r * chunks, chunks)], sem.at[s])

    def start_gather(tile, s):
        for r in range(tm):
            for k in range(EXPERT_TOPK):
                row_copy(pos_ref[k * n_tokens + tile * tm + r], s, k, r).start(priority=k)

    def wait_gather(s):
        for k in range(EXPERT_TOPK):
            pltpu.make_async_copy(y_hbm.at[pl.ds(0, tm * chunks)], ys_ref.at[s, k], sem.at[s]).wait()

    @pl.when(i == 0)
    def _():
        start_gather(0, 0)

    @pl.when(i + 1 < n_steps)
    def _():
        start_gather(i + 1, 1 - slot)

    wait_gather(slot)
    w = wts_ref[...]
    half = chunks * LANES
    a_lo, a_hi = _unpack_bf16_halves(_load_row_tiles(ys_ref.at[slot, 0], chunks), _F32)
    b_lo, b_hi = _unpack_bf16_halves(_load_row_tiles(ys_ref.at[slot, 1], chunks), _F32)
    o_ref[:, :half] = h_ref[:, :half] + (w[:, 0:1] * a_lo + w[:, 1:2] * b_lo)
    o_ref[:, half:] = h_ref[:, half:] + (w[:, 0:1] * a_hi + w[:, 1:2] * b_hi)


def _combine(pos, y, wts, h):
    n, d = h.shape
    tm = COMBINE_ROWS
    grid_spec = pltpu.PrefetchScalarGridSpec(
        num_scalar_prefetch=1,
        grid=(n // tm,),
        in_specs=[
            pl.BlockSpec(memory_space=pl.ANY),
            pl.BlockSpec((tm, LANES), lambda i, p: (i, 0)),
            pl.BlockSpec((tm, d), lambda i, p: (i, 0)),
        ],
        out_specs=pl.BlockSpec((tm, d), lambda i, p: (i, 0)),
        scratch_shapes=[
            pltpu.VMEM((2, EXPERT_TOPK, tm * (d // 2 // LANES), LANES), jnp.uint32),
            pltpu.SemaphoreType.DMA((2,)),
        ],
    )
    return pl.pallas_call(
        _combine_body,
        grid_spec=grid_spec,
        out_shape=jax.ShapeDtypeStruct((n, d), _F32),
        compiler_params=_params("arbitrary"),
        name="combine",
    )(pos, y, wts, h)


def _dispatch_plan(ids, counts):
    n = ids.shape[1]
    rows = EXPERT_ROWS
    m = n * EXPERT_TOPK
    n_blocks = -(-(m + N_EXPERTS * (rows - 1)) // rows)
    flat_e = ids[:EXPERT_TOPK].reshape(m)
    rank = ids[EXPERT_TOPK:2 * EXPERT_TOPK].reshape(m)
    counts = counts[:N_EXPERTS, 0]
    starts = jnp.cumsum(counts) - counts
    padded = (counts + rows - 1) // rows * rows
    pends = jnp.cumsum(padded)
    pstarts = pends - padded
    pos = rank
    for e in range(N_EXPERTS):
        pos = pos + jnp.where(flat_e == e, pstarts[e], 0)
    blk_start = jnp.arange(n_blocks, dtype=jnp.int32)[:, None] * rows
    in_seg = (pstarts[None, :] <= blk_start) & (blk_start < pends[None, :])
    shift = jnp.sum(jnp.where(in_seg, (starts - pstarts)[None, :], 0), axis=1, keepdims=True)
    seg_end = jnp.sum(jnp.where(in_seg, (starts + counts)[None, :], 0), axis=1, keepdims=True)
    token = jnp.arange(m, dtype=jnp.int32) % n
    order = jnp.argsort(flat_e * n + token)
    sorted_idx = blk_start + shift + jnp.arange(rows, dtype=jnp.int32)[None, :]
    valid = sorted_idx < seg_end
    row_tok = jnp.where(valid, order[jnp.clip(sorted_idx, 0, m - 1)] % n, 0)
    bounds = jnp.concatenate([pstarts, pends[-1:]]) // rows
    return (bounds.astype(jnp.int32), row_tok.reshape(n_blocks * rows).astype(jnp.int32),
            pos.astype(jnp.int32))


def _hier_moe_residual(h, gain, w_rg, w_re, layer, w_gate, w_up, w_down):
    hn, ids, wts, counts = _router(h, gain, w_rg, w_re)
    bounds, row_tok, pos = _dispatch_plan(ids, counts)
    y = _expert_mlp(bounds, row_tok, hn, layer, w_gate, w_up, w_down)
    return _combine(pos, y, wts, h)


def kernel(x, mix_norm, ffn_norm, conv_w_in, conv_w, conv_w_out, kv_norm, w_kv, k_norm, w_q, q_norm,
           w_o, router_group, router_expert, w_gate, w_up, w_down):
    bsz, seq_len, d = x.shape
    depth = mix_norm.shape[0]
    n_conv = conv_w_in.shape[0]
    assert d == N_HEADS * HEAD_DIM
    h = x.reshape(bsz * seq_len, d)
    kv = None
    for l in range(depth):
        if l < n_conv:
            z = _conv_in(h, mix_norm[l], conv_w_in[l], conv_w[l], seq_len)
            h = _matmul_residual(z, conv_w_out[l], h)
        else:
            j = l - n_conv
            gains = jnp.stack([mix_norm[l], kv_norm])
            head_gains = jnp.stack([q_norm[j], k_norm])
            q = _qkv_projection(h, gains, w_q[j], w_kv, head_gains, with_kv=(j == 0))
            if j == 0:
                kv = q
            attn = _moba_attention(q, 0, kv, d, 2 * d, bsz, seq_len)
            h = _matmul_residual(attn, w_o[j], h)
        h = _hier_moe_residual(h, ffn_norm[l], router_group[l], router_expert[l],
                               l, w_gate, w_up, w_down)
    return h.reshape(bsz, seq_len, d)
```

```python
import functools

import jax
import jax.numpy as jnp
from jax import lax
from jax.experimental import pallas as pl
from jax.experimental.pallas import tpu as pltpu

N_HEADS = 16
HEAD_DIM = 128
CONV_WIDTH = 3
MOBA_BLOCK = 256
MOBA_TOPK = 3
N_GROUPS = 4
EXPERTS_PER_GROUP = 8
N_EXPERTS = N_GROUPS * EXPERTS_PER_GROUP
EXPERT_TOPK = 2
NORM_EPS = 1e-6
ALIBI_MAX_BIAS = 8.0

LANES = 128
SUBLANES = 8
VMEM_LIMIT = 56 * 1024 * 1024

ROW_TILE = 1024
COL_TILE = 512
RES_COL_TILE = 1024
CONV_COL_TILE = 256
ROUTER_ROWS = 512
EXPERT_ROWS = 128
EXPERT_SLOTS = 4
EXPERT_ISSUE_GROUPS = 8
COMBINE_ROWS = 256
ATTN_HEADS = 8

_BF16 = jnp.bfloat16
_F32 = jnp.float32


def _params(*semantics):
    return pltpu.CompilerParams(dimension_semantics=semantics, vmem_limit_bytes=VMEM_LIMIT)


def _rms_normalise(x, gain):
    ms = jnp.mean(x * x, axis=-1, keepdims=True)
    return x * lax.rsqrt(ms + NORM_EPS) * gain


def _bf16_dot(a, b):
    return jnp.dot(a, b, preferred_element_type=_F32)


def _pack_bf16_halves(x):
    c = x.shape[1] // 2
    return _pack_bf16_pair(x[:, :c], x[:, c:])


def _pack_bf16_pair(lo, hi):
    as_bits = lambda v: lax.bitcast_convert_type(v.astype(_BF16).astype(_F32), jnp.uint32)
    return (as_bits(lo) >> 16) | (as_bits(hi) & jnp.uint32(0xFFFF0000))


def _store_row_tiles(ref, value):
    rows, chunks = value.shape[0], value.shape[1] // LANES
    for c in range(chunks):
        ref[pl.ds(c, rows, stride=chunks), :] = value[:, c * LANES:(c + 1) * LANES]


def _load_row_tiles(ref, chunks):
    rows = ref.shape[0] // chunks
    return jnp.concatenate([ref[pl.ds(c, rows, stride=chunks), :] for c in range(chunks)], axis=1)


def _unpack_bf16_halves(p, dtype):
    lo = lax.bitcast_convert_type(p << 16, _F32).astype(dtype)
    hi = lax.bitcast_convert_type(p & jnp.uint32(0xFFFF0000), _F32).astype(dtype)
    return lo, hi


def _qkv_body(x_ref, g_ref, *refs, with_kv):
    if with_kv:
        wq_ref, wk_ref, wv_ref, hg_ref, q_ref, k_ref, v_ref, xn_ref = refs
    else:
        wq_ref, hg_ref, q_ref, xn_ref = refs

    @pl.when(pl.program_id(1) == 0)
    def _():
        x = x_ref[...]
        normed = x * lax.rsqrt(jnp.mean(x * x, axis=-1, keepdims=True) + NORM_EPS)
        xn_ref[0] = (normed * g_ref[0:1, :]).astype(_BF16)
        if with_kv:
            xn_ref[1] = (normed * g_ref[1:2, :]).astype(_BF16)

    def store_head_normalised(o_ref, y, gain_row):
        head_gain = hg_ref[gain_row:gain_row + 1, :]
        for hh in range(y.shape[1] // HEAD_DIM):
            cols = slice(hh * HEAD_DIM, (hh + 1) * HEAD_DIM)
            o_ref[:, cols] = _rms_normalise(y[:, cols], head_gain).astype(o_ref.dtype)

    products = [_bf16_dot(xn_ref[0], wq_ref[...].astype(_BF16))]
    if with_kv:
        products += [_bf16_dot(xn_ref[1], w_ref[...].astype(_BF16)) for w_ref in (wk_ref, wv_ref)]
    store_head_normalised(q_ref, products[0], 0)
    if with_kv:
        store_head_normalised(k_ref, products[1], 1)
        v_ref[...] = products[2].astype(v_ref.dtype)


def _qkv_projection(x, gains, w_q, w_kv, head_gains, with_kv):
    n, d = x.shape
    tm, tn = ROW_TILE, CONV_COL_TILE
    n_q = w_q.shape[1]
    nj = n_q // tn
    assert n_q % tn == 0 and w_kv.shape[1] == 2 * n_q
    tile = lambda i, j: (i, j)
    fixed = lambda i, j: (0, 0)
    weights = [(w_q, lambda i, j: (0, j))]
    if with_kv:
        weights += [(w_kv, lambda i, j: (0, j)), (w_kv, lambda i, j: (0, j + nj))]
    n_out = len(weights)
    outs = pl.pallas_call(
        functools.partial(_qkv_body, with_kv=with_kv),
        grid=(n // tm, nj),
        in_specs=([pl.BlockSpec((tm, d), lambda i, j: (i, 0)), pl.BlockSpec((2, d), fixed)]
                  + [pl.BlockSpec((d, tn), index_map) for _, index_map in weights]
                  + [pl.BlockSpec((2, HEAD_DIM), fixed)]),
        out_specs=[pl.BlockSpec((tm, tn), tile)] * n_out,
        out_shape=[jax.ShapeDtypeStruct((n, n_q), _BF16)] * n_out,
        scratch_shapes=[pltpu.VMEM((2, tm, d), _BF16)],
        compiler_params=_params("arbitrary", "arbitrary"),
        name="qkv_projection",
    )(x, gains, *[w for w, _ in weights], head_gains)
    return tuple(outs)


def _conv_in_body(x_ref, g_ref, wb_ref, wc_ref, wx_ref, cw_ref, z_ref, xn_ref, carry_ref,
                  *, tiles_per_seq):
    i = pl.program_id(0)
    j = pl.program_id(1)
    tm = x_ref.shape[0]

    @pl.when(j == 0)
    def _():
        xn_ref[...] = _rms_normalise(x_ref[...], g_ref[...]).astype(_BF16)

    @pl.when(i % tiles_per_seq == 0)
    def _():
        carry_ref[j] = jnp.zeros(carry_ref.shape[1:], _F32)

    xn = xn_ref[...]
    b_gate = _bf16_dot(xn, wb_ref[...].astype(_BF16))
    c_gate = _bf16_dot(xn, wc_ref[...].astype(_BF16))
    xh = _bf16_dot(xn, wx_ref[...].astype(_BF16))
    u = c_gate * xh
    prev = carry_ref[j]
    row = lax.broadcasted_iota(jnp.int32, u.shape, 0)
    last = prev[SUBLANES - 1:SUBLANES]
    u1 = jnp.where(row == 0, last, pltpu.roll(u, 1, axis=0))
    u2 = jnp.where(row == 0, prev[SUBLANES - 2:SUBLANES - 1],
                   jnp.where(row == 1, last, pltpu.roll(u, 2, axis=0)))
    cw = cw_ref[...]
    conv = cw[0:1] * u2 + cw[1:2] * u1 + cw[2:3] * u
    z_ref[...] = (b_gate * conv).astype(z_ref.dtype)
    carry_ref[j] = u[tm - SUBLANES:tm]


def _conv_in(x, gain, w_in, conv_w, seq_len):
    n, d = x.shape
    tm, tn = ROW_TILE, CONV_COL_TILE
    nj = d // tn
    assert CONV_WIDTH - 1 <= SUBLANES and seq_len % tm == 0
    return pl.pallas_call(
        functools.partial(_conv_in_body, tiles_per_seq=seq_len // tm),
        grid=(n // tm, nj),
        in_specs=[
            pl.BlockSpec((tm, d), lambda i, j: (i, 0)),
            pl.BlockSpec((1, d), lambda i, j: (0, 0)),
            pl.BlockSpec((d, tn), lambda i, j: (0, j)),
            pl.BlockSpec((d, tn), lambda i, j: (0, j + nj)),
            pl.BlockSpec((d, tn), lambda i, j: (0, j + 2 * nj)),
            pl.BlockSpec((CONV_WIDTH, tn), lambda i, j: (0, j)),
        ],
        out_specs=pl.BlockSpec((tm, tn), lambda i, j: (i, j)),
        out_shape=jax.ShapeDtypeStruct((n, d), _BF16),
        scratch_shapes=[pltpu.VMEM((tm, d), _BF16), pltpu.VMEM((nj, SUBLANES, tn), _F32)],
        compiler_params=_params("arbitrary", "arbitrary"),
        name="conv_in",
    )(x, gain.reshape(1, d), w_in, w_in, w_in, conv_w)


def _matmul_residual_body(x_ref, w_ref, r_ref, o_ref):
    o_ref[...] = r_ref[...] + _bf16_dot(x_ref[...], w_ref[...].astype(_BF16))


def _matmul_residual(x, w, res):
    n, k = x.shape
    d = w.shape[1]
    tm, tn = ROW_TILE, RES_COL_TILE
    return pl.pallas_call(
        _matmul_residual_body,
        grid=(n // tm, d // tn),
        in_specs=[
            pl.BlockSpec((tm, k), lambda i, j: (i, 0)),
            pl.BlockSpec((k, tn), lambda i, j: (0, j)),
            pl.BlockSpec((tm, tn), lambda i, j: (i, j)),
        ],
        out_specs=pl.BlockSpec((tm, tn), lambda i, j: (i, j)),
        out_shape=jax.ShapeDtypeStruct((n, d), _F32),
        compiler_params=_params("arbitrary", "arbitrary"),
        name="matmul_residual",
    )(x, w, res)


def _moba_body(q_ref, k_ref, v_ref, o_ref, km_ref, vt_ref, bias_ref):
    heads = range(q_ref.shape[1] // HEAD_DIM)
    head0 = pl.program_id(1) * len(heads)
    i = pl.program_id(2)
    kb = MOBA_BLOCK
    n_blocks = k_ref.shape[0] // kb
    nt = (((1,), (1,)), ((), ()))
    log2e = 1.4426950408889634
    scale = HEAD_DIM ** -0.5 * log2e
    slopes = [log2e * jnp.exp2(jnp.full((1, kb), -ALIBI_MAX_BIAS / N_HEADS, _F32)
                               * (head0 + hh + 1).astype(_F32)) for hh in heads]

    def lanes(hh):
        return slice(hh * HEAD_DIM, (hh + 1) * HEAD_DIM)

    @pl.when(i == 0)
    def _():
        rel = (lax.broadcasted_iota(jnp.int32, (kb, kb), 1)
               - lax.broadcasted_iota(jnp.int32, (kb, kb), 0))
        for hh in heads:
            for jb in range(n_blocks):
                rows = slice(jb * kb, (jb + 1) * kb)
                km_ref[hh, jb:jb + 1, :] = jnp.mean(k_ref[rows, lanes(hh)].astype(_F32),
                                                    axis=0, keepdims=True)
                vt_ref[hh, :, rows] = v_ref[rows, lanes(hh)].astype(_F32).T.astype(_BF16)
            bias = slopes[hh] * rel.astype(_F32)
            bias_ref[hh, 0] = bias
            bias_ref[hh, 1] = jnp.where(rel >= 0, bias, jnp.inf)

    q = [q_ref[:, lanes(hh)] for hh in heads]
    sel = []
    for hh in heads:
        gate = lax.dot_general(km_ref[hh], q[hh].astype(_F32), nt,
                               precision=lax.Precision.HIGHEST, preferred_element_type=_F32)
        blk = lax.broadcasted_iota(jnp.int32, gate.shape, 0)
        gate = jnp.where(blk < i, gate, -jnp.inf)
        picked = jnp.zeros(gate.shape, _F32)
        for jb in range(n_blocks - 1):
            g_jb = gate[jb:jb + 1, :]
            beats = (gate > g_jb) | ((gate == g_jb) & (blk < jb))
            n_beat = jnp.sum(beats.astype(_F32), axis=0, keepdims=True)
            picked = jnp.where((blk == jb) & (blk < i) & (n_beat < MOBA_TOPK), 1.0, picked)
        sel.append(picked)

    def attend(own):
        nb = own + 1
        raw = [lax.dot_general(k_ref[0:nb * kb, lanes(hh)], q[hh], nt, preferred_element_type=_F32)
               for hh in heads]
        tiles, col_max = [[] for _ in heads], [[] for _ in heads]
        for jb in range(nb):
            for hh in heads:
                tile = raw[hh][jb * kb:(jb + 1) * kb, :] * scale - bias_ref[hh, 1 if jb == own else 0]
                mx = jnp.max(tile, axis=0, keepdims=True)
                if jb != own:
                    mx = jnp.where(sel[hh][jb:jb + 1, :] > 0.5,
                                   mx - slopes[hh] * float((own - jb) * kb), -jnp.inf)
                tiles[hh].append(tile), col_max[hh].append(mx)
        m = [functools.reduce(jnp.maximum, col_max[hh]) for hh in heads]
        probs, l = [[] for _ in heads], [None for _ in heads]
        for jb in range(nb):
            for hh in heads:
                offset = m[hh] if jb == own else jnp.where(
                    sel[hh][jb:jb + 1, :] > 0.5, m[hh] + slopes[hh] * float((own - jb) * kb), jnp.inf)
                p = jnp.exp2(tiles[hh][jb] - offset)
                p_sum = jnp.sum(p, axis=0, keepdims=True)
                l[hh] = p_sum if l[hh] is None else l[hh] + p_sum
                probs[hh].append(p.astype(_BF16))
        for hh in heads:
            p_all = probs[hh][0] if nb == 1 else jnp.concatenate(probs[hh], axis=0)
            acc = _bf16_dot(vt_ref[hh, :, 0:nb * kb], p_all)
            o_ref[:, lanes(hh)] = (acc / l[hh]).T.astype(o_ref.dtype)

    for own in range(n_blocks):
        pl.when(i == own)(functools.partial(attend, own))


def _moba_attention(q, k, v, batch, seq_len):
    n, d = q.shape
    kb = MOBA_BLOCK
    nqb = seq_len // kb
    hp = ATTN_HEADS
    width = hp * HEAD_DIM
    assert seq_len % kb == 0 and nqb == SUBLANES and N_HEADS % hp == 0 and d == N_HEADS * HEAD_DIM
    return pl.pallas_call(
        _moba_body,
        grid=(batch, N_HEADS // hp, nqb),
        in_specs=[
            pl.BlockSpec((kb, width), lambda b, h, i: (b * nqb + i, h)),
            pl.BlockSpec((seq_len, width), lambda b, h, i: (b, h)),
            pl.BlockSpec((seq_len, width), lambda b, h, i: (b, h)),
        ],
        out_specs=pl.BlockSpec((kb, width), lambda b, h, i: (b * nqb + i, h)),
        out_shape=jax.ShapeDtypeStruct((n, d), _BF16),
        scratch_shapes=[pltpu.VMEM((hp, nqb, HEAD_DIM), _F32),
                        pltpu.VMEM((hp, HEAD_DIM, seq_len), _BF16),
                        pltpu.VMEM((hp, 2, kb, kb), _F32)],
        compiler_params=_params("arbitrary", "arbitrary", "arbitrary"),
        name="moba_attention",
    )(q, k, v)


def _router_body(x_ref, g_ref, wr_ref, hn_ref, ids_ref, wts_ref, cnt_ref, run_ref):
    step = pl.program_id(0)
    hn = _rms_normalise(x_ref[...], g_ref[...])
    _store_row_tiles(hn_ref, _pack_bf16_halves(hn))
    w = wr_ref[...]
    hn_hi, w_hi = hn.astype(_BF16), w.astype(_BF16)
    hn_lo = (hn - hn_hi.astype(_F32)).astype(_BF16)
    w_lo = (w - w_hi.astype(_F32)).astype(_BF16)
    logits = (_bf16_dot(hn_hi, w_hi) + (_bf16_dot(hn_hi, w_lo) + _bf16_dot(hn_lo, w_hi))).T
    row = lax.broadcasted_iota(jnp.int32, logits.shape, 0)
    far = jnp.int32(logits.shape[0])

    def first_row(mask):
        return jnp.min(jnp.where(mask, row, far), axis=0, keepdims=True)

    is_group = (row >= N_EXPERTS) & (row < N_EXPERTS + N_GROUPS)
    g_max = jnp.max(jnp.where(is_group, logits, -jnp.inf), axis=0, keepdims=True)
    g_sel = first_row(is_group & (logits == g_max)) - N_EXPERTS
    g_w = 1.0 / jnp.sum(jnp.where(is_group, jnp.exp(logits - g_max), 0.0), axis=0, keepdims=True)

    in_group = (row >= g_sel * EXPERTS_PER_GROUP) & (row < (g_sel + 1) * EXPERTS_PER_GROUP)
    top1 = jnp.max(jnp.where(in_group, logits, -jnp.inf), axis=0, keepdims=True)
    idx1 = first_row(in_group & (logits == top1))
    rest = in_group & (row != idx1)
    top2 = jnp.max(jnp.where(rest, logits, -jnp.inf), axis=0, keepdims=True)
    idx2 = first_row(rest & (logits == top2))
    t = jnp.exp(top2 - top1)
    w1 = 1.0 / (1.0 + t) * g_w
    w2 = t / (1.0 + t) * g_w

    @pl.when(step == 0)
    def _():
        run_ref[...] = jnp.zeros(run_ref.shape, _F32)

    tm = logits.shape[1]
    chosen = jnp.where((row == idx1) | (row == idx2), 1.0, 0.0)
    earlier = (lax.broadcasted_iota(jnp.int32, (tm, tm), 0)
               < lax.broadcasted_iota(jnp.int32, (tm, tm), 1))
    before = run_ref[:, 0:1] + _bf16_dot(chosen.astype(_BF16),
                                         jnp.where(earlier, 1.0, 0.0).astype(_BF16))
    rank1 = jnp.sum(jnp.where(row == idx1, before, 0.0), axis=0, keepdims=True).astype(jnp.int32)
    rank2 = jnp.sum(jnp.where(row == idx2, before, 0.0), axis=0, keepdims=True).astype(jnp.int32)
    run_ref[...] += jnp.sum(chosen, axis=1, keepdims=True)

    out_row = lax.broadcasted_iota(jnp.int32, ids_ref.shape, 0)
    ids_ref[...] = jnp.where(out_row == 0, idx1, jnp.where(out_row == 1, idx2,
                             jnp.where(out_row == 2, rank1, jnp.where(out_row == 3, rank2, 0))))
    wts_ref[...] = jnp.where(row == 0, w1, jnp.where(row == 1, w2, 0.0)).T
    cnt_ref[...] = run_ref[...].astype(jnp.int32)


def _router(x, gain, w_rg, w_re):
    n, d = x.shape
    tm = ROUTER_ROWS
    assert EXPERT_TOPK == 2 and N_EXPERTS + N_GROUPS <= LANES
    chunks = d // 2 // LANES
    wr = jnp.concatenate(
        [w_re, w_rg, jnp.zeros((d, LANES - N_EXPERTS - N_GROUPS), _F32)], axis=1)
    row = lambda i: (i, 0)
    col = lambda i: (0, i)
    fixed = lambda i: (0, 0)
    return pl.pallas_call(
        _router_body,
        grid=(n // tm,),
        in_specs=[
            pl.BlockSpec((tm, d), row),
            pl.BlockSpec((1, d), fixed),
            pl.BlockSpec((d, LANES), fixed),
        ],
        out_specs=[pl.BlockSpec((tm * chunks, LANES), row), pl.BlockSpec((SUBLANES, tm), col),
                   pl.BlockSpec((tm, LANES), row), pl.BlockSpec((LANES, LANES), fixed)],
        out_shape=[jax.ShapeDtypeStruct((n * chunks, LANES), jnp.uint32),
                   jax.ShapeDtypeStruct((SUBLANES, n), jnp.int32),
                   jax.ShapeDtypeStruct((n, LANES), _F32),
                   jax.ShapeDtypeStruct((LANES, LANES), jnp.int32)],
        scratch_shapes=[pltpu.VMEM((LANES, LANES), _F32)],
        compiler_params=_params("arbitrary"),
        name="router",
    )(x, gain.reshape(1, d), wr)


def _expert_body(bounds_ref, row_tok_ref, hn_hbm, wg_ref, wu_ref, wd_ref, y_hbm,
                 xs_ref, ys_ref, zero_ref, wg_bf, wu_bf, wd_bf, gather_sem, out_sem, tail_sem):
    e = pl.program_id(0)
    n_experts = pl.num_programs(0)
    n_slots = xs_ref.shape[0]
    chunks = wg_bf.shape[0] // 2 // LANES
    rows = xs_ref.shape[1] // chunks
    tile_rows = rows * chunks
    ahead = n_slots - 1
    first, stop = bounds_ref[e], bounds_ref[e + 1]
    n_total = bounds_ref[n_experts]

    def row_copy(tok, s, r):
        src = hn_hbm.at[pl.ds(pl.multiple_of(tok * chunks, chunks), chunks)]
        return pltpu.make_async_copy(src, xs_ref.at[s, pl.ds(r * chunks, chunks)], gather_sem.at[s])

    def start_gather(block):
        def body(r, carry):
            row_copy(row_tok_ref[block * rows + r], block % n_slots, r).start(priority=1)
            return carry
        lax.fori_loop(0, rows, body, 0)

    def wait_gather(block):
        s = block % n_slots
        pltpu.make_async_copy(hn_hbm.at[pl.ds(0, rows * chunks)], xs_ref.at[s],
                              gather_sem.at[s]).wait()

    def out_copy(block):
        s = block % n_slots
        dst = y_hbm.at[pl.ds(pl.multiple_of(block * tile_rows, tile_rows), tile_rows)]
        return pltpu.make_async_copy(ys_ref.at[s], dst, out_sem.at[s])

    def tail_copy(block):
        dst = y_hbm.at[pl.ds(pl.multiple_of(block * tile_rows, tile_rows), tile_rows)]
        return pltpu.make_async_copy(zero_ref, dst, tail_sem.at[0])

    def for_tail_blocks(fn):
        def body(block, carry):
            fn(tail_copy(block))
            return carry
        lax.fori_loop(n_total, y_hbm.shape[0] // tile_rows, body, 0)

    @pl.when(e == 0)
    def _():
        for block in range(ahead):
            start_gather(block)
        zero_ref[...] = jnp.zeros(zero_ref.shape, zero_ref.dtype)
        for_tail_blocks(lambda copy: copy.start())

    @pl.when(stop > first)
    def _():
        wg_bf[...] = wg_ref[0, 0].astype(_BF16)
        wu_bf[...] = wu_ref[0, 0].astype(_BF16)
        wd_bf[...] = wd_ref[0, 0].astype(_BF16)

    def block_step(block, carry):
        @pl.when(block >= n_slots)
        def _():
            out_copy(block - n_slots).wait()

        nxt = block + ahead
        nxt_slot = nxt % n_slots
        nxt_base = jnp.minimum(nxt, n_total - 1) * rows
        per_group = rows // EXPERT_ISSUE_GROUPS

        def issue(group):
            for r in range(group * per_group, (group + 1) * per_group):
                row_copy(row_tok_ref[nxt_base + r], nxt_slot, r).start(priority=1)

        wait_gather(block)
        s = block % n_slots
        x_lo, x_hi = _unpack_bf16_halves(_load_row_tiles(xs_ref.at[s], chunks), _BF16)
        half = x_lo.shape[1]
        gate = _bf16_dot(x_lo, wg_bf[:half, :])
        issue(0)
        gate = gate + _bf16_dot(x_hi, wg_bf[half:, :])
        issue(1)
        up = _bf16_dot(x_lo, wu_bf[:half, :])
        issue(2)
        up = up + _bf16_dot(x_hi, wu_bf[half:, :])
        issue(3)
        act = (jax.nn.silu(gate) * up).astype(_BF16)
        quarter = wd_bf.shape[1] // 4
        y = []
        for c in range(4):
            y.append(_bf16_dot(act, wd_bf[:, c * quarter:(c + 1) * quarter]))
            issue(4 + c)
        packed = jnp.concatenate([_pack_bf16_pair(y[0], y[2]), _pack_bf16_pair(y[1], y[3])], axis=1)
        _store_row_tiles(ys_ref.at[s], packed)
        out_copy(block).start()
        return carry

    lax.fori_loop(first, stop, block_step, 0)

    @pl.when(e == n_experts - 1)
    def _():
        for extra in range(ahead):
            wait_gather(n_total + extra)
        for_tail_blocks(lambda copy: copy.wait())
        for back in range(1, n_slots + 1):
            @pl.when(n_total >= back)
            def _():
                out_copy(n_total - back).wait()


def _expert_mlp(bounds, row_tok, hn, layer, w_gate, w_up, w_down):
    n_experts, d, de = w_gate.shape[1:]
    rows = EXPERT_ROWS
    chunks = d // 2 // LANES
    assert hn.shape[1] == LANES and hn.dtype == jnp.uint32
    grid_spec = pltpu.PrefetchScalarGridSpec(
        num_scalar_prefetch=2,
        grid=(n_experts,),
        in_specs=[
            pl.BlockSpec(memory_space=pl.ANY),
            pl.BlockSpec((1, 1, d, de), lambda e, bd, rt: (layer, e, 0, 0)),
            pl.BlockSpec((1, 1, d, de), lambda e, bd, rt: (layer, e, 0, 0)),
            pl.BlockSpec((1, 1, de, d), lambda e, bd, rt: (layer, e, 0, 0)),
        ],
        out_specs=pl.BlockSpec(memory_space=pl.ANY),
        scratch_shapes=[
            pltpu.VMEM((EXPERT_SLOTS, rows * chunks, LANES), jnp.uint32),
            pltpu.VMEM((EXPERT_SLOTS, rows * chunks, LANES), jnp.uint32),
            pltpu.VMEM((rows * chunks, LANES), jnp.uint32),
            pltpu.VMEM((d, de), _BF16),
            pltpu.VMEM((d, de), _BF16),
            pltpu.VMEM((de, d), _BF16),
            pltpu.SemaphoreType.DMA((EXPERT_SLOTS,)),
            pltpu.SemaphoreType.DMA((EXPERT_SLOTS,)),
            pltpu.SemaphoreType.DMA((1,)),
        ],
    )
    return pl.pallas_call(
        _expert_body,
        grid_spec=grid_spec,
        out_shape=jax.ShapeDtypeStruct((row_tok.shape[0] * chunks, LANES), jnp.uint32),
        compiler_params=_params("arbitrary"),
        name="expert_mlp",
    )(bounds, row_tok, hn, w_gate, w_up, w_down)


def _combine_body(pos_ref, y_hbm, wts_ref, h_ref, o_ref, ys_ref, sem):
    i = pl.program_id(0)
    n_steps = pl.num_programs(0)
    tm = h_ref.shape[0]
    chunks = h_ref.shape[1] // 2 // LANES
    n_tokens = n_steps * tm
    slot = i % 2

    def row_copy(p, s, k, r):
        src = y_hbm.at[pl.ds(pl.multiple_of(p * chunks, chunks), chunks)]
        return pltpu.make_async_copy(src, ys_ref.at[s, k, pl.ds(r * chunks, chunks)], sem.at[s])

    def start_gather(tile, s):
        for r in range(tm):
            for k in range(EXPERT_TOPK):
                row_copy(pos_ref[k * n_tokens + tile * tm + r], s, k, r).start(priority=k)

    def wait_gather(s):
        for k in range(EXPERT_TOPK):
            pltpu.make_async_copy(y_hbm.at[pl.ds(0, tm * chunks)], ys_ref.at[s, k], sem.at[s]).wait()

    @pl.when(i == 0)
    def _():
        start_gather(0, 0)

    @pl.when(i + 1 < n_steps)
    def _():
        start_gather(i + 1, 1 - slot)

    wait_gather(slot)
    w = wts_ref[...]
    half = chunks * LANES
    a_lo, a_hi = _unpack_bf16_halves(_load_row_tiles(ys_ref.at[slot, 0], chunks), _F32)
    b_lo, b_hi = _unpack_bf16_halves(_load_row_tiles(ys_ref.at[slot, 1], chunks), _F32)
    o_ref[:, :half] = h_ref[:, :half] + (w[:, 0:1] * a_lo + w[:, 1:2] * b_lo)
    o_ref[:, half:] = h_ref[:, half:] + (w[:, 0:1] * a_hi + w[:, 1:2] * b_hi)


def _combine(pos, y, wts, h):
    n, d = h.shape
    tm = COMBINE_ROWS
    grid_spec = pltpu.PrefetchScalarGridSpec(
        num_scalar_prefetch=1,
        grid=(n // tm,),
        in_specs=[
            pl.BlockSpec(memory_space=pl.ANY),
            pl.BlockSpec((tm, LANES), lambda i, p: (i, 0)),
            pl.BlockSpec((tm, d), lambda i, p: (i, 0)),
        ],
        out_specs=pl.BlockSpec((tm, d), lambda i, p: (i, 0)),
        scratch_shapes=[
            pltpu.VMEM((2, EXPERT_TOPK, tm * (d // 2 // LANES), LANES), jnp.uint32),
            pltpu.SemaphoreType.DMA((2,)),
        ],
    )
    return pl.pallas_call(
        _combine_body,
        grid_spec=grid_spec,
        out_shape=jax.ShapeDtypeStruct((n, d), _F32),
        compiler_params=_params("arbitrary"),
        name="combine",
    )(pos, y, wts, h)


def _dispatch_plan(ids, counts):
    n = ids.shape[1]
    rows = EXPERT_ROWS
    m = n * EXPERT_TOPK
    n_blocks = -(-(m + N_EXPERTS * (rows - 1)) // rows)
    flat_e = ids[:EXPERT_TOPK].reshape(m)
    rank = ids[EXPERT_TOPK:2 * EXPERT_TOPK].reshape(m)
    counts = counts[:N_EXPERTS, 0]
    starts = jnp.cumsum(counts) - counts
    padded = (counts + rows - 1) // rows * rows
    pends = jnp.cumsum(padded)
    pstarts = pends - padded
    pos = rank
    for e in range(N_EXPERTS):
        pos = pos + jnp.where(flat_e == e, pstarts[e], 0)
    blk_start = jnp.arange(n_blocks, dtype=jnp.int32)[:, None] * rows
    in_seg = (pstarts[None, :] <= blk_start) & (blk_start < pends[None, :])
    shift = jnp.sum(jnp.where(in_seg, (starts - pstarts)[None, :], 0), axis=1, keepdims=True)
    seg_end = jnp.sum(jnp.where(in_seg, (starts + counts)[None, :], 0), axis=1, keepdims=True)
    token = jnp.arange(m, dtype=jnp.int32) % n
    order = jnp.argsort(flat_e * n + token)
    sorted_idx = blk_start + shift + jnp.arange(rows, dtype=jnp.int32)[None, :]
    valid = sorted_idx < seg_end
    row_tok = jnp.where(valid, order[jnp.clip(sorted_idx, 0, m - 1)] % n, 0)
    bounds = jnp.concatenate([pstarts, pends[-1:]]) // rows
    return (bounds.astype(jnp.int32), row_tok.reshape(n_blocks * rows).astype(jnp.int32),
            pos.astype(jnp.int32))


def _hier_moe_residual(h, gain, w_rg, w_re, layer, w_gate, w_up, w_down):
    hn, ids, wts, counts = _router(h, gain, w_rg, w_re)
    bounds, row_tok, pos = _dispatch_plan(ids, counts)
    y = _expert_mlp(bounds, row_tok, hn, layer, w_gate, w_up, w_down)
    return _combine(pos, y, wts, h)


def kernel(x, mix_norm, ffn_norm, conv_w_in, conv_w, conv_w_out, kv_norm, w_kv, k_norm, w_q, q_norm,
           w_o, router_group, router_expert, w_gate, w_up, w_down):
    bsz, seq_len, d = x.shape
    depth = mix_norm.shape[0]
    n_conv = conv_w_in.shape[0]
    assert d == N_HEADS * HEAD_DIM
    h = x.reshape(bsz * seq_len, d)
    k_sh = v_sh = None
    for l in range(depth):
        if l < n_conv:
            z = _conv_in(h, mix_norm[l], conv_w_in[l], conv_w[l], seq_len)
            h = _matmul_residual(z, conv_w_out[l], h)
        else:
            j = l - n_conv
            gains = jnp.stack([mix_norm[l], kv_norm])
            head_gains = jnp.stack([q_norm[j], k_norm])
            q, *kv = _qkv_projection(h, gains, w_q[j], w_kv, head_gains, with_kv=(j == 0))
            if j == 0:
                k_sh, v_sh = kv
            attn = _moba_attention(q, k_sh, v_sh, bsz, seq_len)
            h = _matmul_residual(attn, w_o[j], h)
        h = _hier_moe_residual(h, ffn_norm[l], router_group[l], router_expert[l],
                               l, w_gate, w_up, w_down)
    return h.reshape(bsz, seq_len, d)
```

```python
import functools

import jax
import jax.numpy as jnp
from jax import lax
from jax.experimental import pallas as pl
from jax.experimental.pallas import tpu as pltpu

N_HEADS = 16
HEAD_DIM = 128
CONV_WIDTH = 3
MOBA_BLOCK = 256
MOBA_TOPK = 3
N_GROUPS = 4
EXPERTS_PER_GROUP = 8
N_EXPERTS = N_GROUPS * EXPERTS_PER_GROUP
EXPERT_TOPK = 2
NORM_EPS = 1e-6
ALIBI_MAX_BIAS = 8.0

LANES = 128
SUBLANES = 8
VMEM_LIMIT = 56 * 1024 * 1024

ROW_TILE = 1024
RES_COL_TILE = 1024
TRIPLE_COL_TILE = 256
ROUTER_ROWS = 512
EXPERT_ROWS = 128
EXPERT_SLOTS = 6
EXPERT_ISSUE_GROUPS = 8
COMBINE_ROWS = 512
ATTN_HEADS = 8

_BF16 = jnp.bfloat16
_F32 = jnp.float32


def _params(*semantics):
    return pltpu.CompilerParams(dimension_semantics=semantics, vmem_limit_bytes=VMEM_LIMIT)


def _rms_normalise(x, gain):
    ms = jnp.mean(x * x, axis=-1, keepdims=True)
    return x * lax.rsqrt(ms + NORM_EPS) * gain


def _bf16_dot(a, b):
    return jnp.dot(a, b, preferred_element_type=_F32)


def _pack_bf16_halves(x):
    c = x.shape[1] // 2
    return _pack_bf16_pair(x[:, :c], x[:, c:])


def _pack_bf16_pair(lo, hi):
    as_bits = lambda v: lax.bitcast_convert_type(v.astype(_BF16).astype(_F32), jnp.uint32)
    return (as_bits(lo) >> 16) | (as_bits(hi) & jnp.uint32(0xFFFF0000))


def _store_row_tiles(ref, value):
    rows, chunks = value.shape[0], value.shape[1] // LANES
    for c in range(chunks):
        ref[pl.ds(c, rows, stride=chunks), :] = value[:, c * LANES:(c + 1) * LANES]


def _load_row_tiles(ref, chunks):
    rows = ref.shape[0] // chunks
    return jnp.concatenate([ref[pl.ds(c, rows, stride=chunks), :] for c in range(chunks)], axis=1)


def _unpack_bf16_halves(p, dtype):
    lo = lax.bitcast_convert_type(p << 16, _F32).astype(dtype)
    hi = lax.bitcast_convert_type(p & jnp.uint32(0xFFFF0000), _F32).astype(dtype)
    return lo, hi


def _qkv_body(x_ref, g_ref, *refs, with_kv):
    if with_kv:
        wq_ref, wk_ref, wv_ref, hg_ref, q_ref, k_ref, v_ref, xn_ref = refs
    else:
        wq_ref, hg_ref, q_ref, xn_ref = refs

    @pl.when(pl.program_id(1) == 0)
    def _():
        x = x_ref[...]
        normed = x * lax.rsqrt(jnp.mean(x * x, axis=-1, keepdims=True) + NORM_EPS)
        xn_ref[0] = (normed * g_ref[0:1, :]).astype(_BF16)
        if with_kv:
            xn_ref[1] = (normed * g_ref[1:2, :]).astype(_BF16)

    def store_head_normalised(o_ref, y, gain_row):
        head_gain = hg_ref[gain_row:gain_row + 1, :]
        for hh in range(y.shape[1] // HEAD_DIM):
            cols = slice(hh * HEAD_DIM, (hh + 1) * HEAD_DIM)
            o_ref[:, cols] = _rms_normalise(y[:, cols], head_gain).astype(o_ref.dtype)

    products = [_bf16_dot(xn_ref[0], wq_ref[...].astype(_BF16))]
    if with_kv:
        products += [_bf16_dot(xn_ref[1], w_ref[...].astype(_BF16)) for w_ref in (wk_ref, wv_ref)]
    store_head_normalised(q_ref, products[0], 0)
    if with_kv:
        store_head_normalised(k_ref, products[1], 1)
        v_ref[...] = products[2].astype(v_ref.dtype)


def _qkv_projection(x, gains, w_q, w_kv, head_gains, with_kv):
    n, d = x.shape
    tm, tn = ROW_TILE, TRIPLE_COL_TILE
    n_q = w_q.shape[1]
    nj = n_q // tn
    assert n_q % tn == 0 and w_kv.shape[1] == 2 * n_q
    tile = lambda i, j: (i, j)
    fixed = lambda i, j: (0, 0)
    weights = [(w_q, lambda i, j: (0, j))]
    if with_kv:
        weights += [(w_kv, lambda i, j: (0, j)), (w_kv, lambda i, j: (0, j + nj))]
    n_out = len(weights)
    outs = pl.pallas_call(
        functools.partial(_qkv_body, with_kv=with_kv),
        grid=(n // tm, nj),
        in_specs=([pl.BlockSpec((tm, d), lambda i, j: (i, 0)), pl.BlockSpec((2, d), fixed)]
                  + [pl.BlockSpec((d, tn), index_map) for _, index_map in weights]
                  + [pl.BlockSpec((2, HEAD_DIM), fixed)]),
        out_specs=[pl.BlockSpec((tm, tn), tile)] * n_out,
        out_shape=[jax.ShapeDtypeStruct((n, n_q), _BF16)] * n_out,
        scratch_shapes=[pltpu.VMEM((2, tm, d), _BF16)],
        compiler_params=_params("arbitrary", "arbitrary"),
        name="qkv_projection",
    )(x, gains, *[w for w, _ in weights], head_gains)
    return tuple(outs)


def _conv_in_body(x_ref, g_ref, wb_ref, wc_ref, wx_ref, cw_ref, z_ref, xn_ref, carry_ref,
                  *, tiles_per_seq):
    i = pl.program_id(0)
    j = pl.program_id(1)
    tm = x_ref.shape[0]

    @pl.when(j == 0)
    def _():
        xn_ref[...] = _rms_normalise(x_ref[...], g_ref[...]).astype(_BF16)

    @pl.when(i % tiles_per_seq == 0)
    def _():
        carry_ref[j] = jnp.zeros(carry_ref.shape[1:], _F32)

    xn = xn_ref[...]
    b_gate = _bf16_dot(xn, wb_ref[...].astype(_BF16))
    c_gate = _bf16_dot(xn, wc_ref[...].astype(_BF16))
    xh = _bf16_dot(xn, wx_ref[...].astype(_BF16))
    u = c_gate * xh
    prev = carry_ref[j]
    row = lax.broadcasted_iota(jnp.int32, u.shape, 0)
    last = prev[SUBLANES - 1:SUBLANES]
    u1 = jnp.where(row == 0, last, pltpu.roll(u, 1, axis=0))
    u2 = jnp.where(row == 0, prev[SUBLANES - 2:SUBLANES - 1],
                   jnp.where(row == 1, last, pltpu.roll(u, 2, axis=0)))
    cw = cw_ref[...]
    conv = cw[0:1] * u2 + cw[1:2] * u1 + cw[2:3] * u
    z_ref[...] = (b_gate * conv).astype(z_ref.dtype)
    carry_ref[j] = u[tm - SUBLANES:tm]


def _conv_in(x, gain, w_in, conv_w, seq_len):
    n, d = x.shape
    tm, tn = ROW_TILE, TRIPLE_COL_TILE
    nj = d // tn
    assert CONV_WIDTH - 1 <= SUBLANES and seq_len % tm == 0
    return pl.pallas_call(
        functools.partial(_conv_in_body, tiles_per_seq=seq_len // tm),
        grid=(n // tm, nj),
        in_specs=[
            pl.BlockSpec((tm, d), lambda i, j: (i, 0)),
            pl.BlockSpec((1, d), lambda i, j: (0, 0)),
            pl.BlockSpec((d, tn), lambda i, j: (0, j)),
            pl.BlockSpec((d, tn), lambda i, j: (0, j + nj)),
            pl.BlockSpec((d, tn), lambda i, j: (0, j + 2 * nj)),
            pl.BlockSpec((CONV_WIDTH, tn), lambda i, j: (0, j)),
        ],
        out_specs=pl.BlockSpec((tm, tn), lambda i, j: (i, j)),
        out_shape=jax.ShapeDtypeStruct((n, d), _BF16),
        scratch_shapes=[pltpu.VMEM((tm, d), _BF16), pltpu.VMEM((nj, SUBLANES, tn), _F32)],
        compiler_params=_params("arbitrary", "arbitrary"),
        name="conv_in",
    )(x, gain.reshape(1, d), w_in, w_in, w_in, conv_w)


def _matmul_residual_body(x_ref, w_ref, r_ref, o_ref):
    o_ref[...] = r_ref[...] + _bf16_dot(x_ref[...], w_ref[...].astype(_BF16))


def _matmul_residual(x, w, res):
    n, k = x.shape
    d = w.shape[1]
    tm, tn = ROW_TILE, RES_COL_TILE
    return pl.pallas_call(
        _matmul_residual_body,
        grid=(n // tm, d // tn),
        in_specs=[
            pl.BlockSpec((tm, k), lambda i, j: (i, 0)),
            pl.BlockSpec((k, tn), lambda i, j: (0, j)),
            pl.BlockSpec((tm, tn), lambda i, j: (i, j)),
        ],
        out_specs=pl.BlockSpec((tm, tn), lambda i, j: (i, j)),
        out_shape=jax.ShapeDtypeStruct((n, d), _F32),
        compiler_params=_params("arbitrary", "arbitrary"),
        name="matmul_residual",
    )(x, w, res)


def _moba_body(q_ref, k_ref, v_ref, o_ref, km_ref, vt_ref, bias_ref):
    heads = range(q_ref.shape[1] // HEAD_DIM)
    head0 = pl.program_id(1) * len(heads)
    i = pl.program_id(2)
    kb = MOBA_BLOCK
    n_blocks = k_ref.shape[0] // kb
    nt = (((1,), (1,)), ((), ()))
    log2e = 1.4426950408889634
    scale = HEAD_DIM ** -0.5 * log2e
    slopes = [log2e * jnp.exp2(jnp.full((1, kb), -ALIBI_MAX_BIAS / N_HEADS, _F32)
                               * (head0 + hh + 1).astype(_F32)) for hh in heads]

    def lanes(hh):
        return slice(hh * HEAD_DIM, (hh + 1) * HEAD_DIM)

    @pl.when(i == 0)
    def _():
        rel = (lax.broadcasted_iota(jnp.int32, (kb, kb), 1)
               - lax.broadcasted_iota(jnp.int32, (kb, kb), 0))
        for hh in heads:
            for jb in range(n_blocks):
                rows = slice(jb * kb, (jb + 1) * kb)
                km_ref[hh, jb:jb + 1, :] = jnp.mean(k_ref[rows, lanes(hh)].astype(_F32),
                                                    axis=0, keepdims=True)
                vt_ref[hh, :, rows] = v_ref[rows, lanes(hh)].astype(_F32).T.astype(_BF16)
            bias = slopes[hh] * rel.astype(_F32)
            bias_ref[hh, 0] = bias
            bias_ref[hh, 1] = jnp.where(rel >= 0, bias, jnp.inf)

    q = [q_ref[:, lanes(hh)] for hh in heads]
    sel = []
    for hh in heads:
        gate = lax.dot_general(km_ref[hh], q[hh].astype(_F32), nt,
                               precision=lax.Precision.HIGHEST, preferred_element_type=_F32)
        blk = lax.broadcasted_iota(jnp.int32, gate.shape, 0)
        gate = jnp.where(blk < i, gate, -jnp.inf)
        picked = jnp.zeros(gate.shape, _F32)
        for jb in range(n_blocks - 1):
            g_jb = gate[jb:jb + 1, :]
            beats = (gate > g_jb) | ((gate == g_jb) & (blk < jb))
            n_beat = jnp.sum(beats.astype(_F32), axis=0, keepdims=True)
            picked = jnp.where((blk == jb) & (blk < i) & (n_beat < MOBA_TOPK), 1.0, picked)
        sel.append(picked)

    def attend(own):
        nb = own + 1
        raw = [lax.dot_general(k_ref[0:nb * kb, lanes(hh)], q[hh], nt, preferred_element_type=_F32)
               for hh in heads]
        tiles, col_max = [[] for _ in heads], [[] for _ in heads]
        for jb in range(nb):
            for hh in heads:
                tile = raw[hh][jb * kb:(jb + 1) * kb, :] * scale - bias_ref[hh, 1 if jb == own else 0]
                mx = jnp.max(tile, axis=0, keepdims=True)
                if jb != own:
                    mx = jnp.where(sel[hh][jb:jb + 1, :] > 0.5,
                                   mx - slopes[hh] * float((own - jb) * kb), -jnp.inf)
                tiles[hh].append(tile), col_max[hh].append(mx)
        m = [functools.reduce(jnp.maximum, col_max[hh]) for hh in heads]
        probs, l = [[] for _ in heads], [None for _ in heads]
        for jb in range(nb):
            for hh in heads:
                offset = m[hh] if jb == own else jnp.where(
                    sel[hh][jb:jb + 1, :] > 0.5, m[hh] + slopes[hh] * float((own - jb) * kb), jnp.inf)
                p = jnp.exp2(tiles[hh][jb] - offset)
                p_sum = jnp.sum(p, axis=0, keepdims=True)
                l[hh] = p_sum if l[hh] is None else l[hh] + p_sum
                probs[hh].append(p.astype(_BF16))
        for hh in heads:
            p_all = probs[hh][0] if nb == 1 else jnp.concatenate(probs[hh], axis=0)
            acc = _bf16_dot(vt_ref[hh, :, 0:nb * kb], p_all)
            o_ref[:, lanes(hh)] = (acc / l[hh]).T.astype(o_ref.dtype)

    for own in range(n_blocks):
        pl.when(i == own)(functools.partial(attend, own))


def _moba_attention(q, k, v, batch, seq_len):
    n, d = q.shape
    kb = MOBA_BLOCK
    nqb = seq_len // kb
    hp = ATTN_HEADS
    width = hp * HEAD_DIM
    assert seq_len % kb == 0 and nqb == SUBLANES and N_HEADS % hp == 0 and d == N_HEADS * HEAD_DIM
    return pl.pallas_call(
        _moba_body,
        grid=(batch, N_HEADS // hp, nqb),
        in_specs=[
            pl.BlockSpec((kb, width), lambda b, h, i: (b * nqb + i, h)),
            pl.BlockSpec((seq_len, width), lambda b, h, i: (b, h)),
            pl.BlockSpec((seq_len, width), lambda b, h, i: (b, h)),
        ],
        out_specs=pl.BlockSpec((kb, width), lambda b, h, i: (b * nqb + i, h)),
        out_shape=jax.ShapeDtypeStruct((n, d), _BF16),
        scratch_shapes=[pltpu.VMEM((hp, nqb, HEAD_DIM), _F32),
                        pltpu.VMEM((hp, HEAD_DIM, seq_len), _BF16),
                        pltpu.VMEM((hp, 2, kb, kb), _F32)],
        compiler_params=_params("arbitrary", "arbitrary", "arbitrary"),
        name="moba_attention",
    )(q, k, v)


def _router_body(x_ref, g_ref, wr_ref, hn_ref, ids_ref, wts_ref, cnt_ref, run_ref):
    step = pl.program_id(0)
    hn = _rms_normalise(x_ref[...], g_ref[...])
    _store_row_tiles(hn_ref, _pack_bf16_halves(hn))
    w = wr_ref[...]
    hn_hi, w_hi = hn.astype(_BF16), w.astype(_BF16)
    hn_lo = (hn - hn_hi.astype(_F32)).astype(_BF16)
    w_lo = (w - w_hi.astype(_F32)).astype(_BF16)
    logits = (_bf16_dot(hn_hi, w_hi) + (_bf16_dot(hn_hi, w_lo) + _bf16_dot(hn_lo, w_hi))).T
    row = lax.broadcasted_iota(jnp.int32, logits.shape, 0)
    far = jnp.int32(logits.shape[0])

    def first_row(mask):
        return jnp.min(jnp.where(mask, row, far), axis=0, keepdims=True)

    is_group = (row >= N_EXPERTS) & (row < N_EXPERTS + N_GROUPS)
    g_max = jnp.max(jnp.where(is_group, logits, -jnp.inf), axis=0, keepdims=True)
    g_sel = first_row(is_group & (logits == g_max)) - N_EXPERTS
    g_w = 1.0 / jnp.sum(jnp.where(is_group, jnp.exp(logits - g_max), 0.0), axis=0, keepdims=True)

    in_group = (row >= g_sel * EXPERTS_PER_GROUP) & (row < (g_sel + 1) * EXPERTS_PER_GROUP)
    top1 = jnp.max(jnp.where(in_group, logits, -jnp.inf), axis=0, keepdims=True)
    idx1 = first_row(in_group & (logits == top1))
    rest = in_group & (row != idx1)
    top2 = jnp.max(jnp.where(rest, logits, -jnp.inf), axis=0, keepdims=True)
    idx2 = first_row(rest & (logits == top2))
    t = jnp.exp(top2 - top1)
    w1 = 1.0 / (1.0 + t) * g_w
    w2 = t / (1.0 + t) * g_w

    @pl.when(step == 0)
    def _():
        run_ref[...] = jnp.zeros(run_ref.shape, _F32)

    tm = logits.shape[1]
    chosen = jnp.where((row == idx1) | (row == idx2), 1.0, 0.0)
    earlier = (lax.broadcasted_iota(jnp.int32, (tm, tm), 0)
               < lax.broadcasted_iota(jnp.int32, (tm, tm), 1))
    before = run_ref[:, 0:1] + _bf16_dot(chosen.astype(_BF16),
                                         jnp.where(earlier, 1.0, 0.0).astype(_BF16))
    rank1 = jnp.sum(jnp.where(row == idx1, before, 0.0), axis=0, keepdims=True).astype(jnp.int32)
    rank2 = jnp.sum(jnp.where(row == idx2, before, 0.0), axis=0, keepdims=True).astype(jnp.int32)
    run_ref[...] += jnp.sum(chosen, axis=1, keepdims=True)

    out_row = lax.broadcasted_iota(jnp.int32, ids_ref.shape, 0)
    ids_ref[...] = jnp.where(out_row == 0, idx1, jnp.where(out_row == 1, idx2,
                             jnp.where(out_row == 2, rank1, jnp.where(out_row == 3, rank2, 0))))
    wts_ref[...] = jnp.where(row == 0, w1, jnp.where(row == 1, w2, 0.0)).T
    cnt_ref[...] = run_ref[...].astype(jnp.int32)


def _router(x, gain, w_rg, w_re):
    n, d = x.shape
    tm = ROUTER_ROWS
    assert EXPERT_TOPK == 2 and N_EXPERTS + N_GROUPS <= LANES
    chunks = d // 2 // LANES
    wr = jnp.concatenate(
        [w_re, w_rg, jnp.zeros((d, LANES - N_EXPERTS - N_GROUPS), _F32)], axis=1)
    row = lambda i: (i, 0)
    col = lambda i: (0, i)
    fixed = lambda i: (0, 0)
    return pl.pallas_call(
        _router_body,
        grid=(n // tm,),
        in_specs=[
            pl.BlockSpec((tm, d), row),
            pl.BlockSpec((1, d), fixed),
            pl.BlockSpec((d, LANES), fixed),
        ],
        out_specs=[pl.BlockSpec((tm * chunks, LANES), row), pl.BlockSpec((SUBLANES, tm), col),
                   pl.BlockSpec((tm, LANES), row), pl.BlockSpec((LANES, LANES), fixed)],
        out_shape=[jax.ShapeDtypeStruct((n * chunks, LANES), jnp.uint32),
                   jax.ShapeDtypeStruct((SUBLANES, n), jnp.int32),
                   jax.ShapeDtypeStruct((n, LANES), _F32),
                   jax.ShapeDtypeStruct((LANES, LANES), jnp.int32)],
        scratch_shapes=[pltpu.VMEM((LANES, LANES), _F32)],
        compiler_params=_params("arbitrary"),
        name="router",
    )(x, gain.reshape(1, d), wr)


def _expert_body(bounds_ref, row_tok_ref, hn_hbm, wg_ref, wu_ref, wd_ref, y_hbm,
                 xs_ref, ys_ref, zero_ref, wg_bf, wu_bf, wd_bf, gather_sem, out_sem, tail_sem):
    e = pl.program_id(0)
    n_experts = pl.num_programs(0)
    n_slots = xs_ref.shape[0]
    chunks = wg_bf.shape[0] // 2 // LANES
    rows = xs_ref.shape[1] // chunks
    tile_rows = rows * chunks
    ahead = n_slots - 1
    first, stop = bounds_ref[e], bounds_ref[e + 1]
    n_total = bounds_ref[n_experts]

    def row_copy(tok, s, r):
        src = hn_hbm.at[pl.ds(pl.multiple_of(tok * chunks, chunks), chunks)]
        return pltpu.make_async_copy(src, xs_ref.at[s, pl.ds(r * chunks, chunks)], gather_sem.at[s])

    def start_gather(block):
        def body(r, carry):
            row_copy(row_tok_ref[block * rows + r], block % n_slots, r).start(priority=1)
            return carry
        lax.fori_loop(0, rows, body, 0)

    def wait_gather(block):
        s = block % n_slots
        pltpu.make_async_copy(hn_hbm.at[pl.ds(0, rows * chunks)], xs_ref.at[s],
                              gather_sem.at[s]).wait()

    def out_copy(block):
        s = block % n_slots
        dst = y_hbm.at[pl.ds(pl.multiple_of(block * tile_rows, tile_rows), tile_rows)]
        return pltpu.make_async_copy(ys_ref.at[s], dst, out_sem.at[s])

    def tail_copy(block):
        dst = y_hbm.at[pl.ds(pl.multiple_of(block * tile_rows, tile_rows), tile_rows)]
        return pltpu.make_async_copy(zero_ref, dst, tail_sem.at[0])

    def for_tail_blocks(fn):
        def body(block, carry):
            fn(tail_copy(block))
            return carry
        lax.fori_loop(n_total, y_hbm.shape[0] // tile_rows, body, 0)

    @pl.when(e == 0)
    def _():
        for block in range(ahead):
            start_gather(block)
        zero_ref[...] = jnp.zeros(zero_ref.shape, zero_ref.dtype)
        for_tail_blocks(lambda copy: copy.start())

    @pl.when(stop > first)
    def _():
        wg_bf[...] = wg_ref[0, 0].astype(_BF16)
        wu_bf[...] = wu_ref[0, 0].astype(_BF16)
        wd_bf[...] = wd_ref[0, 0].astype(_BF16)

    def block_step(block, carry):
        @pl.when(block >= n_slots)
        def _():
            out_copy(block - n_slots).wait()

        nxt = block + ahead
        nxt_slot = nxt % n_slots
        nxt_base = jnp.minimum(nxt, n_total - 1) * rows
        per_group = rows // EXPERT_ISSUE_GROUPS

        def issue(group):
            for r in range(group * per_group, (group + 1) * per_group):
                row_copy(row_tok_ref[nxt_base + r], nxt_slot, r).start(priority=1)

        wait_gather(block)
        s = block % n_slots
        x_lo, x_hi = _unpack_bf16_halves(_load_row_tiles(xs_ref.at[s], chunks), _BF16)
        half = x_lo.shape[1]
        gate = _bf16_dot(x_lo, wg_bf[:half, :])
        issue(0)
        gate = gate + _bf16_dot(x_hi, wg_bf[half:, :])
        issue(1)
        up = _bf16_dot(x_lo, wu_bf[:half, :])
        issue(2)
        up = up + _bf16_dot(x_hi, wu_bf[half:, :])
        issue(3)
        act = (jax.nn.silu(gate) * up).astype(_BF16)
        quarter = wd_bf.shape[1] // 4
        y = []
        for c in range(4):
            y.append(_bf16_dot(act, wd_bf[:, c * quarter:(c + 1) * quarter]))
            issue(4 + c)
        packed = jnp.concatenate([_pack_bf16_pair(y[0], y[2]), _pack_bf16_pair(y[1], y[3])], axis=1)
        _store_row_tiles(ys_ref.at[s], packed)
        out_copy(block).start()
        return carry

    lax.fori_loop(first, stop, block_step, 0)

    @pl.when(e == n_experts - 1)
    def _():
        for extra in range(ahead):
            wait_gather(n_total + extra)
        for_tail_blocks(lambda copy: copy.wait())
        for back in range(1, n_slots + 1):
            @pl.when(n_total >= back)
            def _():
                out_copy(n_total - back).wait()


def _expert_mlp(bounds, row_tok, hn, layer, w_gate, w_up, w_down):
    n_experts, d, de = w_gate.shape[1:]
    rows = EXPERT_ROWS
    chunks = d // 2 // LANES
    assert hn.shape[1] == LANES and hn.dtype == jnp.uint32
    grid_spec = pltpu.PrefetchScalarGridSpec(
        num_scalar_prefetch=2,
        grid=(n_experts,),
        in_specs=[
            pl.BlockSpec(memory_space=pl.ANY),
            pl.BlockSpec((1, 1, d, de), lambda e, bd, rt: (layer, e, 0, 0)),
            pl.BlockSpec((1, 1, d, de), lambda e, bd, rt: (layer, e, 0, 0)),
            pl.BlockSpec((1, 1, de, d), lambda e, bd, rt: (layer, e, 0, 0)),
        ],
        out_specs=pl.BlockSpec(memory_space=pl.ANY),
        scratch_shapes=[
            pltpu.VMEM((EXPERT_SLOTS, rows * chunks, LANES), jnp.uint32),
            pltpu.VMEM((EXPERT_SLOTS, rows * chunks, LANES), jnp.uint32),
            pltpu.VMEM((rows * chunks, LANES), jnp.uint32),
            pltpu.VMEM((d, de), _BF16),
            pltpu.VMEM((d, de), _BF16),
            pltpu.VMEM((de, d), _BF16),
            pltpu.SemaphoreType.DMA((EXPERT_SLOTS,)),
            pltpu.SemaphoreType.DMA((EXPERT_SLOTS,)),
            pltpu.SemaphoreType.DMA((1,)),
        ],
    )
    return pl.pallas_call(
        _expert_body,
        grid_spec=grid_spec,
        out_shape=jax.ShapeDtypeStruct((row_tok.shape[0] * chunks, LANES), jnp.uint32),
        compiler_params=_params("arbitrary"),
        name="expert_mlp",
    )(bounds, row_tok, hn, w_gate, w_up, w_down)


def _combine_body(pos_ref, y_hbm, wts_ref, h_ref, o_ref, ys_ref, sem):
    i = pl.program_id(0)
    n_steps = pl.num_programs(0)
    tm = h_ref.shape[0]
    chunks = h_ref.shape[1] // 2 // LANES
    n_tokens = n_steps * tm
    slot = i % 2

    def row_copy(p, s, k, r):
        src = y_hbm.at[pl.ds(pl.multiple_of(p * chunks, chunks), chunks)]
        return pltpu.make_async_copy(src, ys_ref.at[s, k, pl.ds(r * chunks, chunks)], sem.at[s])

    def start_gather(tile, s):
        for r in range(tm):
            for k in range(EXPERT_TOPK):
                row_copy(pos_ref[k * n_tokens + tile * tm + r], s, k, r).start(priority=k)

    def wait_gather(s):
        for k in range(EXPERT_TOPK):
            pltpu.make_async_copy(y_hbm.at[pl.ds(0, tm * chunks)], ys_ref.at[s, k], sem.at[s]).wait()

    @pl.when(i == 0)
    def _():
        start_gather(0, 0)

    @pl.when(i + 1 < n_steps)
    def _():
        start_gather(i + 1, 1 - slot)

    wait_gather(slot)
    w = wts_ref[...]
    half = chunks * LANES
    a_lo, a_hi = _unpack_bf16_halves(_load_row_tiles(ys_ref.at[slot, 0], chunks), _F32)
    b_lo, b_hi = _unpack_bf16_halves(_load_row_tiles(ys_ref.at[slot, 1], chunks), _F32)
    o_ref[:, :half] = h_ref[:, :half] + (w[:, 0:1] * a_lo + w[:, 1:2] * b_lo)
    o_ref[:, half:] = h_ref[:, half:] + (w[:, 0:1] * a_hi + w[:, 1:2] * b_hi)


def _combine(pos, y, wts, h):
    n, d = h.shape
    tm = COMBINE_ROWS
    grid_spec = pltpu.PrefetchScalarGridSpec(
        num_scalar_prefetch=1,
        grid=(n // tm,),
        in_specs=[
            pl.BlockSpec(memory_space=pl.ANY),
            pl.BlockSpec((tm, LANES), lambda i, p: (i, 0)),
            pl.BlockSpec((tm, d), lambda i, p: (i, 0)),
        ],
        out_specs=pl.BlockSpec((tm, d), lambda i, p: (i, 0)),
        scratch_shapes=[
            pltpu.VMEM((2, EXPERT_TOPK, tm * (d // 2 // LANES), LANES), jnp.uint32),
            pltpu.SemaphoreType.DMA((2,)),
        ],
    )
    return pl.pallas_call(
        _combine_body,
        grid_spec=grid_spec,
        out_shape=jax.ShapeDtypeStruct((n, d), _F32),
        compiler_params=_params("arbitrary"),
        name="combine",
    )(pos, y, wts, h)


def _dispatch_plan(ids, counts):
    n = ids.shape[1]
    rows = EXPERT_ROWS
    m = n * EXPERT_TOPK
    n_blocks = -(-(m + N_EXPERTS * (rows - 1)) // rows)
    flat_e = ids[:EXPERT_TOPK].reshape(m)
    rank = ids[EXPERT_TOPK:2 * EXPERT_TOPK].reshape(m)
    counts = counts[:N_EXPERTS, 0]
    starts = jnp.cumsum(counts) - counts
    padded = (counts + rows - 1) // rows * rows
    pends = jnp.cumsum(padded)
    pstarts = pends - padded
    pos = rank
    for e in range(N_EXPERTS):
        pos = pos + jnp.where(flat_e == e, pstarts[e], 0)
    blk_start = jnp.arange(n_blocks, dtype=jnp.int32)[:, None] * rows
    in_seg = (pstarts[None, :] <= blk_start) & (blk_start < pends[None, :])
    shift = jnp.sum(jnp.where(in_seg, (starts - pstarts)[None, :], 0), axis=1, keepdims=True)
    seg_end = jnp.sum(jnp.where(in_seg, (starts + counts)[None, :], 0), axis=1, keepdims=True)
    token = jnp.arange(m, dtype=jnp.int32) % n
    order = jnp.argsort(flat_e * n + token)
    sorted_idx = blk_start + shift + jnp.arange(rows, dtype=jnp.int32)[None, :]
    valid = sorted_idx < seg_end
    row_tok = jnp.where(valid, order[jnp.clip(sorted_idx, 0, m - 1)] % n, 0)
    bounds = jnp.concatenate([pstarts, pends[-1:]]) // rows
    return (bounds.astype(jnp.int32), row_tok.reshape(n_blocks * rows).astype(jnp.int32),
            pos.astype(jnp.int32))


def _hier_moe_residual(h, gain, w_rg, w_re, layer, w_gate, w_up, w_down):
    hn, ids, wts, counts = _router(h, gain, w_rg, w_re)
    bounds, row_tok, pos = _dispatch_plan(ids, counts)
    y = _expert_mlp(bounds, row_tok, hn, layer, w_gate, w_up, w_down)
    return _combine(pos, y, wts, h)


def kernel(x, mix_norm, ffn_norm, conv_w_in, conv_w, conv_w_out, kv_norm, w_kv, k_norm, w_q, q_norm,
           w_o, router_group, router_expert, w_gate, w_up, w_down):
    bsz, seq_len, d = x.shape
    depth = mix_norm.shape[0]
    n_conv = conv_w_in.shape[0]
    assert d == N_HEADS * HEAD_DIM
    h = x.reshape(bsz * seq_len, d)
    k_sh = v_sh = None
    for l in range(depth):
        if l < n_conv:
            z = _conv_in(h, mix_norm[l], conv_w_in[l], conv_w[l], seq_len)
            h = _matmul_residual(z, conv_w_out[l], h)
        else:
            j = l - n_conv
            gains = jnp.stack([mix_norm[l], kv_norm])
            head_gains = jnp.stack([q_norm[j], k_norm])
            q, *kv = _qkv_projection(h, gains, w_q[j], w_kv, head_gains, with_kv=(j == 0))
            if j == 0:
                k_sh, v_sh = kv
            attn = _moba_attention(q, k_sh, v_sh, bsz, seq_len)
            h = _matmul_residual(attn, w_o[j], h)
        h = _hier_moe_residual(h, ffn_norm[l], router_group[l], router_expert[l],
                               l, w_gate, w_up, w_down)
    return h.reshape(bsz, seq_len, d)
```

```python
import functools

import jax
import jax.numpy as jnp
from jax import lax
from jax.experimental import pallas as pl
from jax.experimental.pallas import tpu as pltpu

N_HEADS = 16
HEAD_DIM = 128
CONV_WIDTH = 3
MOBA_BLOCK = 256
MOBA_TOPK = 3
N_GROUPS = 4
EXPERTS_PER_GROUP = 8
N_EXPERTS = N_GROUPS * EXPERTS_PER_GROUP
EXPERT_TOPK = 2
NORM_EPS = 1e-6
ALIBI_MAX_BIAS = 8.0

LANES = 128
SUBLANES = 8
VMEM_LIMIT = 56 * 1024 * 1024

ROW_TILE = 1024
RES_COL_TILE = 1024
TRIPLE_COL_TILE = 256
ROUTER_ROWS = 512
EXPERT_ROWS = 128
EXPERT_SLOTS = 12
EXPERT_ISSUE_GROUPS = 8
COMBINE_ROWS = 256
COMBINE_SLOTS = 4
ATTN_HEADS = 8

_BF16 = jnp.bfloat16
_F32 = jnp.float32


def _params(*semantics):
    return pltpu.CompilerParams(dimension_semantics=semantics, vmem_limit_bytes=VMEM_LIMIT)


def _rms_normalise(x, gain):
    ms = jnp.mean(x * x, axis=-1, keepdims=True)
    return x * lax.rsqrt(ms + NORM_EPS) * gain


def _bf16_dot(a, b):
    return jnp.dot(a, b, preferred_element_type=_F32)


def _pack_bf16_halves(x):
    c = x.shape[1] // 2
    return _pack_bf16_pair(x[:, :c], x[:, c:])


def _pack_bf16_pair(lo, hi):
    as_bits = lambda v: lax.bitcast_convert_type(v.astype(_BF16).astype(_F32), jnp.uint32)
    return (as_bits(lo) >> 16) | (as_bits(hi) & jnp.uint32(0xFFFF0000))


def _store_row_tiles(ref, value):
    rows, chunks = value.shape[0], value.shape[1] // LANES
    for c in range(chunks):
        ref[pl.ds(c, rows, stride=chunks), :] = value[:, c * LANES:(c + 1) * LANES]


def _load_row_tiles(ref, chunks):
    rows = ref.shape[0] // chunks
    return jnp.concatenate([ref[pl.ds(c, rows, stride=chunks), :] for c in range(chunks)], axis=1)


def _unpack_bf16_halves(p, dtype):
    lo = lax.bitcast_convert_type(p << 16, _F32).astype(dtype)
    hi = lax.bitcast_convert_type(p & jnp.uint32(0xFFFF0000), _F32).astype(dtype)
    return lo, hi


def _qkv_body(x_ref, g_ref, *refs, with_kv):
    if with_kv:
        wq_ref, wk_ref, wv_ref, hg_ref, q_ref, k_ref, v_ref, xn_ref = refs
    else:
        wq_ref, hg_ref, q_ref, xn_ref = refs

    @pl.when(pl.program_id(1) == 0)
    def _():
        x = x_ref[...]
        normed = x * lax.rsqrt(jnp.mean(x * x, axis=-1, keepdims=True) + NORM_EPS)
        xn_ref[0] = (normed * g_ref[0:1, :]).astype(_BF16)
        if with_kv:
            xn_ref[1] = (normed * g_ref[1:2, :]).astype(_BF16)

    def store_head_normalised(o_ref, y, gain_row):
        head_gain = hg_ref[gain_row:gain_row + 1, :]
        for hh in range(y.shape[1] // HEAD_DIM):
            cols = slice(hh * HEAD_DIM, (hh + 1) * HEAD_DIM)
            o_ref[:, cols] = _rms_normalise(y[:, cols], head_gain).astype(o_ref.dtype)

    products = [_bf16_dot(xn_ref[0], wq_ref[...].astype(_BF16))]
    if with_kv:
        products += [_bf16_dot(xn_ref[1], w_ref[...].astype(_BF16)) for w_ref in (wk_ref, wv_ref)]
    store_head_normalised(q_ref, products[0], 0)
    if with_kv:
        store_head_normalised(k_ref, products[1], 1)
        v_ref[...] = products[2].astype(v_ref.dtype)


def _qkv_projection(x, gains, w_q, w_kv, head_gains, with_kv):
    n, d = x.shape
    tm, tn = ROW_TILE, TRIPLE_COL_TILE
    n_q = w_q.shape[1]
    nj = n_q // tn
    assert n_q % tn == 0 and w_kv.shape[1] == 2 * n_q
    tile = lambda i, j: (i, j)
    fixed = lambda i, j: (0, 0)
    weights = [(w_q, lambda i, j: (0, j))]
    if with_kv:
        weights += [(w_kv, lambda i, j: (0, j)), (w_kv, lambda i, j: (0, j + nj))]
    n_out = len(weights)
    outs = pl.pallas_call(
        functools.partial(_qkv_body, with_kv=with_kv),
        grid=(n // tm, nj),
        in_specs=([pl.BlockSpec((tm, d), lambda i, j: (i, 0)), pl.BlockSpec((2, d), fixed)]
                  + [pl.BlockSpec((d, tn), index_map) for _, index_map in weights]
                  + [pl.BlockSpec((2, HEAD_DIM), fixed)]),
        out_specs=[pl.BlockSpec((tm, tn), tile)] * n_out,
        out_shape=[jax.ShapeDtypeStruct((n, n_q), _BF16)] * n_out,
        scratch_shapes=[pltpu.VMEM((2, tm, d), _BF16)],
        compiler_params=_params("arbitrary", "arbitrary"),
        name="qkv_projection",
    )(x, gains, *[w for w, _ in weights], head_gains)
    return tuple(outs)


def _conv_in_body(x_ref, g_ref, wb_ref, wc_ref, wx_ref, cw_ref, z_ref, xn_ref, carry_ref,
                  *, tiles_per_seq):
    i = pl.program_id(0)
    j = pl.program_id(1)
    tm = x_ref.shape[0]

    @pl.when(j == 0)
    def _():
        xn_ref[...] = _rms_normalise(x_ref[...], g_ref[...]).astype(_BF16)

    @pl.when(i % tiles_per_seq == 0)
    def _():
        carry_ref[j] = jnp.zeros(carry_ref.shape[1:], _F32)

    xn = xn_ref[...]
    b_gate = _bf16_dot(xn, wb_ref[...].astype(_BF16))
    c_gate = _bf16_dot(xn, wc_ref[...].astype(_BF16))
    xh = _bf16_dot(xn, wx_ref[...].astype(_BF16))
    u = c_gate * xh
    prev = carry_ref[j]
    row = lax.broadcasted_iota(jnp.int32, u.shape, 0)
    last = prev[SUBLANES - 1:SUBLANES]
    u1 = jnp.where(row == 0, last, pltpu.roll(u, 1, axis=0))
    u2 = jnp.where(row == 0, prev[SUBLANES - 2:SUBLANES - 1],
                   jnp.where(row == 1, last, pltpu.roll(u, 2, axis=0)))
    cw = cw_ref[...]
    conv = cw[0:1] * u2 + cw[1:2] * u1 + cw[2:3] * u
    z_ref[...] = (b_gate * conv).astype(z_ref.dtype)
    carry_ref[j] = u[tm - SUBLANES:tm]


def _conv_in(x, gain, w_in, conv_w, seq_len):
    n, d = x.shape
    tm, tn = ROW_TILE, TRIPLE_COL_TILE
    nj = d // tn
    assert CONV_WIDTH - 1 <= SUBLANES and seq_len % tm == 0
    return pl.pallas_call(
        functools.partial(_conv_in_body, tiles_per_seq=seq_len // tm),
        grid=(n // tm, nj),
        in_specs=[
            pl.BlockSpec((tm, d), lambda i, j: (i, 0)),
            pl.BlockSpec((1, d), lambda i, j: (0, 0)),
            pl.BlockSpec((d, tn), lambda i, j: (0, j)),
            pl.BlockSpec((d, tn), lambda i, j: (0, j + nj)),
            pl.BlockSpec((d, tn), lambda i, j: (0, j + 2 * nj)),
            pl.BlockSpec((CONV_WIDTH, tn), lambda i, j: (0, j)),
        ],
        out_specs=pl.BlockSpec((tm, tn), lambda i, j: (i, j)),
        out_shape=jax.ShapeDtypeStruct((n, d), _BF16),
        scratch_shapes=[pltpu.VMEM((tm, d), _BF16), pltpu.VMEM((nj, SUBLANES, tn), _F32)],
        compiler_params=_params("arbitrary", "arbitrary"),
        name="conv_in",
    )(x, gain.reshape(1, d), w_in, w_in, w_in, conv_w)


def _matmul_residual_body(x_ref, w_ref, r_ref, o_ref):
    o_ref[...] = r_ref[...] + _bf16_dot(x_ref[...], w_ref[...].astype(_BF16))


def _matmul_residual(x, w, res):
    n, k = x.shape
    d = w.shape[1]
    tm, tn = ROW_TILE, RES_COL_TILE
    return pl.pallas_call(
        _matmul_residual_body,
        grid=(n // tm, d // tn),
        in_specs=[
            pl.BlockSpec((tm, k), lambda i, j: (i, 0)),
            pl.BlockSpec((k, tn), lambda i, j: (0, j)),
            pl.BlockSpec((tm, tn), lambda i, j: (i, j)),
        ],
        out_specs=pl.BlockSpec((tm, tn), lambda i, j: (i, j)),
        out_shape=jax.ShapeDtypeStruct((n, d), _F32),
        compiler_params=_params("arbitrary", "arbitrary"),
        name="matmul_residual",
    )(x, w, res)


def _moba_body(q_ref, k_ref, v_ref, o_ref, km_ref, vt_ref, bias_ref):
    heads = range(q_ref.shape[1] // HEAD_DIM)
    head0 = pl.program_id(1) * len(heads)
    i = pl.program_id(2)
    kb = MOBA_BLOCK
    n_blocks = k_ref.shape[0] // kb
    nt = (((1,), (1,)), ((), ()))
    log2e = 1.4426950408889634
    scale = HEAD_DIM ** -0.5 * log2e
    slopes = [log2e * jnp.exp2(jnp.full((1, kb), -ALIBI_MAX_BIAS / N_HEADS, _F32)
                               * (head0 + hh + 1).astype(_F32)) for hh in heads]

    def lanes(hh):
        return slice(hh * HEAD_DIM, (hh + 1) * HEAD_DIM)

    @pl.when(i == 0)
    def _():
        rel = (lax.broadcasted_iota(jnp.int32, (kb, kb), 1)
               - lax.broadcasted_iota(jnp.int32, (kb, kb), 0))
        for hh in heads:
            for jb in range(n_blocks):
                rows = slice(jb * kb, (jb + 1) * kb)
                km_ref[hh, jb:jb + 1, :] = jnp.mean(k_ref[rows, lanes(hh)].astype(_F32),
                                                    axis=0, keepdims=True)
                vt_ref[hh, :, rows] = v_ref[rows, lanes(hh)].astype(_F32).T.astype(_BF16)
            bias = slopes[hh] * rel.astype(_F32)
            bias_ref[hh, 0] = bias
            bias_ref[hh, 1] = jnp.where(rel >= 0, bias, jnp.inf)

    q = [q_ref[:, lanes(hh)] for hh in heads]
    sel = []
    for hh in heads:
        gate = lax.dot_general(km_ref[hh], q[hh].astype(_F32), nt,
                               precision=lax.Precision.HIGHEST, preferred_element_type=_F32)
        blk = lax.broadcasted_iota(jnp.int32, gate.shape, 0)
        gate = jnp.where(blk < i, gate, -jnp.inf)
        picked = jnp.zeros(gate.shape, _F32)
        for jb in range(n_blocks - 1):
            g_jb = gate[jb:jb + 1, :]
            beats = (gate > g_jb) | ((gate == g_jb) & (blk < jb))
            n_beat = jnp.sum(beats.astype(_F32), axis=0, keepdims=True)
            picked = jnp.where((blk == jb) & (blk < i) & (n_beat < MOBA_TOPK), 1.0, picked)
        sel.append(picked)

    def attend(own):
        nb = own + 1
        raw = [lax.dot_general(k_ref[0:nb * kb, lanes(hh)], q[hh], nt, preferred_element_type=_F32)
               for hh in heads]
        tiles, col_max = [[] for _ in heads], [[] for _ in heads]
        for jb in range(nb):
            for hh in heads:
                tile = raw[hh][jb * kb:(jb + 1) * kb, :] * scale - bias_ref[hh, 1 if jb == own else 0]
                mx = jnp.max(tile, axis=0, keepdims=True)
                if jb != own:
                    mx = jnp.where(sel[hh][jb:jb + 1, :] > 0.5,
                                   mx - slopes[hh] * float((own - jb) * kb), -jnp.inf)
                tiles[hh].append(tile), col_max[hh].append(mx)
        m = [functools.reduce(jnp.maximum, col_max[hh]) for hh in heads]
        probs, l = [[] for _ in heads], [None for _ in heads]
        for jb in range(nb):
            for hh in heads:
                offset = m[hh] if jb == own else jnp.where(
                    sel[hh][jb:jb + 1, :] > 0.5, m[hh] + slopes[hh] * float((own - jb) * kb), jnp.inf)
                p = jnp.exp2(tiles[hh][jb] - offset)
                p_sum = jnp.sum(p, axis=0, keepdims=True)
                l[hh] = p_sum if l[hh] is None else l[hh] + p_sum
                probs[hh].append(p.astype(_BF16))
        for hh in heads:
            p_all = probs[hh][0] if nb == 1 else jnp.concatenate(probs[hh], axis=0)
            acc = _bf16_dot(vt_ref[hh, :, 0:nb * kb], p_all)
            o_ref[:, lanes(hh)] = (acc / l[hh]).T.astype(o_ref.dtype)

    for own in range(n_blocks):
        pl.when(i == own)(functools.partial(attend, own))


def _moba_attention(q, k, v, batch, seq_len):
    n, d = q.shape
    kb = MOBA_BLOCK
    nqb = seq_len // kb
    hp = ATTN_HEADS
    width = hp * HEAD_DIM
    assert seq_len % kb == 0 and nqb == SUBLANES and N_HEADS % hp == 0 and d == N_HEADS * HEAD_DIM
    return pl.pallas_call(
        _moba_body,
        grid=(batch, N_HEADS // hp, nqb),
        in_specs=[
            pl.BlockSpec((kb, width), lambda b, h, i: (b * nqb + i, h)),
            pl.BlockSpec((seq_len, width), lambda b, h, i: (b, h)),
            pl.BlockSpec((seq_len, width), lambda b, h, i: (b, h)),
        ],
        out_specs=pl.BlockSpec((kb, width), lambda b, h, i: (b * nqb + i, h)),
        out_shape=jax.ShapeDtypeStruct((n, d), _BF16),
        scratch_shapes=[pltpu.VMEM((hp, nqb, HEAD_DIM), _F32),
                        pltpu.VMEM((hp, HEAD_DIM, seq_len), _BF16),
                        pltpu.VMEM((hp, 2, kb, kb), _F32)],
        compiler_params=_params("arbitrary", "arbitrary", "arbitrary"),
        name="moba_attention",
    )(q, k, v)


def _router_body(x_ref, g_ref, wr_ref, hn_ref, ids_ref, wts_ref, cnt_ref, run_ref):
    step = pl.program_id(0)
    hn = _rms_normalise(x_ref[...], g_ref[...])
    _store_row_tiles(hn_ref, _pack_bf16_halves(hn))
    w = wr_ref[...]
    hn_hi, w_hi = hn.astype(_BF16), w.astype(_BF16)
    hn_lo = (hn - hn_hi.astype(_F32)).astype(_BF16)
    w_lo = (w - w_hi.astype(_F32)).astype(_BF16)
    logits = (_bf16_dot(hn_hi, w_hi) + (_bf16_dot(hn_hi, w_lo) + _bf16_dot(hn_lo, w_hi))).T
    row = lax.broadcasted_iota(jnp.int32, logits.shape, 0)
    far = jnp.int32(logits.shape[0])

    def first_row(mask):
        return jnp.min(jnp.where(mask, row, far), axis=0, keepdims=True)

    is_group = (row >= N_EXPERTS) & (row < N_EXPERTS + N_GROUPS)
    g_max = jnp.max(jnp.where(is_group, logits, -jnp.inf), axis=0, keepdims=True)
    g_sel = first_row(is_group & (logits == g_max)) - N_EXPERTS
    g_w = 1.0 / jnp.sum(jnp.where(is_group, jnp.exp(logits - g_max), 0.0), axis=0, keepdims=True)

    in_group = (row >= g_sel * EXPERTS_PER_GROUP) & (row < (g_sel + 1) * EXPERTS_PER_GROUP)
    top1 = jnp.max(jnp.where(in_group, logits, -jnp.inf), axis=0, keepdims=True)
    idx1 = first_row(in_group & (logits == top1))
    rest = in_group & (row != idx1)
    top2 = jnp.max(jnp.where(rest, logits, -jnp.inf), axis=0, keepdims=True)
    idx2 = first_row(rest & (logits == top2))
    t = jnp.exp(top2 - top1)
    w1 = 1.0 / (1.0 + t) * g_w
    w2 = t / (1.0 + t) * g_w

    @pl.when(step == 0)
    def _():
        run_ref[...] = jnp.zeros(run_ref.shape, _F32)

    tm = logits.shape[1]
    chosen = jnp.where((row == idx1) | (row == idx2), 1.0, 0.0)
    earlier = (lax.broadcasted_iota(jnp.int32, (tm, tm), 0)
               < lax.broadcasted_iota(jnp.int32, (tm, tm), 1))
    before = run_ref[:, 0:1] + _bf16_dot(chosen.astype(_BF16),
                                         jnp.where(earlier, 1.0, 0.0).astype(_BF16))
    rank1 = jnp.sum(jnp.where(row == idx1, before, 0.0), axis=0, keepdims=True).astype(jnp.int32)
    rank2 = jnp.sum(jnp.where(row == idx2, before, 0.0), axis=0, keepdims=True).astype(jnp.int32)
    run_ref[...] += jnp.sum(chosen, axis=1, keepdims=True)

    out_row = lax.broadcasted_iota(jnp.int32, ids_ref.shape, 0)
    ids_ref[...] = jnp.where(out_row == 0, idx1, jnp.where(out_row == 1, idx2,
                             jnp.where(out_row == 2, rank1, jnp.where(out_row == 3, rank2, 0))))
    wts_ref[...] = jnp.where(row == 0, w1, jnp.where(row == 1, w2, 0.0)).T
    cnt_ref[...] = run_ref[...].astype(jnp.int32)


def _router(x, gain, w_rg, w_re):
    n, d = x.shape
    tm = ROUTER_ROWS
    assert EXPERT_TOPK == 2 and N_EXPERTS + N_GROUPS <= LANES
    chunks = d // 2 // LANES
    wr = jnp.concatenate(
        [w_re, w_rg, jnp.zeros((d, LANES - N_EXPERTS - N_GROUPS), _F32)], axis=1)
    row = lambda i: (i, 0)
    col = lambda i: (0, i)
    fixed = lambda i: (0, 0)
    return pl.pallas_call(
        _router_body,
        grid=(n // tm,),
        in_specs=[
            pl.BlockSpec((tm, d), row),
            pl.BlockSpec((1, d), fixed),
            pl.BlockSpec((d, LANES), fixed),
        ],
        out_specs=[pl.BlockSpec((tm * chunks, LANES), row), pl.BlockSpec((SUBLANES, tm), col),
                   pl.BlockSpec((tm, LANES), row), pl.BlockSpec((LANES, LANES), fixed)],
        out_shape=[jax.ShapeDtypeStruct((n * chunks, LANES), jnp.uint32),
                   jax.ShapeDtypeStruct((SUBLANES, n), jnp.int32),
                   jax.ShapeDtypeStruct((n, LANES), _F32),
                   jax.ShapeDtypeStruct((LANES, LANES), jnp.int32)],
        scratch_shapes=[pltpu.VMEM((LANES, LANES), _F32)],
        compiler_params=_params("arbitrary"),
        name="router",
    )(x, gain.reshape(1, d), wr)


def _expert_body(bounds_ref, row_tok_ref, hn_hbm, wg_ref, wu_ref, wd_ref, y_hbm,
                 xs_ref, ys_ref, zero_ref, wg_bf, wu_bf, wd_bf, gather_sem, out_sem, tail_sem):
    e = pl.program_id(0)
    n_experts = pl.num_programs(0)
    n_slots = xs_ref.shape[0]
    chunks = wg_bf.shape[0] // 2 // LANES
    rows = xs_ref.shape[1] // chunks
    tile_rows = rows * chunks
    ahead = n_slots - 1
    first, stop = bounds_ref[e], bounds_ref[e + 1]
    n_total = bounds_ref[n_experts]

    def row_copy(tok, s, r):
        src = hn_hbm.at[pl.ds(pl.multiple_of(tok * chunks, chunks), chunks)]
        return pltpu.make_async_copy(src, xs_ref.at[s, pl.ds(r * chunks, chunks)], gather_sem.at[s])

    def start_gather(block):
        def body(r, carry):
            row_copy(row_tok_ref[block * rows + r], block % n_slots, r).start(priority=1)
            return carry
        lax.fori_loop(0, rows, body, 0)

    def wait_gather(block):
        s = block % n_slots
        pltpu.make_async_copy(hn_hbm.at[pl.ds(0, rows * chunks)], xs_ref.at[s],
                              gather_sem.at[s]).wait()

    def out_copy(block):
        s = block % n_slots
        dst = y_hbm.at[pl.ds(pl.multiple_of(block * tile_rows, tile_rows), tile_rows)]
        return pltpu.make_async_copy(ys_ref.at[s], dst, out_sem.at[s])

    def tail_copy(block):
        dst = y_hbm.at[pl.ds(pl.multiple_of(block * tile_rows, tile_rows), tile_rows)]
        return pltpu.make_async_copy(zero_ref, dst, tail_sem.at[0])

    def for_tail_blocks(fn):
        def body(block, carry):
            fn(tail_copy(block))
            return carry
        lax.fori_loop(n_total, y_hbm.shape[0] // tile_rows, body, 0)

    @pl.when(e == 0)
    def _():
        for block in range(ahead):
            start_gather(block)
        zero_ref[...] = jnp.zeros(zero_ref.shape, zero_ref.dtype)
        for_tail_blocks(lambda copy: copy.start())

    @pl.when(stop > first)
    def _():
        wg_bf[...] = wg_ref[0, 0].astype(_BF16)
        wu_bf[...] = wu_ref[0, 0].astype(_BF16)
        wd_bf[...] = wd_ref[0, 0].astype(_BF16)

    def block_step(block, carry):
        @pl.when(block >= n_slots)
        def _():
            out_copy(block - n_slots).wait()

        nxt = block + ahead
        nxt_slot = nxt % n_slots
        nxt_base = jnp.minimum(nxt, n_total - 1) * rows
        per_group = rows // EXPERT_ISSUE_GROUPS

        def issue(group):
            for r in range(group * per_group, (group + 1) * per_group):
                row_copy(row_tok_ref[nxt_base + r], nxt_slot, r).start(priority=1)

        wait_gather(block)
        s = block % n_slots
        x_lo, x_hi = _unpack_bf16_halves(_load_row_tiles(xs_ref.at[s], chunks), _BF16)
        half = x_lo.shape[1]
        gate = _bf16_dot(x_lo, wg_bf[:half, :])
        issue(0)
        gate = gate + _bf16_dot(x_hi, wg_bf[half:, :])
        issue(1)
        up = _bf16_dot(x_lo, wu_bf[:half, :])
        issue(2)
        up = up + _bf16_dot(x_hi, wu_bf[half:, :])
        issue(3)
        act = (jax.nn.silu(gate) * up).astype(_BF16)
        quarter = wd_bf.shape[1] // 4
        y = []
        for c in range(4):
            y.append(_bf16_dot(act, wd_bf[:, c * quarter:(c + 1) * quarter]))
            issue(4 + c)
        packed = jnp.concatenate([_pack_bf16_pair(y[0], y[2]), _pack_bf16_pair(y[1], y[3])], axis=1)
        _store_row_tiles(ys_ref.at[s], packed)
        out_copy(block).start()
        return carry

    lax.fori_loop(first, stop, block_step, 0)

    @pl.when(e == n_experts - 1)
    def _():
        for extra in range(ahead):
            wait_gather(n_total + extra)
        for_tail_blocks(lambda copy: copy.wait())
        for back in range(1, n_slots + 1):
            @pl.when(n_total >= back)
            def _():
                out_copy(n_total - back).wait()


def _expert_mlp(bounds, row_tok, hn, layer, w_gate, w_up, w_down):
    n_experts, d, de = w_gate.shape[1:]
    rows = EXPERT_ROWS
    chunks = d // 2 // LANES
    assert hn.shape[1] == LANES and hn.dtype == jnp.uint32
    grid_spec = pltpu.PrefetchScalarGridSpec(
        num_scalar_prefetch=2,
        grid=(n_experts,),
        in_specs=[
            pl.BlockSpec(memory_space=pl.ANY),
            pl.BlockSpec((1, 1, d, de), lambda e, bd, rt: (layer, e, 0, 0)),
            pl.BlockSpec((1, 1, d, de), lambda e, bd, rt: (layer, e, 0, 0)),
            pl.BlockSpec((1, 1, de, d), lambda e, bd, rt: (layer, e, 0, 0)),
        ],
        out_specs=pl.BlockSpec(memory_space=pl.ANY),
        scratch_shapes=[
            pltpu.VMEM((EXPERT_SLOTS, rows * chunks, LANES), jnp.uint32),
            pltpu.VMEM((EXPERT_SLOTS, rows * chunks, LANES), jnp.uint32),
            pltpu.VMEM((rows * chunks, LANES), jnp.uint32),
            pltpu.VMEM((d, de), _BF16),
            pltpu.VMEM((d, de), _BF16),
            pltpu.VMEM((de, d), _BF16),
            pltpu.SemaphoreType.DMA((EXPERT_SLOTS,)),
            pltpu.SemaphoreType.DMA((EXPERT_SLOTS,)),
            pltpu.SemaphoreType.DMA((1,)),
        ],
    )
    return pl.pallas_call(
        _expert_body,
        grid_spec=grid_spec,
        out_shape=jax.ShapeDtypeStruct((row_tok.shape[0] * chunks, LANES), jnp.uint32),
        compiler_params=_params("arbitrary"),
        name="expert_mlp",
    )(bounds, row_tok, hn, w_gate, w_up, w_down)


def _combine_body(pos_ref, y_hbm, wts_ref, h_ref, o_ref, ys_ref, sem):
    i = pl.program_id(0)
    n_steps = pl.num_programs(0)
    tm = h_ref.shape[0]
    chunks = h_ref.shape[1] // 2 // LANES
    n_tokens = n_steps * tm
    n_slots = ys_ref.shape[0]
    ahead = n_slots - 1
    slot = i % n_slots

    def row_copy(p, s, k, r):
        src = y_hbm.at[pl.ds(pl.multiple_of(p * chunks, chunks), chunks)]
        return pltpu.make_async_copy(src, ys_ref.at[s, k, pl.ds(r * chunks, chunks)], sem.at[s])

    def start_row(tile, r):
        for k in range(EXPERT_TOPK):
            row_copy(pos_ref[k * n_tokens + tile * tm + r], tile % n_slots, k, r).start(priority=k)

    def wait_gather(s):
        for k in range(EXPERT_TOPK):
            pltpu.make_async_copy(y_hbm.at[pl.ds(0, tm * chunks)], ys_ref.at[s, k], sem.at[s]).wait()

    @pl.when(i == 0)
    def _():
        for tile in range(ahead):
            def body(r, carry):
                start_row(tile, r)
                return carry
            lax.fori_loop(0, tm, body, 0)

    @pl.when(i + ahead < n_steps)
    def _():
        for r in range(tm):
            start_row(i + ahead, r)

    wait_gather(slot)
    w = wts_ref[...]
    half = chunks * LANES
    a_lo, a_hi = _unpack_bf16_halves(_load_row_tiles(ys_ref.at[slot, 0], chunks), _F32)
    b_lo, b_hi = _unpack_bf16_halves(_load_row_tiles(ys_ref.at[slot, 1], chunks), _F32)
    o_ref[:, :half] = h_ref[:, :half] + (w[:, 0:1] * a_lo + w[:, 1:2] * b_lo)
    o_ref[:, half:] = h_ref[:, half:] + (w[:, 0:1] * a_hi + w[:, 1:2] * b_hi)


def _combine(pos, y, wts, h):
    n, d = h.shape
    tm = COMBINE_ROWS
    grid_spec = pltpu.PrefetchScalarGridSpec(
        num_scalar_prefetch=1,
        grid=(n // tm,),
        in_specs=[
            pl.BlockSpec(memory_space=pl.ANY),
            pl.BlockSpec((tm, LANES), lambda i, p: (i, 0)),
            pl.BlockSpec((tm, d), lambda i, p: (i, 0)),
        ],
        out_specs=pl.BlockSpec((tm, d), lambda i, p: (i, 0)),
        scratch_shapes=[
            pltpu.VMEM((COMBINE_SLOTS, EXPERT_TOPK, tm * (d // 2 // LANES), LANES), jnp.uint32),
            pltpu.SemaphoreType.DMA((COMBINE_SLOTS,)),
        ],
    )
    return pl.pallas_call(
        _combine_body,
        grid_spec=grid_spec,
        out_shape=jax.ShapeDtypeStruct((n, d), _F32),
        compiler_params=_params("arbitrary"),
        name="combine",
    )(pos, y, wts, h)


def _dispatch_plan(ids, counts):
    n = ids.shape[1]
    rows = EXPERT_ROWS
    m = n * EXPERT_TOPK
    n_blocks = -(-(m + N_EXPERTS * (rows - 1)) // rows)
    flat_e = ids[:EXPERT_TOPK].reshape(m)
    rank = ids[EXPERT_TOPK:2 * EXPERT_TOPK].reshape(m)
    counts = counts[:N_EXPERTS, 0]
    starts = jnp.cumsum(counts) - counts
    padded = (counts + rows - 1) // rows * rows
    pends = jnp.cumsum(padded)
    pstarts = pends - padded
    pos = rank
    for e in range(N_EXPERTS):
        pos = pos + jnp.where(flat_e == e, pstarts[e], 0)
    blk_start = jnp.arange(n_blocks, dtype=jnp.int32)[:, None] * rows
    in_seg = (pstarts[None, :] <= blk_start) & (blk_start < pends[None, :])
    shift = jnp.sum(jnp.where(in_seg, (starts - pstarts)[None, :], 0), axis=1, keepdims=True)
    seg_end = jnp.sum(jnp.where(in_seg, (starts + counts)[None, :], 0), axis=1, keepdims=True)
    token = jnp.arange(m, dtype=jnp.int32) % n
    order = jnp.argsort(flat_e * n + token)
    sorted_idx = blk_start + shift + jnp.arange(rows, dtype=jnp.int32)[None, :]
    valid = sorted_idx < seg_end
    row_tok = jnp.where(valid, order[jnp.clip(sorted_idx, 0, m - 1)] % n, 0)
    bounds = jnp.concatenate([pstarts, pends[-1:]]) // rows
    return (bounds.astype(jnp.int32), row_tok.reshape(n_blocks * rows).astype(jnp.int32),
            pos.astype(jnp.int32))


def _hier_moe_residual(h, gain, w_rg, w_re, layer, w_gate, w_up, w_down):
    hn, ids, wts, counts = _router(h, gain, w_rg, w_re)
    bounds, row_tok, pos = _dispatch_plan(ids, counts)
    y = _expert_mlp(bounds, row_tok, hn, layer, w_gate, w_up, w_down)
    return _combine(pos, y, wts, h)


def kernel(x, mix_norm, ffn_norm, conv_w_in, conv_w, conv_w_out, kv_norm, w_kv, k_norm, w_q, q_norm,
           w_o, router_group, router_expert, w_gate, w_up, w_down):
    bsz, seq_len, d = x.shape
    depth = mix_norm.shape[0]
    n_conv = conv_w_in.shape[0]
    assert d == N_HEADS * HEAD_DIM
    h = x.reshape(bsz * seq_len, d)
    k_sh = v_sh = None
    for l in range(depth):
        if l < n_conv:
            z = _conv_in(h, mix_norm[l], conv_w_in[l], conv_w[l], seq_len)
            h = _matmul_residual(z, conv_w_out[l], h)
        else:
            j = l - n_conv
            gains = jnp.stack([mix_norm[l], kv_norm])
            head_gains = jnp.stack([q_norm[j], k_norm])
            q, *kv = _qkv_projection(h, gains, w_q[j], w_kv, head_gains, with_kv=(j == 0))
            if j == 0:
                k_sh, v_sh = kv
            attn = _moba_attention(q, k_sh, v_sh, bsz, seq_len)
            h = _matmul_residual(attn, w_o[j], h)
        h = _hier_moe_residual(h, ffn_norm[l], router_group[l], router_expert[l],
                               l, w_gate, w_up, w_down)
    return h.reshape(bsz, seq_len, d)
```

```python
import functools

import jax
import jax.numpy as jnp
from jax import lax
from jax.experimental import pallas as pl
from jax.experimental.pallas import tpu as pltpu

N_HEADS = 16
HEAD_DIM = 128
CONV_WIDTH = 3
MOBA_BLOCK = 256
MOBA_TOPK = 3
N_GROUPS = 4
EXPERTS_PER_GROUP = 8
N_EXPERTS = N_GROUPS * EXPERTS_PER_GROUP
EXPERT_TOPK = 2
NORM_EPS = 1e-6
ALIBI_MAX_BIAS = 8.0

LANES = 128
SUBLANES = 8
VMEM_LIMIT = 56 * 1024 * 1024

ROW_TILE = 1024
RES_COL_TILE = 1024
TRIPLE_COL_TILE = 256
ROUTER_ROWS = 512
EXPERT_ROWS = 128
EXPERT_SLOTS = 6
EXPERT_ISSUE_GROUPS = 8
COMBINE_ROWS = 256
COMBINE_SLOTS = 2
ATTN_HEADS = 8

_BF16 = jnp.bfloat16
_F32 = jnp.float32


def _params(*semantics):
    return pltpu.CompilerParams(dimension_semantics=semantics, vmem_limit_bytes=VMEM_LIMIT)


def _rms_normalise(x, gain):
    ms = jnp.mean(x * x, axis=-1, keepdims=True)
    return x * lax.rsqrt(ms + NORM_EPS) * gain


def _bf16_dot(a, b):
    return jnp.dot(a, b, preferred_element_type=_F32)


def _pack_bf16_halves(x):
    c = x.shape[1] // 2
    return _pack_bf16_pair(x[:, :c], x[:, c:])


def _pack_bf16_pair(lo, hi):
    as_bits = lambda v: lax.bitcast_convert_type(v.astype(_BF16).astype(_F32), jnp.uint32)
    return (as_bits(lo) >> 16) | (as_bits(hi) & jnp.uint32(0xFFFF0000))


def _store_row_tiles(ref, value):
    rows, chunks = value.shape[0], value.shape[1] // LANES
    for c in range(chunks):
        ref[pl.ds(c, rows, stride=chunks), :] = value[:, c * LANES:(c + 1) * LANES]


def _load_row_tiles(ref, chunks):
    rows = ref.shape[0] // chunks
    return jnp.concatenate([ref[pl.ds(c, rows, stride=chunks), :] for c in range(chunks)], axis=1)


def _unpack_bf16_halves(p, dtype):
    lo = lax.bitcast_convert_type(p << 16, _F32).astype(dtype)
    hi = lax.bitcast_convert_type(p & jnp.uint32(0xFFFF0000), _F32).astype(dtype)
    return lo, hi


def _qkv_body(x_ref, g_ref, *refs, with_kv):
    if with_kv:
        wq_ref, wk_ref, wv_ref, hg_ref, q_ref, k_ref, v_ref, xn_ref = refs
    else:
        wq_ref, hg_ref, q_ref, xn_ref = refs

    @pl.when(pl.program_id(1) == 0)
    def _():
        x = x_ref[...]
        normed = x * lax.rsqrt(jnp.mean(x * x, axis=-1, keepdims=True) + NORM_EPS)
        xn_ref[0] = (normed * g_ref[0:1, :]).astype(_BF16)
        if with_kv:
            xn_ref[1] = (normed * g_ref[1:2, :]).astype(_BF16)

    def store_head_normalised(o_ref, y, gain_row):
        head_gain = hg_ref[gain_row:gain_row + 1, :]
        for hh in range(y.shape[1] // HEAD_DIM):
            cols = slice(hh * HEAD_DIM, (hh + 1) * HEAD_DIM)
            o_ref[:, cols] = _rms_normalise(y[:, cols], head_gain).astype(o_ref.dtype)

    products = [_bf16_dot(xn_ref[0], wq_ref[...].astype(_BF16))]
    if with_kv:
        products += [_bf16_dot(xn_ref[1], w_ref[...].astype(_BF16)) for w_ref in (wk_ref, wv_ref)]
    store_head_normalised(q_ref, products[0], 0)
    if with_kv:
        store_head_normalised(k_ref, products[1], 1)
        v_ref[...] = products[2].astype(v_ref.dtype)


def _qkv_projection(x, gains, w_q, w_kv, head_gains, with_kv):
    n, d = x.shape
    tm, tn = ROW_TILE, TRIPLE_COL_TILE
    n_q = w_q.shape[1]
    nj = n_q // tn
    assert n_q % tn == 0 and w_kv.shape[1] == 2 * n_q
    tile = lambda i, j: (i, j)
    fixed = lambda i, j: (0, 0)
    weights = [(w_q, lambda i, j: (0, j))]
    if with_kv:
        weights += [(w_kv, lambda i, j: (0, j)), (w_kv, lambda i, j: (0, j + nj))]
    n_out = len(weights)
    outs = pl.pallas_call(
        functools.partial(_qkv_body, with_kv=with_kv),
        grid=(n // tm, nj),
        in_specs=([pl.BlockSpec((tm, d), lambda i, j: (i, 0)), pl.BlockSpec((2, d), fixed)]
                  + [pl.BlockSpec((d, tn), index_map) for _, index_map in weights]
                  + [pl.BlockSpec((2, HEAD_DIM), fixed)]),
        out_specs=[pl.BlockSpec((tm, tn), tile)] * n_out,
        out_shape=[jax.ShapeDtypeStruct((n, n_q), _BF16)] * n_out,
        scratch_shapes=[pltpu.VMEM((2, tm, d), _BF16)],
        compiler_params=_params("arbitrary", "arbitrary"),
        name="qkv_projection",
    )(x, gains, *[w for w, _ in weights], head_gains)
    return tuple(outs)


def _conv_in_body(x_ref, g_ref, wb_ref, wc_ref, wx_ref, cw_ref, z_ref, xn_ref, carry_ref,
                  *, tiles_per_seq):
    i = pl.program_id(0)
    j = pl.program_id(1)
    tm = x_ref.shape[0]

    @pl.when(j == 0)
    def _():
        xn_ref[...] = _rms_normalise(x_ref[...], g_ref[...]).astype(_BF16)

    @pl.when(i % tiles_per_seq == 0)
    def _():
        carry_ref[j] = jnp.zeros(carry_ref.shape[1:], _F32)

    xn = xn_ref[...]
    b_gate = _bf16_dot(xn, wb_ref[...].astype(_BF16))
    c_gate = _bf16_dot(xn, wc_ref[...].astype(_BF16))
    xh = _bf16_dot(xn, wx_ref[...].astype(_BF16))
    u = c_gate * xh
    prev = carry_ref[j]
    row = lax.broadcasted_iota(jnp.int32, u.shape, 0)
    last = prev[SUBLANES - 1:SUBLANES]
    u1 = jnp.where(row == 0, last, pltpu.roll(u, 1, axis=0))
    u2 = jnp.where(row == 0, prev[SUBLANES - 2:SUBLANES - 1],
                   jnp.where(row == 1, last, pltpu.roll(u, 2, axis=0)))
    cw = cw_ref[...]
    conv = cw[0:1] * u2 + cw[1:2] * u1 + cw[2:3] * u
    z_ref[...] = (b_gate * conv).astype(z_ref.dtype)
    carry_ref[j] = u[tm - SUBLANES:tm]


def _conv_in(x, gain, w_in, conv_w, seq_len):
    n, d = x.shape
    tm, tn = ROW_TILE, TRIPLE_COL_TILE
    nj = d // tn
    assert CONV_WIDTH - 1 <= SUBLANES and seq_len % tm == 0
    return pl.pallas_call(
        functools.partial(_conv_in_body, tiles_per_seq=seq_len // tm),
        grid=(n // tm, nj),
        in_specs=[
            pl.BlockSpec((tm, d), lambda i, j: (i, 0)),
            pl.BlockSpec((1, d), lambda i, j: (0, 0)),
            pl.BlockSpec((d, tn), lambda i, j: (0, j)),
            pl.BlockSpec((d, tn), lambda i, j: (0, j + nj)),
            pl.BlockSpec((d, tn), lambda i, j: (0, j + 2 * nj)),
            pl.BlockSpec((CONV_WIDTH, tn), lambda i, j: (0, j)),
        ],
        out_specs=pl.BlockSpec((tm, tn), lambda i, j: (i, j)),
        out_shape=jax.ShapeDtypeStruct((n, d), _BF16),
        scratch_shapes=[pltpu.VMEM((tm, d), _BF16), pltpu.VMEM((nj, SUBLANES, tn), _F32)],
        compiler_params=_params("arbitrary", "arbitrary"),
        name="conv_in",
    )(x, gain.reshape(1, d), w_in, w_in, w_in, conv_w)


def _matmul_residual_body(x_ref, w_ref, r_ref, o_ref):
    o_ref[...] = r_ref[...] + _bf16_dot(x_ref[...], w_ref[...].astype(_BF16))


def _matmul_residual(x, w, res):
    n, k = x.shape
    d = w.shape[1]
    tm, tn = ROW_TILE, RES_COL_TILE
    return pl.pallas_call(
        _matmul_residual_body,
        grid=(n // tm, d // tn),
        in_specs=[
            pl.BlockSpec((tm, k), lambda i, j: (i, 0)),
            pl.BlockSpec((k, tn), lambda i, j: (0, j)),
            pl.BlockSpec((tm, tn), lambda i, j: (i, j)),
        ],
        out_specs=pl.BlockSpec((tm, tn), lambda i, j: (i, j)),
        out_shape=jax.ShapeDtypeStruct((n, d), _F32),
        compiler_params=_params("arbitrary", "arbitrary"),
        name="matmul_residual",
    )(x, w, res)


def _moba_body(q_ref, k_ref, v_ref, o_ref, km_ref, vt_ref, bias_ref):
    heads = range(q_ref.shape[1] // HEAD_DIM)
    head0 = pl.program_id(1) * len(heads)
    i = pl.program_id(2)
    kb = MOBA_BLOCK
    n_blocks = k_ref.shape[0] // kb
    nt = (((1,), (1,)), ((), ()))
    log2e = 1.4426950408889634
    scale = HEAD_DIM ** -0.5 * log2e
    slopes = [log2e * jnp.exp2(jnp.full((1, kb), -ALIBI_MAX_BIAS / N_HEADS, _F32)
                               * (head0 + hh + 1).astype(_F32)) for hh in heads]

    def lanes(hh):
        return slice(hh * HEAD_DIM, (hh + 1) * HEAD_DIM)

    @pl.when(i == 0)
    def _():
        rel = (lax.broadcasted_iota(jnp.int32, (kb, kb), 1)
               - lax.broadcasted_iota(jnp.int32, (kb, kb), 0))
        for hh in heads:
            for jb in range(n_blocks):
                rows = slice(jb * kb, (jb + 1) * kb)
                km_ref[hh, jb:jb + 1, :] = jnp.mean(k_ref[rows, lanes(hh)].astype(_F32),
                                                    axis=0, keepdims=True)
                vt_ref[hh, :, rows] = v_ref[rows, lanes(hh)].astype(_F32).T.astype(_BF16)
            bias = slopes[hh] * rel.astype(_F32)
            bias_ref[hh, 0] = bias
            bias_ref[hh, 1] = jnp.where(rel >= 0, bias, jnp.inf)

    q = [q_ref[:, lanes(hh)] for hh in heads]
    sel = []
    for hh in heads:
        gate = lax.dot_general(km_ref[hh], q[hh].astype(_F32), nt,
                               precision=lax.Precision.HIGHEST, preferred_element_type=_F32)
        blk = lax.broadcasted_iota(jnp.int32, gate.shape, 0)
        gate = jnp.where(blk < i, gate, -jnp.inf)
        picked = jnp.zeros(gate.shape, _F32)
        for jb in range(n_blocks - 1):
            g_jb = gate[jb:jb + 1, :]
            beats = (gate > g_jb) | ((gate == g_jb) & (blk < jb))
            n_beat = jnp.sum(beats.astype(_F32), axis=0, keepdims=True)
            picked = jnp.where((blk == jb) & (blk < i) & (n_beat < MOBA_TOPK), 1.0, picked)
        sel.append(picked)

    def attend(own):
        nb = own + 1
        raw = [lax.dot_general(k_ref[0:nb * kb, lanes(hh)], q[hh], nt, preferred_element_type=_F32)
               for hh in heads]
        tiles, col_max = [[] for _ in heads], [[] for _ in heads]
        for jb in range(nb):
            for hh in heads:
                tile = raw[hh][jb * kb:(jb + 1) * kb, :] * scale - bias_ref[hh, 1 if jb == own else 0]
                mx = jnp.max(tile, axis=0, keepdims=True)
                if jb != own:
                    mx = jnp.where(sel[hh][jb:jb + 1, :] > 0.5,
                                   mx - slopes[hh] * float((own - jb) * kb), -jnp.inf)
                tiles[hh].append(tile), col_max[hh].append(mx)
        m = [functools.reduce(jnp.maximum, col_max[hh]) for hh in heads]
        probs, l = [[] for _ in heads], [None for _ in heads]
        for jb in range(nb):
            for hh in heads:
                offset = m[hh] if jb == own else jnp.where(
                    sel[hh][jb:jb + 1, :] > 0.5, m[hh] + slopes[hh] * float((own - jb) * kb), jnp.inf)
                p = jnp.exp2(tiles[hh][jb] - offset)
                p_sum = jnp.sum(p, axis=0, keepdims=True)
                l[hh] = p_sum if l[hh] is None else l[hh] + p_sum
                probs[hh].append(p.astype(_BF16))
        for hh in heads:
            p_all = probs[hh][0] if nb == 1 else jnp.concatenate(probs[hh], axis=0)
            acc = _bf16_dot(vt_ref[hh, :, 0:nb * kb], p_all)
            o_ref[:, lanes(hh)] = (acc / l[hh]).T.astype(o_ref.dtype)

    for own in range(n_blocks):
        pl.when(i == own)(functools.partial(attend, own))


def _moba_attention(q, k, v, batch, seq_len):
    n, d = q.shape
    kb = MOBA_BLOCK
    nqb = seq_len // kb
    hp = ATTN_HEADS
    width = hp * HEAD_DIM
    assert seq_len % kb == 0 and nqb == SUBLANES and N_HEADS % hp == 0 and d == N_HEADS * HEAD_DIM
    return pl.pallas_call(
        _moba_body,
        grid=(batch, N_HEADS // hp, nqb),
        in_specs=[
            pl.BlockSpec((kb, width), lambda b, h, i: (b * nqb + i, h)),
            pl.BlockSpec((seq_len, width), lambda b, h, i: (b, h)),
            pl.BlockSpec((seq_len, width), lambda b, h, i: (b, h)),
        ],
        out_specs=pl.BlockSpec((kb, width), lambda b, h, i: (b * nqb + i, h)),
        out_shape=jax.ShapeDtypeStruct((n, d), _BF16),
        scratch_shapes=[pltpu.VMEM((hp, nqb, HEAD_DIM), _F32),
                        pltpu.VMEM((hp, HEAD_DIM, seq_len), _BF16),
                        pltpu.VMEM((hp, 2, kb, kb), _F32)],
        compiler_params=_params("arbitrary", "arbitrary", "arbitrary"),
        name="moba_attention",
    )(q, k, v)


def _router_body(x_ref, g_ref, wr_ref, hn_ref, ids_ref, wts_ref, cnt_ref, run_ref):
    step = pl.program_id(0)
    hn = _rms_normalise(x_ref[...], g_ref[...])
    _store_row_tiles(hn_ref, _pack_bf16_halves(hn))
    w = wr_ref[...]
    hn_hi, w_hi = hn.astype(_BF16), w.astype(_BF16)
    hn_lo = (hn - hn_hi.astype(_F32)).astype(_BF16)
    w_lo = (w - w_hi.astype(_F32)).astype(_BF16)
    logits = (_bf16_dot(hn_hi, w_hi) + (_bf16_dot(hn_hi, w_lo) + _bf16_dot(hn_lo, w_hi))).T
    row = lax.broadcasted_iota(jnp.int32, logits.shape, 0)
    far = jnp.int32(logits.shape[0])

    def first_row(mask):
        return jnp.min(jnp.where(mask, row, far), axis=0, keepdims=True)

    is_group = (row >= N_EXPERTS) & (row < N_EXPERTS + N_GROUPS)
    g_max = jnp.max(jnp.where(is_group, logits, -jnp.inf), axis=0, keepdims=True)
    g_sel = first_row(is_group & (logits == g_max)) - N_EXPERTS
    g_w = 1.0 / jnp.sum(jnp.where(is_group, jnp.exp(logits - g_max), 0.0), axis=0, keepdims=True)

    in_group = (row >= g_sel * EXPERTS_PER_GROUP) & (row < (g_sel + 1) * EXPERTS_PER_GROUP)
    top1 = jnp.max(jnp.where(in_group, logits, -jnp.inf), axis=0, keepdims=True)
    idx1 = first_row(in_group & (logits == top1))
    rest = in_group & (row != idx1)
    top2 = jnp.max(jnp.where(rest, logits, -jnp.inf), axis=0, keepdims=True)
    idx2 = first_row(rest & (logits == top2))
    t = jnp.exp(top2 - top1)
    w1 = 1.0 / (1.0 + t) * g_w
    w2 = t / (1.0 + t) * g_w

    @pl.when(step == 0)
    def _():
        run_ref[...] = jnp.zeros(run_ref.shape, _F32)

    tm = logits.shape[1]
    chosen = jnp.where((row == idx1) | (row == idx2), 1.0, 0.0)
    earlier = (lax.broadcasted_iota(jnp.int32, (tm, tm), 0)
               < lax.broadcasted_iota(jnp.int32, (tm, tm), 1))
    before = run_ref[:, 0:1] + _bf16_dot(chosen.astype(_BF16),
                                         jnp.where(earlier, 1.0, 0.0).astype(_BF16))
    rank1 = jnp.sum(jnp.where(row == idx1, before, 0.0), axis=0, keepdims=True).astype(jnp.int32)
    rank2 = jnp.sum(jnp.where(row == idx2, before, 0.0), axis=0, keepdims=True).astype(jnp.int32)
    run_ref[...] += jnp.sum(chosen, axis=1, keepdims=True)

    out_row = lax.broadcasted_iota(jnp.int32, ids_ref.shape, 0)
    ids_ref[...] = jnp.where(out_row == 0, idx1, jnp.where(out_row == 1, idx2,
                             jnp.where(out_row == 2, rank1, jnp.where(out_row == 3, rank2, 0))))
    wts_ref[...] = jnp.where(row == 0, w1, jnp.where(row == 1, w2, 0.0)).T
    cnt_ref[...] = run_ref[...].astype(jnp.int32)


def _router(x, gain, w_rg, w_re):
    n, d = x.shape
    tm = ROUTER_ROWS
    assert EXPERT_TOPK == 2 and N_EXPERTS + N_GROUPS <= LANES
    chunks = d // 2 // LANES
    wr = jnp.concatenate(
        [w_re, w_rg, jnp.zeros((d, LANES - N_EXPERTS - N_GROUPS), _F32)], axis=1)
    row = lambda i: (i, 0)
    col = lambda i: (0, i)
    fixed = lambda i: (0, 0)
    return pl.pallas_call(
        _router_body,
        grid=(n // tm,),
        in_specs=[
            pl.BlockSpec((tm, d), row),
            pl.BlockSpec((1, d), fixed),
            pl.BlockSpec((d, LANES), fixed),
        ],
        out_specs=[pl.BlockSpec((tm * chunks, LANES), row), pl.BlockSpec((SUBLANES, tm), col),
                   pl.BlockSpec((tm, LANES), row), pl.BlockSpec((LANES, LANES), fixed)],
        out_shape=[jax.ShapeDtypeStruct((n * chunks, LANES), jnp.uint32),
                   jax.ShapeDtypeStruct((SUBLANES, n), jnp.int32),
                   jax.ShapeDtypeStruct((n, LANES), _F32),
                   jax.ShapeDtypeStruct((LANES, LANES), jnp.int32)],
        scratch_shapes=[pltpu.VMEM((LANES, LANES), _F32)],
        compiler_params=_params("arbitrary"),
        name="router",
    )(x, gain.reshape(1, d), wr)


def _expert_body(bounds_ref, row_tok_ref, hn_hbm, wg_ref, wu_ref, wd_ref, y_hbm,
                 xs_ref, ys_ref, zero_ref, wg_bf, wu_bf, wd_bf, gather_sem, out_sem, tail_sem):
    e = pl.program_id(0)
    n_experts = pl.num_programs(0)
    n_slots = xs_ref.shape[0]
    chunks = wg_bf.shape[0] // 2 // LANES
    rows = xs_ref.shape[1] // chunks
    tile_rows = rows * chunks
    ahead = n_slots - 1
    first, stop = bounds_ref[e], bounds_ref[e + 1]
    n_total = bounds_ref[n_experts]

    def row_copy(tok, s, r):
        src = hn_hbm.at[pl.ds(pl.multiple_of(tok * chunks, chunks), chunks)]
        return pltpu.make_async_copy(src, xs_ref.at[s, pl.ds(r * chunks, chunks)], gather_sem.at[s])

    def start_gather(block):
        def body(r, carry):
            row_copy(row_tok_ref[block * rows + r], block % n_slots, r).start(priority=1)
            return carry
        lax.fori_loop(0, rows, body, 0)

    def wait_gather(block):
        s = block % n_slots
        pltpu.make_async_copy(hn_hbm.at[pl.ds(0, rows * chunks)], xs_ref.at[s],
                              gather_sem.at[s]).wait()

    def out_copy(block):
        s = block % n_slots
        dst = y_hbm.at[pl.ds(pl.multiple_of(block * tile_rows, tile_rows), tile_rows)]
        return pltpu.make_async_copy(ys_ref.at[s], dst, out_sem.at[s])

    def tail_copy(block):
        dst = y_hbm.at[pl.ds(pl.multiple_of(block * tile_rows, tile_rows), tile_rows)]
        return pltpu.make_async_copy(zero_ref, dst, tail_sem.at[0])

    def for_tail_blocks(fn):
        def body(block, carry):
            fn(tail_copy(block))
            return carry
        lax.fori_loop(n_total, y_hbm.shape[0] // tile_rows, body, 0)

    @pl.when(e == 0)
    def _():
        for block in range(ahead):
            start_gather(block)
        zero_ref[...] = jnp.zeros(zero_ref.shape, zero_ref.dtype)
        for_tail_blocks(lambda copy: copy.start())

    @pl.when(stop > first)
    def _():
        wg_bf[...] = wg_ref[0, 0].astype(_BF16)
        wu_bf[...] = wu_ref[0, 0].astype(_BF16)
        wd_bf[...] = wd_ref[0, 0].astype(_BF16)

    def block_step(block, carry):
        @pl.when(block >= n_slots)
        def _():
            out_copy(block - n_slots).wait()

        nxt = block + ahead
        nxt_slot = nxt % n_slots
        nxt_base = jnp.minimum(nxt, n_total - 1) * rows
        per_group = rows // EXPERT_ISSUE_GROUPS

        def issue(group):
            for r in range(group * per_group, (group + 1) * per_group):
                row_copy(row_tok_ref[nxt_base + r], nxt_slot, r).start(priority=1)

        wait_gather(block)
        s = block % n_slots
        x_lo, x_hi = _unpack_bf16_halves(_load_row_tiles(xs_ref.at[s], chunks), _BF16)
        half = x_lo.shape[1]
        gate = _bf16_dot(x_lo, wg_bf[:half, :])
        issue(0)
        gate = gate + _bf16_dot(x_hi, wg_bf[half:, :])
        issue(1)
        up = _bf16_dot(x_lo, wu_bf[:half, :])
        issue(2)
        up = up + _bf16_dot(x_hi, wu_bf[half:, :])
        issue(3)
        act = (jax.nn.silu(gate) * up).astype(_BF16)
        quarter = wd_bf.shape[1] // 4
        y = []
        for c in range(4):
            y.append(_bf16_dot(act, wd_bf[:, c * quarter:(c + 1) * quarter]))
            issue(4 + c)
        packed = jnp.concatenate([_pack_bf16_pair(y[0], y[2]), _pack_bf16_pair(y[1], y[3])], axis=1)
        _store_row_tiles(ys_ref.at[s], packed)
        out_copy(block).start()
        return carry

    lax.fori_loop(first, stop, block_step, 0)

    @pl.when(e == n_experts - 1)
    def _():
        for extra in range(ahead):
            wait_gather(n_total + extra)
        for_tail_blocks(lambda copy: copy.wait())
        for back in range(1, n_slots + 1):
            @pl.when(n_total >= back)
            def _():
                out_copy(n_total - back).wait()


def _expert_mlp(bounds, row_tok, hn, layer, w_gate, w_up, w_down):
    n_experts, d, de = w_gate.shape[1:]
    rows = EXPERT_ROWS
    chunks = d // 2 // LANES
    assert hn.shape[1] == LANES and hn.dtype == jnp.uint32
    assert hn.shape[0] // chunks * EXPERT_TOPK // rows >= EXPERT_SLOTS
    grid_spec = pltpu.PrefetchScalarGridSpec(
        num_scalar_prefetch=2,
        grid=(n_experts,),
        in_specs=[
            pl.BlockSpec(memory_space=pl.ANY),
            pl.BlockSpec((1, 1, d, de), lambda e, bd, rt: (layer, e, 0, 0)),
            pl.BlockSpec((1, 1, d, de), lambda e, bd, rt: (layer, e, 0, 0)),
            pl.BlockSpec((1, 1, de, d), lambda e, bd, rt: (layer, e, 0, 0)),
        ],
        out_specs=pl.BlockSpec(memory_space=pl.ANY),
        scratch_shapes=[
            pltpu.VMEM((EXPERT_SLOTS, rows * chunks, LANES), jnp.uint32),
            pltpu.VMEM((EXPERT_SLOTS, rows * chunks, LANES), jnp.uint32),
            pltpu.VMEM((rows * chunks, LANES), jnp.uint32),
            pltpu.VMEM((d, de), _BF16),
            pltpu.VMEM((d, de), _BF16),
            pltpu.VMEM((de, d), _BF16),
            pltpu.SemaphoreType.DMA((EXPERT_SLOTS,)),
            pltpu.SemaphoreType.DMA((EXPERT_SLOTS,)),
            pltpu.SemaphoreType.DMA((1,)),
        ],
    )
    return pl.pallas_call(
        _expert_body,
        grid_spec=grid_spec,
        out_shape=jax.ShapeDtypeStruct((row_tok.shape[0] * chunks, LANES), jnp.uint32),
        compiler_params=_params("arbitrary"),
        name="expert_mlp",
    )(bounds, row_tok, hn, w_gate, w_up, w_down)


def _combine_body(pos_ref, y_hbm, wts_ref, h_ref, o_ref, ys_ref, sem):
    i = pl.program_id(0)
    n_steps = pl.num_programs(0)
    tm = h_ref.shape[0]
    chunks = h_ref.shape[1] // 2 // LANES
    n_tokens = n_steps * tm
    n_slots = ys_ref.shape[0]
    ahead = n_slots - 1
    slot = i % n_slots

    def row_copy(p, s, k, r):
        src = y_hbm.at[pl.ds(pl.multiple_of(p * chunks, chunks), chunks)]
        return pltpu.make_async_copy(src, ys_ref.at[s, k, pl.ds(r * chunks, chunks)], sem.at[s])

    def start_row(tile, r):
        for k in range(EXPERT_TOPK):
            row_copy(pos_ref[k * n_tokens + tile * tm + r], tile % n_slots, k, r).start(priority=k)

    def wait_gather(s):
        for k in range(EXPERT_TOPK):
            pltpu.make_async_copy(y_hbm.at[pl.ds(0, tm * chunks)], ys_ref.at[s, k], sem.at[s]).wait()

    @pl.when(i == 0)
    def _():
        for tile in range(ahead):
            def body(r, carry):
                start_row(tile, r)
                return carry
            lax.fori_loop(0, tm, body, 0)

    @pl.when(i + ahead < n_steps)
    def _():
        for r in range(tm):
            start_row(i + ahead, r)

    wait_gather(slot)
    w = wts_ref[...]
    half = chunks * LANES
    a_lo, a_hi = _unpack_bf16_halves(_load_row_tiles(ys_ref.at[slot, 0], chunks), _F32)
    b_lo, b_hi = _unpack_bf16_halves(_load_row_tiles(ys_ref.at[slot, 1], chunks), _F32)
    o_ref[:, :half] = h_ref[:, :half] + (w[:, 0:1] * a_lo + w[:, 1:2] * b_lo)
    o_ref[:, half:] = h_ref[:, half:] + (w[:, 0:1] * a_hi + w[:, 1:2] * b_hi)


def _combine(pos, y, wts, h):
    n, d = h.shape
    tm = COMBINE_ROWS
    grid_spec = pltpu.PrefetchScalarGridSpec(
        num_scalar_prefetch=1,
        grid=(n // tm,),
        in_specs=[
            pl.BlockSpec(memory_space=pl.ANY),
            pl.BlockSpec((tm, LANES), lambda i, p: (i, 0)),
            pl.BlockSpec((tm, d), lambda i, p: (i, 0)),
        ],
        out_specs=pl.BlockSpec((tm, d), lambda i, p: (i, 0)),
        scratch_shapes=[
            pltpu.VMEM((COMBINE_SLOTS, EXPERT_TOPK, tm * (d // 2 // LANES), LANES), jnp.uint32),
            pltpu.SemaphoreType.DMA((COMBINE_SLOTS,)),
        ],
    )
    return pl.pallas_call(
        _combine_body,
        grid_spec=grid_spec,
        out_shape=jax.ShapeDtypeStruct((n, d), _F32),
        compiler_params=_params("arbitrary"),
        name="combine",
    )(pos, y, wts, h)


def _dispatch_plan(ids, counts):
    n = ids.shape[1]
    rows = EXPERT_ROWS
    m = n * EXPERT_TOPK
    n_blocks = -(-(m + N_EXPERTS * (rows - 1)) // rows)
    flat_e = ids[:EXPERT_TOPK].reshape(m)
    rank = ids[EXPERT_TOPK:2 * EXPERT_TOPK].reshape(m)
    counts = counts[:N_EXPERTS, 0]
    starts = jnp.cumsum(counts) - counts
    padded = (counts + rows - 1) // rows * rows
    pends = jnp.cumsum(padded)
    pstarts = pends - padded
    pos = rank
    for e in range(N_EXPERTS):
        pos = pos + jnp.where(flat_e == e, pstarts[e], 0)
    blk_start = jnp.arange(n_blocks, dtype=jnp.int32)[:, None] * rows
    in_seg = (pstarts[None, :] <= blk_start) & (blk_start < pends[None, :])
    shift = jnp.sum(jnp.where(in_seg, (starts - pstarts)[None, :], 0), axis=1, keepdims=True)
    seg_end = jnp.sum(jnp.where(in_seg, (starts + counts)[None, :], 0), axis=1, keepdims=True)
    token = jnp.arange(m, dtype=jnp.int32) % n
    order = jnp.argsort(flat_e * n + token)
    sorted_idx = blk_start + shift + jnp.arange(rows, dtype=jnp.int32)[None, :]
    valid = sorted_idx < seg_end
    row_tok = jnp.where(valid, order[jnp.clip(sorted_idx, 0, m - 1)] % n, 0)
    bounds = jnp.concatenate([pstarts, pends[-1:]]) // rows
    return (bounds.astype(jnp.int32), row_tok.reshape(n_blocks * rows).astype(jnp.int32),
            pos.astype(jnp.int32))


def _hier_moe_residual(h, gain, w_rg, w_re, layer, w_gate, w_up, w_down):
    hn, ids, wts, counts = _router(h, gain, w_rg, w_re)
    bounds, row_tok, pos = _dispatch_plan(ids, counts)
    y = _expert_mlp(bounds, row_tok, hn, layer, w_gate, w_up, w_down)
    return _combine(pos, y, wts, h)


def kernel(x, mix_norm, ffn_norm, conv_w_in, conv_w, conv_w_out, kv_norm, w_kv, k_norm, w_q, q_norm,
           w_o, router_group, router_expert, w_gate, w_up, w_down):
    bsz, seq_len, d = x.shape
    depth = mix_norm.shape[0]
    n_conv = conv_w_in.shape[0]
    assert d == N_HEADS * HEAD_DIM
    h = x.reshape(bsz * seq_len, d)
    k_sh = v_sh = None
    for l in range(depth):
        if l < n_conv:
            z = _conv_in(h, mix_norm[l], conv_w_in[l], conv_w[l], seq_len)
            h = _matmul_residual(z, conv_w_out[l], h)
        else:
            j = l - n_conv
            gains = jnp.stack([mix_norm[l], kv_norm])
            head_gains = jnp.stack([q_norm[j], k_norm])
            q, *kv = _qkv_projection(h, gains, w_q[j], w_kv, head_gains, with_kv=(j == 0))
            if j == 0:
                k_sh, v_sh = kv
            attn = _moba_attention(q, k_sh, v_sh, bsz, seq_len)
            h = _matmul_residual(attn, w_o[j], h)
        h = _hier_moe_residual(h, ffn_norm[l], router_group[l], router_expert[l],
                               l, w_gate, w_up, w_down)
    return h.reshape(bsz, seq_len, d)
```

```python
import functools

import jax
import jax.numpy as jnp
from jax import lax
from jax.experimental import pallas as pl
from jax.experimental.pallas import tpu as pltpu

N_HEADS = 16
HEAD_DIM = 128
CONV_WIDTH = 3
MOBA_BLOCK = 256
MOBA_TOPK = 3
N_GROUPS = 4
EXPERTS_PER_GROUP = 8
N_EXPERTS = N_GROUPS * EXPERTS_PER_GROUP
EXPERT_TOPK = 2
NORM_EPS = 1e-6
ALIBI_MAX_BIAS = 8.0

LANES = 128
SUBLANES = 8
VMEM_LIMIT = 56 * 1024 * 1024

ROW_TILE = 1024
RES_COL_TILE = 1024
TRIPLE_COL_TILE = 256
ROUTER_ROWS = 512
EXPERT_ROWS = 128
EXPERT_SLOTS = 6
EXPERT_ISSUE_GROUPS = 8
COMBINE_ROWS = 256
COMBINE_SLOTS = 2
ATTN_HEADS = 8

_BF16 = jnp.bfloat16
_F32 = jnp.float32


def _params(*semantics):
    return pltpu.CompilerParams(dimension_semantics=semantics, vmem_limit_bytes=VMEM_LIMIT)


def _rms_normalise(x, gain):
    ms = jnp.mean(x * x, axis=-1, keepdims=True)
    return x * lax.rsqrt(ms + NORM_EPS) * gain


def _bf16_dot(a, b):
    return jnp.dot(a, b, preferred_element_type=_F32)


def _pack_bf16_halves(x):
    c = x.shape[1] // 2
    return _pack_bf16_pair(x[:, :c], x[:, c:])


def _pack_bf16_pair(lo, hi):
    as_bits = lambda v: lax.bitcast_convert_type(v.astype(_BF16).astype(_F32), jnp.uint32)
    return (as_bits(lo) >> 16) | (as_bits(hi) & jnp.uint32(0xFFFF0000))


def _store_row_tiles(ref, value):
    rows, chunks = value.shape[0], value.shape[1] // LANES
    for c in range(chunks):
        ref[pl.ds(c, rows, stride=chunks), :] = value[:, c * LANES:(c + 1) * LANES]


def _load_row_tiles(ref, chunks):
    rows = ref.shape[0] // chunks
    return jnp.concatenate([ref[pl.ds(c, rows, stride=chunks), :] for c in range(chunks)], axis=1)


def _unpack_bf16_halves(p, dtype):
    lo = lax.bitcast_convert_type(p << 16, _F32).astype(dtype)
    hi = lax.bitcast_convert_type(p & jnp.uint32(0xFFFF0000), _F32).astype(dtype)
    return lo, hi


def _qkv_body(x_ref, g_ref, *refs, with_kv):
    if with_kv:
        wq_ref, wk_ref, wv_ref, hg_ref, q_ref, k_ref, v_ref, xn_ref = refs
    else:
        wq_ref, hg_ref, q_ref, xn_ref = refs

    @pl.when(pl.program_id(1) == 0)
    def _():
        x = x_ref[...]
        normed = x * lax.rsqrt(jnp.mean(x * x, axis=-1, keepdims=True) + NORM_EPS)
        xn_ref[0] = (normed * g_ref[0:1, :]).astype(_BF16)
        if with_kv:
            xn_ref[1] = (normed * g_ref[1:2, :]).astype(_BF16)

    def store_head_normalised(o_ref, y, gain_row):
        head_gain = hg_ref[gain_row:gain_row + 1, :]
        for hh in range(y.shape[1] // HEAD_DIM):
            cols = slice(hh * HEAD_DIM, (hh + 1) * HEAD_DIM)
            o_ref[:, cols] = _rms_normalise(y[:, cols], head_gain).astype(o_ref.dtype)

    products = [_bf16_dot(xn_ref[0], wq_ref[...].astype(_BF16))]
    if with_kv:
        products += [_bf16_dot(xn_ref[1], w_ref[...].astype(_BF16)) for w_ref in (wk_ref, wv_ref)]
    store_head_normalised(q_ref, products[0], 0)
    if with_kv:
        store_head_normalised(k_ref, products[1], 1)
        v_ref[...] = products[2].astype(v_ref.dtype)


def _qkv_projection(x, gains, w_q, w_kv, head_gains, with_kv):
    n, d = x.shape
    tm, tn = ROW_TILE, TRIPLE_COL_TILE
    n_q = w_q.shape[1]
    nj = n_q // tn
    assert n_q % tn == 0 and w_kv.shape[1] == 2 * n_q
    tile = lambda i, j: (i, j)
    fixed = lambda i, j: (0, 0)
    weights = [(w_q, lambda i, j: (0, j))]
    if with_kv:
        weights += [(w_kv, lambda i, j: (0, j)), (w_kv, lambda i, j: (0, j + nj))]
    n_out = len(weights)
    outs = pl.pallas_call(
        functools.partial(_qkv_body, with_kv=with_kv),
        grid=(n // tm, nj),
        in_specs=([pl.BlockSpec((tm, d), lambda i, j: (i, 0)), pl.BlockSpec((2, d), fixed)]
                  + [pl.BlockSpec((d, tn), index_map) for _, index_map in weights]
                  + [pl.BlockSpec((2, HEAD_DIM), fixed)]),
        out_specs=[pl.BlockSpec((tm, tn), tile)] * n_out,
        out_shape=[jax.ShapeDtypeStruct((n, n_q), _BF16)] * n_out,
        scratch_shapes=[pltpu.VMEM((2, tm, d), _BF16)],
        compiler_params=_params("arbitrary", "arbitrary"),
        name="qkv_projection",
    )(x, gains, *[w for w, _ in weights], head_gains)
    return tuple(outs)


def _conv_in_body(x_ref, g_ref, wb_ref, wc_ref, wx_ref, cw_ref, z_ref, xn_ref, carry_ref,
                  *, tiles_per_seq):
    i = pl.program_id(0)
    j = pl.program_id(1)
    tm = x_ref.shape[0]

    @pl.when(j == 0)
    def _():
        xn_ref[...] = _rms_normalise(x_ref[...], g_ref[...]).astype(_BF16)

    @pl.when(i % tiles_per_seq == 0)
    def _():
        carry_ref[j] = jnp.zeros(carry_ref.shape[1:], _F32)

    xn = xn_ref[...]
    b_gate = _bf16_dot(xn, wb_ref[...].astype(_BF16))
    c_gate = _bf16_dot(xn, wc_ref[...].astype(_BF16))
    xh = _bf16_dot(xn, wx_ref[...].astype(_BF16))
    u = c_gate * xh
    prev = carry_ref[j]
    row = lax.broadcasted_iota(jnp.int32, u.shape, 0)
    last = prev[SUBLANES - 1:SUBLANES]
    u1 = jnp.where(row == 0, last, pltpu.roll(u, 1, axis=0))
    u2 = jnp.where(row == 0, prev[SUBLANES - 2:SUBLANES - 1],
                   jnp.where(row == 1, last, pltpu.roll(u, 2, axis=0)))
    cw = cw_ref[...]
    conv = cw[0:1] * u2 + cw[1:2] * u1 + cw[2:3] * u
    z_ref[...] = (b_gate * conv).astype(z_ref.dtype)
    carry_ref[j] = u[tm - SUBLANES:tm]


def _conv_in(x, gain, w_in, conv_w, seq_len):
    n, d = x.shape
    tm, tn = ROW_TILE, TRIPLE_COL_TILE
    nj = d // tn
    assert CONV_WIDTH - 1 <= SUBLANES and seq_len % tm == 0
    return pl.pallas_call(
        functools.partial(_conv_in_body, tiles_per_seq=seq_len // tm),
        grid=(n // tm, nj),
        in_specs=[
            pl.BlockSpec((tm, d), lambda i, j: (i, 0)),
            pl.BlockSpec((1, d), lambda i, j: (0, 0)),
            pl.BlockSpec((d, tn), lambda i, j: (0, j)),
            pl.BlockSpec((d, tn), lambda i, j: (0, j + nj)),
            pl.BlockSpec((d, tn), lambda i, j: (0, j + 2 * nj)),
            pl.BlockSpec((CONV_WIDTH, tn), lambda i, j: (0, j)),
        ],
        out_specs=pl.BlockSpec((tm, tn), lambda i, j: (i, j)),
        out_shape=jax.ShapeDtypeStruct((n, d), _BF16),
        scratch_shapes=[pltpu.VMEM((tm, d), _BF16), pltpu.VMEM((nj, SUBLANES, tn), _F32)],
        compiler_params=_params("arbitrary", "arbitrary"),
        name="conv_in",
    )(x, gain.reshape(1, d), w_in, w_in, w_in, conv_w)


def _matmul_residual_body(x_ref, w_ref, r_ref, o_ref):
    o_ref[...] = r_ref[...] + _bf16_dot(x_ref[...], w_ref[...].astype(_BF16))


def _matmul_residual(x, w, res):
    n, k = x.shape
    d = w.shape[1]
    tm, tn = ROW_TILE, RES_COL_TILE
    return pl.pallas_call(
        _matmul_residual_body,
        grid=(n // tm, d // tn),
        in_specs=[
            pl.BlockSpec((tm, k), lambda i, j: (i, 0)),
            pl.BlockSpec((k, tn), lambda i, j: (0, j)),
            pl.BlockSpec((tm, tn), lambda i, j: (i, j)),
        ],
        out_specs=pl.BlockSpec((tm, tn), lambda i, j: (i, j)),
        out_shape=jax.ShapeDtypeStruct((n, d), _F32),
        compiler_params=_params("arbitrary", "arbitrary"),
        name="matmul_residual",
    )(x, w, res)


def _moba_body(q_ref, k_ref, v_ref, o_ref, km_ref, vt_ref, bias_ref):
    heads = range(q_ref.shape[1] // HEAD_DIM)
    head0 = pl.program_id(1) * len(heads)
    i = pl.program_id(2)
    kb = MOBA_BLOCK
    n_blocks = k_ref.shape[0] // kb
    nt = (((1,), (1,)), ((), ()))
    log2e = 1.4426950408889634
    scale = HEAD_DIM ** -0.5 * log2e
    slopes = [log2e * jnp.exp2(jnp.full((1, kb), -ALIBI_MAX_BIAS / N_HEADS, _F32)
                               * (head0 + hh + 1).astype(_F32)) for hh in heads]

    def lanes(hh):
        return slice(hh * HEAD_DIM, (hh + 1) * HEAD_DIM)

    @pl.when(i == 0)
    def _():
        rel = (lax.broadcasted_iota(jnp.int32, (kb, kb), 1)
               - lax.broadcasted_iota(jnp.int32, (kb, kb), 0))
        for hh in heads:
            for jb in range(n_blocks):
                rows = slice(jb * kb, (jb + 1) * kb)
                km_ref[hh, jb:jb + 1, :] = jnp.mean(k_ref[rows, lanes(hh)].astype(_F32),
                                                    axis=0, keepdims=True)
                vt_ref[hh, :, rows] = v_ref[rows, lanes(hh)].astype(_F32).T.astype(_BF16)
            bias = slopes[hh] * rel.astype(_F32)
            bias_ref[hh, 0] = bias
            bias_ref[hh, 1] = jnp.where(rel >= 0, bias, jnp.inf)

    q = [q_ref[:, lanes(hh)] for hh in heads]
    sel = []
    for hh in heads:
        gate = lax.dot_general(km_ref[hh], q[hh].astype(_F32), nt,
                               precision=lax.Precision.HIGHEST, preferred_element_type=_F32)
        blk = lax.broadcasted_iota(jnp.int32, gate.shape, 0)
        gate = jnp.where(blk < i, gate, -jnp.inf)
        picked = jnp.zeros(gate.shape, _F32)
        for jb in range(n_blocks - 1):
            g_jb = gate[jb:jb + 1, :]
            beats = (gate > g_jb) | ((gate == g_jb) & (blk < jb))
            n_beat = jnp.sum(beats.astype(_F32), axis=0, keepdims=True)
            picked = jnp.where((blk == jb) & (blk < i) & (n_beat < MOBA_TOPK), 1.0, picked)
        sel.append(picked)

    def attend(own):
        nb = own + 1
        raw = [lax.dot_general(k_ref[0:nb * kb, lanes(hh)], q[hh], nt, preferred_element_type=_F32)
               for hh in heads]
        tiles, col_max = [[] for _ in heads], [[] for _ in heads]
        for jb in range(nb):
            for hh in heads:
                tile = raw[hh][jb * kb:(jb + 1) * kb, :] * scale - bias_ref[hh, 1 if jb == own else 0]
                mx = jnp.max(tile, axis=0, keepdims=True)
                if jb != own:
                    mx = jnp.where(sel[hh][jb:jb + 1, :] > 0.5,
                                   mx - slopes[hh] * float((own - jb) * kb), -jnp.inf)
                tiles[hh].append(tile), col_max[hh].append(mx)
        m = [functools.reduce(jnp.maximum, col_max[hh]) for hh in heads]
        probs, l = [[] for _ in heads], [None for _ in heads]
        for jb in range(nb):
            for hh in heads:
                offset = m[hh] if jb == own else jnp.where(
                    sel[hh][jb:jb + 1, :] > 0.5, m[hh] + slopes[hh] * float((own - jb) * kb), jnp.inf)
                p = jnp.exp2(tiles[hh][jb] - offset)
                p_sum = jnp.sum(p, axis=0, keepdims=True)
                l[hh] = p_sum if l[hh] is None else l[hh] + p_sum
                probs[hh].append(p.astype(_BF16))
        for hh in heads:
            p_all = probs[hh][0] if nb == 1 else jnp.concatenate(probs[hh], axis=0)
            acc = _bf16_dot(vt_ref[hh, :, 0:nb * kb], p_all)
            o_ref[:, lanes(hh)] = (acc / l[hh]).T.astype(o_ref.dtype)

    for own in range(n_blocks):
        pl.when(i == own)(functools.partial(attend, own))


def _moba_attention(q, k, v, batch, seq_len):
    n, d = q.shape
    kb = MOBA_BLOCK
    nqb = seq_len // kb
    hp = ATTN_HEADS
    width = hp * HEAD_DIM
    assert seq_len % kb == 0 and nqb == SUBLANES and N_HEADS % hp == 0 and d == N_HEADS * HEAD_DIM
    return pl.pallas_call(
        _moba_body,
        grid=(batch, N_HEADS // hp, nqb),
        in_specs=[
            pl.BlockSpec((kb, width), lambda b, h, i: (b * nqb + i, h)),
            pl.BlockSpec((seq_len, width), lambda b, h, i: (b, h)),
            pl.BlockSpec((seq_len, width), lambda b, h, i: (b, h)),
        ],
        out_specs=pl.BlockSpec((kb, width), lambda b, h, i: (b * nqb + i, h)),
        out_shape=jax.ShapeDtypeStruct((n, d), _BF16),
        scratch_shapes=[pltpu.VMEM((hp, nqb, HEAD_DIM), _F32),
                        pltpu.VMEM((hp, HEAD_DIM, seq_len), _BF16),
                        pltpu.VMEM((hp, 2, kb, kb), _F32)],
        compiler_params=_params("arbitrary", "arbitrary", "arbitrary"),
        name="moba_attention",
    )(q, k, v)


def _router_body(x_ref, g_ref, wr_ref, hn_ref, ids_ref, wts_ref, cnt_ref, run_ref):
    step = pl.program_id(0)
    hn = _rms_normalise(x_ref[...], g_ref[...])
    _store_row_tiles(hn_ref, _pack_bf16_halves(hn))
    w = wr_ref[...]
    hn_hi, w_hi = hn.astype(_BF16), w.astype(_BF16)
    hn_lo = (hn - hn_hi.astype(_F32)).astype(_BF16)
    w_lo = (w - w_hi.astype(_F32)).astype(_BF16)
    logits = (_bf16_dot(hn_hi, w_hi) + (_bf16_dot(hn_hi, w_lo) + _bf16_dot(hn_lo, w_hi))).T
    row = lax.broadcasted_iota(jnp.int32, logits.shape, 0)
    far = jnp.int32(logits.shape[0])

    def first_row(mask):
        return jnp.min(jnp.where(mask, row, far), axis=0, keepdims=True)

    is_group = (row >= N_EXPERTS) & (row < N_EXPERTS + N_GROUPS)
    g_max = jnp.max(jnp.where(is_group, logits, -jnp.inf), axis=0, keepdims=True)
    g_sel = first_row(is_group & (logits == g_max)) - N_EXPERTS
    g_w = 1.0 / jnp.sum(jnp.where(is_group, jnp.exp(logits - g_max), 0.0), axis=0, keepdims=True)

    in_group = (row >= g_sel * EXPERTS_PER_GROUP) & (row < (g_sel + 1) * EXPERTS_PER_GROUP)
    top1 = jnp.max(jnp.where(in_group, logits, -jnp.inf), axis=0, keepdims=True)
    idx1 = first_row(in_group & (logits == top1))
    rest = in_group & (row != idx1)
    top2 = jnp.max(jnp.where(rest, logits, -jnp.inf), axis=0, keepdims=True)
    idx2 = first_row(rest & (logits == top2))
    t = jnp.exp(top2 - top1)
    w1 = 1.0 / (1.0 + t) * g_w
    w2 = t / (1.0 + t) * g_w

    @pl.when(step == 0)
    def _():
        run_ref[...] = jnp.zeros(run_ref.shape, _F32)

    tm = logits.shape[1]
    chosen = jnp.where((row == idx1) | (row == idx2), 1.0, 0.0)
    earlier = (lax.broadcasted_iota(jnp.int32, (tm, tm), 0)
               < lax.broadcasted_iota(jnp.int32, (tm, tm), 1))
    before = run_ref[:, 0:1] + _bf16_dot(chosen.astype(_BF16),
                                         jnp.where(earlier, 1.0, 0.0).astype(_BF16))
    rank1 = jnp.sum(jnp.where(row == idx1, before, 0.0), axis=0, keepdims=True).astype(jnp.int32)
    rank2 = jnp.sum(jnp.where(row == idx2, before, 0.0), axis=0, keepdims=True).astype(jnp.int32)
    run_ref[...] += jnp.sum(chosen, axis=1, keepdims=True)

    out_row = lax.broadcasted_iota(jnp.int32, ids_ref.shape, 0)
    ids_ref[...] = jnp.where(out_row == 0, idx1, jnp.where(out_row == 1, idx2,
                             jnp.where(out_row == 2, rank1, jnp.where(out_row == 3, rank2, 0))))
    wts_ref[...] = jnp.where(row == 0, w1, jnp.where(row == 1, w2, 0.0)).T
    cnt_ref[...] = run_ref[...].astype(jnp.int32)


def _router(x, gain, w_rg, w_re):
    n, d = x.shape
    tm = ROUTER_ROWS
    assert EXPERT_TOPK == 2 and N_EXPERTS + N_GROUPS <= LANES
    chunks = d // 2 // LANES
    wr = jnp.concatenate(
        [w_re, w_rg, jnp.zeros((d, LANES - N_EXPERTS - N_GROUPS), _F32)], axis=1)
    row = lambda i: (i, 0)
    col = lambda i: (0, i)
    fixed = lambda i: (0, 0)
    return pl.pallas_call(
        _router_body,
        grid=(n // tm,),
        in_specs=[
            pl.BlockSpec((tm, d), row),
            pl.BlockSpec((1, d), fixed),
            pl.BlockSpec((d, LANES), fixed),
        ],
        out_specs=[pl.BlockSpec((tm * chunks, LANES), row), pl.BlockSpec((SUBLANES, tm), col),
                   pl.BlockSpec((tm, LANES), row), pl.BlockSpec((LANES, LANES), fixed)],
        out_shape=[jax.ShapeDtypeStruct((n * chunks, LANES), jnp.uint32),
                   jax.ShapeDtypeStruct((SUBLANES, n), jnp.int32),
                   jax.ShapeDtypeStruct((n, LANES), _F32),
                   jax.ShapeDtypeStruct((LANES, LANES), jnp.int32)],
        scratch_shapes=[pltpu.VMEM((LANES, LANES), _F32)],
        compiler_params=_params("arbitrary"),
        name="router",
    )(x, gain.reshape(1, d), wr)


def _expert_body(bounds_ref, blk_base_ref, blk_last_ref, sorted_tok_ref, hn_hbm, wg_ref, wu_ref, wd_ref, y_hbm,
                 xs_ref, ys_ref, zero_ref, wg_bf, wu_bf, wd_bf, gather_sem, out_sem, tail_sem):
    e = pl.program_id(0)
    n_experts = pl.num_programs(0)
    n_slots = xs_ref.shape[0]
    chunks = wg_bf.shape[0] // 2 // LANES
    rows = xs_ref.shape[1] // chunks
    tile_rows = rows * chunks
    ahead = n_slots - 1
    first, stop = bounds_ref[e], bounds_ref[e + 1]
    n_total = bounds_ref[n_experts]

    def row_copy(tok, s, r):
        src = hn_hbm.at[pl.ds(pl.multiple_of(tok * chunks, chunks), chunks)]
        return pltpu.make_async_copy(src, xs_ref.at[s, pl.ds(r * chunks, chunks)], gather_sem.at[s])

    def row_token(base, last, r):
        return sorted_tok_ref[base + jnp.minimum(r, last)]

    def start_gather(block):
        base, last = blk_base_ref[block], blk_last_ref[block]

        def body(r, carry):
            row_copy(row_token(base, last, r), block % n_slots, r).start(priority=1)
            return carry
        lax.fori_loop(0, rows, body, 0)

    def wait_gather(block):
        s = block % n_slots
        pltpu.make_async_copy(hn_hbm.at[pl.ds(0, rows * chunks)], xs_ref.at[s],
                              gather_sem.at[s]).wait()

    def out_copy(block):
        s = block % n_slots
        dst = y_hbm.at[pl.ds(pl.multiple_of(block * tile_rows, tile_rows), tile_rows)]
        return pltpu.make_async_copy(ys_ref.at[s], dst, out_sem.at[s])

    def tail_copy(block):
        dst = y_hbm.at[pl.ds(pl.multiple_of(block * tile_rows, tile_rows), tile_rows)]
        return pltpu.make_async_copy(zero_ref, dst, tail_sem.at[0])

    def for_tail_blocks(fn):
        def body(block, carry):
            fn(tail_copy(block))
            return carry
        lax.fori_loop(n_total, y_hbm.shape[0] // tile_rows, body, 0)

    @pl.when(e == 0)
    def _():
        for block in range(ahead):
            start_gather(block)
        zero_ref[...] = jnp.zeros(zero_ref.shape, zero_ref.dtype)
        for_tail_blocks(lambda copy: copy.start())

    @pl.when(stop > first)
    def _():
        wg_bf[...] = wg_ref[0, 0].astype(_BF16)
        wu_bf[...] = wu_ref[0, 0].astype(_BF16)
        wd_bf[...] = wd_ref[0, 0].astype(_BF16)

    def block_step(block, carry):
        @pl.when(block >= n_slots)
        def _():
            out_copy(block - n_slots).wait()

        nxt = block + ahead
        nxt_slot = nxt % n_slots
        nxt_block = jnp.minimum(nxt, n_total - 1)
        nxt_base, nxt_last = blk_base_ref[nxt_block], blk_last_ref[nxt_block]
        per_group = rows // EXPERT_ISSUE_GROUPS

        def issue(group):
            for r in range(group * per_group, (group + 1) * per_group):
                row_copy(row_token(nxt_base, nxt_last, r), nxt_slot, r).start(priority=1)

        wait_gather(block)
        s = block % n_slots
        x_lo, x_hi = _unpack_bf16_halves(_load_row_tiles(xs_ref.at[s], chunks), _BF16)
        half = x_lo.shape[1]
        gate = _bf16_dot(x_lo, wg_bf[:half, :])
        issue(0)
        gate = gate + _bf16_dot(x_hi, wg_bf[half:, :])
        issue(1)
        up = _bf16_dot(x_lo, wu_bf[:half, :])
        issue(2)
        up = up + _bf16_dot(x_hi, wu_bf[half:, :])
        issue(3)
        act = (jax.nn.silu(gate) * up).astype(_BF16)
        quarter = wd_bf.shape[1] // 4
        y = []
        for c in range(4):
            y.append(_bf16_dot(act, wd_bf[:, c * quarter:(c + 1) * quarter]))
            issue(4 + c)
        packed = jnp.concatenate([_pack_bf16_pair(y[0], y[2]), _pack_bf16_pair(y[1], y[3])], axis=1)
        _store_row_tiles(ys_ref.at[s], packed)
        out_copy(block).start()
        return carry

    lax.fori_loop(first, stop, block_step, 0)

    @pl.when(e == n_experts - 1)
    def _():
        for extra in range(ahead):
            wait_gather(n_total + extra)
        for_tail_blocks(lambda copy: copy.wait())
        for back in range(1, n_slots + 1):
            @pl.when(n_total >= back)
            def _():
                out_copy(n_total - back).wait()


def _expert_mlp(bounds, blk_base, blk_last, sorted_tok, hn, layer, w_gate, w_up, w_down):
    n_experts, d, de = w_gate.shape[1:]
    rows = EXPERT_ROWS
    chunks = d // 2 // LANES
    assert hn.shape[1] == LANES and hn.dtype == jnp.uint32
    assert hn.shape[0] // chunks * EXPERT_TOPK // rows >= EXPERT_SLOTS
    grid_spec = pltpu.PrefetchScalarGridSpec(
        num_scalar_prefetch=4,
        grid=(n_experts,),
        in_specs=[
            pl.BlockSpec(memory_space=pl.ANY),
            pl.BlockSpec((1, 1, d, de), lambda e, *prefetch: (layer, e, 0, 0)),
            pl.BlockSpec((1, 1, d, de), lambda e, *prefetch: (layer, e, 0, 0)),
            pl.BlockSpec((1, 1, de, d), lambda e, *prefetch: (layer, e, 0, 0)),
        ],
        out_specs=pl.BlockSpec(memory_space=pl.ANY),
        scratch_shapes=[
            pltpu.VMEM((EXPERT_SLOTS, rows * chunks, LANES), jnp.uint32),
            pltpu.VMEM((EXPERT_SLOTS, rows * chunks, LANES), jnp.uint32),
            pltpu.VMEM((rows * chunks, LANES), jnp.uint32),
            pltpu.VMEM((d, de), _BF16),
            pltpu.VMEM((d, de), _BF16),
            pltpu.VMEM((de, d), _BF16),
            pltpu.SemaphoreType.DMA((EXPERT_SLOTS,)),
            pltpu.SemaphoreType.DMA((EXPERT_SLOTS,)),
            pltpu.SemaphoreType.DMA((1,)),
        ],
    )
    return pl.pallas_call(
        _expert_body,
        grid_spec=grid_spec,
        out_shape=jax.ShapeDtypeStruct((blk_base.shape[0] * rows * chunks, LANES), jnp.uint32),
        compiler_params=_params("arbitrary"),
        name="expert_mlp",
    )(bounds, blk_base, blk_last, sorted_tok, hn, w_gate, w_up, w_down)


def _combine_body(pos_ref, y_hbm, wts_ref, h_ref, o_ref, ys_ref, sem):
    i = pl.program_id(0)
    n_steps = pl.num_programs(0)
    tm = h_ref.shape[0]
    chunks = h_ref.shape[1] // 2 // LANES
    n_tokens = n_steps * tm
    n_slots = ys_ref.shape[0]
    ahead = n_slots - 1
    slot = i % n_slots

    def row_copy(p, s, k, r):
        src = y_hbm.at[pl.ds(pl.multiple_of(p * chunks, chunks), chunks)]
        return pltpu.make_async_copy(src, ys_ref.at[s, k, pl.ds(r * chunks, chunks)], sem.at[s])

    def start_row(tile, r):
        for k in range(EXPERT_TOPK):
            row_copy(pos_ref[k * n_tokens + tile * tm + r], tile % n_slots, k, r).start(priority=k)

    def wait_gather(s):
        for k in range(EXPERT_TOPK):
            pltpu.make_async_copy(y_hbm.at[pl.ds(0, tm * chunks)], ys_ref.at[s, k], sem.at[s]).wait()

    @pl.when(i == 0)
    def _():
        for tile in range(ahead):
            def body(r, carry):
                start_row(tile, r)
                return carry
            lax.fori_loop(0, tm, body, 0)

    @pl.when(i + ahead < n_steps)
    def _():
        for r in range(tm):
            start_row(i + ahead, r)

    wait_gather(slot)
    w = wts_ref[...]
    half = chunks * LANES
    a_lo, a_hi = _unpack_bf16_halves(_load_row_tiles(ys_ref.at[slot, 0], chunks), _F32)
    b_lo, b_hi = _unpack_bf16_halves(_load_row_tiles(ys_ref.at[slot, 1], chunks), _F32)
    o_ref[:, :half] = h_ref[:, :half] + (w[:, 0:1] * a_lo + w[:, 1:2] * b_lo)
    o_ref[:, half:] = h_ref[:, half:] + (w[:, 0:1] * a_hi + w[:, 1:2] * b_hi)


def _combine(pos, y, wts, h):
    n, d = h.shape
    tm = COMBINE_ROWS
    grid_spec = pltpu.PrefetchScalarGridSpec(
        num_scalar_prefetch=1,
        grid=(n // tm,),
        in_specs=[
            pl.BlockSpec(memory_space=pl.ANY),
            pl.BlockSpec((tm, LANES), lambda i, p: (i, 0)),
            pl.BlockSpec((tm, d), lambda i, p: (i, 0)),
        ],
        out_specs=pl.BlockSpec((tm, d), lambda i, p: (i, 0)),
        scratch_shapes=[
            pltpu.VMEM((COMBINE_SLOTS, EXPERT_TOPK, tm * (d // 2 // LANES), LANES), jnp.uint32),
            pltpu.SemaphoreType.DMA((COMBINE_SLOTS,)),
        ],
    )
    return pl.pallas_call(
        _combine_body,
        grid_spec=grid_spec,
        out_shape=jax.ShapeDtypeStruct((n, d), _F32),
        compiler_params=_params("arbitrary"),
        name="combine",
    )(pos, y, wts, h)


def _dispatch_plan(ids, counts):
    n = ids.shape[1]
    rows = EXPERT_ROWS
    m = n * EXPERT_TOPK
    n_blocks = -(-(m + N_EXPERTS * (rows - 1)) // rows)
    flat_e = ids[:EXPERT_TOPK].reshape(m)
    rank = ids[EXPERT_TOPK:2 * EXPERT_TOPK].reshape(m)
    counts = counts[:N_EXPERTS, 0]
    starts = jnp.cumsum(counts) - counts
    padded = (counts + rows - 1) // rows * rows
    pends = jnp.cumsum(padded)
    pstarts = pends - padded
    pos = rank
    for e in range(N_EXPERTS):
        pos = pos + jnp.where(flat_e == e, pstarts[e], 0)
    blk_start = jnp.arange(n_blocks, dtype=jnp.int32)[:, None] * rows
    in_seg = (pstarts[None, :] <= blk_start) & (blk_start < pends[None, :])
    shift = jnp.sum(jnp.where(in_seg, (starts - pstarts)[None, :], 0), axis=1, keepdims=True)
    seg_end = jnp.sum(jnp.where(in_seg, (starts + counts)[None, :], 0), axis=1, keepdims=True)
    token = jnp.arange(m, dtype=jnp.int32) % n
    _, sorted_tok = lax.sort((flat_e * n + token, token), num_keys=1)
    blk_base = (blk_start + shift)[:, 0]
    blk_last = jnp.clip(seg_end[:, 0] - blk_base, 1, rows) - 1
    bounds = jnp.concatenate([pstarts, pends[-1:]]) // rows
    return (bounds.astype(jnp.int32), blk_base.astype(jnp.int32), blk_last.astype(jnp.int32),
            sorted_tok.astype(jnp.int32), pos.astype(jnp.int32))


def _hier_moe_residual(h, gain, w_rg, w_re, layer, w_gate, w_up, w_down):
    hn, ids, wts, counts = _router(h, gain, w_rg, w_re)
    bounds, blk_base, blk_last, sorted_tok, pos = _dispatch_plan(ids, counts)
    y = _expert_mlp(bounds, blk_base, blk_last, sorted_tok, hn, layer, w_gate, w_up, w_down)
    return _combine(pos, y, wts, h)


def kernel(x, mix_norm, ffn_norm, conv_w_in, conv_w, conv_w_out, kv_norm, w_kv, k_norm, w_q, q_norm,
           w_o, router_group, router_expert, w_gate, w_up, w_down):
    bsz, seq_len, d = x.shape
    depth = mix_norm.shape[0]
    n_conv = conv_w_in.shape[0]
    assert d == N_HEADS * HEAD_DIM
    h = x.reshape(bsz * seq_len, d)
    k_sh = v_sh = None
    for l in range(depth):
        if l < n_conv:
            z = _conv_in(h, mix_norm[l], conv_w_in[l], conv_w[l], seq_len)
            h = _matmul_residual(z, conv_w_out[l], h)
        else:
            j = l - n_conv
            gains = jnp.stack([mix_norm[l], kv_norm])
            head_gains = jnp.stack([q_norm[j], k_norm])
            q, *kv = _qkv_projection(h, gains, w_q[j], w_kv, head_gains, with_kv=(j == 0))
            if j == 0:
                k_sh, v_sh = kv
            attn = _moba_attention(q, k_sh, v_sh, bsz, seq_len)
            h = _matmul_residual(attn, w_o[j], h)
        h = _hier_moe_residual(h, ffn_norm[l], router_group[l], router_expert[l],
                               l, w_gate, w_up, w_down)
    return h.reshape(bsz, seq_len, d)
```

```python
import functools

import jax
import jax.numpy as jnp
from jax import lax
from jax.experimental import pallas as pl
from jax.experimental.pallas import tpu as pltpu

N_HEADS = 16
HEAD_DIM = 128
CONV_WIDTH = 3
MOBA_BLOCK = 256
MOBA_TOPK = 3
N_GROUPS = 4
EXPERTS_PER_GROUP = 8
N_EXPERTS = N_GROUPS * EXPERTS_PER_GROUP
EXPERT_TOPK = 2
NORM_EPS = 1e-6
ALIBI_MAX_BIAS = 8.0

LANES = 128
SUBLANES = 8
VMEM_LIMIT = 56 * 1024 * 1024

ROW_TILE = 1024
RES_COL_TILE = 1024
TRIPLE_COL_TILE = 256
ROUTER_ROWS = 512
EXPERT_ROWS = 128
EXPERT_SLOTS = 6
EXPERT_ISSUE_GROUPS = 8
COMBINE_ROWS = 256
COMBINE_SLOTS = 2
ATTN_HEADS = 8

_BF16 = jnp.bfloat16
_F32 = jnp.float32


def _params(*semantics):
    return pltpu.CompilerParams(dimension_semantics=semantics, vmem_limit_bytes=VMEM_LIMIT)


def _rms_normalise(x, gain):
    ms = jnp.mean(x * x, axis=-1, keepdims=True)
    return x * lax.rsqrt(ms + NORM_EPS) * gain


def _bf16_dot(a, b):
    return jnp.dot(a, b, preferred_element_type=_F32)


def _pack_bf16_halves(x):
    c = x.shape[1] // 2
    return _pack_bf16_pair(x[:, :c], x[:, c:])


def _pack_bf16_pair(lo, hi):
    as_bits = lambda v: lax.bitcast_convert_type(v.astype(_BF16).astype(_F32), jnp.uint32)
    return (as_bits(lo) >> 16) | (as_bits(hi) & jnp.uint32(0xFFFF0000))


def _store_row_tiles(ref, value):
    rows, chunks = value.shape[0], value.shape[1] // LANES
    for c in range(chunks):
        ref[pl.ds(c, rows, stride=chunks), :] = value[:, c * LANES:(c + 1) * LANES]


def _load_row_tiles(ref, chunks):
    rows = ref.shape[0] // chunks
    return jnp.concatenate([ref[pl.ds(c, rows, stride=chunks), :] for c in range(chunks)], axis=1)


def _unpack_bf16_halves(p, dtype):
    lo = lax.bitcast_convert_type(p << 16, _F32).astype(dtype)
    hi = lax.bitcast_convert_type(p & jnp.uint32(0xFFFF0000), _F32).astype(dtype)
    return lo, hi


def _qkv_body(x_ref, g_ref, *refs, with_kv):
    if with_kv:
        wq_ref, wk_ref, wv_ref, hg_ref, q_ref, k_ref, v_ref, xn_ref = refs
    else:
        wq_ref, hg_ref, q_ref, xn_ref = refs

    @pl.when(pl.program_id(1) == 0)
    def _():
        x = x_ref[...]
        normed = x * lax.rsqrt(jnp.mean(x * x, axis=-1, keepdims=True) + NORM_EPS)
        xn_ref[0] = (normed * g_ref[0:1, :]).astype(_BF16)
        if with_kv:
            xn_ref[1] = (normed * g_ref[1:2, :]).astype(_BF16)

    def store_head_normalised(o_ref, y, gain_row):
        head_gain = hg_ref[gain_row:gain_row + 1, :]
        for hh in range(y.shape[1] // HEAD_DIM):
            cols = slice(hh * HEAD_DIM, (hh + 1) * HEAD_DIM)
            o_ref[:, cols] = _rms_normalise(y[:, cols], head_gain).astype(o_ref.dtype)

    products = [_bf16_dot(xn_ref[0], wq_ref[...].astype(_BF16))]
    if with_kv:
        products += [_bf16_dot(xn_ref[1], w_ref[...].astype(_BF16)) for w_ref in (wk_ref, wv_ref)]
    store_head_normalised(q_ref, products[0], 0)
    if with_kv:
        store_head_normalised(k_ref, products[1], 1)
        v_ref[...] = products[2].astype(v_ref.dtype)


def _qkv_projection(x, gains, w_q, w_kv, head_gains, with_kv):
    n, d = x.shape
    tm, tn = ROW_TILE, TRIPLE_COL_TILE
    n_q = w_q.shape[1]
    nj = n_q // tn
    assert n_q % tn == 0 and w_kv.shape[1] == 2 * n_q
    tile = lambda i, j: (i, j)
    fixed = lambda i, j: (0, 0)
    weights = [(w_q, lambda i, j: (0, j))]
    if with_kv:
        weights += [(w_kv, lambda i, j: (0, j)), (w_kv, lambda i, j: (0, j + nj))]
    n_out = len(weights)
    outs = pl.pallas_call(
        functools.partial(_qkv_body, with_kv=with_kv),
        grid=(n // tm, nj),
        in_specs=([pl.BlockSpec((tm, d), lambda i, j: (i, 0)), pl.BlockSpec((2, d), fixed)]
                  + [pl.BlockSpec((d, tn), index_map) for _, index_map in weights]
                  + [pl.BlockSpec((2, HEAD_DIM), fixed)]),
        out_specs=[pl.BlockSpec((tm, tn), tile)] * n_out,
        out_shape=[jax.ShapeDtypeStruct((n, n_q), _BF16)] * n_out,
        scratch_shapes=[pltpu.VMEM((2, tm, d), _BF16)],
        compiler_params=_params("arbitrary", "arbitrary"),
        name="qkv_projection",
    )(x, gains, *[w for w, _ in weights], head_gains)
    return tuple(outs)


def _conv_in_body(x_ref, g_ref, wb_ref, wc_ref, wx_ref, cw_ref, z_ref, xn_ref, carry_ref,
                  *, tiles_per_seq):
    i = pl.program_id(0)
    j = pl.program_id(1)
    tm = x_ref.shape[0]

    @pl.when(j == 0)
    def _():
        xn_ref[...] = _rms_normalise(x_ref[...], g_ref[...]).astype(_BF16)

    @pl.when(i % tiles_per_seq == 0)
    def _():
        carry_ref[j] = jnp.zeros(carry_ref.shape[1:], _F32)

    xn = xn_ref[...]
    b_gate = _bf16_dot(xn, wb_ref[...].astype(_BF16))
    c_gate = _bf16_dot(xn, wc_ref[...].astype(_BF16))
    xh = _bf16_dot(xn, wx_ref[...].astype(_BF16))
    u = c_gate * xh
    prev = carry_ref[j]
    row = lax.broadcasted_iota(jnp.int32, u.shape, 0)
    last = prev[SUBLANES - 1:SUBLANES]
    u1 = jnp.where(row == 0, last, pltpu.roll(u, 1, axis=0))
    u2 = jnp.where(row == 0, prev[SUBLANES - 2:SUBLANES - 1],
                   jnp.where(row == 1, last, pltpu.roll(u, 2, axis=0)))
    cw = cw_ref[...]
    conv = cw[0:1] * u2 + cw[1:2] * u1 + cw[2:3] * u
    z_ref[...] = (b_gate * conv).astype(z_ref.dtype)
    carry_ref[j] = u[tm - SUBLANES:tm]


def _conv_in(x, gain, w_in, conv_w, seq_len):
    n, d = x.shape
    tm, tn = ROW_TILE, TRIPLE_COL_TILE
    nj = d // tn
    assert CONV_WIDTH - 1 <= SUBLANES and seq_len % tm == 0
    return pl.pallas_call(
        functools.partial(_conv_in_body, tiles_per_seq=seq_len // tm),
        grid=(n // tm, nj),
        in_specs=[
            pl.BlockSpec((tm, d), lambda i, j: (i, 0)),
            pl.BlockSpec((1, d), lambda i, j: (0, 0)),
            pl.BlockSpec((d, tn), lambda i, j: (0, j)),
            pl.BlockSpec((d, tn), lambda i, j: (0, j + nj)),
            pl.BlockSpec((d, tn), lambda i, j: (0, j + 2 * nj)),
            pl.BlockSpec((CONV_WIDTH, tn), lambda i, j: (0, j)),
        ],
        out_specs=pl.BlockSpec((tm, tn), lambda i, j: (i, j)),
        out_shape=jax.ShapeDtypeStruct((n, d), _BF16),
        scratch_shapes=[pltpu.VMEM((tm, d), _BF16), pltpu.VMEM((nj, SUBLANES, tn), _F32)],
        compiler_params=_params("arbitrary", "arbitrary"),
        name="conv_in",
    )(x, gain.reshape(1, d), w_in, w_in, w_in, conv_w)


def _matmul_residual_body(x_ref, w_ref, r_ref, o_ref):
    o_ref[...] = r_ref[...] + _bf16_dot(x_ref[...], w_ref[...].astype(_BF16))


def _matmul_residual(x, w, res):
    n, k = x.shape
    d = w.shape[1]
    tm, tn = ROW_TILE, RES_COL_TILE
    return pl.pallas_call(
        _matmul_residual_body,
        grid=(n // tm, d // tn),
        in_specs=[
            pl.BlockSpec((tm, k), lambda i, j: (i, 0)),
            pl.BlockSpec((k, tn), lambda i, j: (0, j)),
            pl.BlockSpec((tm, tn), lambda i, j: (i, j)),
        ],
        out_specs=pl.BlockSpec((tm, tn), lambda i, j: (i, j)),
        out_shape=jax.ShapeDtypeStruct((n, d), _F32),
        compiler_params=_params("arbitrary", "arbitrary"),
        name="matmul_residual",
    )(x, w, res)


def _moba_body(q_ref, k_ref, v_ref, o_ref, km_ref, vt_ref, bias_ref):
    heads = range(q_ref.shape[1] // HEAD_DIM)
    head0 = pl.program_id(1) * len(heads)
    i = pl.program_id(2)
    kb = MOBA_BLOCK
    n_blocks = k_ref.shape[0] // kb
    nt = (((1,), (1,)), ((), ()))
    log2e = 1.4426950408889634
    scale = HEAD_DIM ** -0.5 * log2e
    slopes = [log2e * jnp.exp2(jnp.full((1, kb), -ALIBI_MAX_BIAS / N_HEADS, _F32)
                               * (head0 + hh + 1).astype(_F32)) for hh in heads]

    def lanes(hh):
        return slice(hh * HEAD_DIM, (hh + 1) * HEAD_DIM)

    @pl.when(i == 0)
    def _():
        rel = (lax.broadcasted_iota(jnp.int32, (kb, kb), 1)
               - lax.broadcasted_iota(jnp.int32, (kb, kb), 0))
        for hh in heads:
            for jb in range(n_blocks):
                rows = slice(jb * kb, (jb + 1) * kb)
                km_ref[hh, jb:jb + 1, :] = jnp.mean(k_ref[rows, lanes(hh)].astype(_F32),
                                                    axis=0, keepdims=True)
                vt_ref[hh, :, rows] = v_ref[rows, lanes(hh)].astype(_F32).T.astype(_BF16)
            bias = slopes[hh] * rel.astype(_F32)
            bias_ref[hh, 0] = bias
            bias_ref[hh, 1] = jnp.where(rel >= 0, bias, jnp.inf)

    q = [q_ref[:, lanes(hh)] for hh in heads]
    sel = []
    for hh in heads:
        gate, rest = None, km_ref[hh]
        for _ in range(3):
            term = rest.astype(_BF16)
            rest = rest - term.astype(_F32)
            part = lax.dot_general(term, q[hh], nt, preferred_element_type=_F32)
            gate = part if gate is None else gate + part
        blk = lax.broadcasted_iota(jnp.int32, gate.shape, 0)
        gate = jnp.where(blk < i, gate, -jnp.inf)
        picked = jnp.zeros(gate.shape, _F32)
        for jb in range(n_blocks - 1):
            g_jb = gate[jb:jb + 1, :]
            beats = (gate > g_jb) | ((gate == g_jb) & (blk < jb))
            n_beat = jnp.sum(beats.astype(_F32), axis=0, keepdims=True)
            picked = jnp.where((blk == jb) & (blk < i) & (n_beat < MOBA_TOPK), 1.0, picked)
        sel.append(picked)

    def attend(own):
        nb = own + 1
        raw = [lax.dot_general(k_ref[0:nb * kb, lanes(hh)], q[hh], nt, preferred_element_type=_F32)
               for hh in heads]
        tiles, col_max = [[] for _ in heads], [[] for _ in heads]
        for jb in range(nb):
            for hh in heads:
                tile = raw[hh][jb * kb:(jb + 1) * kb, :] * scale - bias_ref[hh, 1 if jb == own else 0]
                mx = jnp.max(tile, axis=0, keepdims=True)
                if jb != own:
                    mx = jnp.where(sel[hh][jb:jb + 1, :] > 0.5,
                                   mx - slopes[hh] * float((own - jb) * kb), -jnp.inf)
                tiles[hh].append(tile), col_max[hh].append(mx)
        m = [functools.reduce(jnp.maximum, col_max[hh]) for hh in heads]
        probs, l = [[] for _ in heads], [None for _ in heads]
        for jb in range(nb):
            for hh in heads:
                offset = m[hh] if jb == own else jnp.where(
                    sel[hh][jb:jb + 1, :] > 0.5, m[hh] + slopes[hh] * float((own - jb) * kb), jnp.inf)
                p = jnp.exp2(tiles[hh][jb] - offset)
                p_sum = jnp.sum(p, axis=0, keepdims=True)
                l[hh] = p_sum if l[hh] is None else l[hh] + p_sum
                probs[hh].append(p.astype(_BF16))
        for hh in heads:
            p_all = probs[hh][0] if nb == 1 else jnp.concatenate(probs[hh], axis=0)
            acc = _bf16_dot(vt_ref[hh, :, 0:nb * kb], p_all)
            o_ref[:, lanes(hh)] = (acc / l[hh]).T.astype(o_ref.dtype)

    for own in range(n_blocks):
        pl.when(i == own)(functools.partial(attend, own))


def _moba_attention(q, k, v, batch, seq_len):
    n, d = q.shape
    kb = MOBA_BLOCK
    nqb = seq_len // kb
    hp = ATTN_HEADS
    width = hp * HEAD_DIM
    assert seq_len % kb == 0 and nqb == SUBLANES and N_HEADS % hp == 0 and d == N_HEADS * HEAD_DIM
    return pl.pallas_call(
        _moba_body,
        grid=(batch, N_HEADS // hp, nqb),
        in_specs=[
            pl.BlockSpec((kb, width), lambda b, h, i: (b * nqb + i, h)),
            pl.BlockSpec((seq_len, width), lambda b, h, i: (b, h)),
            pl.BlockSpec((seq_len, width), lambda b, h, i: (b, h)),
        ],
        out_specs=pl.BlockSpec((kb, width), lambda b, h, i: (b * nqb + i, h)),
        out_shape=jax.ShapeDtypeStruct((n, d), _BF16),
        scratch_shapes=[pltpu.VMEM((hp, nqb, HEAD_DIM), _F32),
                        pltpu.VMEM((hp, HEAD_DIM, seq_len), _BF16),
                        pltpu.VMEM((hp, 2, kb, kb), _F32)],
        compiler_params=_params("arbitrary", "arbitrary", "arbitrary"),
        name="moba_attention",
    )(q, k, v)


def _router_body(x_ref, g_ref, wr_ref, hn_ref, ids_ref, wts_ref, cnt_ref, run_ref):
    step = pl.program_id(0)
    hn = _rms_normalise(x_ref[...], g_ref[...])
    _store_row_tiles(hn_ref, _pack_bf16_halves(hn))
    w = wr_ref[...]
    hn_hi, w_hi = hn.astype(_BF16), w.astype(_BF16)
    hn_lo = (hn - hn_hi.astype(_F32)).astype(_BF16)
    w_lo = (w - w_hi.astype(_F32)).astype(_BF16)
    logits = (_bf16_dot(hn_hi, w_hi) + (_bf16_dot(hn_hi, w_lo) + _bf16_dot(hn_lo, w_hi))).T
    row = lax.broadcasted_iota(jnp.int32, logits.shape, 0)
    far = jnp.int32(logits.shape[0])

    def first_row(mask):
        return jnp.min(jnp.where(mask, row, far), axis=0, keepdims=True)

    is_group = (row >= N_EXPERTS) & (row < N_EXPERTS + N_GROUPS)
    g_max = jnp.max(jnp.where(is_group, logits, -jnp.inf), axis=0, keepdims=True)
    g_sel = first_row(is_group & (logits == g_max)) - N_EXPERTS
    g_w = 1.0 / jnp.sum(jnp.where(is_group, jnp.exp(logits - g_max), 0.0), axis=0, keepdims=True)

    in_group = (row >= g_sel * EXPERTS_PER_GROUP) & (row < (g_sel + 1) * EXPERTS_PER_GROUP)
    top1 = jnp.max(jnp.where(in_group, logits, -jnp.inf), axis=0, keepdims=True)
    idx1 = first_row(in_group & (logits == top1))
    rest = in_group & (row != idx1)
    top2 = jnp.max(jnp.where(rest, logits, -jnp.inf), axis=0, keepdims=True)
    idx2 = first_row(rest & (logits == top2))
    t = jnp.exp(top2 - top1)
    w1 = 1.0 / (1.0 + t) * g_w
    w2 = t / (1.0 + t) * g_w

    @pl.when(step == 0)
    def _():
        run_ref[...] = jnp.zeros(run_ref.shape, _F32)

    tm = logits.shape[1]
    chosen = jnp.where((row == idx1) | (row == idx2), 1.0, 0.0)
    earlier = (lax.broadcasted_iota(jnp.int32, (tm, tm), 0)
               < lax.broadcasted_iota(jnp.int32, (tm, tm), 1))
    before = run_ref[:, 0:1] + _bf16_dot(chosen.astype(_BF16),
                                         jnp.where(earlier, 1.0, 0.0).astype(_BF16))
    rank1 = jnp.sum(jnp.where(row == idx1, before, 0.0), axis=0, keepdims=True).astype(jnp.int32)
    rank2 = jnp.sum(jnp.where(row == idx2, before, 0.0), axis=0, keepdims=True).astype(jnp.int32)
    run_ref[...] += jnp.sum(chosen, axis=1, keepdims=True)

    out_row = lax.broadcasted_iota(jnp.int32, ids_ref.shape, 0)
    ids_ref[...] = jnp.where(out_row == 0, idx1, jnp.where(out_row == 1, idx2,
                             jnp.where(out_row == 2, rank1, jnp.where(out_row == 3, rank2, 0))))
    wts_ref[...] = jnp.where(row == 0, w1, jnp.where(row == 1, w2, 0.0)).T
    cnt_ref[...] = run_ref[...].astype(jnp.int32)


def _router(x, gain, w_rg, w_re):
    n, d = x.shape
    tm = ROUTER_ROWS
    assert EXPERT_TOPK == 2 and N_EXPERTS + N_GROUPS <= LANES
    chunks = d // 2 // LANES
    wr = jnp.concatenate(
        [w_re, w_rg, jnp.zeros((d, LANES - N_EXPERTS - N_GROUPS), _F32)], axis=1)
    row = lambda i: (i, 0)
    col = lambda i: (0, i)
    fixed = lambda i: (0, 0)
    return pl.pallas_call(
        _router_body,
        grid=(n // tm,),
        in_specs=[
            pl.BlockSpec((tm, d), row),
            pl.BlockSpec((1, d), fixed),
            pl.BlockSpec((d, LANES), fixed),
        ],
        out_specs=[pl.BlockSpec((tm * chunks, LANES), row), pl.BlockSpec((SUBLANES, tm), col),
                   pl.BlockSpec((tm, LANES), row), pl.BlockSpec((LANES, LANES), fixed)],
        out_shape=[jax.ShapeDtypeStruct((n * chunks, LANES), jnp.uint32),
                   jax.ShapeDtypeStruct((SUBLANES, n), jnp.int32),
                   jax.ShapeDtypeStruct((n, LANES), _F32),
                   jax.ShapeDtypeStruct((LANES, LANES), jnp.int32)],
        scratch_shapes=[pltpu.VMEM((LANES, LANES), _F32)],
        compiler_params=_params("arbitrary"),
        name="router",
    )(x, gain.reshape(1, d), wr)


def _expert_body(bounds_ref, blk_base_ref, blk_last_ref, sorted_tok_ref, hn_hbm, wg_ref, wu_ref, wd_ref, y_hbm,
                 xs_ref, ys_ref, zero_ref, wg_bf, wu_bf, wd_bf, gather_sem, out_sem, tail_sem):
    e = pl.program_id(0)
    n_experts = pl.num_programs(0)
    n_slots = xs_ref.shape[0]
    chunks = wg_bf.shape[0] // 2 // LANES
    rows = xs_ref.shape[1] // chunks
    tile_rows = rows * chunks
    ahead = n_slots - 1
    first, stop = bounds_ref[e], bounds_ref[e + 1]
    n_total = bounds_ref[n_experts]

    def row_copy(tok, s, r):
        src = hn_hbm.at[pl.ds(pl.multiple_of(tok * chunks, chunks), chunks)]
        return pltpu.make_async_copy(src, xs_ref.at[s, pl.ds(r * chunks, chunks)], gather_sem.at[s])

    def row_token(base, last, r):
        return sorted_tok_ref[base + jnp.minimum(r, last)]

    def start_gather(block):
        base, last = blk_base_ref[block], blk_last_ref[block]

        def body(r, carry):
            row_copy(row_token(base, last, r), block % n_slots, r).start(priority=1)
            return carry
        lax.fori_loop(0, rows, body, 0)

    def wait_gather(block):
        s = block % n_slots
        pltpu.make_async_copy(hn_hbm.at[pl.ds(0, rows * chunks)], xs_ref.at[s],
                              gather_sem.at[s]).wait()

    def out_copy(block):
        s = block % n_slots
        dst = y_hbm.at[pl.ds(pl.multiple_of(block * tile_rows, tile_rows), tile_rows)]
        return pltpu.make_async_copy(ys_ref.at[s], dst, out_sem.at[s])

    def tail_copy(block):
        dst = y_hbm.at[pl.ds(pl.multiple_of(block * tile_rows, tile_rows), tile_rows)]
        return pltpu.make_async_copy(zero_ref, dst, tail_sem.at[0])

    def for_tail_blocks(fn):
        def body(block, carry):
            fn(tail_copy(block))
            return carry
        lax.fori_loop(n_total, y_hbm.shape[0] // tile_rows, body, 0)

    @pl.when(e == 0)
    def _():
        for block in range(ahead):
            start_gather(block)
        zero_ref[...] = jnp.zeros(zero_ref.shape, zero_ref.dtype)
        for_tail_blocks(lambda copy: copy.start())

    @pl.when(stop > first)
    def _():
        wg_bf[...] = wg_ref[0, 0].astype(_BF16)
        wu_bf[...] = wu_ref[0, 0].astype(_BF16)
        wd_bf[...] = wd_ref[0, 0].astype(_BF16)

    def block_step(block, carry):
        @pl.when(block >= n_slots)
        def _():
            out_copy(block - n_slots).wait()

        nxt = block + ahead
        nxt_slot = nxt % n_slots
        nxt_block = jnp.minimum(nxt, n_total - 1)
        nxt_base, nxt_last = blk_base_ref[nxt_block], blk_last_ref[nxt_block]
        per_group = rows // EXPERT_ISSUE_GROUPS

        def issue(group):
            for r in range(group * per_group, (group + 1) * per_group):
                row_copy(row_token(nxt_base, nxt_last, r), nxt_slot, r).start(priority=1)

        wait_gather(block)
        s = block % n_slots
        x_lo, x_hi = _unpack_bf16_halves(_load_row_tiles(xs_ref.at[s], chunks), _BF16)
        half = x_lo.shape[1]
        gate = _bf16_dot(x_lo, wg_bf[:half, :])
        issue(0)
        gate = gate + _bf16_dot(x_hi, wg_bf[half:, :])
        issue(1)
        up = _bf16_dot(x_lo, wu_bf[:half, :])
        issue(2)
        up = up + _bf16_dot(x_hi, wu_bf[half:, :])
        issue(3)
        act = (jax.nn.silu(gate) * up).astype(_BF16)
        quarter = wd_bf.shape[1] // 4
        y = []
        for c in range(4):
            y.append(_bf16_dot(act, wd_bf[:, c * quarter:(c + 1) * quarter]))
            issue(4 + c)
        packed = jnp.concatenate([_pack_bf16_pair(y[0], y[2]), _pack_bf16_pair(y[1], y[3])], axis=1)
        _store_row_tiles(ys_ref.at[s], packed)
        out_copy(block).start()
        return carry

    lax.fori_loop(first, stop, block_step, 0)

    @pl.when(e == n_experts - 1)
    def _():
        for extra in range(ahead):
            wait_gather(n_total + extra)
        for_tail_blocks(lambda copy: copy.wait())
        for back in range(1, n_slots + 1):
            @pl.when(n_total >= back)
            def _():
                out_copy(n_total - back).wait()


def _expert_mlp(bounds, blk_base, blk_last, sorted_tok, hn, layer, w_gate, w_up, w_down):
    n_experts, d, de = w_gate.shape[1:]
    rows = EXPERT_ROWS
    chunks = d // 2 // LANES
    assert hn.shape[1] == LANES and hn.dtype == jnp.uint32
    assert hn.shape[0] // chunks * EXPERT_TOPK // rows >= EXPERT_SLOTS
    grid_spec = pltpu.PrefetchScalarGridSpec(
        num_scalar_prefetch=4,
        grid=(n_experts,),
        in_specs=[
            pl.BlockSpec(memory_space=pl.ANY),
            pl.BlockSpec((1, 1, d, de), lambda e, *prefetch: (layer, e, 0, 0)),
            pl.BlockSpec((1, 1, d, de), lambda e, *prefetch: (layer, e, 0, 0)),
            pl.BlockSpec((1, 1, de, d), lambda e, *prefetch: (layer, e, 0, 0)),
        ],
        out_specs=pl.BlockSpec(memory_space=pl.ANY),
        scratch_shapes=[
            pltpu.VMEM((EXPERT_SLOTS, rows * chunks, LANES), jnp.uint32),
            pltpu.VMEM((EXPERT_SLOTS, rows * chunks, LANES), jnp.uint32),
            pltpu.VMEM((rows * chunks, LANES), jnp.uint32),
            pltpu.VMEM((d, de), _BF16),
            pltpu.VMEM((d, de), _BF16),
            pltpu.VMEM((de, d), _BF16),
            pltpu.SemaphoreType.DMA((EXPERT_SLOTS,)),
            pltpu.SemaphoreType.DMA((EXPERT_SLOTS,)),
            pltpu.SemaphoreType.DMA((1,)),
        ],
    )
    return pl.pallas_call(
        _expert_body,
        grid_spec=grid_spec,
        out_shape=jax.ShapeDtypeStruct((blk_base.shape[0] * rows * chunks, LANES), jnp.uint32),
        compiler_params=_params("arbitrary"),
        name="expert_mlp",
    )(bounds, blk_base, blk_last, sorted_tok, hn, w_gate, w_up, w_down)


def _combine_body(pos_ref, y_hbm, wts_ref, h_ref, o_ref, ys_ref, sem):
    i = pl.program_id(0)
    n_steps = pl.num_programs(0)
    tm = h_ref.shape[0]
    chunks = h_ref.shape[1] // 2 // LANES
    n_tokens = n_steps * tm
    n_slots = ys_ref.shape[0]
    ahead = n_slots - 1
    slot = i % n_slots

    def row_copy(p, s, k, r):
        src = y_hbm.at[pl.ds(pl.multiple_of(p * chunks, chunks), chunks)]
        return pltpu.make_async_copy(src, ys_ref.at[s, k, pl.ds(r * chunks, chunks)], sem.at[s])

    def start_row(tile, r):
        for k in range(EXPERT_TOPK):
            row_copy(pos_ref[k * n_tokens + tile * tm + r], tile % n_slots, k, r).start(priority=k)

    def wait_gather(s):
        for k in range(EXPERT_TOPK):
            pltpu.make_async_copy(y_hbm.at[pl.ds(0, tm * chunks)], ys_ref.at[s, k], sem.at[s]).wait()

    @pl.when(i == 0)
    def _():
        for tile in range(ahead):
            def body(r, carry):
                start_row(tile, r)
                return carry
            lax.fori_loop(0, tm, body, 0)

    @pl.when(i + ahead < n_steps)
    def _():
        for r in range(tm):
            start_row(i + ahead, r)

    wait_gather(slot)
    w = wts_ref[...]
    half = chunks * LANES
    a_lo, a_hi = _unpack_bf16_halves(_load_row_tiles(ys_ref.at[slot, 0], chunks), _F32)
    b_lo, b_hi = _unpack_bf16_halves(_load_row_tiles(ys_ref.at[slot, 1], chunks), _F32)
    o_ref[:, :half] = h_ref[:, :half] + (w[:, 0:1] * a_lo + w[:, 1:2] * b_lo)
    o_ref[:, half:] = h_ref[:, half:] + (w[:, 0:1] * a_hi + w[:, 1:2] * b_hi)


def _combine(pos, y, wts, h):
    n, d = h.shape
    tm = COMBINE_ROWS
    grid_spec = pltpu.PrefetchScalarGridSpec(
        num_scalar_prefetch=1,
        grid=(n // tm,),
        in_specs=[
            pl.BlockSpec(memory_space=pl.ANY),
            pl.BlockSpec((tm, LANES), lambda i, p: (i, 0)),
            pl.BlockSpec((tm, d), lambda i, p: (i, 0)),
        ],
        out_specs=pl.BlockSpec((tm, d), lambda i, p: (i, 0)),
        scratch_shapes=[
            pltpu.VMEM((COMBINE_SLOTS, EXPERT_TOPK, tm * (d // 2 // LANES), LANES), jnp.uint32),
            pltpu.SemaphoreType.DMA((COMBINE_SLOTS,)),
        ],
    )
    return pl.pallas_call(
        _combine_body,
        grid_spec=grid_spec,
        out_shape=jax.ShapeDtypeStruct((n, d), _F32),
        compiler_params=_params("arbitrary"),
        name="combine",
    )(pos, y, wts, h)


def _dispatch_plan(ids, counts):
    n = ids.shape[1]
    rows = EXPERT_ROWS
    m = n * EXPERT_TOPK
    n_blocks = -(-(m + N_EXPERTS * (rows - 1)) // rows)
    flat_e = ids[:EXPERT_TOPK].reshape(m)
    rank = ids[EXPERT_TOPK:2 * EXPERT_TOPK].reshape(m)
    counts = counts[:N_EXPERTS, 0]
    starts = jnp.cumsum(counts) - counts
    padded = (counts + rows - 1) // rows * rows
    pends = jnp.cumsum(padded)
    pstarts = pends - padded
    pos = rank
    for e in range(N_EXPERTS):
        pos = pos + jnp.where(flat_e == e, pstarts[e], 0)
    blk_start = jnp.arange(n_blocks, dtype=jnp.int32)[:, None] * rows
    in_seg = (pstarts[None, :] <= blk_start) & (blk_start < pends[None, :])
    shift = jnp.sum(jnp.where(in_seg, (starts - pstarts)[None, :], 0), axis=1, keepdims=True)
    seg_end = jnp.sum(jnp.where(in_seg, (starts + counts)[None, :], 0), axis=1, keepdims=True)
    token = jnp.arange(m, dtype=jnp.int32) % n
    _, sorted_tok = lax.sort((flat_e * n + token, token), num_keys=1)
    blk_base = (blk_start + shift)[:, 0]
    blk_last = jnp.clip(seg_end[:, 0] - blk_base, 1, rows) - 1
    bounds = jnp.concatenate([pstarts, pends[-1:]]) // rows
    return (bounds.astype(jnp.int32), blk_base.astype(jnp.int32), blk_last.astype(jnp.int32),
            sorted_tok.astype(jnp.int32), pos.astype(jnp.int32))


def _hier_moe_residual(h, gain, w_rg, w_re, layer, w_gate, w_up, w_down):
    hn, ids, wts, counts = _router(h, gain, w_rg, w_re)
    bounds, blk_base, blk_last, sorted_tok, pos = _dispatch_plan(ids, counts)
    y = _expert_mlp(bounds, blk_base, blk_last, sorted_tok, hn, layer, w_gate, w_up, w_down)
    return _combine(pos, y, wts, h)


def kernel(x, mix_norm, ffn_norm, conv_w_in, conv_w, conv_w_out, kv_norm, w_kv, k_norm, w_q, q_norm,
           w_o, router_group, router_expert, w_gate, w_up, w_down):
    bsz, seq_len, d = x.shape
    depth = mix_norm.shape[0]
    n_conv = conv_w_in.shape[0]
    assert d == N_HEADS * HEAD_DIM
    h = x.reshape(bsz * seq_len, d)
    k_sh = v_sh = None
    for l in range(depth):
        if l < n_conv:
            z = _conv_in(h, mix_norm[l], conv_w_in[l], conv_w[l], seq_len)
            h = _matmul_residual(z, conv_w_out[l], h)
        else:
            j = l - n_conv
            gains = jnp.stack([mix_norm[l], kv_norm])
            head_gains = jnp.stack([q_norm[j], k_norm])
            q, *kv = _qkv_projection(h, gains, w_q[j], w_kv, head_gains, with_kv=(j == 0))
            if j == 0:
                k_sh, v_sh = kv
            attn = _moba_attention(q, k_sh, v_sh, bsz, seq_len)
            h = _matmul_residual(attn, w_o[j], h)
        h = _hier_moe_residual(h, ffn_norm[l], router_group[l], router_expert[l],
                               l, w_gate, w_up, w_down)
    return h.reshape(bsz, seq_len, d)
```

```python
import functools

import jax
import jax.numpy as jnp
from jax import lax
from jax.experimental import pallas as pl
from jax.experimental.pallas import tpu as pltpu

N_HEADS = 16
HEAD_DIM = 128
CONV_WIDTH = 3
MOBA_BLOCK = 256
MOBA_TOPK = 3
N_GROUPS = 4
EXPERTS_PER_GROUP = 8
N_EXPERTS = N_GROUPS * EXPERTS_PER_GROUP
EXPERT_TOPK = 2
NORM_EPS = 1e-6
ALIBI_MAX_BIAS = 8.0

LANES = 128
SUBLANES = 8
VMEM_LIMIT = 56 * 1024 * 1024

ROW_TILE = 1024
RES_ROW_TILE = 2048
RES_COL_TILE = 512
TRIPLE_COL_TILE = 256
ROUTER_ROWS = 512
EXPERT_ROWS = 128
EXPERT_SLOTS = 6
EXPERT_ISSUE_GROUPS = 8
COMBINE_ROWS = 256
COMBINE_SLOTS = 2
ATTN_HEADS = 8

_BF16 = jnp.bfloat16
_F32 = jnp.float32


def _params(*semantics):
    return pltpu.CompilerParams(dimension_semantics=semantics, vmem_limit_bytes=VMEM_LIMIT)


def _rms_normalise(x, gain):
    ms = jnp.mean(x * x, axis=-1, keepdims=True)
    return x * lax.rsqrt(ms + NORM_EPS) * gain


def _bf16_dot(a, b):
    return jnp.dot(a, b, preferred_element_type=_F32)


def _pack_bf16_halves(x):
    c = x.shape[1] // 2
    return _pack_bf16_pair(x[:, :c], x[:, c:])


def _pack_bf16_pair(lo, hi):
    as_bits = lambda v: lax.bitcast_convert_type(v.astype(_BF16).astype(_F32), jnp.uint32)
    return (as_bits(lo) >> 16) | (as_bits(hi) & jnp.uint32(0xFFFF0000))


def _store_row_tiles(ref, value):
    rows, chunks = value.shape[0], value.shape[1] // LANES
    for c in range(chunks):
        ref[pl.ds(c, rows, stride=chunks), :] = value[:, c * LANES:(c + 1) * LANES]


def _load_row_tiles(ref, chunks):
    rows = ref.shape[0] // chunks
    return jnp.concatenate([ref[pl.ds(c, rows, stride=chunks), :] for c in range(chunks)], axis=1)


def _unpack_bf16_halves(p, dtype):
    lo = lax.bitcast_convert_type(p << 16, _F32).astype(dtype)
    hi = lax.bitcast_convert_type(p & jnp.uint32(0xFFFF0000), _F32).astype(dtype)
    return lo, hi


def _qkv_body(x_ref, g_ref, *refs, with_kv):
    if with_kv:
        wq_ref, wk_ref, wv_ref, hg_ref, q_ref, k_ref, v_ref, xn_ref = refs
    else:
        wq_ref, hg_ref, q_ref, xn_ref = refs

    @pl.when(pl.program_id(1) == 0)
    def _():
        x = x_ref[...]
        normed = x * lax.rsqrt(jnp.mean(x * x, axis=-1, keepdims=True) + NORM_EPS)
        xn_ref[0] = (normed * g_ref[0:1, :]).astype(_BF16)
        if with_kv:
            xn_ref[1] = (normed * g_ref[1:2, :]).astype(_BF16)

    def store_head_normalised(o_ref, y, gain_row):
        head_gain = hg_ref[gain_row:gain_row + 1, :]
        for hh in range(y.shape[1] // HEAD_DIM):
            cols = slice(hh * HEAD_DIM, (hh + 1) * HEAD_DIM)
            o_ref[:, cols] = _rms_normalise(y[:, cols], head_gain).astype(o_ref.dtype)

    products = [_bf16_dot(xn_ref[0], wq_ref[...].astype(_BF16))]
    if with_kv:
        products += [_bf16_dot(xn_ref[1], w_ref[...].astype(_BF16)) for w_ref in (wk_ref, wv_ref)]
    store_head_normalised(q_ref, products[0], 0)
    if with_kv:
        store_head_normalised(k_ref, products[1], 1)
        v_ref[...] = products[2].astype(v_ref.dtype)


def _qkv_projection(x, gains, w_q, w_kv, head_gains, with_kv):
    n, d = x.shape
    tm, tn = ROW_TILE, TRIPLE_COL_TILE
    n_q = w_q.shape[1]
    nj = n_q // tn
    assert n_q % tn == 0 and w_kv.shape[1] == 2 * n_q
    tile = lambda i, j: (i, j)
    fixed = lambda i, j: (0, 0)
    weights = [(w_q, lambda i, j: (0, j))]
    if with_kv:
        weights += [(w_kv, lambda i, j: (0, j)), (w_kv, lambda i, j: (0, j + nj))]
    n_out = len(weights)
    outs = pl.pallas_call(
        functools.partial(_qkv_body, with_kv=with_kv),
        grid=(n // tm, nj),
        in_specs=([pl.BlockSpec((tm, d), lambda i, j: (i, 0)), pl.BlockSpec((2, d), fixed)]
                  + [pl.BlockSpec((d, tn), index_map) for _, index_map in weights]
                  + [pl.BlockSpec((2, HEAD_DIM), fixed)]),
        out_specs=[pl.BlockSpec((tm, tn), tile)] * n_out,
        out_shape=[jax.ShapeDtypeStruct((n, n_q), _BF16)] * n_out,
        scratch_shapes=[pltpu.VMEM((2, tm, d), _BF16)],
        compiler_params=_params("arbitrary", "arbitrary"),
        name="qkv_projection",
    )(x, gains, *[w for w, _ in weights], head_gains)
    return tuple(outs)


def _conv_in_body(x_ref, g_ref, wb_ref, wc_ref, wx_ref, cw_ref, z_ref, xn_ref, carry_ref,
                  *, tiles_per_seq):
    i = pl.program_id(0)
    j = pl.program_id(1)
    tm = x_ref.shape[0]

    @pl.when(j == 0)
    def _():
        xn_ref[...] = _rms_normalise(x_ref[...], g_ref[...]).astype(_BF16)

    @pl.when(i % tiles_per_seq == 0)
    def _():
        carry_ref[j] = jnp.zeros(carry_ref.shape[1:], _F32)

    xn = xn_ref[...]
    b_gate = _bf16_dot(xn, wb_ref[...].astype(_BF16))
    c_gate = _bf16_dot(xn, wc_ref[...].astype(_BF16))
    xh = _bf16_dot(xn, wx_ref[...].astype(_BF16))
    u = c_gate * xh
    prev = carry_ref[j]
    row = lax.broadcasted_iota(jnp.int32, u.shape, 0)
    last = prev[SUBLANES - 1:SUBLANES]
    u1 = jnp.where(row == 0, last, pltpu.roll(u, 1, axis=0))
    u2 = jnp.where(row == 0, prev[SUBLANES - 2:SUBLANES - 1],
                   jnp.where(row == 1, last, pltpu.roll(u, 2, axis=0)))
    cw = cw_ref[...]
    conv = cw[0:1] * u2 + cw[1:2] * u1 + cw[2:3] * u
    z_ref[...] = (b_gate * conv).astype(z_ref.dtype)
    carry_ref[j] = u[tm - SUBLANES:tm]


def _conv_in(x, gain, w_in, conv_w, seq_len):
    n, d = x.shape
    tm, tn = ROW_TILE, TRIPLE_COL_TILE
    nj = d // tn
    assert CONV_WIDTH - 1 <= SUBLANES and seq_len % tm == 0
    return pl.pallas_call(
        functools.partial(_conv_in_body, tiles_per_seq=seq_len // tm),
        grid=(n // tm, nj),
        in_specs=[
            pl.BlockSpec((tm, d), lambda i, j: (i, 0)),
            pl.BlockSpec((1, d), lambda i, j: (0, 0)),
            pl.BlockSpec((d, tn), lambda i, j: (0, j)),
            pl.BlockSpec((d, tn), lambda i, j: (0, j + nj)),
            pl.BlockSpec((d, tn), lambda i, j: (0, j + 2 * nj)),
            pl.BlockSpec((CONV_WIDTH, tn), lambda i, j: (0, j)),
        ],
        out_specs=pl.BlockSpec((tm, tn), lambda i, j: (i, j)),
        out_shape=jax.ShapeDtypeStruct((n, d), _BF16),
        scratch_shapes=[pltpu.VMEM((tm, d), _BF16), pltpu.VMEM((nj, SUBLANES, tn), _F32)],
        compiler_params=_params("arbitrary", "arbitrary"),
        name="conv_in",
    )(x, gain.reshape(1, d), w_in, w_in, w_in, conv_w)


def _matmul_residual_body(x_ref, w_ref, r_ref, o_ref):
    o_ref[...] = r_ref[...] + _bf16_dot(x_ref[...], w_ref[...].astype(_BF16))


def _matmul_residual(x, w, res):
    n, k = x.shape
    d = w.shape[1]
    tm, tn = RES_ROW_TILE, RES_COL_TILE
    return pl.pallas_call(
        _matmul_residual_body,
        grid=(n // tm, d // tn),
        in_specs=[
            pl.BlockSpec((tm, k), lambda i, j: (i, 0)),
            pl.BlockSpec((k, tn), lambda i, j: (0, j)),
            pl.BlockSpec((tm, tn), lambda i, j: (i, j)),
        ],
        out_specs=pl.BlockSpec((tm, tn), lambda i, j: (i, j)),
        out_shape=jax.ShapeDtypeStruct((n, d), _F32),
        compiler_params=_params("arbitrary", "arbitrary"),
        name="matmul_residual",
    )(x, w, res)


def _moba_body(q_ref, k_ref, v_ref, o_ref, km_ref, vt_ref, bias_ref):
    heads = range(q_ref.shape[1] // HEAD_DIM)
    head0 = pl.program_id(1) * len(heads)
    i = pl.program_id(2)
    kb = MOBA_BLOCK
    n_blocks = k_ref.shape[0] // kb
    nt = (((1,), (1,)), ((), ()))
    log2e = 1.4426950408889634
    scale = HEAD_DIM ** -0.5 * log2e
    slopes = [log2e * jnp.exp2(jnp.full((1, kb), -ALIBI_MAX_BIAS / N_HEADS, _F32)
                               * (head0 + hh + 1).astype(_F32)) for hh in heads]

    def lanes(hh):
        return slice(hh * HEAD_DIM, (hh + 1) * HEAD_DIM)

    @pl.when(i == 0)
    def _():
        rel = (lax.broadcasted_iota(jnp.int32, (kb, kb), 1)
               - lax.broadcasted_iota(jnp.int32, (kb, kb), 0))
        for hh in heads:
            for jb in range(n_blocks):
                rows = slice(jb * kb, (jb + 1) * kb)
                km_ref[hh, jb:jb + 1, :] = jnp.mean(k_ref[rows, lanes(hh)].astype(_F32),
                                                    axis=0, keepdims=True)
                vt_ref[hh, :, rows] = v_ref[rows, lanes(hh)].astype(_F32).T.astype(_BF16)
            bias = slopes[hh] * rel.astype(_F32)
            bias_ref[hh, 0] = bias
            bias_ref[hh, 1] = jnp.where(rel >= 0, bias, jnp.inf)

    q = [q_ref[:, lanes(hh)] for hh in heads]
    sel = []
    for hh in heads:
        gate, rest = None, km_ref[hh]
        for _ in range(3):
            term = rest.astype(_BF16)
            rest = rest - term.astype(_F32)
            part = lax.dot_general(term, q[hh], nt, preferred_element_type=_F32)
            gate = part if gate is None else gate + part
        blk = lax.broadcasted_iota(jnp.int32, gate.shape, 0)
        gate = jnp.where(blk < i, gate, -jnp.inf)
        picked = jnp.zeros(gate.shape, _F32)
        for jb in range(n_blocks - 1):
            g_jb = gate[jb:jb + 1, :]
            beats = (gate > g_jb) | ((gate == g_jb) & (blk < jb))
            n_beat = jnp.sum(beats.astype(_F32), axis=0, keepdims=True)
            picked = jnp.where((blk == jb) & (blk < i) & (n_beat < MOBA_TOPK), 1.0, picked)
        sel.append(picked)

    def attend(own):
        nb = own + 1
        raw = [lax.dot_general(k_ref[0:nb * kb, lanes(hh)], q[hh], nt, preferred_element_type=_F32)
               for hh in heads]
        tiles, col_max = [[] for _ in heads], [[] for _ in heads]
        for jb in range(nb):
            for hh in heads:
                tile = raw[hh][jb * kb:(jb + 1) * kb, :] * scale - bias_ref[hh, 1 if jb == own else 0]
                mx = jnp.max(tile, axis=0, keepdims=True)
                if jb != own:
                    mx = jnp.where(sel[hh][jb:jb + 1, :] > 0.5,
                                   mx - slopes[hh] * float((own - jb) * kb), -jnp.inf)
                tiles[hh].append(tile), col_max[hh].append(mx)
        m = [functools.reduce(jnp.maximum, col_max[hh]) for hh in heads]
        probs, l = [[] for _ in heads], [None for _ in heads]
        for jb in range(nb):
            for hh in heads:
                offset = m[hh] if jb == own else jnp.where(
                    sel[hh][jb:jb + 1, :] > 0.5, m[hh] + slopes[hh] * float((own - jb) * kb), jnp.inf)
                p = jnp.exp2(tiles[hh][jb] - offset)
                p_sum = jnp.sum(p, axis=0, keepdims=True)
                l[hh] = p_sum if l[hh] is None else l[hh] + p_sum
                probs[hh].append(p.astype(_BF16))
        for hh in heads:
            p_all = probs[hh][0] if nb == 1 else jnp.concatenate(probs[hh], axis=0)
            acc = _bf16_dot(vt_ref[hh, :, 0:nb * kb], p_all)
            o_ref[:, lanes(hh)] = (acc / l[hh]).T.astype(o_ref.dtype)

    for own in range(n_blocks):
        pl.when(i == own)(functools.partial(attend, own))


def _moba_attention(q, k, v, batch, seq_len):
    n, d = q.shape
    kb = MOBA_BLOCK
    nqb = seq_len // kb
    hp = ATTN_HEADS
    width = hp * HEAD_DIM
    assert seq_len % kb == 0 and nqb == SUBLANES and N_HEADS % hp == 0 and d == N_HEADS * HEAD_DIM
    return pl.pallas_call(
        _moba_body,
        grid=(batch, N_HEADS // hp, nqb),
        in_specs=[
            pl.BlockSpec((kb, width), lambda b, h, i: (b * nqb + i, h)),
            pl.BlockSpec((seq_len, width), lambda b, h, i: (b, h)),
            pl.BlockSpec((seq_len, width), lambda b, h, i: (b, h)),
        ],
        out_specs=pl.BlockSpec((kb, width), lambda b, h, i: (b * nqb + i, h)),
        out_shape=jax.ShapeDtypeStruct((n, d), _BF16),
        scratch_shapes=[pltpu.VMEM((hp, nqb, HEAD_DIM), _F32),
                        pltpu.VMEM((hp, HEAD_DIM, seq_len), _BF16),
                        pltpu.VMEM((hp, 2, kb, kb), _F32)],
        compiler_params=_params("arbitrary", "arbitrary", "arbitrary"),
        name="moba_attention",
    )(q, k, v)


def _router_body(x_ref, g_ref, wr_ref, hn_ref, ids_ref, wts_ref, cnt_ref, run_ref):
    step = pl.program_id(0)
    hn = _rms_normalise(x_ref[...], g_ref[...])
    _store_row_tiles(hn_ref, _pack_bf16_halves(hn))
    w = wr_ref[...]
    hn_hi, w_hi = hn.astype(_BF16), w.astype(_BF16)
    hn_lo = (hn - hn_hi.astype(_F32)).astype(_BF16)
    w_lo = (w - w_hi.astype(_F32)).astype(_BF16)
    logits = (_bf16_dot(hn_hi, w_hi) + (_bf16_dot(hn_hi, w_lo) + _bf16_dot(hn_lo, w_hi))).T
    row = lax.broadcasted_iota(jnp.int32, logits.shape, 0)
    far = jnp.int32(logits.shape[0])

    def first_row(mask):
        return jnp.min(jnp.where(mask, row, far), axis=0, keepdims=True)

    is_group = (row >= N_EXPERTS) & (row < N_EXPERTS + N_GROUPS)
    g_max = jnp.max(jnp.where(is_group, logits, -jnp.inf), axis=0, keepdims=True)
    g_sel = first_row(is_group & (logits == g_max)) - N_EXPERTS
    g_w = 1.0 / jnp.sum(jnp.where(is_group, jnp.exp(logits - g_max), 0.0), axis=0, keepdims=True)

    in_group = (row >= g_sel * EXPERTS_PER_GROUP) & (row < (g_sel + 1) * EXPERTS_PER_GROUP)
    top1 = jnp.max(jnp.where(in_group, logits, -jnp.inf), axis=0, keepdims=True)
    idx1 = first_row(in_group & (logits == top1))
    rest = in_group & (row != idx1)
    top2 = jnp.max(jnp.where(rest, logits, -jnp.inf), axis=0, keepdims=True)
    idx2 = first_row(rest & (logits == top2))
    t = jnp.exp(top2 - top1)
    w1 = 1.0 / (1.0 + t) * g_w
    w2 = t / (1.0 + t) * g_w

    @pl.when(step == 0)
    def _():
        run_ref[...] = jnp.zeros(run_ref.shape, _F32)

    tm = logits.shape[1]
    chosen = jnp.where((row == idx1) | (row == idx2), 1.0, 0.0)
    earlier = (lax.broadcasted_iota(jnp.int32, (tm, tm), 0)
               < lax.broadcasted_iota(jnp.int32, (tm, tm), 1))
    before = run_ref[:, 0:1] + _bf16_dot(chosen.astype(_BF16),
                                         jnp.where(earlier, 1.0, 0.0).astype(_BF16))
    rank1 = jnp.sum(jnp.where(row == idx1, before, 0.0), axis=0, keepdims=True).astype(jnp.int32)
    rank2 = jnp.sum(jnp.where(row == idx2, before, 0.0), axis=0, keepdims=True).astype(jnp.int32)
    run_ref[...] += jnp.sum(chosen, axis=1, keepdims=True)

    out_row = lax.broadcasted_iota(jnp.int32, ids_ref.shape, 0)
    ids_ref[...] = jnp.where(out_row == 0, idx1, jnp.where(out_row == 1, idx2,
                             jnp.where(out_row == 2, rank1, jnp.where(out_row == 3, rank2, 0))))
    wts_ref[...] = jnp.where(row == 0, w1, jnp.where(row == 1, w2, 0.0)).T
    cnt_ref[...] = run_ref[...].astype(jnp.int32)


def _router(x, gain, w_rg, w_re):
    n, d = x.shape
    tm = ROUTER_ROWS
    assert EXPERT_TOPK == 2 and N_EXPERTS + N_GROUPS <= LANES
    chunks = d // 2 // LANES
    wr = jnp.concatenate(
        [w_re, w_rg, jnp.zeros((d, LANES - N_EXPERTS - N_GROUPS), _F32)], axis=1)
    row = lambda i: (i, 0)
    col = lambda i: (0, i)
    fixed = lambda i: (0, 0)
    return pl.pallas_call(
        _router_body,
        grid=(n // tm,),
        in_specs=[
            pl.BlockSpec((tm, d), row),
            pl.BlockSpec((1, d), fixed),
            pl.BlockSpec((d, LANES), fixed),
        ],
        out_specs=[pl.BlockSpec((tm * chunks, LANES), row), pl.BlockSpec((SUBLANES, tm), col),
                   pl.BlockSpec((tm, LANES), row), pl.BlockSpec((LANES, LANES), fixed)],
        out_shape=[jax.ShapeDtypeStruct((n * chunks, LANES), jnp.uint32),
                   jax.ShapeDtypeStruct((SUBLANES, n), jnp.int32),
                   jax.ShapeDtypeStruct((n, LANES), _F32),
                   jax.ShapeDtypeStruct((LANES, LANES), jnp.int32)],
        scratch_shapes=[pltpu.VMEM((LANES, LANES), _F32)],
        compiler_params=_params("arbitrary"),
        name="router",
    )(x, gain.reshape(1, d), wr)


def _expert_body(bounds_ref, blk_base_ref, blk_last_ref, sorted_tok_ref, hn_hbm, wg_ref, wu_ref, wd_ref, y_hbm,
                 xs_ref, ys_ref, zero_ref, wg_bf, wu_bf, wd_bf, gather_sem, out_sem, tail_sem):
    e = pl.program_id(0)
    n_experts = pl.num_programs(0)
    n_slots = xs_ref.shape[0]
    chunks = wg_bf.shape[0] // 2 // LANES
    rows = xs_ref.shape[1] // chunks
    tile_rows = rows * chunks
    ahead = n_slots - 1
    first, stop = bounds_ref[e], bounds_ref[e + 1]
    n_total = bounds_ref[n_experts]

    def row_copy(tok, s, r):
        src = hn_hbm.at[pl.ds(pl.multiple_of(tok * chunks, chunks), chunks)]
        return pltpu.make_async_copy(src, xs_ref.at[s, pl.ds(r * chunks, chunks)], gather_sem.at[s])

    def row_token(base, last, r):
        return sorted_tok_ref[base + jnp.minimum(r, last)]

    def start_gather(block):
        base, last = blk_base_ref[block], blk_last_ref[block]

        def body(r, carry):
            row_copy(row_token(base, last, r), block % n_slots, r).start(priority=1)
            return carry
        lax.fori_loop(0, rows, body, 0)

    def wait_gather(block):
        s = block % n_slots
        pltpu.make_async_copy(hn_hbm.at[pl.ds(0, rows * chunks)], xs_ref.at[s],
                              gather_sem.at[s]).wait()

    def out_copy(block):
        s = block % n_slots
        dst = y_hbm.at[pl.ds(pl.multiple_of(block * tile_rows, tile_rows), tile_rows)]
        return pltpu.make_async_copy(ys_ref.at[s], dst, out_sem.at[s])

    def tail_copy(block):
        dst = y_hbm.at[pl.ds(pl.multiple_of(block * tile_rows, tile_rows), tile_rows)]
        return pltpu.make_async_copy(zero_ref, dst, tail_sem.at[0])

    def for_tail_blocks(fn):
        def body(block, carry):
            fn(tail_copy(block))
            return carry
        lax.fori_loop(n_total, y_hbm.shape[0] // tile_rows, body, 0)

    @pl.when(e == 0)
    def _():
        for block in range(ahead):
            start_gather(block)
        zero_ref[...] = jnp.zeros(zero_ref.shape, zero_ref.dtype)
        for_tail_blocks(lambda copy: copy.start())

    @pl.when(stop > first)
    def _():
        wg_bf[...] = wg_ref[0, 0].astype(_BF16)
        wu_bf[...] = wu_ref[0, 0].astype(_BF16)
        wd_bf[...] = wd_ref[0, 0].astype(_BF16)

    def block_step(block, carry):
        @pl.when(block >= n_slots)
        def _():
            out_copy(block - n_slots).wait()

        nxt = block + ahead
        nxt_slot = nxt % n_slots
        nxt_block = jnp.minimum(nxt, n_total - 1)
        nxt_base, nxt_last = blk_base_ref[nxt_block], blk_last_ref[nxt_block]
        per_group = rows // EXPERT_ISSUE_GROUPS

        def issue(group):
            for r in range(group * per_group, (group + 1) * per_group):
                row_copy(row_token(nxt_base, nxt_last, r), nxt_slot, r).start(priority=1)

        wait_gather(block)
        s = block % n_slots
        x_lo, x_hi = _unpack_bf16_halves(_load_row_tiles(xs_ref.at[s], chunks), _BF16)
        half = x_lo.shape[1]
        gate = _bf16_dot(x_lo, wg_bf[:half, :])
        issue(0)
        gate = gate + _bf16_dot(x_hi, wg_bf[half:, :])
        issue(1)
        up = _bf16_dot(x_lo, wu_bf[:half, :])
        issue(2)
        up = up + _bf16_dot(x_hi, wu_bf[half:, :])
        issue(3)
        act = (jax.nn.silu(gate) * up).astype(_BF16)
        quarter = wd_bf.shape[1] // 4
        y = []
        for c in range(4):
            y.append(_bf16_dot(act, wd_bf[:, c * quarter:(c + 1) * quarter]))
            issue(4 + c)
        packed = jnp.concatenate([_pack_bf16_pair(y[0], y[2]), _pack_bf16_pair(y[1], y[3])], axis=1)
        _store_row_tiles(ys_ref.at[s], packed)
        out_copy(block).start()
        return carry

    lax.fori_loop(first, stop, block_step, 0)

    @pl.when(e == n_experts - 1)
    def _():
        for extra in range(ahead):
            wait_gather(n_total + extra)
        for_tail_blocks(lambda copy: copy.wait())
        for back in range(1, n_slots + 1):
            @pl.when(n_total >= back)
            def _():
                out_copy(n_total - back).wait()


def _expert_mlp(bounds, blk_base, blk_last, sorted_tok, hn, layer, w_gate, w_up, w_down):
    n_experts, d, de = w_gate.shape[1:]
    rows = EXPERT_ROWS
    chunks = d // 2 // LANES
    assert hn.shape[1] == LANES and hn.dtype == jnp.uint32
    assert hn.shape[0] // chunks * EXPERT_TOPK // rows >= EXPERT_SLOTS
    grid_spec = pltpu.PrefetchScalarGridSpec(
        num_scalar_prefetch=4,
        grid=(n_experts,),
        in_specs=[
            pl.BlockSpec(memory_space=pl.ANY),
            pl.BlockSpec((1, 1, d, de), lambda e, *prefetch: (layer, e, 0, 0)),
            pl.BlockSpec((1, 1, d, de), lambda e, *prefetch: (layer, e, 0, 0)),
            pl.BlockSpec((1, 1, de, d), lambda e, *prefetch: (layer, e, 0, 0)),
        ],
        out_specs=pl.BlockSpec(memory_space=pl.ANY),
        scratch_shapes=[
            pltpu.VMEM((EXPERT_SLOTS, rows * chunks, LANES), jnp.uint32),
            pltpu.VMEM((EXPERT_SLOTS, rows * chunks, LANES), jnp.uint32),
            pltpu.VMEM((rows * chunks, LANES), jnp.uint32),
            pltpu.VMEM((d, de), _BF16),
            pltpu.VMEM((d, de), _BF16),
            pltpu.VMEM((de, d), _BF16),
            pltpu.SemaphoreType.DMA((EXPERT_SLOTS,)),
            pltpu.SemaphoreType.DMA((EXPERT_SLOTS,)),
            pltpu.SemaphoreType.DMA((1,)),
        ],
    )
    return pl.pallas_call(
        _expert_body,
        grid_spec=grid_spec,
        out_shape=jax.ShapeDtypeStruct((blk_base.shape[0] * rows * chunks, LANES), jnp.uint32),
        compiler_params=_params("arbitrary"),
        name="expert_mlp",
    )(bounds, blk_base, blk_last, sorted_tok, hn, w_gate, w_up, w_down)


def _combine_body(pos_ref, y_hbm, wts_ref, h_ref, o_ref, ys_ref, sem):
    i = pl.program_id(0)
    n_steps = pl.num_programs(0)
    tm = h_ref.shape[0]
    chunks = h_ref.shape[1] // 2 // LANES
    n_tokens = n_steps * tm
    n_slots = ys_ref.shape[0]
    ahead = n_slots - 1
    slot = i % n_slots

    def row_copy(p, s, k, r):
        src = y_hbm.at[pl.ds(pl.multiple_of(p * chunks, chunks), chunks)]
        return pltpu.make_async_copy(src, ys_ref.at[s, k, pl.ds(r * chunks, chunks)], sem.at[s])

    def start_row(tile, r):
        for k in range(EXPERT_TOPK):
            row_copy(pos_ref[k * n_tokens + tile * tm + r], tile % n_slots, k, r).start(priority=k)

    def wait_gather(s):
        for k in range(EXPERT_TOPK):
            pltpu.make_async_copy(y_hbm.at[pl.ds(0, tm * chunks)], ys_ref.at[s, k], sem.at[s]).wait()

    @pl.when(i == 0)
    def _():
        for tile in range(ahead):
            def body(r, carry):
                start_row(tile, r)
                return carry
            lax.fori_loop(0, tm, body, 0)

    @pl.when(i + ahead < n_steps)
    def _():
        for r in range(tm):
            start_row(i + ahead, r)

    wait_gather(slot)
    w = wts_ref[...]
    half = chunks * LANES
    a_lo, a_hi = _unpack_bf16_halves(_load_row_tiles(ys_ref.at[slot, 0], chunks), _F32)
    b_lo, b_hi = _unpack_bf16_halves(_load_row_tiles(ys_ref.at[slot, 1], chunks), _F32)
    o_ref[:, :half] = h_ref[:, :half] + (w[:, 0:1] * a_lo + w[:, 1:2] * b_lo)
    o_ref[:, half:] = h_ref[:, half:] + (w[:, 0:1] * a_hi + w[:, 1:2] * b_hi)


def _combine(pos, y, wts, h):
    n, d = h.shape
    tm = COMBINE_ROWS
    grid_spec = pltpu.PrefetchScalarGridSpec(
        num_scalar_prefetch=1,
        grid=(n // tm,),
        in_specs=[
            pl.BlockSpec(memory_space=pl.ANY),
            pl.BlockSpec((tm, LANES), lambda i, p: (i, 0)),
            pl.BlockSpec((tm, d), lambda i, p: (i, 0)),
        ],
        out_specs=pl.BlockSpec((tm, d), lambda i, p: (i, 0)),
        scratch_shapes=[
            pltpu.VMEM((COMBINE_SLOTS, EXPERT_TOPK, tm * (d // 2 // LANES), LANES), jnp.uint32),
            pltpu.SemaphoreType.DMA((COMBINE_SLOTS,)),
        ],
    )
    return pl.pallas_call(
        _combine_body,
        grid_spec=grid_spec,
        out_shape=jax.ShapeDtypeStruct((n, d), _F32),
        compiler_params=_params("arbitrary"),
        name="combine",
    )(pos, y, wts, h)


def _dispatch_plan(ids, counts):
    n = ids.shape[1]
    rows = EXPERT_ROWS
    m = n * EXPERT_TOPK
    n_blocks = -(-(m + N_EXPERTS * (rows - 1)) // rows)
    flat_e = ids[:EXPERT_TOPK].reshape(m)
    rank = ids[EXPERT_TOPK:2 * EXPERT_TOPK].reshape(m)
    counts = counts[:N_EXPERTS, 0]
    starts = jnp.cumsum(counts) - counts
    padded = (counts + rows - 1) // rows * rows
    pends = jnp.cumsum(padded)
    pstarts = pends - padded
    pos = rank
    for e in range(N_EXPERTS):
        pos = pos + jnp.where(flat_e == e, pstarts[e], 0)
    blk_start = jnp.arange(n_blocks, dtype=jnp.int32)[:, None] * rows
    in_seg = (pstarts[None, :] <= blk_start) & (blk_start < pends[None, :])
    shift = jnp.sum(jnp.where(in_seg, (starts - pstarts)[None, :], 0), axis=1, keepdims=True)
    seg_end = jnp.sum(jnp.where(in_seg, (starts + counts)[None, :], 0), axis=1, keepdims=True)
    token = jnp.arange(m, dtype=jnp.int32) % n
    _, sorted_tok = lax.sort((flat_e * n + token, token), num_keys=1)
    blk_base = (blk_start + shift)[:, 0]
    blk_last = jnp.clip(seg_end[:, 0] - blk_base, 1, rows) - 1
    bounds = jnp.concatenate([pstarts, pends[-1:]]) // rows
    return (bounds.astype(jnp.int32), blk_base.astype(jnp.int32), blk_last.astype(jnp.int32),
            sorted_tok.astype(jnp.int32), pos.astype(jnp.int32))


def _hier_moe_residual(h, gain, w_rg, w_re, layer, w_gate, w_up, w_down):
    hn, ids, wts, counts = _router(h, gain, w_rg, w_re)
    bounds, blk_base, blk_last, sorted_tok, pos = _dispatch_plan(ids, counts)
    y = _expert_mlp(bounds, blk_base, blk_last, sorted_tok, hn, layer, w_gate, w_up, w_down)
    return _combine(pos, y, wts, h)


def kernel(x, mix_norm, ffn_norm, conv_w_in, conv_w, conv_w_out, kv_norm, w_kv, k_norm, w_q, q_norm,
           w_o, router_group, router_expert, w_gate, w_up, w_down):
    bsz, seq_len, d = x.shape
    depth = mix_norm.shape[0]
    n_conv = conv_w_in.shape[0]
    assert d == N_HEADS * HEAD_DIM
    h = x.reshape(bsz * seq_len, d)
    k_sh = v_sh = None
    for l in range(depth):
        if l < n_conv:
            z = _conv_in(h, mix_norm[l], conv_w_in[l], conv_w[l], seq_len)
            h = _matmul_residual(z, conv_w_out[l], h)
        else:
            j = l - n_conv
            gains = jnp.stack([mix_norm[l], kv_norm])
            head_gains = jnp.stack([q_norm[j], k_norm])
            q, *kv = _qkv_projection(h, gains, w_q[j], w_kv, head_gains, with_kv=(j == 0))
            if j == 0:
                k_sh, v_sh = kv
            attn = _moba_attention(q, k_sh, v_sh, bsz, seq_len)
            h = _matmul_residual(attn, w_o[j], h)
        h = _hier_moe_residual(h, ffn_norm[l], router_group[l], router_expert[l],
                               l, w_gate, w_up, w_down)
    return h.reshape(bsz, seq_len, d)
```
